```python
import jax, jax.numpy as jnp
from jax import lax
import numpy as np

D_MODEL = 1024
BATCH = 32
SEQ = 2048
DEPTH = 2

N_A_LAYERS = max(1, DEPTH // 2)
N_B_LAYERS = DEPTH - N_A_LAYERS
RMS_EPS = 1e-6
NEG_INF = -1e30
MLSTM_HEADS = 4
MLSTM_V_DIM = D_MODEL // MLSTM_HEADS
MLSTM_QK_DIM = MLSTM_V_DIM // 2
MLSTM_CHUNK = 64
GATE_SOFTCAP = 15.0
NSA_HEADS = 16
NSA_HEAD_DIM = D_MODEL // NSA_HEADS
NSA_KV_GROUPS = 4
NSA_GROUP_SIZE = NSA_HEADS // NSA_KV_GROUPS
CMP_BLOCK = 32
CMP_STRIDE = 16
SEL_BLOCK = 64
SEL_TOPK = 16
WINDOW = 512
Q_BLOCK = 128
FORCED_SCORE = 1e6
ROPE_THETA = 500000.0
ROPE_DIM = NSA_HEAD_DIM // 4
MOE_GROUPS = 4
MOE_EXPERTS_PER_GROUP = 8
MOE_EXPERTS = MOE_GROUPS * MOE_EXPERTS_PER_GROUP
MOE_TOPK = 2
MOE_HIDDEN = D_MODEL // 4

kernel_name = 'yoco_mlstm_nsa_hier_moe'


def rmsnorm(x, g):
    xf = x.astype(jnp.float32)
    y = xf * lax.rsqrt(jnp.mean(xf * xf, axis=-1, keepdims=True) + RMS_EPS)
    return (y * g.astype(jnp.float32)).astype(x.dtype)


def partial_rope(x, pos):
    half = ROPE_DIM // 2
    inv_freq = jnp.power(jnp.float32(ROPE_THETA), -jnp.arange(half, dtype=jnp.float32) * (2.0 / ROPE_DIM))
    ang = pos.astype(jnp.float32)[:, None] * inv_freq[None, :]
    cos = jnp.cos(ang)[None, :, None, :]
    sin = jnp.sin(ang)[None, :, None, :]
    xf = x.astype(jnp.float32)
    x1 = xf[..., :half]
    x2 = xf[..., half:ROPE_DIM]
    out = jnp.concatenate([x1 * cos - x2 * sin, x2 * cos + x1 * sin, xf[..., ROPE_DIM:]], axis=-1)
    return out.astype(x.dtype)


def mlstm_chunk_scan(q, k, v, log_i, log_f):
    B, H, S, dk = q.shape
    dv = v.shape[-1]
    L = MLSTM_CHUNK
    nc = S // L

    def to_chunks(t):
        return jnp.moveaxis(t.reshape((B, H, nc, L) + t.shape[3:]), 2, 0)

    causal = jnp.tril(jnp.ones((L, L), dtype=bool))

    def step(carry, xs):
        C, n, m = carry
        qc, kc, vc, li, lf = xs
        b = jnp.cumsum(lf, axis=-1)
        dmat = jnp.where(causal, b[..., :, None] - b[..., None, :] + li[..., None, :], NEG_INF)
        m_inter = b + m[..., None]
        m_t = jnp.maximum(m_inter, dmat.max(axis=-1))
        w_intra = jnp.exp(dmat - m_t[..., None])
        w_inter = jnp.exp(m_inter - m_t)
        s = jnp.einsum('bhtd,bhsd->bhts', qc, kc) * w_intra
        num = jnp.einsum('bhts,bhsv->bhtv', s, vc) + w_inter[..., None] * jnp.einsum('bhtd,bhdv->bhtv', qc, C)
        den = s.sum(axis=-1) + w_inter * jnp.einsum('bhtd,bhd->bht', qc, n)
        h = num / jnp.maximum(jnp.abs(den), jnp.exp(-m_t))[..., None]
        b_end = b[..., -1]
        g = b_end[..., None] - b + li
        m_new = jnp.maximum(b_end + m, g.max(axis=-1))
        ws = jnp.exp(g - m_new[..., None])
        decay = jnp.exp(b_end + m - m_new)
        C_new = decay[..., None, None] * C + jnp.einsum('bhsd,bhsv->bhdv', kc * ws[..., None], vc)
        n_new = decay[..., None] * n + jnp.einsum('bhs,bhsd->bhd', ws, kc)
        return (C_new, n_new, m_new), h

    init = (jnp.zeros((B, H, dk, dv), jnp.float32), jnp.zeros((B, H, dk), jnp.float32), jnp.zeros((B, H), jnp.float32))
    xs = (to_chunks(q), to_chunks(k), to_chunks(v), to_chunks(log_i), to_chunks(log_f))
    _, hs = lax.scan(step, init, xs)
    return jnp.moveaxis(hs, 0, 2).reshape(B, H, S, dv)


def mlstm_mixer(h, w_in, gate_bias, head_norm, w_out):
    B, S, _ = h.shape
    H, dk, dv = MLSTM_HEADS, MLSTM_QK_DIM, MLSTM_V_DIM
    f32 = jnp.float32
    proj = jnp.einsum('bsd,de->bse', h, w_in)
    o1 = H * dk
    o2 = 2 * H * dk
    o3 = o2 + H * dv
    o4 = o3 + H * dv

    def heads(t, d):
        return t.reshape(B, S, H, d).transpose(0, 2, 1, 3).astype(f32)

    q = heads(proj[..., :o1], dk)
    k = heads(proj[..., o1:o2], dk) * (dk ** -0.5)
    v = heads(proj[..., o2:o3], dv)
    o_gate = jax.nn.sigmoid(proj[..., o3:o4].astype(f32))
    gates = proj[..., o4:].astype(f32) + gate_bias.astype(f32)
    gates = GATE_SOFTCAP * jnp.tanh(gates / GATE_SOFTCAP)
    log_i = gates[..., :H].transpose(0, 2, 1)
    log_f = jax.nn.log_sigmoid(gates[..., H:]).transpose(0, 2, 1)
    hs = mlstm_chunk_scan(q, k, v, log_i, log_f)
    hs = hs * lax.rsqrt(jnp.mean(hs * hs, axis=-1, keepdims=True) + RMS_EPS)
    hs = hs.transpose(0, 2, 1, 3).reshape(B, S, H * dv) * head_norm.astype(f32)
    out = (hs * o_gate).astype(h.dtype)
    return jnp.einsum('bse,ed->bsd', out, w_out)


def compress_blocks(t, pe, w1, w2):
    B, S, G, dh = t.shape
    ncmp = (S - CMP_BLOCK) // CMP_STRIDE + 1
    idx = jnp.arange(ncmp)[:, None] * CMP_STRIDE + jnp.arange(CMP_BLOCK)[None, :]
    blk = t[:, idx] + pe[None, None, :, None, :]
    blk = blk.transpose(0, 1, 3, 2, 4).reshape(B, ncmp, G, CMP_BLOCK * dh)
    return jnp.einsum('bngf,fd->bngd', jax.nn.silu(jnp.einsum('bngi,if->bngf', blk, w1)), w2)


def nsa_shared_kv(x_mid, kv_norm, kv_w, cmp_pe_k, cmp_w1_k, cmp_w2_k, cmp_pe_v, cmp_w1_v, cmp_w2_v):
    B, S, _ = x_mid.shape
    G, dh = NSA_KV_GROUPS, NSA_HEAD_DIM
    hk = rmsnorm(x_mid, kv_norm)
    kv = jnp.einsum('bsd,de->bse', hk, kv_w).reshape(B, S, 6, G, dh)
    pos = jnp.arange(S)
    ncmp = (S - CMP_BLOCK) // CMP_STRIDE + 1
    cmp_pos = jnp.arange(ncmp) * CMP_STRIDE + CMP_BLOCK - 1
    k_cmp = partial_rope(compress_blocks(kv[:, :, 0], cmp_pe_k, cmp_w1_k, cmp_w2_k), cmp_pos)
    v_cmp = compress_blocks(kv[:, :, 1], cmp_pe_v, cmp_w1_v, cmp_w2_v)
    k_slc = partial_rope(kv[:, :, 2], pos)
    v_slc = kv[:, :, 3]
    k_win = partial_rope(kv[:, :, 4], pos)
    v_win = kv[:, :, 5]
    tg = lambda t: t.transpose(0, 2, 1, 3)
    return (tg(k_cmp), tg(v_cmp), tg(k_slc), tg(v_slc), tg(k_win), tg(v_win))


def nsa_mixer(h, w_q, w_out, k_cmp, v_cmp, k_slc, v_slc, k_win, v_win):
    B, S, _ = h.shape
    H, G, Hg, dh = NSA_HEADS, NSA_KV_GROUPS, NSA_GROUP_SIZE, NSA_HEAD_DIM
    f32 = jnp.float32
    scale = dh ** -0.5
    nqb = S // Q_BLOCK
    nsel = S // SEL_BLOCK
    ncmp = k_cmp.shape[2]
    k_eff = min(SEL_TOPK, nsel)
    proj = jnp.einsum('bsd,de->bse', h, w_q)
    q = partial_rope(proj[..., :H * dh].reshape(B, S, H, dh), jnp.arange(S))
    q = q.astype(f32).reshape(B, nqb, Q_BLOCK, G, Hg, dh).transpose(1, 0, 3, 4, 2, 5)
    gates = jax.nn.sigmoid(proj[..., H * dh:].astype(f32))
    gates = gates.reshape(B, nqb, Q_BLOCK, G, Hg, 3).transpose(1, 0, 3, 4, 2, 5)
    starts = jnp.arange(nqb, dtype=jnp.int32) * Q_BLOCK
    kc = k_cmp.astype(f32)
    vc = v_cmp.astype(f32)
    ks_blk = k_slc.astype(f32).reshape(B, G, nsel, SEL_BLOCK, dh)
    vs_blk = v_slc.astype(f32).reshape(B, G, nsel, SEL_BLOCK, dh)
    pad = ((0, 0), (0, 0), (WINDOW, 0), (0, 0))
    kw_pad = jnp.pad(k_win.astype(f32), pad)
    vw_pad = jnp.pad(v_win.astype(f32), pad)
    cmp_start = jnp.arange(ncmp) * CMP_STRIDE
    cmp_end = cmp_start + CMP_BLOCK - 1
    sel_start = jnp.arange(nsel) * SEL_BLOCK
    overlap = ((cmp_start[:, None] <= sel_start[None, :] + SEL_BLOCK - 1)
               & (cmp_end[:, None] >= sel_start[None, :])).astype(f32)
    sel_ids = jnp.arange(nsel)
    r = jnp.arange(SEL_BLOCK)
    win_off = jnp.arange(WINDOW + Q_BLOCK)
    gather = jax.vmap(jax.vmap(lambda blocks, idx: blocks[idx]))

    def block_fn(args):
        qb, gb, q0 = args
        tpos = q0 + jnp.arange(Q_BLOCK)
        valid_c = cmp_end[None, :] <= tpos[:, None]
        s_c = jnp.einsum('bghqd,bgnd->bghqn', qb, kc) * scale
        p_c = jax.nn.softmax(jnp.where(valid_c, s_c, NEG_INF), axis=-1) * valid_c.any(axis=-1)[:, None].astype(f32)
        o_c = jnp.einsum('bghqn,bgnd->bghqd', p_c, vc)
        imp = jnp.einsum('bghqn,nj->bgqj', p_c, overlap)
        cur = tpos // SEL_BLOCK
        forced = (sel_ids[None, :] == 0) | (sel_ids[None, :] == cur[:, None]) | (sel_ids[None, :] == cur[:, None] - 1)
        imp = jnp.where(forced, FORCED_SCORE, imp)
        imp = jnp.where(sel_ids[None, :] > cur[:, None], NEG_INF, imp)
        _, sel = lax.top_k(imp, k_eff)

        def slot(carry, bidx):
            m_run, l_run, acc = carry
            kj = gather(ks_blk, bidx)
            vj = gather(vs_blk, bidx)
            mask = (bidx[..., None] * SEL_BLOCK + r <= tpos[:, None])[:, :, None]
            s = jnp.where(mask, jnp.einsum('bghqd,bgqrd->bghqr', qb, kj) * scale, NEG_INF)
            m_new = jnp.maximum(m_run, s.max(axis=-1))
            p = jnp.exp(s - m_new[..., None]) * mask
            corr = jnp.exp(m_run - m_new)
            acc_new = acc * corr[..., None] + jnp.einsum('bghqr,bgqrd->bghqd', p, vj)
            return (m_new, l_run * corr + p.sum(axis=-1), acc_new), None

        init = (jnp.full(qb.shape[:-1], NEG_INF, f32), jnp.zeros(qb.shape[:-1], f32), jnp.zeros(qb.shape, f32))
        (_, l_s, acc_s), _ = lax.scan(slot, init, jnp.moveaxis(sel, -1, 0))
        o_s = acc_s / l_s[..., None]
        kw = lax.dynamic_slice_in_dim(kw_pad, q0, WINDOW + Q_BLOCK, axis=2)
        vw = lax.dynamic_slice_in_dim(vw_pad, q0, WINDOW + Q_BLOCK, axis=2)
        spos = q0 - WINDOW + win_off
        valid_w = (spos[None, :] <= tpos[:, None]) & (spos[None, :] > tpos[:, None] - WINDOW) & (spos[None, :] >= 0)
        s_w = jnp.einsum('bghqd,bgkd->bghqk', qb, kw) * scale
        p_w = jax.nn.softmax(jnp.where(valid_w, s_w, NEG_INF), axis=-1)
        o_w = jnp.einsum('bghqk,bgkd->bghqd', p_w, vw)
        return gb[..., 0:1] * o_c + gb[..., 1:2] * o_s + gb[..., 2:3] * o_w

    o = lax.map(block_fn, (q, gates, starts))
    o = o.transpose(1, 0, 4, 2, 3, 5).reshape(B, S, H * dh).astype(h.dtype)
    return jnp.einsum('bse,ed->bsd', o, w_out)


def hier_moe(h, w_group, b_group, w_router, b_router, w_gate, w_up, w_down):
    f32 = jnp.float32
    lg = jnp.einsum('bsd,dg->bsg', h, w_group).astype(f32) + b_group.astype(f32)
    pg = jax.nn.softmax(lg, axis=-1)
    g_onehot = jax.nn.one_hot(jnp.argmax(lg, axis=-1), MOE_GROUPS, dtype=f32)
    pg_top = jnp.sum(pg * g_onehot, axis=-1)
    le = jnp.einsum('bsd,de->bse', h, w_router).astype(f32) + b_router.astype(f32)
    le = le.reshape(h.shape[0], h.shape[1], MOE_GROUPS, MOE_EXPERTS_PER_GROUP)
    le_g = jnp.einsum('bsge,bsg->bse', le, g_onehot)
    top_v, top_i = lax.top_k(le_g, MOE_TOPK)
    w_sel = jax.nn.softmax(top_v, axis=-1) * pg_top[..., None]
    e_idx = jnp.argmax(g_onehot, axis=-1)[..., None] * MOE_EXPERTS_PER_GROUP + top_i
    gate = jnp.sum(jax.nn.one_hot(e_idx, MOE_EXPERTS, dtype=f32) * w_sel[..., None], axis=-2)

    def per_seq(args):
        xs, gs = args
        a = jnp.einsum('sd,edf->sef', xs, w_gate)
        u = jnp.einsum('sd,edf->sef', xs, w_up)
        hid = jax.nn.silu(a) * u * gs[..., None].astype(xs.dtype)
        return jnp.einsum('sef,efd->sd', hid, w_down)

    return lax.map(per_seq, (h, gate))


def setup_inputs(seed: int = 0) -> dict:
    key = jax.random.key(seed)
    ks = jax.random.split(key, 26)
    f32 = jnp.float32
    D = D_MODEL
    H, dk, dv = MLSTM_HEADS, MLSTM_QK_DIM, MLSTM_V_DIM
    G, dh = NSA_KV_GROUPS, NSA_HEAD_DIM
    E, F = MOE_EXPERTS, MOE_HIDDEN

    def nrm(k, shape, fan_in):
        return jax.random.normal(k, shape, f32) * (fan_in ** -0.5)

    def gain(k, shape):
        return 1.0 + 0.02 * jax.random.normal(k, shape, f32)

    n_in = 2 * H * dk + 2 * H * dv + 2 * H
    gate_base = jnp.concatenate([jnp.zeros((H,), f32), jnp.linspace(3.0, 6.0, H, dtype=f32)])
    return {
        'x': jax.random.normal(ks[0], (BATCH, SEQ, D), f32),
        'mlstm_norm': gain(ks[1], (N_A_LAYERS, D)),
        'mlstm_w_in': nrm(ks[2], (N_A_LAYERS, D, n_in), D),
        'mlstm_gate_bias': gate_base[None, :] + 0.1 * jax.random.normal(ks[3], (N_A_LAYERS, 2 * H), f32),
        'mlstm_head_norm': gain(ks[4], (N_A_LAYERS, H * dv)),
        'mlstm_w_out': nrm(ks[5], (N_A_LAYERS, H * dv, D), H * dv),
        'kv_norm': gain(ks[6], (D,)),
        'kv_w': nrm(ks[7], (D, 6 * G * dh), D),
        'cmp_pe_k': 0.02 * jax.random.normal(ks[8], (CMP_BLOCK, dh), f32),
        'cmp_w1_k': nrm(ks[9], (CMP_BLOCK * dh, dh), CMP_BLOCK * dh),
        'cmp_w2_k': nrm(ks[10], (dh, dh), dh),
        'cmp_pe_v': 0.02 * jax.random.normal(ks[11], (CMP_BLOCK, dh), f32),
        'cmp_w1_v': nrm(ks[12], (CMP_BLOCK * dh, dh), CMP_BLOCK * dh),
        'cmp_w2_v': nrm(ks[13], (dh, dh), dh),
        'nsa_norm': gain(ks[14], (N_B_LAYERS, D)),
        'nsa_w_q': nrm(ks[15], (N_B_LAYERS, D, NSA_HEADS * dh + 3 * NSA_HEADS), D),
        'nsa_w_out': nrm(ks[16], (N_B_LAYERS, NSA_HEADS * dh, D), NSA_HEADS * dh),
        'moe_norm': gain(ks[17], (DEPTH, D)),
        'moe_w_group': nrm(ks[18], (DEPTH, D, MOE_GROUPS), D),
        'moe_b_group': 0.01 * jax.random.normal(ks[19], (DEPTH, MOE_GROUPS), f32),
        'moe_w_router': nrm(ks[20], (DEPTH, D, E), D),
        'moe_b_router': 0.01 * jax.random.normal(ks[21], (DEPTH, E), f32),
        'moe_w_gate': nrm(ks[22], (DEPTH, E, D, F), D),
        'moe_w_up': nrm(ks[23], (DEPTH, E, D, F), D),
        'moe_w_down': nrm(ks[24], (DEPTH, E, F, D), F),
        'final_norm': gain(ks[25], (D,)),
    }


def reference(x, mlstm_norm, mlstm_w_in, mlstm_gate_bias, mlstm_head_norm, mlstm_w_out,
              kv_norm, kv_w, cmp_pe_k, cmp_w1_k, cmp_w2_k, cmp_pe_v, cmp_w1_v, cmp_w2_v,
              nsa_norm, nsa_w_q, nsa_w_out,
              moe_norm, moe_w_group, moe_b_group, moe_w_router, moe_b_router, moe_w_gate, moe_w_up, moe_w_down,
              final_norm):
    h = x
    shared = None
    for layer in range(DEPTH):
        if layer < N_A_LAYERS:
            i = layer
            h = h + mlstm_mixer(rmsnorm(h, mlstm_norm[i]), mlstm_w_in[i], mlstm_gate_bias[i],
                                mlstm_head_norm[i], mlstm_w_out[i])
        else:
            j = layer - N_A_LAYERS
            if j == 0:
                shared = nsa_shared_kv(h, kv_norm, kv_w, cmp_pe_k, cmp_w1_k, cmp_w2_k, cmp_pe_v, cmp_w1_v, cmp_w2_v)
            h = h + nsa_mixer(rmsnorm(h, nsa_norm[j]), nsa_w_q[j], nsa_w_out[j],
                              shared[0], shared[1], shared[2], shared[3], shared[4], shared[5])
        h = h + hier_moe(rmsnorm(h, moe_norm[layer]), moe_w_group[layer], moe_b_group[layer],
                         moe_w_router[layer], moe_b_router[layer], moe_w_gate[layer], moe_w_up[layer],
                         moe_w_down[layer])
    return rmsnorm(h, final_norm)
```

```python
import functools

import numpy as np
import jax
import jax.numpy as jnp
from jax import lax
from jax.experimental import pallas as pl
from jax.experimental.pallas import tpu as pltpu

F32 = jnp.float32
BF16 = jnp.bfloat16
HIGHEST = lax.Precision.HIGHEST

D_MODEL = 1024
SEQ = 2048
RMS_EPS = 1e-6
NEG_INF = -1e30

MLSTM_HEADS = 4
MLSTM_V_DIM = 256
MLSTM_QK_DIM = 128
MLSTM_L = 256
GATE_SOFTCAP = 15.0

NSA_HEADS = 16
NSA_HEAD_DIM = 64
NSA_GROUPS = 4
NSA_GROUP_SIZE = 4
CMP_BLOCK = 32
CMP_STRIDE = 16
N_CMP_PAD = 128
SEL_BLOCK = 64
N_SEL = SEQ // SEL_BLOCK
SEL_TOPK = 16
WINDOW = 512
Q_BLOCK = 128
N_QB = SEQ // Q_BLOCK
WIN_KEYS = WINDOW + Q_BLOCK
SLC_TILE = 256
FORCED_SCORE = 1e6
ROPE_THETA = 500000.0
ROPE_DIM = 16

MOE_GROUPS = 4
MOE_PER_GROUP = 8
MOE_EXPERTS = 32
MOE_HIDDEN = 256
MOE_PAIRS = 28
MOE_CLASSES = MOE_GROUPS * MOE_PAIRS
ROW_TILE = 128
XE_W = D_MODEL + 128
LANE_WLO, LANE_WHI, LANE_CLS, LANE_RANK = 0, 1, 2, 3

TM = 512
VMEM_LIMIT = 56 * 1024 * 1024

_NT = (((1,), (1,)), ((), ()))
_TN = (((0,), (0,)), ((), ()))


def _cparams(*sem):
    return pltpu.CompilerParams(dimension_semantics=sem, vmem_limit_bytes=VMEM_LIMIT)


def _rms(x):
    return x * lax.rsqrt(jnp.mean(x * x, axis=-1, keepdims=True) + RMS_EPS)


def _sigmoid(x):
    return 1.0 / (1.0 + jnp.exp(-x))


def _mlstm_in_kernel(x_ref, g_ref, w_ref, wg_ref, bg_ref, q_ref, k_ref, v_ref, o_ref, gr_ref):
    hn = _rms(x_ref[...]) * g_ref[...]
    hb = hn.astype(BF16)
    q_ref[...] = jnp.dot(hb, w_ref[:, 0:512], preferred_element_type=F32).astype(BF16)
    k = jnp.dot(hb, w_ref[:, 512:1024], preferred_element_type=F32)
    k_ref[...] = (k * (MLSTM_QK_DIM ** -0.5)).astype(BF16)
    v_ref[...] = jnp.dot(hb, w_ref[:, 1024:2048], preferred_element_type=F32).astype(BF16)
    o_ref[...] = _sigmoid(jnp.dot(hb, w_ref[:, 2048:3072], preferred_element_type=F32))
    gates = jnp.dot(hn, wg_ref[...], precision=HIGHEST, preferred_element_type=F32) + bg_ref[...]
    gates = GATE_SOFTCAP * jnp.tanh(gates / GATE_SOFTCAP)
    lane = lax.broadcasted_iota(jnp.int32, gates.shape, 1)
    log_f = jnp.minimum(gates, 0.0) - jnp.log1p(jnp.exp(-jnp.abs(gates)))
    lg = jnp.where(lane < MLSTM_HEADS, gates, log_f)
    gr_ref[...] = lg.T[0:8, :]


def _mlstm_in(x2d, norm_g, w_qkvo, w_gate, b_gate):
    n = x2d.shape[0]
    row = lambda i: (i, 0)
    fixed = lambda i: (0, 0)
    return pl.pallas_call(
        _mlstm_in_kernel,
        grid=(n // TM,),
        in_specs=[pl.BlockSpec((TM, D_MODEL), row),
                  pl.BlockSpec((1, D_MODEL), fixed),
                  pl.BlockSpec((D_MODEL, 3072), fixed),
                  pl.BlockSpec((D_MODEL, 128), fixed),
                  pl.BlockSpec((1, 128), fixed)],
        out_specs=[pl.BlockSpec((TM, 512), row),
                   pl.BlockSpec((TM, 512), row),
                   pl.BlockSpec((TM, 1024), row),
                   pl.BlockSpec((TM, 1024), row),
                   pl.BlockSpec((8, TM), lambda i: (0, i))],
        out_shape=[jax.ShapeDtypeStruct((n, 512), BF16),
                   jax.ShapeDtypeStruct((n, 512), BF16),
                   jax.ShapeDtypeStruct((n, 1024), BF16),
                   jax.ShapeDtypeStruct((n, 1024), F32),
                   jax.ShapeDtypeStruct((8, n), F32)],
        compiler_params=_cparams("parallel"),
        name="mlstm_in",
    )(x2d, norm_g, w_qkvo, w_gate, b_gate)


def _mlstm_scan_kernel(q_ref, k_ref, v_ref, o_ref, gr_ref, hn_ref, out_ref, c_ref, n_ref, m_ref):
    L = MLSTM_L

    @pl.when(pl.program_id(1) == 0)
    def _():
        c_ref[...] = jnp.zeros_like(c_ref)
        n_ref[...] = jnp.zeros_like(n_ref)
        m_ref[...] = jnp.zeros_like(m_ref)

    row = lax.broadcasted_iota(jnp.int32, (L, L), 0)
    col = lax.broadcasted_iota(jnp.int32, (L, L), 1)
    causal = col <= row
    tril = causal.astype(F32)
    eye = (col == row).astype(F32)
    gr = gr_ref[...]
    gr_pad = jnp.concatenate([gr, jnp.zeros((120, L), F32)], axis=0)
    b_row = lax.dot_general(gr, tril, _NT, precision=HIGHEST, preferred_element_type=F32)
    b_col = lax.dot_general(tril, gr_pad, _NT, precision=HIGHEST, preferred_element_type=F32)
    g_col = lax.dot_general(eye, gr_pad, _NT, precision=HIGHEST, preferred_element_type=F32)

    for h in range(MLSTM_HEADS):
        li_row = gr[h:h + 1, :]
        li_col = g_col[:, h:h + 1]
        bf_row = b_row[4 + h:5 + h, :]
        bf_col = b_col[:, 4 + h:5 + h]
        m = m_ref[h:h + 1, 0:1]
        dmat = jnp.where(causal, bf_col - bf_row + li_row, NEG_INF)
        m_inter = bf_col + m
        m_t = jnp.maximum(m_inter, jnp.max(dmat, axis=-1, keepdims=True))
        w_intra = jnp.exp(dmat - m_t)
        w_inter = jnp.exp(m_inter - m_t)
        qh = q_ref[:, h * 128:(h + 1) * 128]
        kh = k_ref[:, h * 128:(h + 1) * 128]
        vh = v_ref[:, h * 256:(h + 1) * 256]
        s = lax.dot_general(qh, kh, _NT, preferred_element_type=F32) * w_intra
        c_old = c_ref[h]
        num = (jnp.dot(s.astype(BF16), vh, preferred_element_type=F32)
               + w_inter * jnp.dot(qh, c_old.astype(BF16), preferred_element_type=F32))
        n_old = n_ref[h:h + 1, :]
        qn = jnp.sum(qh.astype(F32) * n_old, axis=-1, keepdims=True)
        den = jnp.sum(s, axis=-1, keepdims=True) + w_inter * qn
        hh = num / jnp.maximum(jnp.abs(den), jnp.exp(-m_t))
        b_end = bf_col[L - 1:L, :]
        g = b_end - bf_col + li_col
        m_new = jnp.maximum(b_end + m, jnp.max(g, axis=0, keepdims=True))
        ws = jnp.exp(g - m_new)
        decay = jnp.exp(b_end + m - m_new)
        kf = kh.astype(F32) * ws
        c_ref[h] = decay * c_old + lax.dot_general(kf.astype(BF16), vh, _TN, preferred_element_type=F32)
        n_ref[h:h + 1, :] = decay * n_old + jnp.sum(kf, axis=0, keepdims=True)
        m_ref[h:h + 1, :] = jnp.broadcast_to(m_new, (1, 128))
        sl = slice(h * 256, (h + 1) * 256)
        out_ref[:, sl] = (_rms(hh) * hn_ref[:, sl] * o_ref[:, sl]).astype(BF16)


def _mlstm_scan(q, k, v, o, gr, head_norm, batch):
    n = q.shape[0]
    nblk = SEQ // MLSTM_L
    row = lambda b, j: (b * nblk + j, 0)
    return pl.pallas_call(
        _mlstm_scan_kernel,
        grid=(batch, nblk),
        in_specs=[pl.BlockSpec((MLSTM_L, 512), row),
                  pl.BlockSpec((MLSTM_L, 512), row),
                  pl.BlockSpec((MLSTM_L, 1024), row),
                  pl.BlockSpec((MLSTM_L, 1024), row),
                  pl.BlockSpec((8, MLSTM_L), lambda b, j: (0, b * nblk + j)),
                  pl.BlockSpec((1, 1024), lambda b, j: (0, 0))],
        out_specs=pl.BlockSpec((MLSTM_L, 1024), row),
        out_shape=jax.ShapeDtypeStruct((n, 1024), BF16),
        scratch_shapes=[pltpu.VMEM((MLSTM_HEADS, MLSTM_QK_DIM, MLSTM_V_DIM), F32),
                        pltpu.VMEM((8, 128), F32),
                        pltpu.VMEM((8, 128), F32)],
        compiler_params=_cparams("parallel", "arbitrary"),
        name="mlstm_scan",
    )(q, k, v, o, gr, head_norm)


def _pair_tables():
    lo, hi = [], []
    for g in range(MOE_GROUPS):
        for a in range(MOE_PER_GROUP):
            for b in range(a + 1, MOE_PER_GROUP):
                lo.append(g * MOE_PER_GROUP + a)
                hi.append(g * MOE_PER_GROUP + b)
    return np.asarray(lo, np.int32), np.asarray(hi, np.int32)


_PAIR_LO, _PAIR_HI = _pair_tables()


def _mix_out_kernel(a_ref, w_ref, res_ref, g_ref, wr_ref, br_ref, tril_ref, h_ref, xe_ref, cnt_ref, run_ref):
    @pl.when(pl.program_id(0) == 0)
    def _():
        run_ref[...] = jnp.zeros_like(run_ref)

    h = res_ref[...] + jnp.dot(a_ref[...], w_ref[...], preferred_element_type=F32)
    h_ref[...] = h
    hn = _rms(h) * g_ref[...]
    xe_ref[:, 0:D_MODEL] = hn

    logits = jnp.dot(hn, wr_ref[...], precision=HIGHEST, preferred_element_type=F32) + br_ref[...]
    lane_i = lax.broadcasted_iota(jnp.int32, logits.shape, 1)
    lane = lane_i.astype(F32)
    ninf = -jnp.inf
    is_g = (lane_i >= MOE_EXPERTS) & (lane_i < MOE_EXPERTS + MOE_GROUPS)
    glog = jnp.where(is_g, logits, ninf)
    gmax = jnp.max(glog, axis=-1, keepdims=True)
    gidx = jnp.min(jnp.where(glog == gmax, lane - MOE_EXPERTS, 99.0), axis=-1, keepdims=True)
    pg_top = 1.0 / jnp.sum(jnp.exp(glog - gmax), axis=-1, keepdims=True)
    lane_grp = (lane_i >> 3).astype(F32)
    in_grp = (lane_i < MOE_EXPERTS) & (lane_grp == gidx)
    ev = jnp.where(in_grp, logits, ninf)
    v1 = jnp.max(ev, axis=-1, keepdims=True)
    i1 = jnp.min(jnp.where(ev == v1, lane, 999.0), axis=-1, keepdims=True)
    ev2 = jnp.where(lane == i1, ninf, ev)
    v2 = jnp.max(ev2, axis=-1, keepdims=True)
    i2 = jnp.min(jnp.where(ev2 == v2, lane, 999.0), axis=-1, keepdims=True)
    e2 = jnp.exp(v2 - v1)
    w1 = pg_top / (1.0 + e2)
    w2 = pg_top * e2 / (1.0 + e2)
    first_lo = i1 < i2
    w_lo = jnp.where(first_lo, w1, w2)
    w_hi = jnp.where(first_lo, w2, w1)
    a = jnp.minimum(i1, i2) - MOE_PER_GROUP * gidx
    b = jnp.maximum(i1, i2) - MOE_PER_GROUP * gidx
    cls = gidx * MOE_PAIRS + a * (15.0 - a) * 0.5 + (b - a - 1.0)

    onehot = lane == cls
    prefix = jnp.dot(tril_ref[...], onehot.astype(BF16), preferred_element_type=F32)
    run = run_ref[0:1, :]
    rank = jnp.sum(jnp.where(onehot, prefix - 1.0 + run, 0.0), axis=-1, keepdims=True)
    run_new = run + prefix[TM - 1:TM, :]
    run_ref[...] = jnp.broadcast_to(run_new, run_ref.shape)
    cnt_ref[...] = jnp.broadcast_to(run_new, cnt_ref.shape)

    meta = jnp.where(lane_i == LANE_WLO, w_lo,
                     jnp.where(lane_i == LANE_WHI, w_hi,
                               jnp.where(lane_i == LANE_CLS, cls,
                                         jnp.where(lane_i == LANE_RANK, rank, 0.0))))
    xe_ref[:, D_MODEL:XE_W] = meta


def _mix_out(a, w, res, g_moe, w_rt, b_rt, tril):
    n, kdim = a.shape
    row = lambda i: (i, 0)
    fixed = lambda i: (0, 0)
    return pl.pallas_call(
        _mix_out_kernel,
        grid=(n // TM,),
        in_specs=[pl.BlockSpec((TM, kdim), row),
                  pl.BlockSpec((kdim, D_MODEL), fixed),
                  pl.BlockSpec((TM, D_MODEL), row),
                  pl.BlockSpec((1, D_MODEL), fixed),
                  pl.BlockSpec((D_MODEL, 128), fixed),
                  pl.BlockSpec((1, 128), fixed),
                  pl.BlockSpec((TM, TM), fixed)],
        out_specs=[pl.BlockSpec((TM, D_MODEL), row),
                   pl.BlockSpec((TM, XE_W), row),
                   pl.BlockSpec((8, 128), fixed)],
        out_shape=[jax.ShapeDtypeStruct((n, D_MODEL), F32),
                   jax.ShapeDtypeStruct((n, XE_W), F32),
                   jax.ShapeDtypeStruct((8, 128), F32)],
        scratch_shapes=[pltpu.VMEM((8, 128), F32)],
        compiler_params=_cparams("arbitrary"),
        name="mix_out",
    )(a, w, res, g_moe, w_rt, b_rt, tril)


DISPATCH_ROWS = 1024


def _dispatch_kernel(pos_ref, xe_ref, zero_ref, xs_ref, sem):
    del zero_ref
    base = pl.program_id(0) * DISPATCH_ROWS

    def issue(t, carry):
        p = pos_ref[base + t]
        pltpu.make_async_copy(xe_ref.at[pl.ds(base + t, 1)], xs_ref.at[pl.ds(p, 1)], sem).start()
        return carry

    lax.fori_loop(0, DISPATCH_ROWS, issue, 0)
    pltpu.make_async_copy(xe_ref.at[pl.ds(0, DISPATCH_ROWS)], xs_ref.at[pl.ds(0, DISPATCH_ROWS)], sem).wait()


def _dispatch(pos, xe, n_rows):
    n = xe.shape[0]
    zeros = jnp.zeros((n_rows, XE_W), F32)
    grid_spec = pltpu.PrefetchScalarGridSpec(
        num_scalar_prefetch=1,
        grid=(n // DISPATCH_ROWS,),
        in_specs=[pl.BlockSpec(memory_space=pl.ANY), pl.BlockSpec(memory_space=pl.ANY)],
        out_specs=pl.BlockSpec(memory_space=pl.ANY),
        scratch_shapes=[pltpu.SemaphoreType.DMA(())],
    )
    return pl.pallas_call(
        _dispatch_kernel,
        grid_spec=grid_spec,
        out_shape=jax.ShapeDtypeStruct((n_rows, XE_W), F32),
        input_output_aliases={2: 0},
        compiler_params=pltpu.CompilerParams(dimension_semantics=("arbitrary",), has_side_effects=True),
        name="moe_dispatch",
    )(pos, xe, zeros)


def _experts_kernel(tlo_ref, thi_ref, nused_ref, xs_ref, wg_lo, wu_lo, wd_lo, wg_hi, wu_hi, wd_hi, y_ref):
    del tlo_ref, thi_ref

    @pl.when(pl.program_id(0) < nused_ref[0])
    def _():
        x = xs_ref[:, 0:D_MODEL].astype(BF16)

        def ffn(wg, wu, wd, w):
            a = jnp.dot(x, wg[0], preferred_element_type=F32)
            u = jnp.dot(x, wu[0], preferred_element_type=F32)
            hid = (a * _sigmoid(a)) * u * w
            return jnp.dot(hid.astype(BF16), wd[0], preferred_element_type=F32)

        w_lo = xs_ref[:, D_MODEL + LANE_WLO:D_MODEL + LANE_WLO + 1]
        w_hi = xs_ref[:, D_MODEL + LANE_WHI:D_MODEL + LANE_WHI + 1]
        y_ref[...] = ffn(wg_lo, wu_lo, wd_lo, w_lo) + ffn(wg_hi, wu_hi, wd_hi, w_hi)

    @pl.when(pl.program_id(0) >= nused_ref[0])
    def _():
        y_ref[...] = jnp.zeros_like(y_ref)


def _experts(tile_lo, tile_hi, n_used, xs, w_gate, w_up, w_down):
    n_tiles = xs.shape[0] // ROW_TILE
    rows = lambda i, tlo, thi, nu: (jnp.minimum(i, nu[0] - 1), 0)
    lo = lambda i, tlo, thi, nu: (tlo[i], 0, 0)
    hi = lambda i, tlo, thi, nu: (thi[i], 0, 0)
    up_spec = lambda m: pl.BlockSpec((1, D_MODEL, MOE_HIDDEN), m)
    dn_spec = lambda m: pl.BlockSpec((1, MOE_HIDDEN, D_MODEL), m)
    grid_spec = pltpu.PrefetchScalarGridSpec(
        num_scalar_prefetch=3,
        grid=(n_tiles,),
        in_specs=[pl.BlockSpec((ROW_TILE, XE_W), rows),
                  up_spec(lo), up_spec(lo), dn_spec(lo),
                  up_spec(hi), up_spec(hi), dn_spec(hi)],
        out_specs=pl.BlockSpec((ROW_TILE, D_MODEL), lambda i, tlo, thi, nu: (i, 0)),
    )
    return pl.pallas_call(
        _experts_kernel,
        grid_spec=grid_spec,
        out_shape=jax.ShapeDtypeStruct((xs.shape[0], D_MODEL), F32),
        compiler_params=_cparams("arbitrary"),
        name="moe_experts",
    )(tile_lo, tile_hi, n_used, xs, w_gate, w_up, w_down, w_gate, w_up, w_down)


def _combine_kernel(pos_ref, h_ref, y_ref, g_ref, out_ref, buf, sem, *, final_norm):
    base = pl.program_id(0) * TM

    def issue(t, carry):
        p = pos_ref[base + t]
        pltpu.make_async_copy(y_ref.at[pl.ds(p, 1)], buf.at[pl.ds(t, 1)], sem).start()
        return carry

    lax.fori_loop(0, TM, issue, 0)
    pltpu.make_async_copy(y_ref.at[pl.ds(0, TM)], buf, sem).wait()
    out = h_ref[...] + buf[...]
    if final_norm:
        out = _rms(out) * g_ref[...]
    out_ref[...] = out


def _combine(pos, h, y, gain, final_norm):
    n = h.shape[0]
    grid_spec = pltpu.PrefetchScalarGridSpec(
        num_scalar_prefetch=1,
        grid=(n // TM,),
        in_specs=[pl.BlockSpec((TM, D_MODEL), lambda i, pos: (i, 0)),
                  pl.BlockSpec(memory_space=pl.ANY),
                  pl.BlockSpec((1, D_MODEL), lambda i, pos: (0, 0))],
        out_specs=pl.BlockSpec((TM, D_MODEL), lambda i, pos: (i, 0)),
        scratch_shapes=[pltpu.VMEM((TM, D_MODEL), F32), pltpu.SemaphoreType.DMA(())],
    )
    return pl.pallas_call(
        functools.partial(_combine_kernel, final_norm=final_norm),
        grid_spec=grid_spec,
        out_shape=jax.ShapeDtypeStruct((n, D_MODEL), F32),
        compiler_params=_cparams("arbitrary"),
        name="moe_combine",
    )(pos, h, y, gain)


def _moe_layer(a, w_out, res, moe_norm, w_group, b_group, w_router, b_router, w_gate, w_up, w_down,
               tril, out_gain, final_norm):
    n = a.shape[0]
    w_rt = jnp.zeros((D_MODEL, 128), F32).at[:, 0:MOE_EXPERTS].set(w_router)
    w_rt = w_rt.at[:, MOE_EXPERTS:MOE_EXPERTS + MOE_GROUPS].set(w_group)
    b_rt = jnp.zeros((1, 128), F32).at[0, 0:MOE_EXPERTS].set(b_router)
    b_rt = b_rt.at[0, MOE_EXPERTS:MOE_EXPERTS + MOE_GROUPS].set(b_group)
    h, xe, counts = _mix_out(a, w_out, res, moe_norm.reshape(1, D_MODEL), w_rt, b_rt, tril)

    n_tiles = n // ROW_TILE + MOE_CLASSES
    cnt = counts[0, 0:MOE_CLASSES].astype(jnp.int32)
    tiles_c = (cnt + ROW_TILE - 1) // ROW_TILE
    tile_end = jnp.cumsum(tiles_c)
    offs = (tile_end - tiles_c) * ROW_TILE
    n_used = tile_end[-1]
    cls = xe[:, D_MODEL + LANE_CLS].astype(jnp.int32)
    rank = xe[:, D_MODEL + LANE_RANK].astype(jnp.int32)
    pos = offs[cls] + rank
    tile_ids = jnp.minimum(jnp.arange(n_tiles, dtype=jnp.int32), n_used - 1)
    tile_cls = jnp.sum((tile_end[None, :] <= tile_ids[:, None]).astype(jnp.int32), axis=1)
    tile_lo = jnp.asarray(_PAIR_LO)[tile_cls]
    tile_hi = jnp.asarray(_PAIR_HI)[tile_cls]

    xs = _dispatch(pos, xe, n_tiles * ROW_TILE)
    y = _experts(tile_lo, tile_hi, n_used.reshape(1), xs, w_gate.astype(BF16), w_up.astype(BF16),
                 w_down.astype(BF16))
    return _combine(pos, h, y, out_gain.reshape(1, D_MODEL), final_norm)


def _rope_tables(pos, width):
    half = ROPE_DIM // 2
    inv_freq = jnp.power(jnp.float32(ROPE_THETA), -jnp.arange(half, dtype=F32) * (2.0 / ROPE_DIM))
    ang = pos.astype(F32)[:, None] * inv_freq[None, :]
    cos, sin = jnp.cos(ang), jnp.sin(ang)
    t = pos.shape[0]
    rest = NSA_HEAD_DIM - ROPE_DIM
    cos_t = jnp.concatenate([cos, cos, jnp.ones((t, rest), F32)], axis=1)
    sin_a = jnp.concatenate([-sin, jnp.zeros((t, half + rest), F32)], axis=1)
    sin_b = jnp.concatenate([jnp.zeros((t, half), F32), sin, jnp.zeros((t, rest), F32)], axis=1)
    rep = width // NSA_HEAD_DIM
    return jnp.tile(cos_t, (1, rep)), jnp.tile(sin_a, (1, rep)), jnp.tile(sin_b, (1, rep))


def _rope(x, cos_t, sin_a, sin_b):
    half = ROPE_DIM // 2
    parts = []
    for c in range(x.shape[1] // 128):
        xc = x[:, c * 128:(c + 1) * 128]
        parts.append(xc * cos_t + pltpu.roll(xc, 128 - half, axis=1) * sin_a + pltpu.roll(xc, half, axis=1) * sin_b)
    return parts[0] if len(parts) == 1 else jnp.concatenate(parts, axis=1)


def _nsa_proj_kernel(h_ref, gq_ref, gkv_ref, wq_ref, wkv_ref, cos_ref, sa_ref, sb_ref,
                     q_ref, gate_ref, kc0_ref, kc1_ref, vc0_ref, vc1_ref, ks_ref, vs_ref, kw_ref, vw_ref):
    r = _rms(h_ref[...])
    hq = (r * gq_ref[...]).astype(BF16)
    hk = (r * gkv_ref[...]).astype(BF16)
    cos_t, sin_a, sin_b = cos_ref[...], sa_ref[...], sb_ref[...]
    q = jnp.dot(hq, wq_ref[:, 0:1024], preferred_element_type=F32)
    q_ref[...] = (_rope(q, cos_t, sin_a, sin_b) * (NSA_HEAD_DIM ** -0.5)).astype(BF16)
    gate_ref[...] = _sigmoid(jnp.dot(hq, wq_ref[:, 1024:1152], preferred_element_type=F32))
    kc0_ref[...] = jnp.dot(hk, wkv_ref[:, 0:128], preferred_element_type=F32)
    kc1_ref[...] = jnp.dot(hk, wkv_ref[:, 128:256], preferred_element_type=F32)
    vc0_ref[...] = jnp.dot(hk, wkv_ref[:, 256:384], preferred_element_type=F32)
    vc1_ref[...] = jnp.dot(hk, wkv_ref[:, 384:512], preferred_element_type=F32)

    def store_groups(ref, val):
        for g in range(NSA_GROUPS):
            ref[0, g] = val[:, g * 64:(g + 1) * 64].astype(BF16)

    store_groups(ks_ref, _rope(jnp.dot(hk, wkv_ref[:, 512:768], preferred_element_type=F32), cos_t, sin_a, sin_b))
    store_groups(vs_ref, jnp.dot(hk, wkv_ref[:, 768:1024], preferred_element_type=F32))
    store_groups(kw_ref, _rope(jnp.dot(hk, wkv_ref[:, 1024:1280], preferred_element_type=F32), cos_t, sin_a, sin_b))
    store_groups(vw_ref, jnp.dot(hk, wkv_ref[:, 1280:1536], preferred_element_type=F32))


def _nsa_proj(h, g_q, g_kv, w_q, w_kv, rope_tabs, batch):
    n = h.shape[0]
    nblk = SEQ // TM
    row = lambda i: (i, 0)
    fixed = lambda i: (0, 0)
    tab = lambda i: (i % nblk, 0)
    grp = lambda i: (i // nblk, 0, i % nblk, 0)
    grp_spec = pl.BlockSpec((1, NSA_GROUPS, TM, NSA_HEAD_DIM), grp)
    grp_shape = jax.ShapeDtypeStruct((batch, NSA_GROUPS, SEQ, NSA_HEAD_DIM), BF16)
    return pl.pallas_call(
        _nsa_proj_kernel,
        grid=(n // TM,),
        in_specs=[pl.BlockSpec((TM, D_MODEL), row),
                  pl.BlockSpec((1, D_MODEL), fixed),
                  pl.BlockSpec((1, D_MODEL), fixed),
                  pl.BlockSpec((D_MODEL, 1152), fixed),
                  pl.BlockSpec((D_MODEL, 1536), fixed),
                  pl.BlockSpec((TM, 128), tab),
                  pl.BlockSpec((TM, 128), tab),
                  pl.BlockSpec((TM, 128), tab)],
        out_specs=[pl.BlockSpec((TM, 1024), row),
                   pl.BlockSpec((TM, 128), row),
                   pl.BlockSpec((TM, 128), row), pl.BlockSpec((TM, 128), row),
                   pl.BlockSpec((TM, 128), row), pl.BlockSpec((TM, 128), row),
                   grp_spec, grp_spec, grp_spec, grp_spec],
        out_shape=[jax.ShapeDtypeStruct((n, 1024), BF16),
                   jax.ShapeDtypeStruct((n, 128), F32),
                   jax.ShapeDtypeStruct((n, 128), F32), jax.ShapeDtypeStruct((n, 128), F32),
                   jax.ShapeDtypeStruct((n, 128), F32), jax.ShapeDtypeStruct((n, 128), F32),
                   grp_shape, grp_shape, grp_shape, grp_shape],
        compiler_params=_cparams("parallel"),
        name="nsa_proj",
    )(h, g_q, g_kv, w_q, w_kv, *rope_tabs)


HALF_BLOCKS = SEQ // CMP_STRIDE
CMP_K = CMP_STRIDE * 256


def _compress_kernel(rk0_ref, rk1_ref, rv0_ref, rv1_ref, w1k_ref, w1v_ref, pek_ref, pev_ref, w2k_ref, w2v_ref,
                     cos_ref, sa_ref, sb_ref, kc_ref, vc_ref):
    lane = lax.broadcasted_iota(jnp.int32, (HALF_BLOCKS, 512), 1)
    first_half = (lane & 127) < 64

    def comp(raw_refs, w1_ref, pe_ref, w2_ref):
        x = jnp.concatenate([r[pl.ds(l, HALF_BLOCKS, stride=CMP_STRIDE), :]
                             for l in range(CMP_STRIDE) for r in raw_refs], axis=1).astype(BF16)
        r = jnp.dot(x, w1_ref[...], preferred_element_type=F32)
        rpe = jnp.dot(pe_ref[...].astype(BF16), w1_ref[...], preferred_element_type=F32)
        r = r + jnp.where(first_half, rpe[0:1, :], rpe[1:2, :])
        nxt = pltpu.roll(r, HALF_BLOCKS - 1, axis=0)
        nxt = jnp.concatenate([pltpu.roll(nxt[:, c * 128:(c + 1) * 128], 64, axis=1) for c in range(4)], axis=1)
        pre = r + nxt
        act = pre * _sigmoid(pre)
        return jnp.dot(act.astype(BF16), w2_ref[...], preferred_element_type=F32)

    kc = comp((rk0_ref, rk1_ref), w1k_ref, pek_ref, w2k_ref)
    kc = _rope(kc, cos_ref[...], sa_ref[...], sb_ref[...])
    vc = comp((rv0_ref, rv1_ref), w1v_ref, pev_ref, w2v_ref)
    for g in range(NSA_GROUPS):
        kc_ref[0, g] = kc[:, g * 64:(g + 1) * 64]
        vc_ref[0, g] = vc[:, g * 64:(g + 1) * 64]


def _compress_weights(pe, w1, w2):
    dh = NSA_HEAD_DIM
    w1r = w1.reshape(2, CMP_STRIDE, dh, dh)
    big = jnp.zeros((CMP_STRIDE, NSA_GROUPS, dh, NSA_GROUPS, 2, dh), F32)
    for g in range(NSA_GROUPS):
        big = big.at[:, g, :, g, 0, :].set(w1r[0])
        big = big.at[:, g, :, g, 1, :].set(w1r[1])
    w1_big = big.reshape(CMP_K, 512).astype(BF16)
    pe_rows = jnp.zeros((8, CMP_STRIDE, NSA_GROUPS, dh), F32)
    pe_rows = pe_rows.at[0].set(jnp.broadcast_to(pe[0:CMP_STRIDE, None, :], (CMP_STRIDE, NSA_GROUPS, dh)))
    pe_rows = pe_rows.at[1].set(jnp.broadcast_to(pe[CMP_STRIDE:, None, :], (CMP_STRIDE, NSA_GROUPS, dh)))
    w2_bd = jnp.zeros((NSA_GROUPS, 2, dh, NSA_GROUPS, dh), F32)
    for g in range(NSA_GROUPS):
        w2_bd = w2_bd.at[g, 0, :, g, :].set(w2)
    return w1_big, pe_rows.reshape(8, CMP_K), w2_bd.reshape(512, 256).astype(BF16)


def _compress(raw_k, raw_v, wk, wv, cmp_tabs, batch):
    fixed = lambda b: (0, 0)
    raw_spec = pl.BlockSpec((SEQ, 128), lambda b: (b, 0))
    out_spec = pl.BlockSpec((1, NSA_GROUPS, N_CMP_PAD, NSA_HEAD_DIM), lambda b: (b, 0, 0, 0))
    out_shape = jax.ShapeDtypeStruct((batch, NSA_GROUPS, N_CMP_PAD, NSA_HEAD_DIM), F32)
    return pl.pallas_call(
        _compress_kernel,
        grid=(batch,),
        in_specs=[raw_spec, raw_spec, raw_spec, raw_spec,
                  pl.BlockSpec((CMP_K, 512), fixed), pl.BlockSpec((CMP_K, 512), fixed),
                  pl.BlockSpec((8, CMP_K), fixed), pl.BlockSpec((8, CMP_K), fixed),
                  pl.BlockSpec((512, 256), fixed), pl.BlockSpec((512, 256), fixed),
                  pl.BlockSpec((N_CMP_PAD, 128), fixed), pl.BlockSpec((N_CMP_PAD, 128), fixed),
                  pl.BlockSpec((N_CMP_PAD, 128), fixed)],
        out_specs=[out_spec, out_spec],
        out_shape=[out_shape, out_shape],
        compiler_params=_cparams("parallel"),
        name="nsa_compress",
    )(*raw_k, *raw_v, wk[0], wv[0], wk[1], wv[1], wk[2], wv[2], *cmp_tabs)


def _nsa_attn_kernel(q_ref, gate_ref, kc_ref, vc_ref, ks_ref, vs_ref, kw_ref, vw_ref, ovl_ref, out_ref):
    g = pl.program_id(1)
    qb = pl.program_id(2)
    q0 = qb * Q_BLOCK
    Q, HG, DH = Q_BLOCK, NSA_GROUP_SIZE, NSA_HEAD_DIM
    R = HG * Q

    q4 = q_ref[...]
    q_all = jnp.concatenate([q4[:, h * DH:(h + 1) * DH] for h in range(HG)], axis=0)

    def stack(x):
        return jnp.concatenate([x] * HG, axis=0)

    kc = kc_ref[0, 0]
    vc = vc_ref[0, 0]
    t_col = q0 + lax.broadcasted_iota(jnp.int32, (Q, N_CMP_PAD), 0)
    n_lane = lax.broadcasted_iota(jnp.int32, (Q, N_CMP_PAD), 1)
    valid_c = stack((n_lane * CMP_STRIDE + CMP_BLOCK - 1 <= t_col) & (n_lane < N_CMP_PAD - 1))
    s_c = lax.dot_general(q_all.astype(F32), kc, _NT, precision=HIGHEST, preferred_element_type=F32)
    s_c = jnp.where(valid_c, s_c, NEG_INF)
    m_c = jnp.max(s_c, axis=-1, keepdims=True)
    e_c = jnp.where(valid_c, jnp.exp(s_c - m_c), 0.0)
    l_c = jnp.sum(e_c, axis=-1, keepdims=True)
    p_c = e_c * jnp.where(l_c > 0.0, 1.0 / l_c, 0.0)
    o_c = jnp.dot(p_c.astype(BF16), vc.astype(BF16), preferred_element_type=F32)

    p_sum = p_c[0:Q] + p_c[Q:2 * Q] + p_c[2 * Q:3 * Q] + p_c[3 * Q:4 * Q]
    imp_t = lax.dot_general(ovl_ref[...], p_sum, _NT, precision=HIGHEST, preferred_element_type=F32)[0:N_SEL]
    j_row = lax.broadcasted_iota(jnp.int32, (N_SEL, Q), 0)
    cur = (q0 + lax.broadcasted_iota(jnp.int32, (N_SEL, Q), 1)) >> 6
    forced = (j_row == 0) | (j_row == cur) | (j_row == cur - 1)
    imp_t = jnp.where(forced, FORCED_SCORE, imp_t)
    imp_t = jnp.where(j_row > cur, NEG_INF, imp_t)
    cnt = jnp.zeros((N_SEL, Q), F32)
    for i in range(N_SEL):
        ri = imp_t[i:i + 1, :]
        ahead = (ri > imp_t) | ((ri == imp_t) & (j_row > i))
        cnt = cnt + jnp.where(ahead, 1.0, 0.0)
    sel_t = jnp.where((cnt < SEL_TOPK) & (j_row <= cur), 1.0, 0.0)
    sel = jnp.concatenate([sel_t, jnp.zeros((128 - N_SEL, Q), F32)], axis=0).T.astype(BF16)

    t_key = q0 + lax.broadcasted_iota(jnp.int32, (Q, SLC_TILE), 0)
    c_key = lax.broadcasted_iota(jnp.int32, (Q, SLC_TILE), 1)
    j_exp = lax.broadcasted_iota(jnp.int32, (128, SLC_TILE), 0)
    c_exp = lax.broadcasted_iota(jnp.int32, (128, SLC_TILE), 1) >> 6

    def slc_step(kt, carry):
        m_run, l_run, acc = carry
        start = pl.multiple_of(kt * SLC_TILE, SLC_TILE)
        ks = ks_ref[0, 0, pl.ds(start, SLC_TILE), :]
        vs = vs_ref[0, 0, pl.ds(start, SLC_TILE), :]
        expand = jnp.where(j_exp == kt * (SLC_TILE // SEL_BLOCK) + c_exp, 1.0, 0.0).astype(BF16)
        picked = jnp.dot(sel, expand, preferred_element_type=F32) > 0.5
        mask = stack(picked & (start + c_key <= t_key))
        s = lax.dot_general(q_all, ks, _NT, preferred_element_type=F32)
        s = jnp.where(mask, s, NEG_INF)
        m_new = jnp.maximum(m_run, jnp.max(s, axis=-1, keepdims=True))
        p = jnp.where(mask, jnp.exp(s - m_new), 0.0)
        corr = jnp.exp(m_run - m_new)
        acc = acc * corr + jnp.dot(p.astype(BF16), vs, preferred_element_type=F32)
        return m_new, l_run * corr + jnp.sum(p, axis=-1, keepdims=True), acc

    n_kt = (q0 + Q + SLC_TILE - 1) >> 8
    init = (jnp.full((R, 1), NEG_INF, F32), jnp.zeros((R, 1), F32), jnp.zeros((R, DH), F32))
    _, l_s, acc_s = lax.fori_loop(0, n_kt, slc_step, init)
    o_s = acc_s / l_s

    w0 = pl.multiple_of(jnp.maximum(q0 - WINDOW, 0), Q_BLOCK)
    kw = kw_ref[0, 0, pl.ds(w0, WIN_KEYS), :]
    vw = vw_ref[0, 0, pl.ds(w0, WIN_KEYS), :]
    t_w = q0 + lax.broadcasted_iota(jnp.int32, (Q, WIN_KEYS), 0)
    s_pos = w0 + lax.broadcasted_iota(jnp.int32, (Q, WIN_KEYS), 1)
    valid_w = stack((s_pos <= t_w) & (s_pos > t_w - WINDOW))
    s_w = lax.dot_general(q_all, kw, _NT, preferred_element_type=F32)
    s_w = jnp.where(valid_w, s_w, NEG_INF)
    m_w = jnp.max(s_w, axis=-1, keepdims=True)
    e_w = jnp.where(valid_w, jnp.exp(s_w - m_w), 0.0)
    o_w = jnp.dot(e_w.astype(BF16), vw, preferred_element_type=F32) / jnp.sum(e_w, axis=-1, keepdims=True)

    gates = gate_ref[...]
    g_lane = lax.broadcasted_iota(jnp.int32, gates.shape, 1)

    def gate_col(branch):
        cols = []
        for h in range(HG):
            idx = (g * HG + h) * 3 + branch
            cols.append(jnp.sum(jnp.where(g_lane == idx, gates, 0.0), axis=-1, keepdims=True))
        return jnp.concatenate(cols, axis=0)

    o = gate_col(0) * o_c + gate_col(1) * o_s + gate_col(2) * o_w
    out_ref[...] = jnp.concatenate([o[h * Q:(h + 1) * Q] for h in range(HG)], axis=1).astype(BF16)


def _overlap_t():
    n = np.arange(N_CMP_PAD)
    j = np.arange(128)
    cmp_start = n * CMP_STRIDE
    cmp_end = cmp_start + CMP_BLOCK - 1
    sel_start = j * SEL_BLOCK
    ovl = ((cmp_start[None, :] <= sel_start[:, None] + SEL_BLOCK - 1) & (cmp_end[None, :] >= sel_start[:, None])
           & (j[:, None] < N_SEL) & (n[None, :] < N_CMP_PAD - 1))
    return jnp.asarray(ovl.astype(np.float32))


def _nsa_attn(q, gates, kc, vc, ks, vs, kw, vw, batch):
    n = q.shape[0]
    qrow = lambda b, g, i: (b * N_QB + i, g)
    grow = lambda b, g, i: (b * N_QB + i, 0)
    kv = lambda b, g, i: (b, g, 0, 0)
    cmp_spec = pl.BlockSpec((1, 1, N_CMP_PAD, NSA_HEAD_DIM), kv)
    seq_spec = pl.BlockSpec((1, 1, SEQ, NSA_HEAD_DIM), kv)
    return pl.pallas_call(
        _nsa_attn_kernel,
        grid=(batch, NSA_GROUPS, N_QB),
        in_specs=[pl.BlockSpec((Q_BLOCK, 256), qrow),
                  pl.BlockSpec((Q_BLOCK, 128), grow),
                  cmp_spec, cmp_spec, seq_spec, seq_spec, seq_spec, seq_spec,
                  pl.BlockSpec((128, N_CMP_PAD), lambda b, g, i: (0, 0))],
        out_specs=pl.BlockSpec((Q_BLOCK, 256), qrow),
        out_shape=jax.ShapeDtypeStruct((n, 1024), BF16),
        compiler_params=_cparams("parallel", "parallel", "arbitrary"),
        name="nsa_attn",
    )(q, gates, kc, vc, ks, vs, kw, vw, _overlap_t())


def kernel(x, mlstm_norm, mlstm_w_in, mlstm_gate_bias, mlstm_head_norm, mlstm_w_out, kv_norm, kv_w, cmp_pe_k, cmp_w1_k, cmp_w2_k, cmp_pe_v, cmp_w1_v, cmp_w2_v, nsa_norm, nsa_w_q, nsa_w_out, moe_norm, moe_w_group, moe_b_group, moe_w_router, moe_b_router, moe_w_gate, moe_w_up, moe_w_down, final_norm):
    batch, seq, d = x.shape
    assert seq == SEQ and d == D_MODEL
    assert mlstm_norm.shape[0] == 1 and nsa_norm.shape[0] == 1 and moe_norm.shape[0] == 2
    n = batch * seq
    x2d = x.reshape(n, d)
    tril = jnp.tril(jnp.ones((TM, TM), F32)).astype(BF16)

    w_in = mlstm_w_in[0]
    w_gate = jnp.zeros((d, 128), F32).at[:, 0:8].set(w_in[:, 3072:3080])
    b_gate = jnp.zeros((1, 128), F32).at[0, 0:8].set(mlstm_gate_bias[0])
    q, k, v, o, gr = _mlstm_in(x2d, mlstm_norm[0].reshape(1, d), w_in[:, 0:3072].astype(BF16), w_gate, b_gate)
    hs = _mlstm_scan(q, k, v, o, gr, mlstm_head_norm[0].reshape(1, d), batch)
    h = _moe_layer(hs, mlstm_w_out[0].astype(BF16), x2d, moe_norm[0], moe_w_group[0], moe_b_group[0],
                   moe_w_router[0], moe_b_router[0], moe_w_gate[0], moe_w_up[0], moe_w_down[0],
                   tril, final_norm, False)

    w_q = jnp.zeros((d, 1152), F32).at[:, 0:1072].set(nsa_w_q[0]).astype(BF16)
    seq_tabs = _rope_tables(jnp.arange(SEQ), 128)
    q, gates, rk0, rk1, rv0, rv1, ks, vs, kw, vw = _nsa_proj(
        h, nsa_norm[0].reshape(1, d), kv_norm.reshape(1, d), w_q, kv_w.astype(BF16), seq_tabs, batch)
    cmp_pos = jnp.arange(N_CMP_PAD) * CMP_STRIDE + CMP_BLOCK - 1
    kc, vc = _compress((rk0, rk1), (rv0, rv1), _compress_weights(cmp_pe_k, cmp_w1_k, cmp_w2_k),
                       _compress_weights(cmp_pe_v, cmp_w1_v, cmp_w2_v), _rope_tables(cmp_pos, 128), batch)
    att = _nsa_attn(q, gates, kc, vc, ks, vs, kw, vw, batch)
    out = _moe_layer(att, nsa_w_out[0].astype(BF16), h, moe_norm[1], moe_w_group[1], moe_b_group[1],
                     moe_w_router[1], moe_b_router[1], moe_w_gate[1], moe_w_up[1], moe_w_down[1],
                     tril, final_norm, True)
    return out.reshape(batch, seq, d)
```

```python
import functools

import numpy as np
import jax
import jax.numpy as jnp
from jax import lax
from jax.experimental import pallas as pl
from jax.experimental.pallas import tpu as pltpu

F32 = jnp.float32
BF16 = jnp.bfloat16
HIGHEST = lax.Precision.HIGHEST

D_MODEL = 1024
SEQ = 2048
RMS_EPS = 1e-6
NEG_INF = -1e30

MLSTM_HEADS = 4
MLSTM_V_DIM = 256
MLSTM_QK_DIM = 128
MLSTM_L = 256
GATE_SOFTCAP = 15.0

NSA_HEADS = 16
NSA_HEAD_DIM = 64
NSA_GROUPS = 4
NSA_GROUP_SIZE = 4
CMP_BLOCK = 32
CMP_STRIDE = 16
N_CMP_PAD = 128
SEL_BLOCK = 64
N_SEL = SEQ // SEL_BLOCK
SEL_TOPK = 16
WINDOW = 512
Q_BLOCK = 128
N_QB = SEQ // Q_BLOCK
WIN_KEYS = WINDOW + Q_BLOCK
SLC_TILE = 256
FORCED_SCORE = 1e6
ROPE_THETA = 500000.0
ROPE_DIM = 16

MOE_GROUPS = 4
MOE_PER_GROUP = 8
MOE_EXPERTS = 32
MOE_HIDDEN = 256
MOE_PAIRS = 28
MOE_CLASSES = MOE_GROUPS * MOE_PAIRS
ROW_TILE = 128
XE_W = D_MODEL + 128
LANE_WLO, LANE_WHI, LANE_ROUTE = 0, 1, 2
ROUTE_SHIFT = 16

TM = 512
VMEM_LIMIT = 56 * 1024 * 1024

_NT = (((1,), (1,)), ((), ()))
_TN = (((0,), (0,)), ((), ()))


def _cparams(*sem):
    return pltpu.CompilerParams(dimension_semantics=sem, vmem_limit_bytes=VMEM_LIMIT)


def _rms(x):
    return x * lax.rsqrt(jnp.mean(x * x, axis=-1, keepdims=True) + RMS_EPS)


def _sigmoid(x):
    return 1.0 / (1.0 + jnp.exp(-x))


def _mlstm_in_kernel(x_ref, g_ref, w_ref, wg_ref, bg_ref, q_ref, k_ref, v_ref, o_ref, gr_ref):
    hn = _rms(x_ref[...]) * g_ref[...]
    hb = hn.astype(BF16)
    q_ref[...] = jnp.dot(hb, w_ref[:, 0:512], preferred_element_type=F32).astype(BF16)
    k = jnp.dot(hb, w_ref[:, 512:1024], preferred_element_type=F32)
    k_ref[...] = (k * (MLSTM_QK_DIM ** -0.5)).astype(BF16)
    v_ref[...] = jnp.dot(hb, w_ref[:, 1024:2048], preferred_element_type=F32).astype(BF16)
    o_ref[...] = _sigmoid(jnp.dot(hb, w_ref[:, 2048:3072], preferred_element_type=F32))
    gates = jnp.dot(hn, wg_ref[...], precision=HIGHEST, preferred_element_type=F32) + bg_ref[...]
    gates = GATE_SOFTCAP * jnp.tanh(gates / GATE_SOFTCAP)
    lane = lax.broadcasted_iota(jnp.int32, gates.shape, 1)
    log_f = jnp.minimum(gates, 0.0) - jnp.log1p(jnp.exp(-jnp.abs(gates)))
    lg = jnp.where(lane < MLSTM_HEADS, gates, log_f)
    gr_ref[...] = lg.T[0:8, :]


def _mlstm_in(x2d, norm_g, w_qkvo, w_gate, b_gate):
    n = x2d.shape[0]
    row = lambda i: (i, 0)
    fixed = lambda i: (0, 0)
    return pl.pallas_call(
        _mlstm_in_kernel,
        grid=(n // TM,),
        in_specs=[pl.BlockSpec((TM, D_MODEL), row),
                  pl.BlockSpec((1, D_MODEL), fixed),
                  pl.BlockSpec((D_MODEL, 3072), fixed),
                  pl.BlockSpec((D_MODEL, 128), fixed),
                  pl.BlockSpec((1, 128), fixed)],
        out_specs=[pl.BlockSpec((TM, 512), row),
                   pl.BlockSpec((TM, 512), row),
                   pl.BlockSpec((TM, 1024), row),
                   pl.BlockSpec((TM, 1024), row),
                   pl.BlockSpec((8, TM), lambda i: (0, i))],
        out_shape=[jax.ShapeDtypeStruct((n, 512), BF16),
                   jax.ShapeDtypeStruct((n, 512), BF16),
                   jax.ShapeDtypeStruct((n, 1024), BF16),
                   jax.ShapeDtypeStruct((n, 1024), F32),
                   jax.ShapeDtypeStruct((8, n), F32)],
        compiler_params=_cparams("parallel"),
        name="mlstm_in",
    )(x2d, norm_g, w_qkvo, w_gate, b_gate)


def _mlstm_scan_kernel(q_ref, k_ref, v_ref, o_ref, gr_ref, hn_ref, out_ref, c_ref, n_ref, m_ref):
    L = MLSTM_L

    @pl.when(pl.program_id(1) == 0)
    def _():
        c_ref[...] = jnp.zeros_like(c_ref)
        n_ref[...] = jnp.zeros_like(n_ref)
        m_ref[...] = jnp.zeros_like(m_ref)

    row = lax.broadcasted_iota(jnp.int32, (L, L), 0)
    col = lax.broadcasted_iota(jnp.int32, (L, L), 1)
    causal = col <= row
    tril = causal.astype(F32)
    eye = (col == row).astype(F32)
    gr = gr_ref[...]
    gr_pad = jnp.concatenate([gr, jnp.zeros((120, L), F32)], axis=0)
    b_row = lax.dot_general(gr, tril, _NT, precision=HIGHEST, preferred_element_type=F32)
    b_col = lax.dot_general(tril, gr_pad, _NT, precision=HIGHEST, preferred_element_type=F32)
    g_col = lax.dot_general(eye, gr_pad, _NT, precision=HIGHEST, preferred_element_type=F32)

    for h in range(MLSTM_HEADS):
        li_row = gr[h:h + 1, :]
        li_col = g_col[:, h:h + 1]
        bf_row = b_row[4 + h:5 + h, :]
        bf_col = b_col[:, 4 + h:5 + h]
        m = m_ref[h:h + 1, 0:1]
        dmat = jnp.where(causal, bf_col - bf_row + li_row, NEG_INF)
        m_inter = bf_col + m
        m_t = jnp.maximum(m_inter, jnp.max(dmat, axis=-1, keepdims=True))
        w_intra = jnp.exp(dmat - m_t)
        w_inter = jnp.exp(m_inter - m_t)
        qh = q_ref[:, h * 128:(h + 1) * 128]
        kh = k_ref[:, h * 128:(h + 1) * 128]
        vh = v_ref[:, h * 256:(h + 1) * 256]
        s = lax.dot_general(qh, kh, _NT, preferred_element_type=F32) * w_intra
        c_old = c_ref[h]
        num = (jnp.dot(s.astype(BF16), vh, preferred_element_type=F32)
               + w_inter * jnp.dot(qh, c_old.astype(BF16), preferred_element_type=F32))
        n_old = n_ref[h:h + 1, :]
        qn = jnp.sum(qh.astype(F32) * n_old, axis=-1, keepdims=True)
        den = jnp.sum(s, axis=-1, keepdims=True) + w_inter * qn
        hh = num / jnp.maximum(jnp.abs(den), jnp.exp(-m_t))
        b_end = bf_col[L - 1:L, :]
        g = b_end - bf_col + li_col
        m_new = jnp.maximum(b_end + m, jnp.max(g, axis=0, keepdims=True))
        ws = jnp.exp(g - m_new)
        decay = jnp.exp(b_end + m - m_new)
        kf = kh.astype(F32) * ws
        c_ref[h] = decay * c_old + lax.dot_general(kf.astype(BF16), vh, _TN, preferred_element_type=F32)
        n_ref[h:h + 1, :] = decay * n_old + jnp.sum(kf, axis=0, keepdims=True)
        m_ref[h:h + 1, :] = jnp.broadcast_to(m_new, (1, 128))
        sl = slice(h * 256, (h + 1) * 256)
        out_ref[:, sl] = (_rms(hh) * hn_ref[:, sl] * o_ref[:, sl]).astype(BF16)


def _mlstm_scan(q, k, v, o, gr, head_norm, batch):
    n = q.shape[0]
    nblk = SEQ // MLSTM_L
    row = lambda b, j: (b * nblk + j, 0)
    return pl.pallas_call(
        _mlstm_scan_kernel,
        grid=(batch, nblk),
        in_specs=[pl.BlockSpec((MLSTM_L, 512), row),
                  pl.BlockSpec((MLSTM_L, 512), row),
                  pl.BlockSpec((MLSTM_L, 1024), row),
                  pl.BlockSpec((MLSTM_L, 1024), row),
                  pl.BlockSpec((8, MLSTM_L), lambda b, j: (0, b * nblk + j)),
                  pl.BlockSpec((1, 1024), lambda b, j: (0, 0))],
        out_specs=pl.BlockSpec((MLSTM_L, 1024), row),
        out_shape=jax.ShapeDtypeStruct((n, 1024), BF16),
        scratch_shapes=[pltpu.VMEM((MLSTM_HEADS, MLSTM_QK_DIM, MLSTM_V_DIM), F32),
                        pltpu.VMEM((8, 128), F32),
                        pltpu.VMEM((8, 128), F32)],
        compiler_params=_cparams("parallel", "arbitrary"),
        name="mlstm_scan",
    )(q, k, v, o, gr, head_norm)


def _pair_tables():
    lo, hi = [], []
    for g in range(MOE_GROUPS):
        for a in range(MOE_PER_GROUP):
            for b in range(a + 1, MOE_PER_GROUP):
                lo.append(g * MOE_PER_GROUP + a)
                hi.append(g * MOE_PER_GROUP + b)
    return np.asarray(lo, np.int32), np.asarray(hi, np.int32)


_PAIR_LO, _PAIR_HI = _pair_tables()


def _mix_out_kernel(a_ref, w_ref, res_ref, g_ref, wr_ref, br_ref, tril_ref, h_ref, xe_ref, route_ref, cnt_ref,
                    run_ref):
    @pl.when(pl.program_id(0) == 0)
    def _():
        run_ref[...] = jnp.zeros_like(run_ref)

    h = res_ref[...] + jnp.dot(a_ref[...], w_ref[...], preferred_element_type=F32)
    h_ref[...] = h
    hn = _rms(h) * g_ref[...]
    xe_ref[:, 0:D_MODEL] = hn

    logits = jnp.dot(hn, wr_ref[...], precision=HIGHEST, preferred_element_type=F32) + br_ref[...]
    lane_i = lax.broadcasted_iota(jnp.int32, logits.shape, 1)
    lane = lane_i.astype(F32)
    ninf = -jnp.inf
    is_g = (lane_i >= MOE_EXPERTS) & (lane_i < MOE_EXPERTS + MOE_GROUPS)
    glog = jnp.where(is_g, logits, ninf)
    gmax = jnp.max(glog, axis=-1, keepdims=True)
    gidx = jnp.min(jnp.where(glog == gmax, lane - MOE_EXPERTS, 99.0), axis=-1, keepdims=True)
    pg_top = 1.0 / jnp.sum(jnp.exp(glog - gmax), axis=-1, keepdims=True)
    lane_grp = (lane_i >> 3).astype(F32)
    in_grp = (lane_i < MOE_EXPERTS) & (lane_grp == gidx)
    ev = jnp.where(in_grp, logits, ninf)
    v1 = jnp.max(ev, axis=-1, keepdims=True)
    i1 = jnp.min(jnp.where(ev == v1, lane, 999.0), axis=-1, keepdims=True)
    ev2 = jnp.where(lane == i1, ninf, ev)
    v2 = jnp.max(ev2, axis=-1, keepdims=True)
    i2 = jnp.min(jnp.where(ev2 == v2, lane, 999.0), axis=-1, keepdims=True)
    e2 = jnp.exp(v2 - v1)
    w1 = pg_top / (1.0 + e2)
    w2 = pg_top * e2 / (1.0 + e2)
    first_lo = i1 < i2
    w_lo = jnp.where(first_lo, w1, w2)
    w_hi = jnp.where(first_lo, w2, w1)
    a = jnp.minimum(i1, i2) - MOE_PER_GROUP * gidx
    b = jnp.maximum(i1, i2) - MOE_PER_GROUP * gidx
    cls = gidx * MOE_PAIRS + a * (15.0 - a) * 0.5 + (b - a - 1.0)

    onehot = lane == cls
    prefix = jnp.dot(tril_ref[...], onehot.astype(BF16), preferred_element_type=F32)
    run = run_ref[0:1, :]
    rank = jnp.sum(jnp.where(onehot, prefix - 1.0 + run, 0.0), axis=-1, keepdims=True)
    run_new = run + prefix[TM - 1:TM, :]
    run_ref[...] = jnp.broadcast_to(run_new, run_ref.shape)
    cnt_ref[...] = jnp.broadcast_to(run_new, cnt_ref.shape)

    route = cls * float(2 ** ROUTE_SHIFT) + rank
    meta = jnp.where(lane_i == LANE_WLO, w_lo,
                     jnp.where(lane_i == LANE_WHI, w_hi,
                               jnp.where(lane_i == LANE_ROUTE, route, 0.0)))
    xe_ref[:, D_MODEL:XE_W] = meta
    route_ref[...] = meta.T[0:8, :].astype(jnp.int32)


def _mix_out(a, w, res, g_moe, w_rt, b_rt, tril):
    n, kdim = a.shape
    row = lambda i: (i, 0)
    fixed = lambda i: (0, 0)
    return pl.pallas_call(
        _mix_out_kernel,
        grid=(n // TM,),
        in_specs=[pl.BlockSpec((TM, kdim), row),
                  pl.BlockSpec((kdim, D_MODEL), fixed),
                  pl.BlockSpec((TM, D_MODEL), row),
                  pl.BlockSpec((1, D_MODEL), fixed),
                  pl.BlockSpec((D_MODEL, 128), fixed),
                  pl.BlockSpec((1, 128), fixed),
                  pl.BlockSpec((TM, TM), fixed)],
        out_specs=[pl.BlockSpec((TM, D_MODEL), row),
                   pl.BlockSpec((TM, XE_W), row),
                   pl.BlockSpec((8, TM), lambda i: (0, i)),
                   pl.BlockSpec((8, 128), fixed)],
        out_shape=[jax.ShapeDtypeStruct((n, D_MODEL), F32),
                   jax.ShapeDtypeStruct((n, XE_W), F32),
                   jax.ShapeDtypeStruct((8, n), jnp.int32),
                   jax.ShapeDtypeStruct((8, 128), F32)],
        scratch_shapes=[pltpu.VMEM((8, 128), F32)],
        compiler_params=_cparams("arbitrary"),
        name="mix_out",
    )(a, w, res, g_moe, w_rt, b_rt, tril)


def _sorted_row(route_ref, offs_ref, idx):
    r = route_ref[idx]
    return offs_ref[r >> ROUTE_SHIFT] + (r & (2 ** ROUTE_SHIFT - 1))


def _dispatch_kernel(route_ref, offs_ref, cnt_ref, nused_ref, xe_ref, xs_ref, zbuf, sem, zsem):
    i = pl.program_id(0)
    base = i * TM

    @pl.when(i == 0)
    def _():
        zbuf[...] = jnp.zeros_like(zbuf)

        def per_class(c, carry):
            cnt = cnt_ref[c]
            start = offs_ref[c] + cnt
            pad = (-cnt) & (ROW_TILE - 1)

            def fill(r, inner):
                pltpu.make_async_copy(zbuf.at[pl.ds(0, 1)], xs_ref.at[pl.ds(start + r, 1)], zsem).start()
                return inner

            def drain(r, inner):
                pltpu.make_async_copy(zbuf.at[pl.ds(0, 1)], xs_ref.at[pl.ds(0, 1)], zsem).wait()
                return inner

            lax.fori_loop(0, pad, fill, 0)
            lax.fori_loop(0, pad, drain, 0)
            return carry

        lax.fori_loop(0, MOE_CLASSES, per_class, 0)

        def tail(t, carry):
            row0 = pl.multiple_of(t * ROW_TILE, ROW_TILE)
            cp = pltpu.make_async_copy(zbuf, xs_ref.at[pl.ds(row0, ROW_TILE)], zsem)
            cp.start()
            cp.wait()
            return carry

        lax.fori_loop(nused_ref[0], xs_ref.shape[0] // ROW_TILE, tail, 0)

    def issue(t, carry):
        p = _sorted_row(route_ref, offs_ref, base + t)
        pltpu.make_async_copy(xe_ref.at[pl.ds(t, 1)], xs_ref.at[pl.ds(p, 1)], sem).start()
        return carry

    lax.fori_loop(0, TM, issue, 0)
    pltpu.make_async_copy(xe_ref, xs_ref.at[pl.ds(0, TM)], sem).wait()


def _dispatch(route, offs, cnt, n_used, xe, n_rows):
    n = xe.shape[0]
    grid_spec = pltpu.PrefetchScalarGridSpec(
        num_scalar_prefetch=4,
        grid=(n // TM,),
        in_specs=[pl.BlockSpec((TM, XE_W), lambda i, *_: (i, 0))],
        out_specs=pl.BlockSpec(memory_space=pl.ANY),
        scratch_shapes=[pltpu.VMEM((ROW_TILE, XE_W), F32), pltpu.SemaphoreType.DMA(()),
                        pltpu.SemaphoreType.DMA(())],
    )
    return pl.pallas_call(
        _dispatch_kernel,
        grid_spec=grid_spec,
        out_shape=jax.ShapeDtypeStruct((n_rows, XE_W), F32),
        compiler_params=_cparams("arbitrary"),
        name="moe_dispatch",
    )(route, offs, cnt, n_used, xe)


def _experts_kernel(tlo_ref, thi_ref, nused_ref, xs_ref, wg_lo, wu_lo, wd_lo, wg_hi, wu_hi, wd_hi, y_ref):
    del tlo_ref, thi_ref

    @pl.when(pl.program_id(0) < nused_ref[0])
    def _():
        x = xs_ref[:, 0:D_MODEL].astype(BF16)

        def ffn(wg, wu, wd, w):
            a = jnp.dot(x, wg[0], preferred_element_type=F32)
            u = jnp.dot(x, wu[0], preferred_element_type=F32)
            hid = (a * _sigmoid(a)) * u * w
            return jnp.dot(hid.astype(BF16), wd[0], preferred_element_type=F32)

        w_lo = xs_ref[:, D_MODEL + LANE_WLO:D_MODEL + LANE_WLO + 1]
        w_hi = xs_ref[:, D_MODEL + LANE_WHI:D_MODEL + LANE_WHI + 1]
        y_ref[...] = ffn(wg_lo, wu_lo, wd_lo, w_lo) + ffn(wg_hi, wu_hi, wd_hi, w_hi)

    @pl.when(pl.program_id(0) >= nused_ref[0])
    def _():
        y_ref[...] = jnp.zeros_like(y_ref)


def _experts(tile_lo, tile_hi, n_used, xs, w_gate, w_up, w_down):
    n_tiles = xs.shape[0] // ROW_TILE
    rows = lambda i, tlo, thi, nu: (jnp.maximum(jnp.minimum(i, nu[0] - 1), 0), 0)
    lo = lambda i, tlo, thi, nu: (tlo[i], 0, 0)
    hi = lambda i, tlo, thi, nu: (thi[i], 0, 0)
    up_spec = lambda m: pl.BlockSpec((1, D_MODEL, MOE_HIDDEN), m)
    dn_spec = lambda m: pl.BlockSpec((1, MOE_HIDDEN, D_MODEL), m)
    grid_spec = pltpu.PrefetchScalarGridSpec(
        num_scalar_prefetch=3,
        grid=(n_tiles,),
        in_specs=[pl.BlockSpec((ROW_TILE, XE_W), rows),
                  up_spec(lo), up_spec(lo), dn_spec(lo),
                  up_spec(hi), up_spec(hi), dn_spec(hi)],
        out_specs=pl.BlockSpec((ROW_TILE, D_MODEL), lambda i, tlo, thi, nu: (i, 0)),
    )
    return pl.pallas_call(
        _experts_kernel,
        grid_spec=grid_spec,
        out_shape=jax.ShapeDtypeStruct((xs.shape[0], D_MODEL), F32),
        compiler_params=_cparams("arbitrary"),
        name="moe_experts",
    )(tile_lo, tile_hi, n_used, xs, w_gate, w_up, w_down, w_gate, w_up, w_down)


def _combine_kernel(route_ref, offs_ref, h_ref, y_ref, g_ref, out_ref, buf, sem, *, final_norm):
    base = pl.program_id(0) * TM

    def issue(t, carry):
        p = _sorted_row(route_ref, offs_ref, base + t)
        pltpu.make_async_copy(y_ref.at[pl.ds(p, 1)], buf.at[pl.ds(t, 1)], sem).start()
        return carry

    lax.fori_loop(0, TM, issue, 0)
    pltpu.make_async_copy(y_ref.at[pl.ds(0, TM)], buf, sem).wait()
    out = h_ref[...] + buf[...]
    if final_norm:
        out = _rms(out) * g_ref[...]
    out_ref[...] = out


def _combine(route, offs, h, y, gain, final_norm):
    n = h.shape[0]
    grid_spec = pltpu.PrefetchScalarGridSpec(
        num_scalar_prefetch=2,
        grid=(n // TM,),
        in_specs=[pl.BlockSpec((TM, D_MODEL), lambda i, *_: (i, 0)),
                  pl.BlockSpec(memory_space=pl.ANY),
                  pl.BlockSpec((1, D_MODEL), lambda i, *_: (0, 0))],
        out_specs=pl.BlockSpec((TM, D_MODEL), lambda i, *_: (i, 0)),
        scratch_shapes=[pltpu.VMEM((TM, D_MODEL), F32), pltpu.SemaphoreType.DMA(())],
    )
    return pl.pallas_call(
        functools.partial(_combine_kernel, final_norm=final_norm),
        grid_spec=grid_spec,
        out_shape=jax.ShapeDtypeStruct((n, D_MODEL), F32),
        compiler_params=_cparams("arbitrary"),
        name="moe_combine",
    )(route, offs, h, y, gain)


def _moe_layer(a, w_out, res, moe_norm, w_group, b_group, w_router, b_router, w_gate, w_up, w_down,
               tril, out_gain, final_norm):
    n = a.shape[0]
    w_rt = jnp.zeros((D_MODEL, 128), F32).at[:, 0:MOE_EXPERTS].set(w_router)
    w_rt = w_rt.at[:, MOE_EXPERTS:MOE_EXPERTS + MOE_GROUPS].set(w_group)
    b_rt = jnp.zeros((1, 128), F32).at[0, 0:MOE_EXPERTS].set(b_router)
    b_rt = b_rt.at[0, MOE_EXPERTS:MOE_EXPERTS + MOE_GROUPS].set(b_group)
    h, xe, route8, counts = _mix_out(a, w_out, res, moe_norm.reshape(1, D_MODEL), w_rt, b_rt, tril)

    n_tiles = n // ROW_TILE + MOE_CLASSES
    cnt = counts[0].astype(jnp.int32)
    tiles_c = (cnt + ROW_TILE - 1) // ROW_TILE
    tile_end = jnp.cumsum(tiles_c)
    offs = (tile_end - tiles_c) * ROW_TILE
    n_used = tile_end[-1]
    tile_ids = jnp.minimum(jnp.arange(n_tiles, dtype=jnp.int32), n_used - 1)
    tile_cls = jnp.sum((tile_end[None, 0:MOE_CLASSES] <= tile_ids[:, None]).astype(jnp.int32), axis=1)
    tile_cls = jnp.clip(tile_cls, 0, MOE_CLASSES - 1)
    tile_lo = jnp.asarray(_PAIR_LO)[tile_cls]
    tile_hi = jnp.asarray(_PAIR_HI)[tile_cls]
    route = route8[LANE_ROUTE]

    n_used = n_used.reshape(1)
    xs = _dispatch(route, offs, cnt, n_used, xe, n_tiles * ROW_TILE)
    y = _experts(tile_lo, tile_hi, n_used, xs, w_gate.astype(BF16), w_up.astype(BF16),
                 w_down.astype(BF16))
    return _combine(route, offs, h, y, out_gain.reshape(1, D_MODEL), final_norm)


def _rope_tables(pos, width):
    half = ROPE_DIM // 2
    inv_freq = jnp.power(jnp.float32(ROPE_THETA), -jnp.arange(half, dtype=F32) * (2.0 / ROPE_DIM))
    ang = pos.astype(F32)[:, None] * inv_freq[None, :]
    cos, sin = jnp.cos(ang), jnp.sin(ang)
    t = pos.shape[0]
    rest = NSA_HEAD_DIM - ROPE_DIM
    cos_t = jnp.concatenate([cos, cos, jnp.ones((t, rest), F32)], axis=1)
    sin_a = jnp.concatenate([-sin, jnp.zeros((t, half + rest), F32)], axis=1)
    sin_b = jnp.concatenate([jnp.zeros((t, half), F32), sin, jnp.zeros((t, rest), F32)], axis=1)
    rep = width // NSA_HEAD_DIM
    return jnp.tile(cos_t, (1, rep)), jnp.tile(sin_a, (1, rep)), jnp.tile(sin_b, (1, rep))


def _rope(x, cos_t, sin_a, sin_b):
    half = ROPE_DIM // 2
    parts = []
    for c in range(x.shape[1] // 128):
        xc = x[:, c * 128:(c + 1) * 128]
        parts.append(xc * cos_t + pltpu.roll(xc, 128 - half, axis=1) * sin_a + pltpu.roll(xc, half, axis=1) * sin_b)
    return parts[0] if len(parts) == 1 else jnp.concatenate(parts, axis=1)


def _nsa_proj_kernel(h_ref, gq_ref, gkv_ref, wq_ref, wkv_ref, cos_ref, sa_ref, sb_ref,
                     q_ref, gate_ref, kc0_ref, kc1_ref, vc0_ref, vc1_ref, ks_ref, vs_ref, kw_ref, vw_ref):
    r = _rms(h_ref[...])
    hq = (r * gq_ref[...]).astype(BF16)
    hk = (r * gkv_ref[...]).astype(BF16)
    cos_t, sin_a, sin_b = cos_ref[...], sa_ref[...], sb_ref[...]
    q = jnp.dot(hq, wq_ref[:, 0:1024], preferred_element_type=F32)
    q_ref[...] = (_rope(q, cos_t, sin_a, sin_b) * (NSA_HEAD_DIM ** -0.5)).astype(BF16)
    gate_ref[...] = _sigmoid(jnp.dot(hq, wq_ref[:, 1024:1152], preferred_element_type=F32))
    kc0_ref[...] = jnp.dot(hk, wkv_ref[:, 0:128], preferred_element_type=F32)
    kc1_ref[...] = jnp.dot(hk, wkv_ref[:, 128:256], preferred_element_type=F32)
    vc0_ref[...] = jnp.dot(hk, wkv_ref[:, 256:384], preferred_element_type=F32)
    vc1_ref[...] = jnp.dot(hk, wkv_ref[:, 384:512], preferred_element_type=F32)

    def store_groups(ref, val):
        for g in range(NSA_GROUPS):
            ref[0, g] = val[:, g * 64:(g + 1) * 64].astype(BF16)

    store_groups(ks_ref, _rope(jnp.dot(hk, wkv_ref[:, 512:768], preferred_element_type=F32), cos_t, sin_a, sin_b))
    store_groups(vs_ref, jnp.dot(hk, wkv_ref[:, 768:1024], preferred_element_type=F32))
    store_groups(kw_ref, _rope(jnp.dot(hk, wkv_ref[:, 1024:1280], preferred_element_type=F32), cos_t, sin_a, sin_b))
    store_groups(vw_ref, jnp.dot(hk, wkv_ref[:, 1280:1536], preferred_element_type=F32))


def _nsa_proj(h, g_q, g_kv, w_q, w_kv, rope_tabs, batch):
    n = h.shape[0]
    nblk = SEQ // TM
    row = lambda i: (i, 0)
    fixed = lambda i: (0, 0)
    tab = lambda i: (i % nblk, 0)
    grp = lambda i: (i // nblk, 0, i % nblk, 0)
    grp_spec = pl.BlockSpec((1, NSA_GROUPS, TM, NSA_HEAD_DIM), grp)
    grp_shape = jax.ShapeDtypeStruct((batch, NSA_GROUPS, SEQ, NSA_HEAD_DIM), BF16)
    return pl.pallas_call(
        _nsa_proj_kernel,
        grid=(n // TM,),
        in_specs=[pl.BlockSpec((TM, D_MODEL), row),
                  pl.BlockSpec((1, D_MODEL), fixed),
                  pl.BlockSpec((1, D_MODEL), fixed),
                  pl.BlockSpec((D_MODEL, 1152), fixed),
                  pl.BlockSpec((D_MODEL, 1536), fixed),
                  pl.BlockSpec((TM, 128), tab),
                  pl.BlockSpec((TM, 128), tab),
                  pl.BlockSpec((TM, 128), tab)],
        out_specs=[pl.BlockSpec((TM, 1024), row),
                   pl.BlockSpec((TM, 128), row),
                   pl.BlockSpec((TM, 128), row), pl.BlockSpec((TM, 128), row),
                   pl.BlockSpec((TM, 128), row), pl.BlockSpec((TM, 128), row),
                   grp_spec, grp_spec, grp_spec, grp_spec],
        out_shape=[jax.ShapeDtypeStruct((n, 1024), BF16),
                   jax.ShapeDtypeStruct((n, 128), F32),
                   jax.ShapeDtypeStruct((n, 128), F32), jax.ShapeDtypeStruct((n, 128), F32),
                   jax.ShapeDtypeStruct((n, 128), F32), jax.ShapeDtypeStruct((n, 128), F32),
                   grp_shape, grp_shape, grp_shape, grp_shape],
        compiler_params=_cparams("parallel"),
        name="nsa_proj",
    )(h, g_q, g_kv, w_q, w_kv, *rope_tabs)


HALF_BLOCKS = SEQ // CMP_STRIDE
CMP_K = CMP_STRIDE * 256


def _compress_kernel(rk0_ref, rk1_ref, rv0_ref, rv1_ref, w1k_ref, w1v_ref, pek_ref, pev_ref, w2k_ref, w2v_ref,
                     cos_ref, sa_ref, sb_ref, kc_ref, vc_ref):
    lane = lax.broadcasted_iota(jnp.int32, (HALF_BLOCKS, 512), 1)
    first_half = (lane & 127) < 64

    def comp(raw_refs, w1_ref, pe_ref, w2_ref):
        x = jnp.concatenate([r[pl.ds(l, HALF_BLOCKS, stride=CMP_STRIDE), :]
                             for l in range(CMP_STRIDE) for r in raw_refs], axis=1).astype(BF16)
        r = jnp.dot(x, w1_ref[...], preferred_element_type=F32)
        rpe = jnp.dot(pe_ref[...].astype(BF16), w1_ref[...], preferred_element_type=F32)
        r = r + jnp.where(first_half, rpe[0:1, :], rpe[1:2, :])
        nxt = pltpu.roll(r, HALF_BLOCKS - 1, axis=0)
        nxt = jnp.concatenate([pltpu.roll(nxt[:, c * 128:(c + 1) * 128], 64, axis=1) for c in range(4)], axis=1)
        pre = r + nxt
        act = pre * _sigmoid(pre)
        return jnp.dot(act.astype(BF16), w2_ref[...], preferred_element_type=F32)

    kc = comp((rk0_ref, rk1_ref), w1k_ref, pek_ref, w2k_ref)
    kc = _rope(kc, cos_ref[...], sa_ref[...], sb_ref[...])
    vc = comp((rv0_ref, rv1_ref), w1v_ref, pev_ref, w2v_ref)
    for g in range(NSA_GROUPS):
        kc_ref[0, g] = kc[:, g * 64:(g + 1) * 64]
        vc_ref[0, g] = vc[:, g * 64:(g + 1) * 64]


def _compress_weights(pe, w1, w2):
    dh = NSA_HEAD_DIM
    w1r = w1.reshape(2, CMP_STRIDE, dh, dh)
    big = jnp.zeros((CMP_STRIDE, NSA_GROUPS, dh, NSA_GROUPS, 2, dh), F32)
    for g in range(NSA_GROUPS):
        big = big.at[:, g, :, g, 0, :].set(w1r[0])
        big = big.at[:, g, :, g, 1, :].set(w1r[1])
    w1_big = big.reshape(CMP_K, 512).astype(BF16)
    pe_rows = jnp.zeros((8, CMP_STRIDE, NSA_GROUPS, dh), F32)
    pe_rows = pe_rows.at[0].set(jnp.broadcast_to(pe[0:CMP_STRIDE, None, :], (CMP_STRIDE, NSA_GROUPS, dh)))
    pe_rows = pe_rows.at[1].set(jnp.broadcast_to(pe[CMP_STRIDE:, None, :], (CMP_STRIDE, NSA_GROUPS, dh)))
    w2_bd = jnp.zeros((NSA_GROUPS, 2, dh, NSA_GROUPS, dh), F32)
    for g in range(NSA_GROUPS):
        w2_bd = w2_bd.at[g, 0, :, g, :].set(w2)
    return w1_big, pe_rows.reshape(8, CMP_K), w2_bd.reshape(512, 256).astype(BF16)


def _compress(raw_k, raw_v, wk, wv, cmp_tabs, batch):
    fixed = lambda b: (0, 0)
    raw_spec = pl.BlockSpec((SEQ, 128), lambda b: (b, 0))
    out_spec = pl.BlockSpec((1, NSA_GROUPS, N_CMP_PAD, NSA_HEAD_DIM), lambda b: (b, 0, 0, 0))
    out_shape = jax.ShapeDtypeStruct((batch, NSA_GROUPS, N_CMP_PAD, NSA_HEAD_DIM), F32)
    return pl.pallas_call(
        _compress_kernel,
        grid=(batch,),
        in_specs=[raw_spec, raw_spec, raw_spec, raw_spec,
                  pl.BlockSpec((CMP_K, 512), fixed), pl.BlockSpec((CMP_K, 512), fixed),
                  pl.BlockSpec((8, CMP_K), fixed), pl.BlockSpec((8, CMP_K), fixed),
                  pl.BlockSpec((512, 256), fixed), pl.BlockSpec((512, 256), fixed),
                  pl.BlockSpec((N_CMP_PAD, 128), fixed), pl.BlockSpec((N_CMP_PAD, 128), fixed),
                  pl.BlockSpec((N_CMP_PAD, 128), fixed)],
        out_specs=[out_spec, out_spec],
        out_shape=[out_shape, out_shape],
        compiler_params=_cparams("parallel"),
        name="nsa_compress",
    )(*raw_k, *raw_v, wk[0], wv[0], wk[1], wv[1], wk[2], wv[2], *cmp_tabs)


def _nsa_attn_kernel(q_ref, gate_ref, kc_ref, vc_ref, ks_ref, vs_ref, kw_ref, vw_ref, ovl_ref, out_ref):
    g = pl.program_id(1)
    qb = pl.program_id(2)
    q0 = qb * Q_BLOCK
    Q, HG, DH = Q_BLOCK, NSA_GROUP_SIZE, NSA_HEAD_DIM
    R = HG * Q

    q4 = q_ref[...]
    q_all = jnp.concatenate([q4[:, h * DH:(h + 1) * DH] for h in range(HG)], axis=0)

    def stack(x):
        return jnp.concatenate([x] * HG, axis=0)

    kc = kc_ref[0, 0]
    vc = vc_ref[0, 0]
    t_col = q0 + lax.broadcasted_iota(jnp.int32, (Q, N_CMP_PAD), 0)
    n_lane = lax.broadcasted_iota(jnp.int32, (Q, N_CMP_PAD), 1)
    valid_c = stack((n_lane * CMP_STRIDE + CMP_BLOCK - 1 <= t_col) & (n_lane < N_CMP_PAD - 1))
    s_c = lax.dot_general(q_all.astype(F32), kc, _NT, precision=HIGHEST, preferred_element_type=F32)
    s_c = jnp.where(valid_c, s_c, NEG_INF)
    m_c = jnp.max(s_c, axis=-1, keepdims=True)
    e_c = jnp.where(valid_c, jnp.exp(s_c - m_c), 0.0)
    l_c = jnp.sum(e_c, axis=-1, keepdims=True)
    p_c = e_c * jnp.where(l_c > 0.0, 1.0 / l_c, 0.0)
    o_c = jnp.dot(p_c.astype(BF16), vc.astype(BF16), preferred_element_type=F32)

    p_sum = p_c[0:Q] + p_c[Q:2 * Q] + p_c[2 * Q:3 * Q] + p_c[3 * Q:4 * Q]
    imp_t = lax.dot_general(ovl_ref[...], p_sum, _NT, precision=HIGHEST, preferred_element_type=F32)[0:N_SEL]
    j_row = lax.broadcasted_iota(jnp.int32, (N_SEL, Q), 0)
    cur = (q0 + lax.broadcasted_iota(jnp.int32, (N_SEL, Q), 1)) >> 6
    forced = (j_row == 0) | (j_row == cur) | (j_row == cur - 1)
    imp_t = jnp.where(forced, FORCED_SCORE, imp_t)
    imp_t = jnp.where(j_row > cur, NEG_INF, imp_t)
    cnt = jnp.zeros((N_SEL, Q), F32)
    for i in range(N_SEL):
        ri = imp_t[i:i + 1, :]
        ahead = (ri > imp_t) | ((ri == imp_t) & (j_row > i))
        cnt = cnt + jnp.where(ahead, 1.0, 0.0)
    sel_t = jnp.where((cnt < SEL_TOPK) & (j_row <= cur), 1.0, 0.0)
    sel = jnp.concatenate([sel_t, jnp.zeros((128 - N_SEL, Q), F32)], axis=0).T.astype(BF16)

    t_key = q0 + lax.broadcasted_iota(jnp.int32, (Q, SLC_TILE), 0)
    c_key = lax.broadcasted_iota(jnp.int32, (Q, SLC_TILE), 1)
    j_exp = lax.broadcasted_iota(jnp.int32, (128, SLC_TILE), 0)
    c_exp = lax.broadcasted_iota(jnp.int32, (128, SLC_TILE), 1) >> 6

    def slc_step(kt, carry):
        m_run, l_run, acc = carry
        start = pl.multiple_of(kt * SLC_TILE, SLC_TILE)
        ks = ks_ref[0, 0, pl.ds(start, SLC_TILE), :]
        vs = vs_ref[0, 0, pl.ds(start, SLC_TILE), :]
        expand = jnp.where(j_exp == kt * (SLC_TILE // SEL_BLOCK) + c_exp, 1.0, 0.0).astype(BF16)
        picked = jnp.dot(sel, expand, preferred_element_type=F32) > 0.5
        mask = stack(picked & (start + c_key <= t_key))
        s = lax.dot_general(q_all, ks, _NT, preferred_element_type=F32)
        s = jnp.where(mask, s, NEG_INF)
        m_new = jnp.maximum(m_run, jnp.max(s, axis=-1, keepdims=True))
        p = jnp.where(mask, jnp.exp(s - m_new), 0.0)
        corr = jnp.exp(m_run - m_new)
        acc = acc * corr + jnp.dot(p.astype(BF16), vs, preferred_element_type=F32)
        return m_new, l_run * corr + jnp.sum(p, axis=-1, keepdims=True), acc

    n_kt = (q0 + Q + SLC_TILE - 1) >> 8
    init = (jnp.full((R, 1), NEG_INF, F32), jnp.zeros((R, 1), F32), jnp.zeros((R, DH), F32))
    _, l_s, acc_s = lax.fori_loop(0, n_kt, slc_step, init)
    o_s = acc_s / l_s

    w0 = pl.multiple_of(jnp.maximum(q0 - WINDOW, 0), Q_BLOCK)
    kw = kw_ref[0, 0, pl.ds(w0, WIN_KEYS), :]
    vw = vw_ref[0, 0, pl.ds(w0, WIN_KEYS), :]
    t_w = q0 + lax.broadcasted_iota(jnp.int32, (Q, WIN_KEYS), 0)
    s_pos = w0 + lax.broadcasted_iota(jnp.int32, (Q, WIN_KEYS), 1)
    valid_w = stack((s_pos <= t_w) & (s_pos > t_w - WINDOW))
    s_w = lax.dot_general(q_all, kw, _NT, preferred_element_type=F32)
    s_w = jnp.where(valid_w, s_w, NEG_INF)
    m_w = jnp.max(s_w, axis=-1, keepdims=True)
    e_w = jnp.where(valid_w, jnp.exp(s_w - m_w), 0.0)
    o_w = jnp.dot(e_w.astype(BF16), vw, preferred_element_type=F32) / jnp.sum(e_w, axis=-1, keepdims=True)

    gates = gate_ref[...]
    g_lane = lax.broadcasted_iota(jnp.int32, gates.shape, 1)

    def gate_col(branch):
        cols = []
        for h in range(HG):
            idx = (g * HG + h) * 3 + branch
            cols.append(jnp.sum(jnp.where(g_lane == idx, gates, 0.0), axis=-1, keepdims=True))
        return jnp.concatenate(cols, axis=0)

    o = gate_col(0) * o_c + gate_col(1) * o_s + gate_col(2) * o_w
    out_ref[...] = jnp.concatenate([o[h * Q:(h + 1) * Q] for h in range(HG)], axis=1).astype(BF16)


def _overlap_t():
    n = np.arange(N_CMP_PAD)
    j = np.arange(128)
    cmp_start = n * CMP_STRIDE
    cmp_end = cmp_start + CMP_BLOCK - 1
    sel_start = j * SEL_BLOCK
    ovl = ((cmp_start[None, :] <= sel_start[:, None] + SEL_BLOCK - 1) & (cmp_end[None, :] >= sel_start[:, None])
           & (j[:, None] < N_SEL) & (n[None, :] < N_CMP_PAD - 1))
    return jnp.asarray(ovl.astype(np.float32))


def _nsa_attn(q, gates, kc, vc, ks, vs, kw, vw, batch):
    n = q.shape[0]
    qrow = lambda b, g, i: (b * N_QB + i, g)
    grow = lambda b, g, i: (b * N_QB + i, 0)
    kv = lambda b, g, i: (b, g, 0, 0)
    cmp_spec = pl.BlockSpec((1, 1, N_CMP_PAD, NSA_HEAD_DIM), kv)
    seq_spec = pl.BlockSpec((1, 1, SEQ, NSA_HEAD_DIM), kv)
    return pl.pallas_call(
        _nsa_attn_kernel,
        grid=(batch, NSA_GROUPS, N_QB),
        in_specs=[pl.BlockSpec((Q_BLOCK, 256), qrow),
                  pl.BlockSpec((Q_BLOCK, 128), grow),
                  cmp_spec, cmp_spec, seq_spec, seq_spec, seq_spec, seq_spec,
                  pl.BlockSpec((128, N_CMP_PAD), lambda b, g, i: (0, 0))],
        out_specs=pl.BlockSpec((Q_BLOCK, 256), qrow),
        out_shape=jax.ShapeDtypeStruct((n, 1024), BF16),
        compiler_params=_cparams("parallel", "parallel", "arbitrary"),
        name="nsa_attn",
    )(q, gates, kc, vc, ks, vs, kw, vw, _overlap_t())


def kernel(x, mlstm_norm, mlstm_w_in, mlstm_gate_bias, mlstm_head_norm, mlstm_w_out, kv_norm, kv_w, cmp_pe_k, cmp_w1_k, cmp_w2_k, cmp_pe_v, cmp_w1_v, cmp_w2_v, nsa_norm, nsa_w_q, nsa_w_out, moe_norm, moe_w_group, moe_b_group, moe_w_router, moe_b_router, moe_w_gate, moe_w_up, moe_w_down, final_norm):
    batch, seq, d = x.shape
    assert seq == SEQ and d == D_MODEL
    assert mlstm_norm.shape[0] == 1 and nsa_norm.shape[0] == 1 and moe_norm.shape[0] == 2
    n = batch * seq
    x2d = x.reshape(n, d)
    tril = jnp.tril(jnp.ones((TM, TM), F32)).astype(BF16)

    w_in = mlstm_w_in[0]
    w_gate = jnp.zeros((d, 128), F32).at[:, 0:8].set(w_in[:, 3072:3080])
    b_gate = jnp.zeros((1, 128), F32).at[0, 0:8].set(mlstm_gate_bias[0])
    q, k, v, o, gr = _mlstm_in(x2d, mlstm_norm[0].reshape(1, d), w_in[:, 0:3072].astype(BF16), w_gate, b_gate)
    hs = _mlstm_scan(q, k, v, o, gr, mlstm_head_norm[0].reshape(1, d), batch)
    h = _moe_layer(hs, mlstm_w_out[0].astype(BF16), x2d, moe_norm[0], moe_w_group[0], moe_b_group[0],
                   moe_w_router[0], moe_b_router[0], moe_w_gate[0], moe_w_up[0], moe_w_down[0],
                   tril, final_norm, False)

    w_q = jnp.zeros((d, 1152), F32).at[:, 0:1072].set(nsa_w_q[0]).astype(BF16)
    seq_tabs = _rope_tables(jnp.arange(SEQ), 128)
    q, gates, rk0, rk1, rv0, rv1, ks, vs, kw, vw = _nsa_proj(
        h, nsa_norm[0].reshape(1, d), kv_norm.reshape(1, d), w_q, kv_w.astype(BF16), seq_tabs, batch)
    cmp_pos = jnp.arange(N_CMP_PAD) * CMP_STRIDE + CMP_BLOCK - 1
    kc, vc = _compress((rk0, rk1), (rv0, rv1), _compress_weights(cmp_pe_k, cmp_w1_k, cmp_w2_k),
                       _compress_weights(cmp_pe_v, cmp_w1_v, cmp_w2_v), _rope_tables(cmp_pos, 128), batch)
    att = _nsa_attn(q, gates, kc, vc, ks, vs, kw, vw, batch)
    out = _moe_layer(att, nsa_w_out[0].astype(BF16), h, moe_norm[1], moe_w_group[1], moe_b_group[1],
                     moe_w_router[1], moe_b_router[1], moe_w_gate[1], moe_w_up[1], moe_w_down[1],
                     tril, final_norm, True)
    return out.reshape(batch, seq, d)
```

```python
import functools

import numpy as np
import jax
import jax.numpy as jnp
from jax import lax
from jax.experimental import pallas as pl
from jax.experimental.pallas import tpu as pltpu

F32 = jnp.float32
BF16 = jnp.bfloat16
HIGHEST = lax.Precision.HIGHEST

D_MODEL = 1024
SEQ = 2048
RMS_EPS = 1e-6
NEG_INF = -1e30

MLSTM_HEADS = 4
MLSTM_V_DIM = 256
MLSTM_QK_DIM = 128
MLSTM_L = 256
GATE_SOFTCAP = 15.0

NSA_HEADS = 16
NSA_HEAD_DIM = 64
NSA_GROUPS = 4
NSA_GROUP_SIZE = 4
CMP_BLOCK = 32
CMP_STRIDE = 16
N_CMP_PAD = 128
SEL_BLOCK = 64
N_SEL = SEQ // SEL_BLOCK
SEL_TOPK = 16
WINDOW = 512
Q_BLOCK = 256
N_QB = SEQ // Q_BLOCK
WIN_KEYS = WINDOW + Q_BLOCK
SLC_TILE = 256
KV_LANES = 2 * NSA_HEAD_DIM
DEN_ROW = NSA_HEAD_DIM
FORCED_SCORE = 1e6
ROPE_THETA = 500000.0
ROPE_DIM = 16

MOE_GROUPS = 4
MOE_PER_GROUP = 8
MOE_EXPERTS = 32
MOE_HIDDEN = 256
MOE_PAIRS = 28
MOE_CLASSES = MOE_GROUPS * MOE_PAIRS
ROW_TILE = 128
XE_W = D_MODEL + 128
LANE_WLO, LANE_WHI, LANE_ROUTE = 0, 1, 2
ROUTE_SHIFT = 16

TM = 512
VMEM_LIMIT = 56 * 1024 * 1024

_NT = (((1,), (1,)), ((), ()))
_TN = (((0,), (0,)), ((), ()))


def _cparams(*sem):
    return pltpu.CompilerParams(dimension_semantics=sem, vmem_limit_bytes=VMEM_LIMIT)


def _rms(x):
    return x * lax.rsqrt(jnp.mean(x * x, axis=-1, keepdims=True) + RMS_EPS)


def _sigmoid(x):
    return 1.0 / (1.0 + jnp.exp(-x))


def _mlstm_in_kernel(x_ref, g_ref, w_ref, wg_ref, bg_ref, q_ref, k_ref, v_ref, o_ref, gr_ref):
    hn = _rms(x_ref[...]) * g_ref[...]
    hb = hn.astype(BF16)
    q_ref[...] = jnp.dot(hb, w_ref[:, 0:512], preferred_element_type=F32).astype(BF16)
    k = jnp.dot(hb, w_ref[:, 512:1024], preferred_element_type=F32)
    k_ref[...] = (k * (MLSTM_QK_DIM ** -0.5)).astype(BF16)
    v_ref[...] = jnp.dot(hb, w_ref[:, 1024:2048], preferred_element_type=F32).astype(BF16)
    o_ref[...] = _sigmoid(jnp.dot(hb, w_ref[:, 2048:3072], preferred_element_type=F32))
    gates = jnp.dot(hn, wg_ref[...], precision=HIGHEST, preferred_element_type=F32) + bg_ref[...]
    gates = GATE_SOFTCAP * jnp.tanh(gates / GATE_SOFTCAP)
    lane = lax.broadcasted_iota(jnp.int32, gates.shape, 1)
    log_f = jnp.minimum(gates, 0.0) - jnp.log1p(jnp.exp(-jnp.abs(gates)))
    lg = jnp.where(lane < MLSTM_HEADS, gates, log_f)
    gr_ref[...] = lg.T[0:8, :]


def _mlstm_in(x2d, norm_g, w_qkvo, w_gate, b_gate):
    n = x2d.shape[0]
    row = lambda i: (i, 0)
    fixed = lambda i: (0, 0)
    return pl.pallas_call(
        _mlstm_in_kernel,
        grid=(n // TM,),
        in_specs=[pl.BlockSpec((TM, D_MODEL), row),
                  pl.BlockSpec((1, D_MODEL), fixed),
                  pl.BlockSpec((D_MODEL, 3072), fixed),
                  pl.BlockSpec((D_MODEL, 128), fixed),
                  pl.BlockSpec((1, 128), fixed)],
        out_specs=[pl.BlockSpec((TM, 512), row),
                   pl.BlockSpec((TM, 512), row),
                   pl.BlockSpec((TM, 1024), row),
                   pl.BlockSpec((TM, 1024), row),
                   pl.BlockSpec((8, TM), lambda i: (0, i))],
        out_shape=[jax.ShapeDtypeStruct((n, 512), BF16),
                   jax.ShapeDtypeStruct((n, 512), BF16),
                   jax.ShapeDtypeStruct((n, 1024), BF16),
                   jax.ShapeDtypeStruct((n, 1024), F32),
                   jax.ShapeDtypeStruct((8, n), F32)],
        compiler_params=_cparams("parallel"),
        name="mlstm_in",
    )(x2d, norm_g, w_qkvo, w_gate, b_gate)


def _mlstm_scan_kernel(q_ref, k_ref, v_ref, o_ref, gr_ref, hn_ref, out_ref, c_ref, n_ref, m_ref):
    L = MLSTM_L

    @pl.when(pl.program_id(1) == 0)
    def _():
        c_ref[...] = jnp.zeros_like(c_ref)
        n_ref[...] = jnp.zeros_like(n_ref)
        m_ref[...] = jnp.zeros_like(m_ref)

    row = lax.broadcasted_iota(jnp.int32, (L, L), 0)
    col = lax.broadcasted_iota(jnp.int32, (L, L), 1)
    causal = col <= row
    tril = causal.astype(F32)
    eye = (col == row).astype(F32)
    gr = gr_ref[...]
    gr_pad = jnp.concatenate([gr, jnp.zeros((120, L), F32)], axis=0)
    b_row = lax.dot_general(gr, tril, _NT, precision=HIGHEST, preferred_element_type=F32)
    b_col = lax.dot_general(tril, gr_pad, _NT, precision=HIGHEST, preferred_element_type=F32)
    g_col = lax.dot_general(eye, gr_pad, _NT, precision=HIGHEST, preferred_element_type=F32)

    for h in range(MLSTM_HEADS):
        li_row = gr[h:h + 1, :]
        li_col = g_col[:, h:h + 1]
        bf_row = b_row[4 + h:5 + h, :]
        bf_col = b_col[:, 4 + h:5 + h]
        m = m_ref[h:h + 1, 0:1]
        dmat = jnp.where(causal, bf_col - bf_row + li_row, NEG_INF)
        m_inter = bf_col + m
        m_t = jnp.maximum(m_inter, jnp.max(dmat, axis=-1, keepdims=True))
        w_intra = jnp.exp(dmat - m_t)
        w_inter = jnp.exp(m_inter - m_t)
        qh = q_ref[:, h * 128:(h + 1) * 128]
        kh = k_ref[:, h * 128:(h + 1) * 128]
        vh = v_ref[:, h * 256:(h + 1) * 256]
        s = lax.dot_general(qh, kh, _NT, preferred_element_type=F32) * w_intra
        c_old = c_ref[h]
        num = (jnp.dot(s.astype(BF16), vh, preferred_element_type=F32)
               + w_inter * jnp.dot(qh, c_old.astype(BF16), preferred_element_type=F32))
        n_old = n_ref[h:h + 1, :]
        qn = jnp.sum(qh.astype(F32) * n_old, axis=-1, keepdims=True)
        den = jnp.sum(s, axis=-1, keepdims=True) + w_inter * qn
        hh = num / jnp.maximum(jnp.abs(den), jnp.exp(-m_t))
        b_end = bf_col[L - 1:L, :]
        g = b_end - bf_col + li_col
        m_new = jnp.maximum(b_end + m, jnp.max(g, axis=0, keepdims=True))
        ws = jnp.exp(g - m_new)
        decay = jnp.exp(b_end + m - m_new)
        kf = kh.astype(F32) * ws
        c_ref[h] = decay * c_old + lax.dot_general(kf.astype(BF16), vh, _TN, preferred_element_type=F32)
        n_ref[h:h + 1, :] = decay * n_old + jnp.sum(kf, axis=0, keepdims=True)
        m_ref[h:h + 1, :] = jnp.broadcast_to(m_new, (1, 128))
        sl = slice(h * 256, (h + 1) * 256)
        out_ref[:, sl] = (_rms(hh) * hn_ref[:, sl] * o_ref[:, sl]).astype(BF16)


def _mlstm_scan(q, k, v, o, gr, head_norm, batch):
    n = q.shape[0]
    nblk = SEQ // MLSTM_L
    row = lambda b, j: (b * nblk + j, 0)
    return pl.pallas_call(
        _mlstm_scan_kernel,
        grid=(batch, nblk),
        in_specs=[pl.BlockSpec((MLSTM_L, 512), row),
                  pl.BlockSpec((MLSTM_L, 512), row),
                  pl.BlockSpec((MLSTM_L, 1024), row),
                  pl.BlockSpec((MLSTM_L, 1024), row),
                  pl.BlockSpec((8, MLSTM_L), lambda b, j: (0, b * nblk + j)),
                  pl.BlockSpec((1, 1024), lambda b, j: (0, 0))],
        out_specs=pl.BlockSpec((MLSTM_L, 1024), row),
        out_shape=jax.ShapeDtypeStruct((n, 1024), BF16),
        scratch_shapes=[pltpu.VMEM((MLSTM_HEADS, MLSTM_QK_DIM, MLSTM_V_DIM), F32),
                        pltpu.VMEM((8, 128), F32),
                        pltpu.VMEM((8, 128), F32)],
        compiler_params=_cparams("parallel", "arbitrary"),
        name="mlstm_scan",
    )(q, k, v, o, gr, head_norm)


def _pair_tables():
    lo, hi = [], []
    for g in range(MOE_GROUPS):
        for a in range(MOE_PER_GROUP):
            for b in range(a + 1, MOE_PER_GROUP):
                lo.append(g * MOE_PER_GROUP + a)
                hi.append(g * MOE_PER_GROUP + b)
    return np.asarray(lo, np.int32), np.asarray(hi, np.int32)


_PAIR_LO, _PAIR_HI = _pair_tables()


def _mix_out_kernel(a_ref, w_ref, res_ref, g_ref, wr_ref, br_ref, tril_ref, h_ref, xe_ref, route_ref, cnt_ref,
                    run_ref, *, a_transposed):
    @pl.when(pl.program_id(0) == 0)
    def _():
        run_ref[...] = jnp.zeros_like(run_ref)

    dims = _TN if a_transposed else (((1,), (0,)), ((), ()))
    h = res_ref[...] + lax.dot_general(a_ref[...], w_ref[...], dims, preferred_element_type=F32)
    h_ref[...] = h
    hn = _rms(h) * g_ref[...]
    xe_ref[:, 0:D_MODEL] = hn

    logits = jnp.dot(hn, wr_ref[...], precision=HIGHEST, preferred_element_type=F32) + br_ref[...]
    lane_i = lax.broadcasted_iota(jnp.int32, logits.shape, 1)
    lane = lane_i.astype(F32)
    ninf = -jnp.inf
    is_g = (lane_i >= MOE_EXPERTS) & (lane_i < MOE_EXPERTS + MOE_GROUPS)
    glog = jnp.where(is_g, logits, ninf)
    gmax = jnp.max(glog, axis=-1, keepdims=True)
    gidx = jnp.min(jnp.where(glog == gmax, lane - MOE_EXPERTS, 99.0), axis=-1, keepdims=True)
    pg_top = 1.0 / jnp.sum(jnp.exp(glog - gmax), axis=-1, keepdims=True)
    lane_grp = (lane_i >> 3).astype(F32)
    in_grp = (lane_i < MOE_EXPERTS) & (lane_grp == gidx)
    ev = jnp.where(in_grp, logits, ninf)
    v1 = jnp.max(ev, axis=-1, keepdims=True)
    i1 = jnp.min(jnp.where(ev == v1, lane, 999.0), axis=-1, keepdims=True)
    ev2 = jnp.where(lane == i1, ninf, ev)
    v2 = jnp.max(ev2, axis=-1, keepdims=True)
    i2 = jnp.min(jnp.where(ev2 == v2, lane, 999.0), axis=-1, keepdims=True)
    e2 = jnp.exp(v2 - v1)
    w1 = pg_top / (1.0 + e2)
    w2 = pg_top * e2 / (1.0 + e2)
    first_lo = i1 < i2
    w_lo = jnp.where(first_lo, w1, w2)
    w_hi = jnp.where(first_lo, w2, w1)
    a = jnp.minimum(i1, i2) - MOE_PER_GROUP * gidx
    b = jnp.maximum(i1, i2) - MOE_PER_GROUP * gidx
    cls = gidx * MOE_PAIRS + a * (15.0 - a) * 0.5 + (b - a - 1.0)

    onehot = lane == cls
    prefix = jnp.dot(tril_ref[...], onehot.astype(BF16), preferred_element_type=F32)
    run = run_ref[0:1, :]
    rank = jnp.sum(jnp.where(onehot, prefix - 1.0 + run, 0.0), axis=-1, keepdims=True)
    run_new = run + prefix[TM - 1:TM, :]
    run_ref[...] = jnp.broadcast_to(run_new, run_ref.shape)
    cnt_ref[...] = jnp.broadcast_to(run_new, cnt_ref.shape)

    route = cls * float(2 ** ROUTE_SHIFT) + rank
    meta = jnp.where(lane_i == LANE_WLO, w_lo,
                     jnp.where(lane_i == LANE_WHI, w_hi,
                               jnp.where(lane_i == LANE_ROUTE, route, 0.0)))
    xe_ref[:, D_MODEL:XE_W] = meta
    route_ref[...] = meta.T[0:8, :].astype(jnp.int32)


def _mix_out(a, w, res, g_moe, w_rt, b_rt, tril, a_transposed):
    n = res.shape[0]
    kdim = w.shape[0]
    row = lambda i: (i, 0)
    fixed = lambda i: (0, 0)
    a_spec = pl.BlockSpec((kdim, TM), lambda i: (0, i)) if a_transposed else pl.BlockSpec((TM, kdim), row)
    return pl.pallas_call(
        functools.partial(_mix_out_kernel, a_transposed=a_transposed),
        grid=(n // TM,),
        in_specs=[a_spec,
                  pl.BlockSpec((kdim, D_MODEL), fixed),
                  pl.BlockSpec((TM, D_MODEL), row),
                  pl.BlockSpec((1, D_MODEL), fixed),
                  pl.BlockSpec((D_MODEL, 128), fixed),
                  pl.BlockSpec((1, 128), fixed),
                  pl.BlockSpec((TM, TM), fixed)],
        out_specs=[pl.BlockSpec((TM, D_MODEL), row),
                   pl.BlockSpec((TM, XE_W), row),
                   pl.BlockSpec((8, TM), lambda i: (0, i)),
                   pl.BlockSpec((8, 128), fixed)],
        out_shape=[jax.ShapeDtypeStruct((n, D_MODEL), F32),
                   jax.ShapeDtypeStruct((n, XE_W), F32),
                   jax.ShapeDtypeStruct((8, n), jnp.int32),
                   jax.ShapeDtypeStruct((8, 128), F32)],
        scratch_shapes=[pltpu.VMEM((8, 128), F32)],
        compiler_params=_cparams("arbitrary"),
        name="mix_out",
    )(a, w, res, g_moe, w_rt, b_rt, tril)


def _sorted_row(route_ref, offs_ref, idx):
    r = route_ref[idx]
    return offs_ref[r >> ROUTE_SHIFT] + (r & (2 ** ROUTE_SHIFT - 1))


def _dispatch_kernel(route_ref, offs_ref, cnt_ref, nused_ref, xe_ref, xs_ref, zbuf, sem, zsem):
    i = pl.program_id(0)
    base = i * TM

    @pl.when(i == 0)
    def _():
        zbuf[...] = jnp.zeros_like(zbuf)

        def per_class(c, carry):
            cnt = cnt_ref[c]
            start = offs_ref[c] + cnt
            pad = (-cnt) & (ROW_TILE - 1)

            def fill(r, inner):
                pltpu.make_async_copy(zbuf.at[pl.ds(0, 1)], xs_ref.at[pl.ds(start + r, 1)], zsem).start()
                return inner

            def drain(r, inner):
                pltpu.make_async_copy(zbuf.at[pl.ds(0, 1)], xs_ref.at[pl.ds(0, 1)], zsem).wait()
                return inner

            lax.fori_loop(0, pad, fill, 0)
            lax.fori_loop(0, pad, drain, 0)
            return carry

        lax.fori_loop(0, MOE_CLASSES, per_class, 0)

        def tail(t, carry):
            row0 = pl.multiple_of(t * ROW_TILE, ROW_TILE)
            cp = pltpu.make_async_copy(zbuf, xs_ref.at[pl.ds(row0, ROW_TILE)], zsem)
            cp.start()
            cp.wait()
            return carry

        lax.fori_loop(nused_ref[0], xs_ref.shape[0] // ROW_TILE, tail, 0)

    def issue(t, carry):
        p = _sorted_row(route_ref, offs_ref, base + t)
        pltpu.make_async_copy(xe_ref.at[pl.ds(t, 1)], xs_ref.at[pl.ds(p, 1)], sem).start()
        return carry

    lax.fori_loop(0, TM, issue, 0)
    pltpu.make_async_copy(xe_ref, xs_ref.at[pl.ds(0, TM)], sem).wait()


def _dispatch(route, offs, cnt, n_used, xe, n_rows):
    n = xe.shape[0]
    grid_spec = pltpu.PrefetchScalarGridSpec(
        num_scalar_prefetch=4,
        grid=(n // TM,),
        in_specs=[pl.BlockSpec((TM, XE_W), lambda i, *_: (i, 0))],
        out_specs=pl.BlockSpec(memory_space=pl.ANY),
        scratch_shapes=[pltpu.VMEM((ROW_TILE, XE_W), F32), pltpu.SemaphoreType.DMA(()),
                        pltpu.SemaphoreType.DMA(())],
    )
    return pl.pallas_call(
        _dispatch_kernel,
        grid_spec=grid_spec,
        out_shape=jax.ShapeDtypeStruct((n_rows, XE_W), F32),
        compiler_params=_cparams("arbitrary"),
        name="moe_dispatch",
    )(route, offs, cnt, n_used, xe)


def _experts_kernel(tlo_ref, thi_ref, nused_ref, xs_ref, wg_lo, wu_lo, wd_lo, wg_hi, wu_hi, wd_hi, y_ref):
    del tlo_ref, thi_ref

    @pl.when(pl.program_id(0) < nused_ref[0])
    def _():
        x = xs_ref[:, 0:D_MODEL].astype(BF16)

        def ffn(wg, wu, wd, w):
            a = jnp.dot(x, wg[0], preferred_element_type=F32)
            u = jnp.dot(x, wu[0], preferred_element_type=F32)
            hid = (a * _sigmoid(a)) * u * w
            return jnp.dot(hid.astype(BF16), wd[0], preferred_element_type=F32)

        w_lo = xs_ref[:, D_MODEL + LANE_WLO:D_MODEL + LANE_WLO + 1]
        w_hi = xs_ref[:, D_MODEL + LANE_WHI:D_MODEL + LANE_WHI + 1]
        y_ref[...] = ffn(wg_lo, wu_lo, wd_lo, w_lo) + ffn(wg_hi, wu_hi, wd_hi, w_hi)

    @pl.when(pl.program_id(0) >= nused_ref[0])
    def _():
        y_ref[...] = jnp.zeros_like(y_ref)


def _experts(tile_lo, tile_hi, n_used, xs, w_gate, w_up, w_down):
    n_tiles = xs.shape[0] // ROW_TILE
    rows = lambda i, tlo, thi, nu: (jnp.maximum(jnp.minimum(i, nu[0] - 1), 0), 0)
    lo = lambda i, tlo, thi, nu: (tlo[i], 0, 0)
    hi = lambda i, tlo, thi, nu: (thi[i], 0, 0)
    up_spec = lambda m: pl.BlockSpec((1, D_MODEL, MOE_HIDDEN), m)
    dn_spec = lambda m: pl.BlockSpec((1, MOE_HIDDEN, D_MODEL), m)
    grid_spec = pltpu.PrefetchScalarGridSpec(
        num_scalar_prefetch=3,
        grid=(n_tiles,),
        in_specs=[pl.BlockSpec((ROW_TILE, XE_W), rows),
                  up_spec(lo), up_spec(lo), dn_spec(lo),
                  up_spec(hi), up_spec(hi), dn_spec(hi)],
        out_specs=pl.BlockSpec((ROW_TILE, D_MODEL), lambda i, tlo, thi, nu: (i, 0)),
    )
    return pl.pallas_call(
        _experts_kernel,
        grid_spec=grid_spec,
        out_shape=jax.ShapeDtypeStruct((xs.shape[0], D_MODEL), F32),
        compiler_params=_cparams("arbitrary"),
        name="moe_experts",
    )(tile_lo, tile_hi, n_used, xs, w_gate, w_up, w_down, w_gate, w_up, w_down)


def _combine_kernel(route_ref, offs_ref, h_ref, y_ref, g_ref, out_ref, buf, sem, *, final_norm):
    base = pl.program_id(0) * TM

    def issue(t, carry):
        p = _sorted_row(route_ref, offs_ref, base + t)
        pltpu.make_async_copy(y_ref.at[pl.ds(p, 1)], buf.at[pl.ds(t, 1)], sem).start()
        return carry

    lax.fori_loop(0, TM, issue, 0)
    pltpu.make_async_copy(y_ref.at[pl.ds(0, TM)], buf, sem).wait()
    out = h_ref[...] + buf[...]
    if final_norm:
        out = _rms(out) * g_ref[...]
    out_ref[...] = out


def _combine(route, offs, h, y, gain, final_norm):
    n = h.shape[0]
    grid_spec = pltpu.PrefetchScalarGridSpec(
        num_scalar_prefetch=2,
        grid=(n // TM,),
        in_specs=[pl.BlockSpec((TM, D_MODEL), lambda i, *_: (i, 0)),
                  pl.BlockSpec(memory_space=pl.ANY),
                  pl.BlockSpec((1, D_MODEL), lambda i, *_: (0, 0))],
        out_specs=pl.BlockSpec((TM, D_MODEL), lambda i, *_: (i, 0)),
        scratch_shapes=[pltpu.VMEM((TM, D_MODEL), F32), pltpu.SemaphoreType.DMA(())],
    )
    return pl.pallas_call(
        functools.partial(_combine_kernel, final_norm=final_norm),
        grid_spec=grid_spec,
        out_shape=jax.ShapeDtypeStruct((n, D_MODEL), F32),
        compiler_params=_cparams("arbitrary"),
        name="moe_combine",
    )(route, offs, h, y, gain)


def _moe_layer(a, w_out, res, moe_norm, w_group, b_group, w_router, b_router, w_gate, w_up, w_down,
               tril, out_gain, final_norm, a_transposed=False):
    n = res.shape[0]
    w_rt = jnp.zeros((D_MODEL, 128), F32).at[:, 0:MOE_EXPERTS].set(w_router)
    w_rt = w_rt.at[:, MOE_EXPERTS:MOE_EXPERTS + MOE_GROUPS].set(w_group)
    b_rt = jnp.zeros((1, 128), F32).at[0, 0:MOE_EXPERTS].set(b_router)
    b_rt = b_rt.at[0, MOE_EXPERTS:MOE_EXPERTS + MOE_GROUPS].set(b_group)
    h, xe, route8, counts = _mix_out(a, w_out, res, moe_norm.reshape(1, D_MODEL), w_rt, b_rt, tril, a_transposed)

    n_tiles = n // ROW_TILE + MOE_CLASSES
    cnt = counts[0].astype(jnp.int32)
    tiles_c = (cnt + ROW_TILE - 1) // ROW_TILE
    tile_end = jnp.cumsum(tiles_c)
    offs = (tile_end - tiles_c) * ROW_TILE
    n_used = tile_end[-1]
    tile_ids = jnp.minimum(jnp.arange(n_tiles, dtype=jnp.int32), n_used - 1)
    tile_cls = jnp.sum((tile_end[None, 0:MOE_CLASSES] <= tile_ids[:, None]).astype(jnp.int32), axis=1)
    tile_cls = jnp.clip(tile_cls, 0, MOE_CLASSES - 1)
    tile_lo = jnp.asarray(_PAIR_LO)[tile_cls]
    tile_hi = jnp.asarray(_PAIR_HI)[tile_cls]
    route = route8[LANE_ROUTE]

    n_used = n_used.reshape(1)
    xs = _dispatch(route, offs, cnt, n_used, xe, n_tiles * ROW_TILE)
    y = _experts(tile_lo, tile_hi, n_used, xs, w_gate.astype(BF16), w_up.astype(BF16),
                 w_down.astype(BF16))
    return _combine(route, offs, h, y, out_gain.reshape(1, D_MODEL), final_norm)


def _rope_tables(pos, width):
    half = ROPE_DIM // 2
    inv_freq = jnp.power(jnp.float32(ROPE_THETA), -jnp.arange(half, dtype=F32) * (2.0 / ROPE_DIM))
    ang = pos.astype(F32)[:, None] * inv_freq[None, :]
    cos, sin = jnp.cos(ang), jnp.sin(ang)
    t = pos.shape[0]
    rest = NSA_HEAD_DIM - ROPE_DIM
    cos_t = jnp.concatenate([cos, cos, jnp.ones((t, rest), F32)], axis=1)
    sin_a = jnp.concatenate([-sin, jnp.zeros((t, half + rest), F32)], axis=1)
    sin_b = jnp.concatenate([jnp.zeros((t, half), F32), sin, jnp.zeros((t, rest), F32)], axis=1)
    rep = width // NSA_HEAD_DIM
    return jnp.tile(cos_t, (1, rep)), jnp.tile(sin_a, (1, rep)), jnp.tile(sin_b, (1, rep))


def _rope(x, cos_t, sin_a, sin_b):
    half = ROPE_DIM // 2
    parts = []
    for c in range(x.shape[1] // 128):
        xc = x[:, c * 128:(c + 1) * 128]
        parts.append(xc * cos_t + pltpu.roll(xc, 128 - half, axis=1) * sin_a + pltpu.roll(xc, half, axis=1) * sin_b)
    return parts[0] if len(parts) == 1 else jnp.concatenate(parts, axis=1)


def _rope_rows(x, cos_t, sin_a, sin_b):
    half = ROPE_DIM // 2
    reps = x.shape[0] // NSA_HEAD_DIM
    tile = lambda t: jnp.concatenate([t] * reps, axis=0)
    up = jnp.concatenate([x[half:], x[:half]], axis=0)
    down = jnp.concatenate([x[-half:], x[:-half]], axis=0)
    return x * tile(cos_t) + up * tile(sin_a) + down * tile(sin_b)


def _nsa_proj_kernel(h_ref, gq_ref, gkv_ref, wqt_ref, wkv_ref, wvt_ref, cos_ref, sa_ref, sb_ref,
                     cost_ref, sat_ref, sbt_ref,
                     qt_ref, gatet_ref, kc0_ref, kc1_ref, vc0_ref, vc1_ref, ks_ref, vst_ref, kw_ref, vwt_ref):
    r = _rms(h_ref[...])
    hq = (r * gq_ref[...]).astype(BF16)
    hk = (r * gkv_ref[...]).astype(BF16)
    qt = lax.dot_general(wqt_ref[0:1024, :], hq, _NT, preferred_element_type=F32)
    qt_ref[...] = (_rope_rows(qt, cost_ref[...], sat_ref[...], sbt_ref[...]) * (NSA_HEAD_DIM ** -0.5)).astype(BF16)
    gatet_ref[...] = _sigmoid(lax.dot_general(wqt_ref[1024:1152, :], hq, _NT, preferred_element_type=F32))
    kc0_ref[...] = jnp.dot(hk, wkv_ref[:, 0:128], preferred_element_type=F32)
    kc1_ref[...] = jnp.dot(hk, wkv_ref[:, 128:256], preferred_element_type=F32)
    vc0_ref[...] = jnp.dot(hk, wkv_ref[:, 256:384], preferred_element_type=F32)
    vc1_ref[...] = jnp.dot(hk, wkv_ref[:, 384:512], preferred_element_type=F32)

    tm = h_ref.shape[0]
    cos_t, sin_a, sin_b = cos_ref[...], sa_ref[...], sb_ref[...]
    lane = lax.broadcasted_iota(jnp.int32, (tm, NSA_HEAD_DIM), 1)
    pos = (pl.program_id(0) % (SEQ // tm)) * tm + lax.broadcasted_iota(jnp.int32, (tm, NSA_HEAD_DIM), 0)
    blk_onehot = jnp.where(lane == (pos >> 6), 1.0, 0.0).astype(BF16)
    zeros = jnp.zeros((tm, NSA_HEAD_DIM), BF16)

    def store_keys(ref, val, aux):
        for g in range(NSA_GROUPS):
            ref[0, g] = jnp.concatenate([val[:, g * 64:(g + 1) * 64].astype(BF16), aux], axis=1)

    store_keys(ks_ref, _rope(jnp.dot(hk, wkv_ref[:, 512:768], preferred_element_type=F32), cos_t, sin_a, sin_b),
               blk_onehot)
    store_keys(kw_ref, _rope(jnp.dot(hk, wkv_ref[:, 1024:1280], preferred_element_type=F32), cos_t, sin_a, sin_b),
               zeros)

    row = lax.broadcasted_iota(jnp.int32, (NSA_HEAD_DIM, tm), 0)
    ones_row = jnp.where(row == 0, 1.0, 0.0).astype(BF16)

    def store_values(ref, val_t):
        for g in range(NSA_GROUPS):
            ref[0, g] = jnp.concatenate([val_t[g * 64:(g + 1) * 64, :].astype(BF16), ones_row], axis=0)

    store_values(vst_ref, lax.dot_general(wvt_ref[0:256, :], hk, _NT, preferred_element_type=F32))
    store_values(vwt_ref, lax.dot_general(wvt_ref[256:512, :], hk, _NT, preferred_element_type=F32))


def _nsa_proj(h, g_q, g_kv, w_qt, w_kv, w_vt, rope_tabs, rope_tabs_t, batch):
    n = h.shape[0]
    nblk = SEQ // TM
    row = lambda i: (i, 0)
    col = lambda i: (0, i)
    fixed = lambda i: (0, 0)
    tab = lambda i: (i % nblk, 0)
    tab_t = lambda i: (0, i % nblk)
    key_spec = pl.BlockSpec((1, NSA_GROUPS, TM, KV_LANES), lambda i: (i // nblk, 0, i % nblk, 0))
    key_shape = jax.ShapeDtypeStruct((batch, NSA_GROUPS, SEQ, KV_LANES), BF16)
    val_spec = pl.BlockSpec((1, NSA_GROUPS, KV_LANES, TM), lambda i: (i // nblk, 0, 0, i % nblk))
    val_shape = jax.ShapeDtypeStruct((batch, NSA_GROUPS, KV_LANES, SEQ), BF16)
    raw_spec = pl.BlockSpec((TM, 128), row)
    raw_shape = jax.ShapeDtypeStruct((n, 128), F32)
    return pl.pallas_call(
        _nsa_proj_kernel,
        grid=(n // TM,),
        in_specs=[pl.BlockSpec((TM, D_MODEL), row),
                  pl.BlockSpec((1, D_MODEL), fixed),
                  pl.BlockSpec((1, D_MODEL), fixed),
                  pl.BlockSpec((1152, D_MODEL), fixed),
                  pl.BlockSpec((D_MODEL, 1536), fixed),
                  pl.BlockSpec((512, D_MODEL), fixed),
                  pl.BlockSpec((TM, 128), tab), pl.BlockSpec((TM, 128), tab), pl.BlockSpec((TM, 128), tab),
                  pl.BlockSpec((NSA_HEAD_DIM, TM), tab_t), pl.BlockSpec((NSA_HEAD_DIM, TM), tab_t),
                  pl.BlockSpec((NSA_HEAD_DIM, TM), tab_t)],
        out_specs=[pl.BlockSpec((1024, TM), col),
                   pl.BlockSpec((128, TM), col),
                   raw_spec, raw_spec, raw_spec, raw_spec,
                   key_spec, val_spec, key_spec, val_spec],
        out_shape=[jax.ShapeDtypeStruct((1024, n), BF16),
                   jax.ShapeDtypeStruct((128, n), F32),
                   raw_shape, raw_shape, raw_shape, raw_shape,
                   key_shape, val_shape, key_shape, val_shape],
        compiler_params=_cparams("parallel"),
        name="nsa_proj",
    )(h, g_q, g_kv, w_qt, w_kv, w_vt, *rope_tabs, *rope_tabs_t)


HALF_BLOCKS = SEQ // CMP_STRIDE
CMP_K = CMP_STRIDE * 256


def _compress_kernel(rk0_ref, rk1_ref, rv0_ref, rv1_ref, w1k_ref, w1v_ref, pek_ref, pev_ref, w2k_ref, w2v_ref,
                     cos_ref, sa_ref, sb_ref, kc_ref, vc_ref):
    lane = lax.broadcasted_iota(jnp.int32, (HALF_BLOCKS, 512), 1)
    first_half = (lane & 127) < 64

    def comp(raw_refs, w1_ref, pe_ref, w2_ref):
        x = jnp.concatenate([r[pl.ds(l, HALF_BLOCKS, stride=CMP_STRIDE), :]
                             for l in range(CMP_STRIDE) for r in raw_refs], axis=1).astype(BF16)
        r = jnp.dot(x, w1_ref[...], preferred_element_type=F32)
        rpe = jnp.dot(pe_ref[...].astype(BF16), w1_ref[...], preferred_element_type=F32)
        r = r + jnp.where(first_half, rpe[0:1, :], rpe[1:2, :])
        nxt = pltpu.roll(r, HALF_BLOCKS - 1, axis=0)
        nxt = jnp.concatenate([pltpu.roll(nxt[:, c * 128:(c + 1) * 128], 64, axis=1) for c in range(4)], axis=1)
        pre = r + nxt
        act = pre * _sigmoid(pre)
        return jnp.dot(act.astype(BF16), w2_ref[...], preferred_element_type=F32)

    kc = comp((rk0_ref, rk1_ref), w1k_ref, pek_ref, w2k_ref)
    kc = _rope(kc, cos_ref[...], sa_ref[...], sb_ref[...])
    vc = comp((rv0_ref, rv1_ref), w1v_ref, pev_ref, w2v_ref)
    for g in range(NSA_GROUPS):
        kc_ref[0, g] = kc[:, g * 64:(g + 1) * 64]
        vc_ref[0, g] = vc[:, g * 64:(g + 1) * 64]


def _compress_weights(pe, w1, w2):
    dh = NSA_HEAD_DIM
    w1r = w1.reshape(2, CMP_STRIDE, dh, dh)
    big = jnp.zeros((CMP_STRIDE, NSA_GROUPS, dh, NSA_GROUPS, 2, dh), F32)
    for g in range(NSA_GROUPS):
        big = big.at[:, g, :, g, 0, :].set(w1r[0])
        big = big.at[:, g, :, g, 1, :].set(w1r[1])
    w1_big = big.reshape(CMP_K, 512).astype(BF16)
    pe_rows = jnp.zeros((8, CMP_STRIDE, NSA_GROUPS, dh), F32)
    pe_rows = pe_rows.at[0].set(jnp.broadcast_to(pe[0:CMP_STRIDE, None, :], (CMP_STRIDE, NSA_GROUPS, dh)))
    pe_rows = pe_rows.at[1].set(jnp.broadcast_to(pe[CMP_STRIDE:, None, :], (CMP_STRIDE, NSA_GROUPS, dh)))
    w2_bd = jnp.zeros((NSA_GROUPS, 2, dh, NSA_GROUPS, dh), F32)
    for g in range(NSA_GROUPS):
        w2_bd = w2_bd.at[g, 0, :, g, :].set(w2)
    return w1_big, pe_rows.reshape(8, CMP_K), w2_bd.reshape(512, 256).astype(BF16)


def _compress(raw_k, raw_v, wk, wv, cmp_tabs, batch):
    fixed = lambda b: (0, 0)
    raw_spec = pl.BlockSpec((SEQ, 128), lambda b: (b, 0))
    out_spec = pl.BlockSpec((1, NSA_GROUPS, N_CMP_PAD, NSA_HEAD_DIM), lambda b: (b, 0, 0, 0))
    out_shape = jax.ShapeDtypeStruct((batch, NSA_GROUPS, N_CMP_PAD, NSA_HEAD_DIM), F32)
    return pl.pallas_call(
        _compress_kernel,
        grid=(batch,),
        in_specs=[raw_spec, raw_spec, raw_spec, raw_spec,
                  pl.BlockSpec((CMP_K, 512), fixed), pl.BlockSpec((CMP_K, 512), fixed),
                  pl.BlockSpec((8, CMP_K), fixed), pl.BlockSpec((8, CMP_K), fixed),
                  pl.BlockSpec((512, 256), fixed), pl.BlockSpec((512, 256), fixed),
                  pl.BlockSpec((N_CMP_PAD, 128), fixed), pl.BlockSpec((N_CMP_PAD, 128), fixed),
                  pl.BlockSpec((N_CMP_PAD, 128), fixed)],
        out_specs=[out_spec, out_spec],
        out_shape=[out_shape, out_shape],
        compiler_params=_cparams("parallel"),
        name="nsa_compress",
    )(*raw_k, *raw_v, wk[0], wv[0], wk[1], wv[1], wk[2], wv[2], *cmp_tabs)


def _nsa_attn_kernel(qt_ref, gatet_ref, kc_ref, vc_ref, ks_ref, vst_ref, kw_ref, vwt_ref, ovl_ref, out_ref,
                     qx_sc, ms_sc, as_sc, aw_sc):
    g = pl.program_id(1)
    qb = pl.program_id(2)
    q0 = qb * Q_BLOCK
    Q, HG, DH = Q_BLOCK, NSA_GROUP_SIZE, NSA_HEAD_DIM

    def heads(x):
        return jnp.concatenate([x] * HG, axis=1)

    q4 = qt_ref[...]
    q_t = jnp.concatenate([q4[h * DH:(h + 1) * DH, :] for h in range(HG)], axis=1)

    n_row = lax.broadcasted_iota(jnp.int32, (N_CMP_PAD, Q), 0)
    t_lane = q0 + lax.broadcasted_iota(jnp.int32, (N_CMP_PAD, Q), 1)
    valid_c = heads((n_row * CMP_STRIDE + CMP_BLOCK - 1 <= t_lane) & (n_row < N_CMP_PAD - 1))
    s_c = jnp.dot(kc_ref[0, 0].astype(BF16), q_t, preferred_element_type=F32)
    s_c = jnp.where(valid_c, s_c, NEG_INF)
    m_c = jnp.max(s_c, axis=0, keepdims=True)
    e_c = jnp.where(valid_c, jnp.exp(s_c - m_c), 0.0)
    l_c = jnp.sum(e_c, axis=0, keepdims=True)
    p_c = e_c * jnp.where(l_c > 0.0, 1.0 / l_c, 0.0)
    o_c = jnp.dot(vc_ref[0, 0].T.astype(BF16), p_c.astype(BF16), preferred_element_type=F32)

    p_sum = p_c[:, 0:Q] + p_c[:, Q:2 * Q] + p_c[:, 2 * Q:3 * Q] + p_c[:, 3 * Q:4 * Q]
    imp_t = jnp.dot(ovl_ref[...], p_sum, precision=HIGHEST, preferred_element_type=F32)[0:N_SEL]
    j_row = lax.broadcasted_iota(jnp.int32, (N_SEL, Q), 0)
    cur = (q0 + lax.broadcasted_iota(jnp.int32, (N_SEL, Q), 1)) >> 6
    forced = (j_row == 0) | (j_row == cur) | (j_row == cur - 1)
    imp_t = jnp.where(forced, FORCED_SCORE, imp_t)
    imp_t = jnp.where(j_row > cur, NEG_INF, imp_t)
    cnt = jnp.zeros((N_SEL, Q), F32)
    for i in range(N_SEL):
        ri = imp_t[i:i + 1, :]
        cnt = cnt + jnp.where(ri > imp_t, 1.0, jnp.where((ri == imp_t) & (j_row > i), 1.0, 0.0))
    sel_bias = jnp.where((cnt < SEL_TOPK) & (j_row <= cur), 0.0, NEG_INF).astype(BF16)

    qx_sc[...] = jnp.concatenate([q_t, heads(sel_bias), jnp.zeros((KV_LANES - DH - N_SEL, HG * Q), BF16)], axis=0)

    def finish(acc):
        return acc[0:DH] / acc[DEN_ROW:DEN_ROW + 1]

    ms_sc[...] = jnp.full(ms_sc.shape, NEG_INF, F32)
    as_sc[...] = jnp.zeros(as_sc.shape, F32)
    n_kt = (q0 + Q + SLC_TILE - 1) // SLC_TILE

    def slc_tile(kt, bias):
        start = pl.multiple_of(kt * SLC_TILE, SLC_TILE)
        s = jnp.dot(ks_ref[0, 0, pl.ds(start, SLC_TILE), :], qx_sc[...], preferred_element_type=F32)
        if bias is not None:
            s = s + heads(bias)
        m_old = ms_sc[0:1, :]
        m_new = jnp.maximum(m_old, jnp.max(s, axis=0, keepdims=True))
        p = jnp.exp(s - m_new).astype(BF16)
        pv = jnp.dot(vst_ref[0, 0, :, pl.ds(start, SLC_TILE)], p, preferred_element_type=F32)
        as_sc[...] = as_sc[...] * jnp.exp(m_old - m_new) + pv
        ms_sc[...] = jnp.broadcast_to(m_new, ms_sc.shape)

    def slc_full(kt, carry):
        slc_tile(kt, None)
        return carry

    lax.fori_loop(0, n_kt - 1, slc_full, 0)
    c_s = lax.broadcasted_iota(jnp.int32, (SLC_TILE, Q), 0)
    r_s = lax.broadcasted_iota(jnp.int32, (SLC_TILE, Q), 1)
    slc_tile(n_kt - 1, jnp.where((n_kt - 1) * SLC_TILE + c_s <= q0 + r_s, 0.0, NEG_INF))

    c_w = lax.broadcasted_iota(jnp.int32, (Q, Q), 0)
    r_w = lax.broadcasted_iota(jnp.int32, (Q, Q), 1)
    n_slab = WIN_KEYS // Q

    def window(start, slab_bias):
        start = pl.multiple_of(start, Q)
        s = jnp.dot(kw_ref[0, 0, pl.ds(start, WIN_KEYS), :], qx_sc[...], preferred_element_type=F32)
        slabs = []
        for j in range(n_slab):
            sj = s[j * Q:(j + 1) * Q]
            slabs.append(sj if slab_bias[j] is None else sj + heads(slab_bias[j]))
        top = slabs[0]
        for sj in slabs[1:]:
            top = jnp.maximum(top, sj)
        m = jnp.max(top, axis=0, keepdims=True)
        p = jnp.concatenate([jnp.exp(sj - m) for sj in slabs], axis=0).astype(BF16)
        aw_sc[...] = jnp.dot(vwt_ref[0, 0, :, pl.ds(start, WIN_KEYS)], p, preferred_element_type=F32)

    @pl.when(qb >= WINDOW // Q)
    def _():
        window(q0 - WINDOW, [jnp.where(c_w > r_w, 0.0, NEG_INF)] + [None] * (n_slab - 2)
               + [jnp.where(c_w <= r_w, 0.0, NEG_INF)])

    @pl.when(qb < WINDOW // Q)
    def _():
        window(0, [jnp.where(j * Q + c_w <= q0 + r_w, 0.0, NEG_INF) for j in range(n_slab)])

    def gate_row(branch):
        return jnp.concatenate([gatet_ref[pl.ds((g * HG + h) * 3 + branch, 1), :] for h in range(HG)], axis=1)

    o = gate_row(0) * o_c + gate_row(1) * finish(as_sc[...]) + gate_row(2) * finish(aw_sc[...])
    out_ref[...] = jnp.concatenate([o[:, h * Q:(h + 1) * Q] for h in range(HG)], axis=0).astype(BF16)


def _overlap_t():
    n = np.arange(N_CMP_PAD)
    j = np.arange(128)
    cmp_start = n * CMP_STRIDE
    cmp_end = cmp_start + CMP_BLOCK - 1
    sel_start = j * SEL_BLOCK
    ovl = ((cmp_start[None, :] <= sel_start[:, None] + SEL_BLOCK - 1) & (cmp_end[None, :] >= sel_start[:, None])
           & (j[:, None] < N_SEL) & (n[None, :] < N_CMP_PAD - 1))
    return jnp.asarray(ovl.astype(np.float32))


def _nsa_attn(q_t, gates_t, kc, vc, ks, vs_t, kw, vw_t, batch):
    n = q_t.shape[1]
    qcol = lambda b, g, i: (g, b * N_QB + i)
    gcol = lambda b, g, i: (0, b * N_QB + i)
    kv = lambda b, g, i: (b, g, 0, 0)
    lanes = NSA_GROUP_SIZE * Q_BLOCK
    return pl.pallas_call(
        _nsa_attn_kernel,
        grid=(batch, NSA_GROUPS, N_QB),
        in_specs=[pl.BlockSpec((NSA_GROUP_SIZE * NSA_HEAD_DIM, Q_BLOCK), qcol),
                  pl.BlockSpec((128, Q_BLOCK), gcol),
                  pl.BlockSpec((1, 1, N_CMP_PAD, NSA_HEAD_DIM), kv),
                  pl.BlockSpec((1, 1, N_CMP_PAD, NSA_HEAD_DIM), kv),
                  pl.BlockSpec((1, 1, SEQ, KV_LANES), kv),
                  pl.BlockSpec((1, 1, KV_LANES, SEQ), kv),
                  pl.BlockSpec((1, 1, SEQ, KV_LANES), kv),
                  pl.BlockSpec((1, 1, KV_LANES, SEQ), kv),
                  pl.BlockSpec((128, N_CMP_PAD), lambda b, g, i: (0, 0))],
        out_specs=pl.BlockSpec((NSA_GROUP_SIZE * NSA_HEAD_DIM, Q_BLOCK), qcol),
        out_shape=jax.ShapeDtypeStruct((NSA_HEADS * NSA_HEAD_DIM, n), BF16),
        scratch_shapes=[pltpu.VMEM((KV_LANES, lanes), BF16),
                        pltpu.VMEM((8, lanes), F32),
                        pltpu.VMEM((KV_LANES, lanes), F32),
                        pltpu.VMEM((KV_LANES, lanes), F32)],
        compiler_params=_cparams("parallel", "parallel", "arbitrary"),
        name="nsa_attn",
    )(q_t, gates_t, kc, vc, ks, vs_t, kw, vw_t, _overlap_t())


def kernel(x, mlstm_norm, mlstm_w_in, mlstm_gate_bias, mlstm_head_norm, mlstm_w_out, kv_norm, kv_w, cmp_pe_k, cmp_w1_k, cmp_w2_k, cmp_pe_v, cmp_w1_v, cmp_w2_v, nsa_norm, nsa_w_q, nsa_w_out, moe_norm, moe_w_group, moe_b_group, moe_w_router, moe_b_router, moe_w_gate, moe_w_up, moe_w_down, final_norm):
    batch, seq, d = x.shape
    assert seq == SEQ and d == D_MODEL
    assert mlstm_norm.shape[0] == 1 and nsa_norm.shape[0] == 1 and moe_norm.shape[0] == 2
    n = batch * seq
    x2d = x.reshape(n, d)
    tril = jnp.tril(jnp.ones((TM, TM), F32)).astype(BF16)

    w_in = mlstm_w_in[0]
    w_gate = jnp.zeros((d, 128), F32).at[:, 0:8].set(w_in[:, 3072:3080])
    b_gate = jnp.zeros((1, 128), F32).at[0, 0:8].set(mlstm_gate_bias[0])
    q, k, v, o, gr = _mlstm_in(x2d, mlstm_norm[0].reshape(1, d), w_in[:, 0:3072].astype(BF16), w_gate, b_gate)
    hs = _mlstm_scan(q, k, v, o, gr, mlstm_head_norm[0].reshape(1, d), batch)
    h = _moe_layer(hs, mlstm_w_out[0].astype(BF16), x2d, moe_norm[0], moe_w_group[0], moe_b_group[0],
                   moe_w_router[0], moe_b_router[0], moe_w_gate[0], moe_w_up[0], moe_w_down[0],
                   tril, final_norm, False)

    w_qt = jnp.zeros((1152, d), F32).at[0:1072].set(nsa_w_q[0].T).astype(BF16)
    w_vt = jnp.concatenate([kv_w[:, 768:1024], kv_w[:, 1280:1536]], axis=1).T.astype(BF16)
    seq_tabs = _rope_tables(jnp.arange(SEQ), 128)
    seq_tabs_t = tuple(t[:, 0:NSA_HEAD_DIM].T for t in seq_tabs)
    q_t, gates_t, rk0, rk1, rv0, rv1, ks, vs_t, kw, vw_t = _nsa_proj(
        h, nsa_norm[0].reshape(1, d), kv_norm.reshape(1, d), w_qt, kv_w.astype(BF16), w_vt, seq_tabs, seq_tabs_t,
        batch)
    cmp_pos = jnp.arange(N_CMP_PAD) * CMP_STRIDE + CMP_BLOCK - 1
    kc, vc = _compress((rk0, rk1), (rv0, rv1), _compress_weights(cmp_pe_k, cmp_w1_k, cmp_w2_k),
                       _compress_weights(cmp_pe_v, cmp_w1_v, cmp_w2_v), _rope_tables(cmp_pos, 128), batch)
    att_t = _nsa_attn(q_t, gates_t, kc, vc, ks, vs_t, kw, vw_t, batch)
    out = _moe_layer(att_t, nsa_w_out[0].astype(BF16), h, moe_norm[1], moe_w_group[1], moe_b_group[1],
                     moe_w_router[1], moe_b_router[1], moe_w_gate[1], moe_w_up[1], moe_w_down[1],
                     tril, final_norm, True, a_transposed=True)
    return out.reshape(batch, seq, d)
```

```python
import functools

import numpy as np
import jax
import jax.numpy as jnp
from jax import lax
from jax.experimental import pallas as pl
from jax.experimental.pallas import tpu as pltpu

F32 = jnp.float32
BF16 = jnp.bfloat16
HIGHEST = lax.Precision.HIGHEST

D_MODEL = 1024
SEQ = 2048
RMS_EPS = 1e-6
NEG_INF = -1e30

MLSTM_HEADS = 4
MLSTM_V_DIM = 256
MLSTM_QK_DIM = 128
MLSTM_L = 256
GATE_SOFTCAP = 15.0

NSA_HEADS = 16
NSA_HEAD_DIM = 64
NSA_GROUPS = 4
NSA_GROUP_SIZE = 4
CMP_BLOCK = 32
CMP_STRIDE = 16
N_CMP_PAD = 128
SEL_BLOCK = 64
N_SEL = SEQ // SEL_BLOCK
SEL_TOPK = 16
WINDOW = 512
Q_BLOCK = 256
N_QB = SEQ // Q_BLOCK
WIN_KEYS = WINDOW + Q_BLOCK
SLC_TILE = 256
ATTN_GROUPS = 2
KV_LANES = 2 * NSA_HEAD_DIM
DEN_ROW = NSA_HEAD_DIM
FORCED_SCORE = 1e6
ROPE_THETA = 500000.0
ROPE_DIM = 16

MOE_GROUPS = 4
MOE_PER_GROUP = 8
MOE_EXPERTS = 32
MOE_HIDDEN = 256
MOE_PAIRS = 28
MOE_CLASSES = MOE_GROUPS * MOE_PAIRS
ROW_TILE = 128
XE_W = D_MODEL + 128
LANE_WLO, LANE_WHI, LANE_ROUTE = 0, 1, 2
ROUTE_SHIFT = 16
ISSUE_UNROLL = 8

TM = 512
VMEM_LIMIT = 56 * 1024 * 1024

_NT = (((1,), (1,)), ((), ()))
_TN = (((0,), (0,)), ((), ()))


def _cparams(*sem):
    return pltpu.CompilerParams(dimension_semantics=sem, vmem_limit_bytes=VMEM_LIMIT)


def _rms(x):
    return x * lax.rsqrt(jnp.mean(x * x, axis=-1, keepdims=True) + RMS_EPS)


def _sigmoid(x):
    return 1.0 / (1.0 + jnp.exp(-x))


def _split_bf16(x):
    hi = x.astype(BF16)
    return hi, (x - hi.astype(F32)).astype(BF16)


def _dot_split(x, w_hi, w_lo):
    x_hi, x_lo = _split_bf16(x)
    return (jnp.dot(x_hi, w_hi, preferred_element_type=F32) + jnp.dot(x_hi, w_lo, preferred_element_type=F32)
            + jnp.dot(x_lo, w_hi, preferred_element_type=F32))


def _mlstm_in_kernel(x_ref, g_ref, w_ref, wgh_ref, wgl_ref, bg_ref, q_ref, k_ref, v_ref, o_ref, gr_ref):
    hn = _rms(x_ref[...]) * g_ref[...]
    hb = hn.astype(BF16)
    q_ref[...] = jnp.dot(hb, w_ref[:, 0:512], preferred_element_type=F32).astype(BF16)
    k = jnp.dot(hb, w_ref[:, 512:1024], preferred_element_type=F32)
    k_ref[...] = (k * (MLSTM_QK_DIM ** -0.5)).astype(BF16)
    v_ref[...] = jnp.dot(hb, w_ref[:, 1024:2048], preferred_element_type=F32).astype(BF16)
    o_ref[...] = _sigmoid(jnp.dot(hb, w_ref[:, 2048:3072], preferred_element_type=F32))
    gates = _dot_split(hn, wgh_ref[...], wgl_ref[...]) + bg_ref[...]
    gates = GATE_SOFTCAP * jnp.tanh(gates / GATE_SOFTCAP)
    lane = lax.broadcasted_iota(jnp.int32, gates.shape, 1)
    log_f = jnp.minimum(gates, 0.0) - jnp.log1p(jnp.exp(-jnp.abs(gates)))
    lg = jnp.where(lane < MLSTM_HEADS, gates, log_f)
    gr_ref[...] = lg.T[0:8, :]


def _mlstm_in(x2d, norm_g, w_qkvo, w_gate, b_gate):
    n = x2d.shape[0]
    w_gate_hi, w_gate_lo = _split_bf16(w_gate)
    row = lambda i: (i, 0)
    fixed = lambda i: (0, 0)
    return pl.pallas_call(
        _mlstm_in_kernel,
        grid=(n // TM,),
        in_specs=[pl.BlockSpec((TM, D_MODEL), row),
                  pl.BlockSpec((1, D_MODEL), fixed),
                  pl.BlockSpec((D_MODEL, 3072), fixed),
                  pl.BlockSpec((D_MODEL, 128), fixed),
                  pl.BlockSpec((D_MODEL, 128), fixed),
                  pl.BlockSpec((1, 128), fixed)],
        out_specs=[pl.BlockSpec((TM, 512), row),
                   pl.BlockSpec((TM, 512), row),
                   pl.BlockSpec((TM, 1024), row),
                   pl.BlockSpec((TM, 1024), row),
                   pl.BlockSpec((8, TM), lambda i: (0, i))],
        out_shape=[jax.ShapeDtypeStruct((n, 512), BF16),
                   jax.ShapeDtypeStruct((n, 512), BF16),
                   jax.ShapeDtypeStruct((n, 1024), BF16),
                   jax.ShapeDtypeStruct((n, 1024), F32),
                   jax.ShapeDtypeStruct((8, n), F32)],
        compiler_params=_cparams("parallel"),
        name="mlstm_in",
    )(x2d, norm_g, w_qkvo, w_gate_hi, w_gate_lo, b_gate)


def _mlstm_scan_kernel(q_ref, k_ref, v_ref, o_ref, gr_ref, hn_ref, out_ref, c_ref, n_ref, m_ref):
    L = MLSTM_L

    @pl.when(pl.program_id(1) == 0)
    def _():
        c_ref[...] = jnp.zeros_like(c_ref)
        n_ref[...] = jnp.zeros_like(n_ref)
        m_ref[...] = jnp.zeros_like(m_ref)

    row = lax.broadcasted_iota(jnp.int32, (L, L), 0)
    col = lax.broadcasted_iota(jnp.int32, (L, L), 1)
    causal = col <= row
    tril = causal.astype(F32)
    eye = (col == row).astype(F32)
    gr = gr_ref[...]
    gr_pad = jnp.concatenate([gr, jnp.zeros((120, L), F32)], axis=0)
    b_row = lax.dot_general(gr, tril, _NT, precision=HIGHEST, preferred_element_type=F32)
    b_col = lax.dot_general(tril, gr_pad, _NT, precision=HIGHEST, preferred_element_type=F32)
    g_col = lax.dot_general(eye, gr_pad, _NT, precision=HIGHEST, preferred_element_type=F32)

    for h in range(MLSTM_HEADS):
        li_row = gr[h:h + 1, :]
        li_col = g_col[:, h:h + 1]
        bf_row = b_row[4 + h:5 + h, :]
        bf_col = b_col[:, 4 + h:5 + h]
        m = m_ref[h:h + 1, 0:1]
        dmat = jnp.where(causal, bf_col - bf_row + li_row, NEG_INF)
        m_inter = bf_col + m
        m_t = jnp.maximum(m_inter, jnp.max(dmat, axis=-1, keepdims=True))
        w_intra = jnp.exp(dmat - m_t)
        w_inter = jnp.exp(m_inter - m_t)
        qh = q_ref[:, h * 128:(h + 1) * 128]
        kh = k_ref[:, h * 128:(h + 1) * 128]
        vh = v_ref[:, h * 256:(h + 1) * 256]
        s = lax.dot_general(qh, kh, _NT, preferred_element_type=F32) * w_intra
        c_old = c_ref[h]
        num = (jnp.dot(s.astype(BF16), vh, preferred_element_type=F32)
               + w_inter * jnp.dot(qh, c_old.astype(BF16), preferred_element_type=F32))
        n_old = n_ref[h:h + 1, :]
        qn = jnp.sum(qh.astype(F32) * n_old, axis=-1, keepdims=True)
        den = jnp.sum(s, axis=-1, keepdims=True) + w_inter * qn
        hh = num / jnp.maximum(jnp.abs(den), jnp.exp(-m_t))
        b_end = bf_col[L - 1:L, :]
        g = b_end - bf_col + li_col
        m_new = jnp.maximum(b_end + m, jnp.max(g, axis=0, keepdims=True))
        ws = jnp.exp(g - m_new)
        decay = jnp.exp(b_end + m - m_new)
        kf = kh.astype(F32) * ws
        c_ref[h] = decay * c_old + lax.dot_general(kf.astype(BF16), vh, _TN, preferred_element_type=F32)
        n_ref[h:h + 1, :] = decay * n_old + jnp.sum(kf, axis=0, keepdims=True)
        m_ref[h:h + 1, :] = jnp.broadcast_to(m_new, (1, 128))
        sl = slice(h * 256, (h + 1) * 256)
        out_ref[:, sl] = (_rms(hh) * hn_ref[:, sl] * o_ref[:, sl]).astype(BF16)


def _mlstm_scan(q, k, v, o, gr, head_norm, batch):
    n = q.shape[0]
    nblk = SEQ // MLSTM_L
    row = lambda b, j: (b * nblk + j, 0)
    return pl.pallas_call(
        _mlstm_scan_kernel,
        grid=(batch, nblk),
        in_specs=[pl.BlockSpec((MLSTM_L, 512), row),
                  pl.BlockSpec((MLSTM_L, 512), row),
                  pl.BlockSpec((MLSTM_L, 1024), row),
                  pl.BlockSpec((MLSTM_L, 1024), row),
                  pl.BlockSpec((8, MLSTM_L), lambda b, j: (0, b * nblk + j)),
                  pl.BlockSpec((1, 1024), lambda b, j: (0, 0))],
        out_specs=pl.BlockSpec((MLSTM_L, 1024), row),
        out_shape=jax.ShapeDtypeStruct((n, 1024), BF16),
        scratch_shapes=[pltpu.VMEM((MLSTM_HEADS, MLSTM_QK_DIM, MLSTM_V_DIM), F32),
                        pltpu.VMEM((8, 128), F32),
                        pltpu.VMEM((8, 128), F32)],
        compiler_params=_cparams("parallel", "arbitrary"),
        name="mlstm_scan",
    )(q, k, v, o, gr, head_norm)


def _pair_tables():
    lo, hi = [], []
    for g in range(MOE_GROUPS):
        for a in range(MOE_PER_GROUP):
            for b in range(a + 1, MOE_PER_GROUP):
                lo.append(g * MOE_PER_GROUP + a)
                hi.append(g * MOE_PER_GROUP + b)
    return np.asarray(lo, np.int32), np.asarray(hi, np.int32)


_PAIR_LO, _PAIR_HI = _pair_tables()


def _mix_out_kernel(a_ref, w_ref, res_ref, g_ref, wrh_ref, wrl_ref, br_ref, tril_ref, h_ref, xe_ref, route_ref, cnt_ref,
                    run_ref, *, a_transposed):
    @pl.when(pl.program_id(0) == 0)
    def _():
        run_ref[...] = jnp.zeros_like(run_ref)

    dims = _TN if a_transposed else (((1,), (0,)), ((), ()))
    h = res_ref[...] + lax.dot_general(a_ref[...], w_ref[...], dims, preferred_element_type=F32)
    h_ref[...] = h
    hn = _rms(h) * g_ref[...]
    xe_ref[:, 0:D_MODEL] = hn

    logits = _dot_split(hn, wrh_ref[...], wrl_ref[...]) + br_ref[...]
    lane_i = lax.broadcasted_iota(jnp.int32, logits.shape, 1)
    lane = lane_i.astype(F32)
    ninf = -jnp.inf
    is_g = (lane_i >= MOE_EXPERTS) & (lane_i < MOE_EXPERTS + MOE_GROUPS)
    glog = jnp.where(is_g, logits, ninf)
    gmax = jnp.max(glog, axis=-1, keepdims=True)
    gidx = jnp.min(jnp.where(glog == gmax, lane - MOE_EXPERTS, 99.0), axis=-1, keepdims=True)
    pg_top = 1.0 / jnp.sum(jnp.exp(glog - gmax), axis=-1, keepdims=True)
    lane_grp = (lane_i >> 3).astype(F32)
    in_grp = (lane_i < MOE_EXPERTS) & (lane_grp == gidx)
    ev = jnp.where(in_grp, logits, ninf)
    v1 = jnp.max(ev, axis=-1, keepdims=True)
    i1 = jnp.min(jnp.where(ev == v1, lane, 999.0), axis=-1, keepdims=True)
    ev2 = jnp.where(lane == i1, ninf, ev)
    v2 = jnp.max(ev2, axis=-1, keepdims=True)
    i2 = jnp.min(jnp.where(ev2 == v2, lane, 999.0), axis=-1, keepdims=True)
    e2 = jnp.exp(v2 - v1)
    w1 = pg_top / (1.0 + e2)
    w2 = pg_top * e2 / (1.0 + e2)
    first_lo = i1 < i2
    w_lo = jnp.where(first_lo, w1, w2)
    w_hi = jnp.where(first_lo, w2, w1)
    a = jnp.minimum(i1, i2) - MOE_PER_GROUP * gidx
    b = jnp.maximum(i1, i2) - MOE_PER_GROUP * gidx
    cls = gidx * MOE_PAIRS + a * (15.0 - a) * 0.5 + (b - a - 1.0)

    onehot = lane == cls
    prefix = jnp.dot(tril_ref[...], onehot.astype(BF16), preferred_element_type=F32)
    run = run_ref[0:1, :]
    rank = jnp.sum(jnp.where(onehot, prefix - 1.0 + run, 0.0), axis=-1, keepdims=True)
    run_new = run + prefix[TM - 1:TM, :]
    run_ref[...] = jnp.broadcast_to(run_new, run_ref.shape)
    cnt_ref[...] = jnp.broadcast_to(run_new, cnt_ref.shape)

    route = cls * float(2 ** ROUTE_SHIFT) + rank
    meta = jnp.where(lane_i == LANE_WLO, w_lo,
                     jnp.where(lane_i == LANE_WHI, w_hi,
                               jnp.where(lane_i == LANE_ROUTE, route, 0.0)))
    xe_ref[:, D_MODEL:XE_W] = meta
    route_ref[...] = meta.T[0:8, :].astype(jnp.int32)


def _mix_out(a, w, res, g_moe, w_rt, b_rt, tril, a_transposed):
    n = res.shape[0]
    kdim = w.shape[0]
    row = lambda i: (i, 0)
    fixed = lambda i: (0, 0)
    a_spec = pl.BlockSpec((kdim, TM), lambda i: (0, i)) if a_transposed else pl.BlockSpec((TM, kdim), row)
    w_rt_hi, w_rt_lo = _split_bf16(w_rt)
    return pl.pallas_call(
        functools.partial(_mix_out_kernel, a_transposed=a_transposed),
        grid=(n // TM,),
        in_specs=[a_spec,
                  pl.BlockSpec((kdim, D_MODEL), fixed),
                  pl.BlockSpec((TM, D_MODEL), row),
                  pl.BlockSpec((1, D_MODEL), fixed),
                  pl.BlockSpec((D_MODEL, 128), fixed),
                  pl.BlockSpec((D_MODEL, 128), fixed),
                  pl.BlockSpec((1, 128), fixed),
                  pl.BlockSpec((TM, TM), fixed)],
        out_specs=[pl.BlockSpec((TM, D_MODEL), row),
                   pl.BlockSpec((TM, XE_W), row),
                   pl.BlockSpec((8, TM), lambda i: (0, i)),
                   pl.BlockSpec((8, 128), fixed)],
        out_shape=[jax.ShapeDtypeStruct((n, D_MODEL), F32),
                   jax.ShapeDtypeStruct((n, XE_W), F32),
                   jax.ShapeDtypeStruct((8, n), jnp.int32),
                   jax.ShapeDtypeStruct((8, 128), F32)],
        scratch_shapes=[pltpu.VMEM((8, 128), F32)],
        compiler_params=_cparams("arbitrary"),
        name="mix_out",
    )(a, w, res, g_moe, w_rt_hi, w_rt_lo, b_rt, tril)


def _sorted_row(route_ref, offs_ref, idx):
    r = route_ref[idx]
    return offs_ref[r >> ROUTE_SHIFT] + (r & (2 ** ROUTE_SHIFT - 1))


def _dispatch_kernel(route_ref, offs_ref, cnt_ref, nused_ref, xe_ref, xs_ref, zbuf, sem, zsem):
    i = pl.program_id(0)
    base = i * TM

    @pl.when(i == 0)
    def _():
        zbuf[...] = jnp.zeros_like(zbuf)

        def per_class(c, carry):
            cnt = cnt_ref[c]
            start = offs_ref[c] + cnt
            pad = (-cnt) & (ROW_TILE - 1)

            def fill(r, inner):
                pltpu.make_async_copy(zbuf.at[pl.ds(0, 1)], xs_ref.at[pl.ds(start + r, 1)], zsem).start()
                return inner

            def drain(r, inner):
                pltpu.make_async_copy(zbuf.at[pl.ds(0, 1)], xs_ref.at[pl.ds(0, 1)], zsem).wait()
                return inner

            lax.fori_loop(0, pad, fill, 0)
            lax.fori_loop(0, pad, drain, 0)
            return carry

        lax.fori_loop(0, MOE_CLASSES, per_class, 0)

        def tail(t, carry):
            row0 = pl.multiple_of(t * ROW_TILE, ROW_TILE)
            cp = pltpu.make_async_copy(zbuf, xs_ref.at[pl.ds(row0, ROW_TILE)], zsem)
            cp.start()
            cp.wait()
            return carry

        lax.fori_loop(nused_ref[0], xs_ref.shape[0] // ROW_TILE, tail, 0)

    def issue(t, carry):
        p = _sorted_row(route_ref, offs_ref, base + t)
        pltpu.make_async_copy(xe_ref.at[pl.ds(t, 1)], xs_ref.at[pl.ds(p, 1)], sem).start()
        return carry

    lax.fori_loop(0, TM, issue, 0, unroll=ISSUE_UNROLL)
    pltpu.make_async_copy(xe_ref, xs_ref.at[pl.ds(0, TM)], sem).wait()


def _dispatch(route, offs, cnt, n_used, xe, n_rows):
    n = xe.shape[0]
    grid_spec = pltpu.PrefetchScalarGridSpec(
        num_scalar_prefetch=4,
        grid=(n // TM,),
        in_specs=[pl.BlockSpec((TM, XE_W), lambda i, *_: (i, 0))],
        out_specs=pl.BlockSpec(memory_space=pl.ANY),
        scratch_shapes=[pltpu.VMEM((ROW_TILE, XE_W), F32), pltpu.SemaphoreType.DMA(()),
                        pltpu.SemaphoreType.DMA(())],
    )
    return pl.pallas_call(
        _dispatch_kernel,
        grid_spec=grid_spec,
        out_shape=jax.ShapeDtypeStruct((n_rows, XE_W), F32),
        compiler_params=_cparams("arbitrary"),
        name="moe_dispatch",
    )(route, offs, cnt, n_used, xe)


def _experts_kernel(tlo_ref, thi_ref, nused_ref, xs_ref, wg_lo, wu_lo, wd_lo, wg_hi, wu_hi, wd_hi, y_ref):
    del tlo_ref, thi_ref

    @pl.when(pl.program_id(0) < nused_ref[0])
    def _():
        x = xs_ref[:, 0:D_MODEL].astype(BF16)

        def ffn(wg, wu, wd, w):
            a = jnp.dot(x, wg[0], preferred_element_type=F32)
            u = jnp.dot(x, wu[0], preferred_element_type=F32)
            hid = (a * _sigmoid(a)) * u * w
            return jnp.dot(hid.astype(BF16), wd[0], preferred_element_type=F32)

        w_lo = xs_ref[:, D_MODEL + LANE_WLO:D_MODEL + LANE_WLO + 1]
        w_hi = xs_ref[:, D_MODEL + LANE_WHI:D_MODEL + LANE_WHI + 1]
        y_ref[...] = ffn(wg_lo, wu_lo, wd_lo, w_lo) + ffn(wg_hi, wu_hi, wd_hi, w_hi)

    @pl.when(pl.program_id(0) >= nused_ref[0])
    def _():
        y_ref[...] = jnp.zeros_like(y_ref)


def _experts(tile_lo, tile_hi, n_used, xs, w_gate, w_up, w_down):
    n_tiles = xs.shape[0] // ROW_TILE
    rows = lambda i, tlo, thi, nu: (jnp.maximum(jnp.minimum(i, nu[0] - 1), 0), 0)
    lo = lambda i, tlo, thi, nu: (tlo[i], 0, 0)
    hi = lambda i, tlo, thi, nu: (thi[i], 0, 0)
    up_spec = lambda m: pl.BlockSpec((1, D_MODEL, MOE_HIDDEN), m)
    dn_spec = lambda m: pl.BlockSpec((1, MOE_HIDDEN, D_MODEL), m)
    grid_spec = pltpu.PrefetchScalarGridSpec(
        num_scalar_prefetch=3,
        grid=(n_tiles,),
        in_specs=[pl.BlockSpec((ROW_TILE, XE_W), rows),
                  up_spec(lo), up_spec(lo), dn_spec(lo),
                  up_spec(hi), up_spec(hi), dn_spec(hi)],
        out_specs=pl.BlockSpec((ROW_TILE, D_MODEL), lambda i, tlo, thi, nu: (i, 0)),
    )
    return pl.pallas_call(
        _experts_kernel,
        grid_spec=grid_spec,
        out_shape=jax.ShapeDtypeStruct((xs.shape[0], D_MODEL), F32),
        compiler_params=_cparams("arbitrary"),
        name="moe_experts",
    )(tile_lo, tile_hi, n_used, xs, w_gate, w_up, w_down, w_gate, w_up, w_down)


def _combine_kernel(route_ref, offs_ref, h_ref, y_ref, g_ref, out_ref, buf, sem, *, final_norm):
    base = pl.program_id(0) * TM

    def issue(t, carry):
        p = _sorted_row(route_ref, offs_ref, base + t)
        pltpu.make_async_copy(y_ref.at[pl.ds(p, 1)], buf.at[pl.ds(t, 1)], sem).start()
        return carry

    lax.fori_loop(0, TM, issue, 0, unroll=ISSUE_UNROLL)
    pltpu.make_async_copy(y_ref.at[pl.ds(0, TM)], buf, sem).wait()
    out = h_ref[...] + buf[...]
    if final_norm:
        out = _rms(out) * g_ref[...]
    out_ref[...] = out


def _combine(route, offs, h, y, gain, final_norm):
    n = h.shape[0]
    grid_spec = pltpu.PrefetchScalarGridSpec(
        num_scalar_prefetch=2,
        grid=(n // TM,),
        in_specs=[pl.BlockSpec((TM, D_MODEL), lambda i, *_: (i, 0)),
                  pl.BlockSpec(memory_space=pl.ANY),
                  pl.BlockSpec((1, D_MODEL), lambda i, *_: (0, 0))],
        out_specs=pl.BlockSpec((TM, D_MODEL), lambda i, *_: (i, 0)),
        scratch_shapes=[pltpu.VMEM((TM, D_MODEL), F32), pltpu.SemaphoreType.DMA(())],
    )
    return pl.pallas_call(
        functools.partial(_combine_kernel, final_norm=final_norm),
        grid_spec=grid_spec,
        out_shape=jax.ShapeDtypeStruct((n, D_MODEL), F32),
        compiler_params=_cparams("arbitrary"),
        name="moe_combine",
    )(route, offs, h, y, gain)


def _moe_layer(a, w_out, res, moe_norm, w_group, b_group, w_router, b_router, w_gate, w_up, w_down,
               tril, out_gain, final_norm, a_transposed=False):
    n = res.shape[0]
    w_rt = jnp.zeros((D_MODEL, 128), F32).at[:, 0:MOE_EXPERTS].set(w_router)
    w_rt = w_rt.at[:, MOE_EXPERTS:MOE_EXPERTS + MOE_GROUPS].set(w_group)
    b_rt = jnp.zeros((1, 128), F32).at[0, 0:MOE_EXPERTS].set(b_router)
    b_rt = b_rt.at[0, MOE_EXPERTS:MOE_EXPERTS + MOE_GROUPS].set(b_group)
    h, xe, route8, counts = _mix_out(a, w_out, res, moe_norm.reshape(1, D_MODEL), w_rt, b_rt, tril, a_transposed)

    n_tiles = n // ROW_TILE + MOE_CLASSES
    cnt = counts[0].astype(jnp.int32)
    tiles_c = (cnt + ROW_TILE - 1) // ROW_TILE
    tile_end = jnp.cumsum(tiles_c)
    offs = (tile_end - tiles_c) * ROW_TILE
    n_used = tile_end[-1]
    tile_ids = jnp.minimum(jnp.arange(n_tiles, dtype=jnp.int32), n_used - 1)
    tile_cls = jnp.sum((tile_end[None, 0:MOE_CLASSES] <= tile_ids[:, None]).astype(jnp.int32), axis=1)
    tile_cls = jnp.clip(tile_cls, 0, MOE_CLASSES - 1)
    tile_lo = jnp.asarray(_PAIR_LO)[tile_cls]
    tile_hi = jnp.asarray(_PAIR_HI)[tile_cls]
    route = route8[LANE_ROUTE]

    n_used = n_used.reshape(1)
    xs = _dispatch(route, offs, cnt, n_used, xe, n_tiles * ROW_TILE)
    y = _experts(tile_lo, tile_hi, n_used, xs, w_gate.astype(BF16), w_up.astype(BF16),
                 w_down.astype(BF16))
    return _combine(route, offs, h, y, out_gain.reshape(1, D_MODEL), final_norm)


def _rope_tables(pos, width):
    half = ROPE_DIM // 2
    inv_freq = jnp.power(jnp.float32(ROPE_THETA), -jnp.arange(half, dtype=F32) * (2.0 / ROPE_DIM))
    ang = pos.astype(F32)[:, None] * inv_freq[None, :]
    cos, sin = jnp.cos(ang), jnp.sin(ang)
    t = pos.shape[0]
    rest = NSA_HEAD_DIM - ROPE_DIM
    cos_t = jnp.concatenate([cos, cos, jnp.ones((t, rest), F32)], axis=1)
    sin_a = jnp.concatenate([-sin, jnp.zeros((t, half + rest), F32)], axis=1)
    sin_b = jnp.concatenate([jnp.zeros((t, half), F32), sin, jnp.zeros((t, rest), F32)], axis=1)
    rep = width // NSA_HEAD_DIM
    return jnp.tile(cos_t, (1, rep)), jnp.tile(sin_a, (1, rep)), jnp.tile(sin_b, (1, rep))


def _rope(x, cos_t, sin_a, sin_b):
    half = ROPE_DIM // 2
    parts = []
    for c in range(x.shape[1] // 128):
        xc = x[:, c * 128:(c + 1) * 128]
        parts.append(xc * cos_t + pltpu.roll(xc, 128 - half, axis=1) * sin_a + pltpu.roll(xc, half, axis=1) * sin_b)
    return parts[0] if len(parts) == 1 else jnp.concatenate(parts, axis=1)


def _rope_rows(x, cos_t, sin_a, sin_b):
    half = ROPE_DIM // 2
    reps = x.shape[0] // NSA_HEAD_DIM
    tile = lambda t: jnp.concatenate([t] * reps, axis=0)
    up = jnp.concatenate([x[half:], x[:half]], axis=0)
    down = jnp.concatenate([x[-half:], x[:-half]], axis=0)
    return x * tile(cos_t) + up * tile(sin_a) + down * tile(sin_b)


def _nsa_proj_kernel(h_ref, gq_ref, gkv_ref, wqt_ref, wkv_ref, wvt_ref, cos_ref, sa_ref, sb_ref,
                     cost_ref, sat_ref, sbt_ref,
                     qt_ref, gatet_ref, kc0_ref, kc1_ref, vc0_ref, vc1_ref, ks_ref, vst_ref, kw_ref, vwt_ref):
    r = _rms(h_ref[...])
    hq = (r * gq_ref[...]).astype(BF16)
    hk = (r * gkv_ref[...]).astype(BF16)
    qt = lax.dot_general(wqt_ref[0:1024, :], hq, _NT, preferred_element_type=F32)
    qt_ref[...] = (_rope_rows(qt, cost_ref[...], sat_ref[...], sbt_ref[...]) * (NSA_HEAD_DIM ** -0.5)).astype(BF16)
    gatet_ref[...] = _sigmoid(lax.dot_general(wqt_ref[1024:1152, :], hq, _NT, preferred_element_type=F32))
    kc0_ref[...] = jnp.dot(hk, wkv_ref[:, 0:128], preferred_element_type=F32)
    kc1_ref[...] = jnp.dot(hk, wkv_ref[:, 128:256], preferred_element_type=F32)
    vc0_ref[...] = jnp.dot(hk, wkv_ref[:, 256:384], preferred_element_type=F32)
    vc1_ref[...] = jnp.dot(hk, wkv_ref[:, 384:512], preferred_element_type=F32)

    tm = h_ref.shape[0]
    cos_t, sin_a, sin_b = cos_ref[...], sa_ref[...], sb_ref[...]
    lane = lax.broadcasted_iota(jnp.int32, (tm, NSA_HEAD_DIM), 1)
    pos = (pl.program_id(0) % (SEQ // tm)) * tm + lax.broadcasted_iota(jnp.int32, (tm, NSA_HEAD_DIM), 0)
    blk_onehot = jnp.where(lane == (pos >> 6), 1.0, 0.0).astype(BF16)
    zeros = jnp.zeros((tm, NSA_HEAD_DIM), BF16)

    def store_keys(ref, val, aux):
        for g in range(NSA_GROUPS):
            ref[0, g] = jnp.concatenate([val[:, g * 64:(g + 1) * 64].astype(BF16), aux], axis=1)

    store_keys(ks_ref, _rope(jnp.dot(hk, wkv_ref[:, 512:768], preferred_element_type=F32), cos_t, sin_a, sin_b),
               blk_onehot)
    store_keys(kw_ref, _rope(jnp.dot(hk, wkv_ref[:, 1024:1280], preferred_element_type=F32), cos_t, sin_a, sin_b),
               zeros)

    row = lax.broadcasted_iota(jnp.int32, (NSA_HEAD_DIM, tm), 0)
    ones_row = jnp.where(row == 0, 1.0, 0.0).astype(BF16)

    def store_values(ref, val_t):
        for g in range(NSA_GROUPS):
            ref[0, g] = jnp.concatenate([val_t[g * 64:(g + 1) * 64, :].astype(BF16), ones_row], axis=0)

    store_values(vst_ref, lax.dot_general(wvt_ref[0:256, :], hk, _NT, preferred_element_type=F32))
    store_values(vwt_ref, lax.dot_general(wvt_ref[256:512, :], hk, _NT, preferred_element_type=F32))


def _nsa_proj(h, g_q, g_kv, w_qt, w_kv, w_vt, rope_tabs, rope_tabs_t, batch):
    n = h.shape[0]
    nblk = SEQ // TM
    row = lambda i: (i, 0)
    col = lambda i: (0, i)
    fixed = lambda i: (0, 0)
    tab = lambda i: (i % nblk, 0)
    tab_t = lambda i: (0, i % nblk)
    key_spec = pl.BlockSpec((1, NSA_GROUPS, TM, KV_LANES), lambda i: (i // nblk, 0, i % nblk, 0))
    key_shape = jax.ShapeDtypeStruct((batch, NSA_GROUPS, SEQ, KV_LANES), BF16)
    val_spec = pl.BlockSpec((1, NSA_GROUPS, KV_LANES, TM), lambda i: (i // nblk, 0, 0, i % nblk))
    val_shape = jax.ShapeDtypeStruct((batch, NSA_GROUPS, KV_LANES, SEQ), BF16)
    raw_spec = pl.BlockSpec((TM, 128), row)
    raw_shape = jax.ShapeDtypeStruct((n, 128), F32)
    return pl.pallas_call(
        _nsa_proj_kernel,
        grid=(n // TM,),
        in_specs=[pl.BlockSpec((TM, D_MODEL), row),
                  pl.BlockSpec((1, D_MODEL), fixed),
                  pl.BlockSpec((1, D_MODEL), fixed),
                  pl.BlockSpec((1152, D_MODEL), fixed),
                  pl.BlockSpec((D_MODEL, 1536), fixed),
                  pl.BlockSpec((512, D_MODEL), fixed),
                  pl.BlockSpec((TM, 128), tab), pl.BlockSpec((TM, 128), tab), pl.BlockSpec((TM, 128), tab),
                  pl.BlockSpec((NSA_HEAD_DIM, TM), tab_t), pl.BlockSpec((NSA_HEAD_DIM, TM), tab_t),
                  pl.BlockSpec((NSA_HEAD_DIM, TM), tab_t)],
        out_specs=[pl.BlockSpec((1024, TM), col),
                   pl.BlockSpec((128, TM), col),
                   raw_spec, raw_spec, raw_spec, raw_spec,
                   key_spec, val_spec, key_spec, val_spec],
        out_shape=[jax.ShapeDtypeStruct((1024, n), BF16),
                   jax.ShapeDtypeStruct((128, n), F32),
                   raw_shape, raw_shape, raw_shape, raw_shape,
                   key_shape, val_shape, key_shape, val_shape],
        compiler_params=_cparams("parallel"),
        name="nsa_proj",
    )(h, g_q, g_kv, w_qt, w_kv, w_vt, *rope_tabs, *rope_tabs_t)


HALF_BLOCKS = SEQ // CMP_STRIDE
CMP_K = CMP_STRIDE * 256


def _compress_kernel(rk0_ref, rk1_ref, rv0_ref, rv1_ref, w1k_ref, w1v_ref, pek_ref, pev_ref, w2k_ref, w2v_ref,
                     cos_ref, sa_ref, sb_ref, kc_ref, vc_ref):
    lane = lax.broadcasted_iota(jnp.int32, (HALF_BLOCKS, 512), 1)
    first_half = (lane & 127) < 64

    def comp(raw_refs, w1_ref, pe_ref, w2_ref):
        x = jnp.concatenate([r[pl.ds(l, HALF_BLOCKS, stride=CMP_STRIDE), :]
                             for l in range(CMP_STRIDE) for r in raw_refs], axis=1).astype(BF16)
        r = jnp.dot(x, w1_ref[...], preferred_element_type=F32)
        rpe = jnp.dot(pe_ref[...].astype(BF16), w1_ref[...], preferred_element_type=F32)
        r = r + jnp.where(first_half, rpe[0:1, :], rpe[1:2, :])
        nxt = pltpu.roll(r, HALF_BLOCKS - 1, axis=0)
        nxt = jnp.concatenate([pltpu.roll(nxt[:, c * 128:(c + 1) * 128], 64, axis=1) for c in range(4)], axis=1)
        pre = r + nxt
        act = pre * _sigmoid(pre)
        return jnp.dot(act.astype(BF16), w2_ref[...], preferred_element_type=F32)

    kc = comp((rk0_ref, rk1_ref), w1k_ref, pek_ref, w2k_ref)
    kc = _rope(kc, cos_ref[...], sa_ref[...], sb_ref[...])
    vc = comp((rv0_ref, rv1_ref), w1v_ref, pev_ref, w2v_ref)
    for g in range(NSA_GROUPS):
        kc_ref[0, g] = kc[:, g * 64:(g + 1) * 64]
        vc_ref[0, g] = vc[:, g * 64:(g + 1) * 64]


def _compress_weights(pe, w1, w2):
    dh = NSA_HEAD_DIM
    w1r = w1.reshape(2, CMP_STRIDE, dh, dh)
    big = jnp.zeros((CMP_STRIDE, NSA_GROUPS, dh, NSA_GROUPS, 2, dh), F32)
    for g in range(NSA_GROUPS):
        big = big.at[:, g, :, g, 0, :].set(w1r[0])
        big = big.at[:, g, :, g, 1, :].set(w1r[1])
    w1_big = big.reshape(CMP_K, 512).astype(BF16)
    pe_rows = jnp.zeros((8, CMP_STRIDE, NSA_GROUPS, dh), F32)
    pe_rows = pe_rows.at[0].set(jnp.broadcast_to(pe[0:CMP_STRIDE, None, :], (CMP_STRIDE, NSA_GROUPS, dh)))
    pe_rows = pe_rows.at[1].set(jnp.broadcast_to(pe[CMP_STRIDE:, None, :], (CMP_STRIDE, NSA_GROUPS, dh)))
    w2_bd = jnp.zeros((NSA_GROUPS, 2, dh, NSA_GROUPS, dh), F32)
    for g in range(NSA_GROUPS):
        w2_bd = w2_bd.at[g, 0, :, g, :].set(w2)
    return w1_big, pe_rows.reshape(8, CMP_K), w2_bd.reshape(512, 256).astype(BF16)


def _compress(raw_k, raw_v, wk, wv, cmp_tabs, batch):
    fixed = lambda b: (0, 0)
    raw_spec = pl.BlockSpec((SEQ, 128), lambda b: (b, 0))
    out_spec = pl.BlockSpec((1, NSA_GROUPS, N_CMP_PAD, NSA_HEAD_DIM), lambda b: (b, 0, 0, 0))
    out_shape = jax.ShapeDtypeStruct((batch, NSA_GROUPS, N_CMP_PAD, NSA_HEAD_DIM), F32)
    return pl.pallas_call(
        _compress_kernel,
        grid=(batch,),
        in_specs=[raw_spec, raw_spec, raw_spec, raw_spec,
                  pl.BlockSpec((CMP_K, 512), fixed), pl.BlockSpec((CMP_K, 512), fixed),
                  pl.BlockSpec((8, CMP_K), fixed), pl.BlockSpec((8, CMP_K), fixed),
                  pl.BlockSpec((512, 256), fixed), pl.BlockSpec((512, 256), fixed),
                  pl.BlockSpec((N_CMP_PAD, 128), fixed), pl.BlockSpec((N_CMP_PAD, 128), fixed),
                  pl.BlockSpec((N_CMP_PAD, 128), fixed)],
        out_specs=[out_spec, out_spec],
        out_shape=[out_shape, out_shape],
        compiler_params=_cparams("parallel"),
        name="nsa_compress",
    )(*raw_k, *raw_v, wk[0], wv[0], wk[1], wv[1], wk[2], wv[2], *cmp_tabs)


def _nsa_attn_kernel(qt_ref, gatet_ref, kc_ref, vc_ref, ks_ref, vst_ref, kw_ref, vwt_ref, ovl_ref, out_ref,
                     qx_sc, ms_sc, as_sc, aw_sc):
    qb = pl.program_id(2)
    q0 = qb * Q_BLOCK
    Q, HG, DH = Q_BLOCK, NSA_GROUP_SIZE, NSA_HEAD_DIM
    groups = range(ATTN_GROUPS)

    def heads(x):
        return jnp.concatenate([x] * HG, axis=1)

    n_row = lax.broadcasted_iota(jnp.int32, (N_CMP_PAD, Q), 0)
    t_lane = q0 + lax.broadcasted_iota(jnp.int32, (N_CMP_PAD, Q), 1)
    valid_c = heads((n_row * CMP_STRIDE + CMP_BLOCK - 1 <= t_lane) & (n_row < N_CMP_PAD - 1))
    j_row = lax.broadcasted_iota(jnp.int32, (N_SEL, Q), 0)
    cur = (q0 + lax.broadcasted_iota(jnp.int32, (N_SEL, Q), 1)) >> 6
    forced = (j_row == 0) | (j_row == cur) | (j_row == cur - 1)
    o_c = []
    for gi in groups:
        q4 = qt_ref[gi * HG * DH:(gi + 1) * HG * DH, :]
        q_t = jnp.concatenate([q4[h * DH:(h + 1) * DH, :] for h in range(HG)], axis=1)
        s_c = jnp.dot(kc_ref[0, gi].astype(BF16), q_t, preferred_element_type=F32)
        s_c = jnp.where(valid_c, s_c, NEG_INF)
        m_c = jnp.max(s_c, axis=0, keepdims=True)
        e_c = jnp.where(valid_c, jnp.exp(s_c - m_c), 0.0)
        l_c = jnp.sum(e_c, axis=0, keepdims=True)
        p_c = e_c * jnp.where(l_c > 0.0, 1.0 / l_c, 0.0)
        o_c.append(jnp.dot(vc_ref[0, gi].T.astype(BF16), p_c.astype(BF16), preferred_element_type=F32))

        p_sum = p_c[:, 0:Q] + p_c[:, Q:2 * Q] + p_c[:, 2 * Q:3 * Q] + p_c[:, 3 * Q:4 * Q]
        imp_t = jnp.dot(ovl_ref[...], p_sum, precision=HIGHEST, preferred_element_type=F32)[0:N_SEL]
        imp_t = jnp.where(forced, FORCED_SCORE, imp_t)
        imp_t = jnp.where(j_row > cur, NEG_INF, imp_t)
        cnt = jnp.zeros((N_SEL, Q), F32)
        for i in range(N_SEL):
            ri = imp_t[i:i + 1, :]
            cnt = cnt + jnp.where(ri > imp_t, 1.0, jnp.where((ri == imp_t) & (j_row > i), 1.0, 0.0))
        sel_bias = jnp.where((cnt < SEL_TOPK) & (j_row <= cur), 0.0, NEG_INF).astype(BF16)

        qx_sc[gi] = jnp.concatenate([q_t, heads(sel_bias), jnp.zeros((KV_LANES - DH - N_SEL, HG * Q), BF16)],
                                    axis=0)

    def finish(acc):
        return acc[0:DH] / acc[DEN_ROW:DEN_ROW + 1]

    ms_sc[...] = jnp.full(ms_sc.shape, NEG_INF, F32)
    as_sc[...] = jnp.zeros(as_sc.shape, F32)
    n_kt = (q0 + Q + SLC_TILE - 1) // SLC_TILE

    def slc_tile(kt, bias):
        start = pl.multiple_of(kt * SLC_TILE, SLC_TILE)
        for gi in groups:
            s = jnp.dot(ks_ref[0, gi, pl.ds(start, SLC_TILE), :], qx_sc[gi], preferred_element_type=F32)
            if bias is not None:
                s = s + bias
            m_old = ms_sc[gi, 0:1, :]
            m_new = jnp.maximum(m_old, jnp.max(s, axis=0, keepdims=True))
            p = jnp.exp(s - m_new).astype(BF16)
            pv = jnp.dot(vst_ref[0, gi, :, pl.ds(start, SLC_TILE)], p, preferred_element_type=F32)
            as_sc[gi] = as_sc[gi] * jnp.exp(m_old - m_new) + pv
            ms_sc[gi] = jnp.broadcast_to(m_new, ms_sc.shape[1:])

    def slc_full(kt, carry):
        slc_tile(kt, None)
        return carry

    lax.fori_loop(0, n_kt - 1, slc_full, 0)
    c_s = lax.broadcasted_iota(jnp.int32, (SLC_TILE, Q), 0)
    r_s = lax.broadcasted_iota(jnp.int32, (SLC_TILE, Q), 1)
    slc_tile(n_kt - 1, heads(jnp.where((n_kt - 1) * SLC_TILE + c_s <= q0 + r_s, 0.0, NEG_INF)))

    c_w = lax.broadcasted_iota(jnp.int32, (Q, Q), 0)
    r_w = lax.broadcasted_iota(jnp.int32, (Q, Q), 1)
    n_slab = WIN_KEYS // Q

    def window(start, slab_bias):
        start = pl.multiple_of(start, Q)
        slab_bias = [None if b is None else heads(b) for b in slab_bias]
        for gi in groups:
            s = jnp.dot(kw_ref[0, gi, pl.ds(start, WIN_KEYS), :], qx_sc[gi], preferred_element_type=F32)
            slabs = []
            for j in range(n_slab):
                sj = s[j * Q:(j + 1) * Q]
                slabs.append(sj if slab_bias[j] is None else sj + slab_bias[j])
            top = slabs[0]
            for sj in slabs[1:]:
                top = jnp.maximum(top, sj)
            m = jnp.max(top, axis=0, keepdims=True)
            p = jnp.concatenate([jnp.exp(sj - m) for sj in slabs], axis=0).astype(BF16)
            aw_sc[gi] = jnp.dot(vwt_ref[0, gi, :, pl.ds(start, WIN_KEYS)], p, preferred_element_type=F32)

    @pl.when(qb >= WINDOW // Q)
    def _():
        window(q0 - WINDOW, [jnp.where(c_w > r_w, 0.0, NEG_INF)] + [None] * (n_slab - 2)
               + [jnp.where(c_w <= r_w, 0.0, NEG_INF)])

    @pl.when(qb < WINDOW // Q)
    def _():
        window(0, [jnp.where(j * Q + c_w <= q0 + r_w, 0.0, NEG_INF) for j in range(n_slab)])

    for gi in groups:
        g = pl.program_id(1) * ATTN_GROUPS + gi

        def gate_row(branch):
            return jnp.concatenate([gatet_ref[pl.ds((g * HG + h) * 3 + branch, 1), :] for h in range(HG)], axis=1)

        o = gate_row(0) * o_c[gi] + gate_row(1) * finish(as_sc[gi]) + gate_row(2) * finish(aw_sc[gi])
        out_ref[gi * HG * DH:(gi + 1) * HG * DH, :] = jnp.concatenate(
            [o[:, h * Q:(h + 1) * Q] for h in range(HG)], axis=0).astype(BF16)


def _overlap_t():
    n = np.arange(N_CMP_PAD)
    j = np.arange(128)
    cmp_start = n * CMP_STRIDE
    cmp_end = cmp_start + CMP_BLOCK - 1
    sel_start = j * SEL_BLOCK
    ovl = ((cmp_start[None, :] <= sel_start[:, None] + SEL_BLOCK - 1) & (cmp_end[None, :] >= sel_start[:, None])
           & (j[:, None] < N_SEL) & (n[None, :] < N_CMP_PAD - 1))
    return jnp.asarray(ovl.astype(np.float32))


def _nsa_attn(q_t, gates_t, kc, vc, ks, vs_t, kw, vw_t, batch):
    n = q_t.shape[1]
    gs = ATTN_GROUPS
    qcol = lambda b, g, i: (g, b * N_QB + i)
    gcol = lambda b, g, i: (0, b * N_QB + i)
    kv = lambda b, g, i: (b, g, 0, 0)
    rows = gs * NSA_GROUP_SIZE * NSA_HEAD_DIM
    lanes = NSA_GROUP_SIZE * Q_BLOCK
    return pl.pallas_call(
        _nsa_attn_kernel,
        grid=(batch, NSA_GROUPS // gs, N_QB),
        in_specs=[pl.BlockSpec((rows, Q_BLOCK), qcol),
                  pl.BlockSpec((128, Q_BLOCK), gcol),
                  pl.BlockSpec((1, gs, N_CMP_PAD, NSA_HEAD_DIM), kv),
                  pl.BlockSpec((1, gs, N_CMP_PAD, NSA_HEAD_DIM), kv),
                  pl.BlockSpec((1, gs, SEQ, KV_LANES), kv),
                  pl.BlockSpec((1, gs, KV_LANES, SEQ), kv),
                  pl.BlockSpec((1, gs, SEQ, KV_LANES), kv),
                  pl.BlockSpec((1, gs, KV_LANES, SEQ), kv),
                  pl.BlockSpec((128, N_CMP_PAD), lambda b, g, i: (0, 0))],
        out_specs=pl.BlockSpec((rows, Q_BLOCK), qcol),
        out_shape=jax.ShapeDtypeStruct((NSA_HEADS * NSA_HEAD_DIM, n), BF16),
        scratch_shapes=[pltpu.VMEM((gs, KV_LANES, lanes), BF16),
                        pltpu.VMEM((gs, 8, lanes), F32),
                        pltpu.VMEM((gs, KV_LANES, lanes), F32),
                        pltpu.VMEM((gs, KV_LANES, lanes), F32)],
        compiler_params=_cparams("parallel", "parallel", "arbitrary"),
        name="nsa_attn",
    )(q_t, gates_t, kc, vc, ks, vs_t, kw, vw_t, _overlap_t())


def kernel(x, mlstm_norm, mlstm_w_in, mlstm_gate_bias, mlstm_head_norm, mlstm_w_out, kv_norm, kv_w, cmp_pe_k, cmp_w1_k, cmp_w2_k, cmp_pe_v, cmp_w1_v, cmp_w2_v, nsa_norm, nsa_w_q, nsa_w_out, moe_norm, moe_w_group, moe_b_group, moe_w_router, moe_b_router, moe_w_gate, moe_w_up, moe_w_down, final_norm):
    batch, seq, d = x.shape
    assert seq == SEQ and d == D_MODEL
    assert mlstm_norm.shape[0] == 1 and nsa_norm.shape[0] == 1 and moe_norm.shape[0] == 2
    n = batch * seq
    x2d = x.reshape(n, d)
    tril = jnp.tril(jnp.ones((TM, TM), F32)).astype(BF16)

    w_in = mlstm_w_in[0]
    w_gate = jnp.zeros((d, 128), F32).at[:, 0:8].set(w_in[:, 3072:3080])
    b_gate = jnp.zeros((1, 128), F32).at[0, 0:8].set(mlstm_gate_bias[0])
    q, k, v, o, gr = _mlstm_in(x2d, mlstm_norm[0].reshape(1, d), w_in[:, 0:3072].astype(BF16), w_gate, b_gate)
    hs = _mlstm_scan(q, k, v, o, gr, mlstm_head_norm[0].reshape(1, d), batch)
    h = _moe_layer(hs, mlstm_w_out[0].astype(BF16), x2d, moe_norm[0], moe_w_group[0], moe_b_group[0],
                   moe_w_router[0], moe_b_router[0], moe_w_gate[0], moe_w_up[0], moe_w_down[0],
                   tril, final_norm, False)

    w_qt = jnp.zeros((1152, d), F32).at[0:1072].set(nsa_w_q[0].T).astype(BF16)
    w_vt = jnp.concatenate([kv_w[:, 768:1024], kv_w[:, 1280:1536]], axis=1).T.astype(BF16)
    seq_tabs = _rope_tables(jnp.arange(SEQ), 128)
    seq_tabs_t = tuple(t[:, 0:NSA_HEAD_DIM].T for t in seq_tabs)
    q_t, gates_t, rk0, rk1, rv0, rv1, ks, vs_t, kw, vw_t = _nsa_proj(
        h, nsa_norm[0].reshape(1, d), kv_norm.reshape(1, d), w_qt, kv_w.astype(BF16), w_vt, seq_tabs, seq_tabs_t,
        batch)
    cmp_pos = jnp.arange(N_CMP_PAD) * CMP_STRIDE + CMP_BLOCK - 1
    kc, vc = _compress((rk0, rk1), (rv0, rv1), _compress_weights(cmp_pe_k, cmp_w1_k, cmp_w2_k),
                       _compress_weights(cmp_pe_v, cmp_w1_v, cmp_w2_v), _rope_tables(cmp_pos, 128), batch)
    att_t = _nsa_attn(q_t, gates_t, kc, vc, ks, vs_t, kw, vw_t, batch)
    out = _moe_layer(att_t, nsa_w_out[0].astype(BF16), h, moe_norm[1], moe_w_group[1], moe_b_group[1],
                     moe_w_router[1], moe_b_router[1], moe_w_gate[1], moe_w_up[1], moe_w_down[1],
                     tril, final_norm, True, a_transposed=True)
    return out.reshape(batch, seq, d)
```

```python
import functools

import numpy as np
import jax
import jax.numpy as jnp
from jax import lax
from jax.experimental import pallas as pl
from jax.experimental.pallas import tpu as pltpu

F32 = jnp.float32
BF16 = jnp.bfloat16
HIGHEST = lax.Precision.HIGHEST

D_MODEL = 1024
SEQ = 2048
RMS_EPS = 1e-6
NEG_INF = -1e30

MLSTM_HEADS = 4
MLSTM_V_DIM = 256
MLSTM_QK_DIM = 128
MLSTM_L = 256
GATE_SOFTCAP = 15.0

NSA_HEADS = 16
NSA_HEAD_DIM = 64
NSA_GROUPS = 4
NSA_GROUP_SIZE = 4
CMP_BLOCK = 32
CMP_STRIDE = 16
N_CMP_PAD = 128
SEL_BLOCK = 64
N_SEL = SEQ // SEL_BLOCK
SEL_TOPK = 16
WINDOW = 512
Q_BLOCK = 256
N_QB = SEQ // Q_BLOCK
WIN_KEYS = WINDOW + Q_BLOCK
SLC_TILE = 512
ATTN_GROUPS = 2
KV_LANES = 2 * NSA_HEAD_DIM
V_ROWS = NSA_HEAD_DIM + 16
DEN_ROW = NSA_HEAD_DIM
FORCED_SCORE = 1e6
ROPE_THETA = 500000.0
ROPE_DIM = 16

MOE_GROUPS = 4
MOE_PER_GROUP = 8
MOE_EXPERTS = 32
MOE_HIDDEN = 256
MOE_PAIRS = 28
MOE_CLASSES = MOE_GROUPS * MOE_PAIRS
ROW_TILE = 256
XE_W = D_MODEL + 128
LANE_WLO, LANE_WHI, LANE_ROUTE = 0, 1, 2
ROUTE_SHIFT = 16
ISSUE_ROWS = 8

TM = 512
VMEM_LIMIT = 56 * 1024 * 1024

_NT = (((1,), (1,)), ((), ()))
_TN = (((0,), (0,)), ((), ()))


def _cparams(*sem):
    return pltpu.CompilerParams(dimension_semantics=sem, vmem_limit_bytes=VMEM_LIMIT)


def _rms(x):
    return x * lax.rsqrt(jnp.mean(x * x, axis=-1, keepdims=True) + RMS_EPS)


def _sigmoid(x):
    return 1.0 / (1.0 + jnp.exp(-x))


LOG2_E = 1.4426950408889634


def _split_bf16(x):
    hi = x.astype(BF16)
    return hi, (x - hi.astype(F32)).astype(BF16)


def _dot_split(x, w_hi, w_lo):
    x_hi, x_lo = _split_bf16(x)
    return (jnp.dot(x_hi, w_hi, preferred_element_type=F32) + jnp.dot(x_hi, w_lo, preferred_element_type=F32)
            + jnp.dot(x_lo, w_hi, preferred_element_type=F32))


def _mlstm_in_kernel(x_ref, g_ref, w_ref, wgh_ref, wgl_ref, bg_ref, q_ref, k_ref, v_ref, o_ref, gr_ref):
    hn = _rms(x_ref[...]) * g_ref[...]
    hb = hn.astype(BF16)
    q_ref[...] = jnp.dot(hb, w_ref[:, 0:512], preferred_element_type=F32).astype(BF16)
    k = jnp.dot(hb, w_ref[:, 512:1024], preferred_element_type=F32)
    k_ref[...] = (k * (MLSTM_QK_DIM ** -0.5)).astype(BF16)
    v_ref[...] = jnp.dot(hb, w_ref[:, 1024:2048], preferred_element_type=F32).astype(BF16)
    o_ref[...] = _sigmoid(jnp.dot(hb, w_ref[:, 2048:3072], preferred_element_type=F32))
    gates = _dot_split(hn, wgh_ref[...], wgl_ref[...]) + bg_ref[...]
    gates = GATE_SOFTCAP * jnp.tanh(gates / GATE_SOFTCAP)
    lane = lax.broadcasted_iota(jnp.int32, gates.shape, 1)
    log_f = jnp.minimum(gates, 0.0) - jnp.log1p(jnp.exp(-jnp.abs(gates)))
    lg = jnp.where(lane < MLSTM_HEADS, gates, log_f)
    gr_ref[...] = lg.T[0:8, :]


def _mlstm_in(x2d, norm_g, w_qkvo, w_gate, b_gate):
    n = x2d.shape[0]
    w_gate_hi, w_gate_lo = _split_bf16(w_gate)
    row = lambda i: (i, 0)
    fixed = lambda i: (0, 0)
    return pl.pallas_call(
        _mlstm_in_kernel,
        grid=(n // TM,),
        in_specs=[pl.BlockSpec((TM, D_MODEL), row),
                  pl.BlockSpec((1, D_MODEL), fixed),
                  pl.BlockSpec((D_MODEL, 3072), fixed),
                  pl.BlockSpec((D_MODEL, 128), fixed),
                  pl.BlockSpec((D_MODEL, 128), fixed),
                  pl.BlockSpec((1, 128), fixed)],
        out_specs=[pl.BlockSpec((TM, 512), row),
                   pl.BlockSpec((TM, 512), row),
                   pl.BlockSpec((TM, 1024), row),
                   pl.BlockSpec((TM, 1024), row),
                   pl.BlockSpec((8, TM), lambda i: (0, i))],
        out_shape=[jax.ShapeDtypeStruct((n, 512), BF16),
                   jax.ShapeDtypeStruct((n, 512), BF16),
                   jax.ShapeDtypeStruct((n, 1024), BF16),
                   jax.ShapeDtypeStruct((n, 1024), F32),
                   jax.ShapeDtypeStruct((8, n), F32)],
        compiler_params=_cparams("parallel"),
        name="mlstm_in",
    )(x2d, norm_g, w_qkvo, w_gate_hi, w_gate_lo, b_gate)


def _mlstm_scan_kernel(q_ref, k_ref, v_ref, o_ref, gr_ref, hn_ref, out_ref, c_ref, n_ref, m_ref):
    L = MLSTM_L

    @pl.when(pl.program_id(1) == 0)
    def _():
        c_ref[...] = jnp.zeros_like(c_ref)
        n_ref[...] = jnp.zeros_like(n_ref)
        m_ref[...] = jnp.zeros_like(m_ref)

    row = lax.broadcasted_iota(jnp.int32, (L, L), 0)
    col = lax.broadcasted_iota(jnp.int32, (L, L), 1)
    causal = col <= row
    tril = causal.astype(F32)
    eye = (col == row).astype(F32)
    gr = gr_ref[...]
    gr_pad = jnp.concatenate([gr, jnp.zeros((120, L), F32)], axis=0)
    b_row = lax.dot_general(gr, tril, _NT, precision=HIGHEST, preferred_element_type=F32)
    b_col = lax.dot_general(tril, gr_pad, _NT, precision=HIGHEST, preferred_element_type=F32)
    g_col = lax.dot_general(eye, gr_pad, _NT, precision=HIGHEST, preferred_element_type=F32)

    for h in range(MLSTM_HEADS):
        li_row = gr[h:h + 1, :]
        li_col = g_col[:, h:h + 1]
        bf_row = b_row[4 + h:5 + h, :]
        bf_col = b_col[:, 4 + h:5 + h]
        m = m_ref[h:h + 1, 0:1]
        dmat = jnp.where(causal, bf_col - bf_row + li_row, NEG_INF)
        m_inter = bf_col + m
        m_t = jnp.maximum(m_inter, jnp.max(dmat, axis=-1, keepdims=True))
        w_intra = jnp.exp(dmat - m_t)
        w_inter = jnp.exp(m_inter - m_t)
        qh = q_ref[:, h * 128:(h + 1) * 128]
        kh = k_ref[:, h * 128:(h + 1) * 128]
        vh = v_ref[:, h * 256:(h + 1) * 256]
        s = lax.dot_general(qh, kh, _NT, preferred_element_type=F32) * w_intra
        c_old = c_ref[h]
        num = (jnp.dot(s.astype(BF16), vh, preferred_element_type=F32)
               + w_inter * jnp.dot(qh, c_old.astype(BF16), preferred_element_type=F32))
        n_old = n_ref[h:h + 1, :]
        qn = jnp.sum(qh.astype(F32) * n_old, axis=-1, keepdims=True)
        den = jnp.sum(s, axis=-1, keepdims=True) + w_inter * qn
        hh = num / jnp.maximum(jnp.abs(den), jnp.exp(-m_t))
        b_end = bf_col[L - 1:L, :]
        g = b_end - bf_col + li_col
        m_new = jnp.maximum(b_end + m, jnp.max(g, axis=0, keepdims=True))
        ws = jnp.exp(g - m_new)
        decay = jnp.exp(b_end + m - m_new)
        kf = kh.astype(F32) * ws
        c_ref[h] = decay * c_old + lax.dot_general(kf.astype(BF16), vh, _TN, preferred_element_type=F32)
        n_ref[h:h + 1, :] = decay * n_old + jnp.sum(kf, axis=0, keepdims=True)
        m_ref[h:h + 1, :] = jnp.broadcast_to(m_new, (1, 128))
        sl = slice(h * 256, (h + 1) * 256)
        out_ref[:, sl] = (_rms(hh) * hn_ref[:, sl] * o_ref[:, sl]).astype(BF16)


def _mlstm_scan(q, k, v, o, gr, head_norm, batch):
    n = q.shape[0]
    nblk = SEQ // MLSTM_L
    row = lambda b, j: (b * nblk + j, 0)
    return pl.pallas_call(
        _mlstm_scan_kernel,
        grid=(batch, nblk),
        in_specs=[pl.BlockSpec((MLSTM_L, 512), row),
                  pl.BlockSpec((MLSTM_L, 512), row),
                  pl.BlockSpec((MLSTM_L, 1024), row),
                  pl.BlockSpec((MLSTM_L, 1024), row),
                  pl.BlockSpec((8, MLSTM_L), lambda b, j: (0, b * nblk + j)),
                  pl.BlockSpec((1, 1024), lambda b, j: (0, 0))],
        out_specs=pl.BlockSpec((MLSTM_L, 1024), row),
        out_shape=jax.ShapeDtypeStruct((n, 1024), BF16),
        scratch_shapes=[pltpu.VMEM((MLSTM_HEADS, MLSTM_QK_DIM, MLSTM_V_DIM), F32),
                        pltpu.VMEM((8, 128), F32),
                        pltpu.VMEM((8, 128), F32)],
        compiler_params=_cparams("parallel", "arbitrary"),
        name="mlstm_scan",
    )(q, k, v, o, gr, head_norm)


def _pair_tables():
    lo, hi = [], []
    for g in range(MOE_GROUPS):
        for a in range(MOE_PER_GROUP):
            for b in range(a + 1, MOE_PER_GROUP):
                lo.append(g * MOE_PER_GROUP + a)
                hi.append(g * MOE_PER_GROUP + b)
    return np.asarray(lo, np.int32), np.asarray(hi, np.int32)


_PAIR_LO, _PAIR_HI = _pair_tables()


def _mix_out_kernel(a_ref, w_ref, res_ref, g_ref, wrh_ref, wrl_ref, br_ref, tril_ref, h_ref, xe_ref, route_ref, cnt_ref,
                    run_ref, *, a_transposed):
    @pl.when(pl.program_id(0) == 0)
    def _():
        run_ref[...] = jnp.zeros_like(run_ref)

    dims = _TN if a_transposed else (((1,), (0,)), ((), ()))
    h = res_ref[...] + lax.dot_general(a_ref[...], w_ref[...], dims, preferred_element_type=F32)
    h_ref[...] = h
    hn = _rms(h) * g_ref[...]
    xe_ref[:, 0:D_MODEL] = hn

    logits = _dot_split(hn, wrh_ref[...], wrl_ref[...]) + br_ref[...]
    lane_i = lax.broadcasted_iota(jnp.int32, logits.shape, 1)
    lane = lane_i.astype(F32)
    ninf = -jnp.inf
    is_g = (lane_i >= MOE_EXPERTS) & (lane_i < MOE_EXPERTS + MOE_GROUPS)
    glog = jnp.where(is_g, logits, ninf)
    gmax = jnp.max(glog, axis=-1, keepdims=True)
    gidx = jnp.min(jnp.where(glog == gmax, lane - MOE_EXPERTS, 99.0), axis=-1, keepdims=True)
    pg_top = 1.0 / jnp.sum(jnp.exp(glog - gmax), axis=-1, keepdims=True)
    lane_grp = (lane_i >> 3).astype(F32)
    in_grp = (lane_i < MOE_EXPERTS) & (lane_grp == gidx)
    ev = jnp.where(in_grp, logits, ninf)
    v1 = jnp.max(ev, axis=-1, keepdims=True)
    i1 = jnp.min(jnp.where(ev == v1, lane, 999.0), axis=-1, keepdims=True)
    ev2 = jnp.where(lane == i1, ninf, ev)
    v2 = jnp.max(ev2, axis=-1, keepdims=True)
    i2 = jnp.min(jnp.where(ev2 == v2, lane, 999.0), axis=-1, keepdims=True)
    e2 = jnp.exp(v2 - v1)
    w1 = pg_top / (1.0 + e2)
    w2 = pg_top * e2 / (1.0 + e2)
    first_lo = i1 < i2
    w_lo = jnp.where(first_lo, w1, w2)
    w_hi = jnp.where(first_lo, w2, w1)
    a = jnp.minimum(i1, i2) - MOE_PER_GROUP * gidx
    b = jnp.maximum(i1, i2) - MOE_PER_GROUP * gidx
    cls = gidx * MOE_PAIRS + a * (15.0 - a) * 0.5 + (b - a - 1.0)

    onehot = lane == cls
    prefix = jnp.dot(tril_ref[...], onehot.astype(BF16), preferred_element_type=F32)
    run = run_ref[0:1, :]
    rank = jnp.sum(jnp.where(onehot, prefix - 1.0 + run, 0.0), axis=-1, keepdims=True)
    run_new = run + prefix[TM - 1:TM, :]
    run_ref[...] = jnp.broadcast_to(run_new, run_ref.shape)
    cnt_ref[...] = jnp.broadcast_to(run_new, cnt_ref.shape)

    route = cls * float(2 ** ROUTE_SHIFT) + rank
    meta = jnp.where(lane_i == LANE_WLO, w_lo,
                     jnp.where(lane_i == LANE_WHI, w_hi,
                               jnp.where(lane_i == LANE_ROUTE, route, 0.0)))
    xe_ref[:, D_MODEL:XE_W] = meta
    route_ref[...] = meta.T[0:8, :].astype(jnp.int32)


def _mix_out(a, w, res, g_moe, w_rt, b_rt, tril, a_transposed):
    n = res.shape[0]
    kdim = w.shape[0]
    row = lambda i: (i, 0)
    fixed = lambda i: (0, 0)
    a_spec = pl.BlockSpec((kdim, TM), lambda i: (0, i)) if a_transposed else pl.BlockSpec((TM, kdim), row)
    w_rt_hi, w_rt_lo = _split_bf16(w_rt)
    return pl.pallas_call(
        functools.partial(_mix_out_kernel, a_transposed=a_transposed),
        grid=(n // TM,),
        in_specs=[a_spec,
                  pl.BlockSpec((kdim, D_MODEL), fixed),
                  pl.BlockSpec((TM, D_MODEL), row),
                  pl.BlockSpec((1, D_MODEL), fixed),
                  pl.BlockSpec((D_MODEL, 128), fixed),
                  pl.BlockSpec((D_MODEL, 128), fixed),
                  pl.BlockSpec((1, 128), fixed),
                  pl.BlockSpec((TM, TM), fixed)],
        out_specs=[pl.BlockSpec((TM, D_MODEL), row),
                   pl.BlockSpec((TM, XE_W), row),
                   pl.BlockSpec((8, TM), lambda i: (0, i)),
                   pl.BlockSpec((8, 128), fixed)],
        out_shape=[jax.ShapeDtypeStruct((n, D_MODEL), F32),
                   jax.ShapeDtypeStruct((n, XE_W), F32),
                   jax.ShapeDtypeStruct((8, n), jnp.int32),
                   jax.ShapeDtypeStruct((8, 128), F32)],
        scratch_shapes=[pltpu.VMEM((8, 128), F32)],
        compiler_params=_cparams("arbitrary"),
        name="mix_out",
    )(a, w, res, g_moe, w_rt_hi, w_rt_lo, b_rt, tril)


def _sorted_row(route_ref, offs_ref, idx):
    r = route_ref[idx]
    return offs_ref[r >> ROUTE_SHIFT] + (r & (2 ** ROUTE_SHIFT - 1))


def _dispatch_kernel(route_ref, offs_ref, cnt_ref, nused_ref, xe_ref, xs_ref, zbuf, sem, zsem):
    i = pl.program_id(0)
    base = i * TM

    @pl.when(i == 0)
    def _():
        zbuf[...] = jnp.zeros_like(zbuf)

        def per_class(c, carry):
            cnt = cnt_ref[c]
            start = offs_ref[c] + cnt
            pad = (-cnt) & (ROW_TILE - 1)

            def fill(r, inner):
                pltpu.make_async_copy(zbuf.at[pl.ds(0, 1)], xs_ref.at[pl.ds(start + r, 1)], zsem).start()
                return inner

            def drain(r, inner):
                pltpu.make_async_copy(zbuf.at[pl.ds(0, 1)], xs_ref.at[pl.ds(0, 1)], zsem).wait()
                return inner

            lax.fori_loop(0, pad, fill, 0)
            lax.fori_loop(0, pad, drain, 0)
            return carry

        lax.fori_loop(0, MOE_CLASSES, per_class, 0)

        def tail(t, carry):
            row0 = pl.multiple_of(t * ROW_TILE, ROW_TILE)
            cp = pltpu.make_async_copy(zbuf, xs_ref.at[pl.ds(row0, ROW_TILE)], zsem)
            cp.start()
            cp.wait()
            return carry

        lax.fori_loop(nused_ref[0], xs_ref.shape[0] // ROW_TILE, tail, 0)

    def issue(t8, carry):
        t0 = pl.multiple_of(t8 * ISSUE_ROWS, ISSUE_ROWS)
        for r in range(ISSUE_ROWS):
            p = _sorted_row(route_ref, offs_ref, base + t0 + r)
            pltpu.make_async_copy(xe_ref.at[pl.ds(t0 + r, 1)], xs_ref.at[pl.ds(p, 1)], sem).start()
        return carry

    lax.fori_loop(0, TM // ISSUE_ROWS, issue, 0)
    pltpu.make_async_copy(xe_ref, xs_ref.at[pl.ds(0, TM)], sem).wait()


def _dispatch(route, offs, cnt, n_used, xe, n_rows):
    n = xe.shape[0]
    grid_spec = pltpu.PrefetchScalarGridSpec(
        num_scalar_prefetch=4,
        grid=(n // TM,),
        in_specs=[pl.BlockSpec((TM, XE_W), lambda i, *_: (i, 0))],
        out_specs=pl.BlockSpec(memory_space=pl.ANY),
        scratch_shapes=[pltpu.VMEM((ROW_TILE, XE_W), F32), pltpu.SemaphoreType.DMA(()),
                        pltpu.SemaphoreType.DMA(())],
    )
    return pl.pallas_call(
        _dispatch_kernel,
        grid_spec=grid_spec,
        out_shape=jax.ShapeDtypeStruct((n_rows, XE_W), F32),
        compiler_params=_cparams("arbitrary"),
        name="moe_dispatch",
    )(route, offs, cnt, n_used, xe)


def _experts_kernel(tlo_ref, thi_ref, nused_ref, xs_ref, wg_lo, wu_lo, wd_lo, wg_hi, wu_hi, wd_hi, y_ref):
    del tlo_ref, thi_ref

    @pl.when(pl.program_id(0) < nused_ref[0])
    def _():
        x = xs_ref[:, 0:D_MODEL].astype(BF16)

        def ffn(wg, wu, wd, w):
            a = jnp.dot(x, wg[0], preferred_element_type=F32)
            u = jnp.dot(x, wu[0], preferred_element_type=F32)
            hid = (a * _sigmoid(a)) * u * w
            return jnp.dot(hid.astype(BF16), wd[0], preferred_element_type=F32)

        w_lo = xs_ref[:, D_MODEL + LANE_WLO:D_MODEL + LANE_WLO + 1]
        w_hi = xs_ref[:, D_MODEL + LANE_WHI:D_MODEL + LANE_WHI + 1]
        y_ref[...] = ffn(wg_lo, wu_lo, wd_lo, w_lo) + ffn(wg_hi, wu_hi, wd_hi, w_hi)

    @pl.when(pl.program_id(0) >= nused_ref[0])
    def _():
        y_ref[...] = jnp.zeros_like(y_ref)


def _experts(tile_lo, tile_hi, n_used, xs, w_gate, w_up, w_down):
    n_tiles = xs.shape[0] // ROW_TILE
    rows = lambda i, tlo, thi, nu: (jnp.maximum(jnp.minimum(i, nu[0] - 1), 0), 0)
    lo = lambda i, tlo, thi, nu: (tlo[i], 0, 0)
    hi = lambda i, tlo, thi, nu: (thi[i], 0, 0)
    up_spec = lambda m: pl.BlockSpec((1, D_MODEL, MOE_HIDDEN), m)
    dn_spec = lambda m: pl.BlockSpec((1, MOE_HIDDEN, D_MODEL), m)
    grid_spec = pltpu.PrefetchScalarGridSpec(
        num_scalar_prefetch=3,
        grid=(n_tiles,),
        in_specs=[pl.BlockSpec((ROW_TILE, XE_W), rows),
                  up_spec(lo), up_spec(lo), dn_spec(lo),
                  up_spec(hi), up_spec(hi), dn_spec(hi)],
        out_specs=pl.BlockSpec((ROW_TILE, D_MODEL), lambda i, tlo, thi, nu: (i, 0)),
    )
    return pl.pallas_call(
        _experts_kernel,
        grid_spec=grid_spec,
        out_shape=jax.ShapeDtypeStruct((xs.shape[0], D_MODEL), F32),
        compiler_params=_cparams("arbitrary"),
        name="moe_experts",
    )(tile_lo, tile_hi, n_used, xs, w_gate, w_up, w_down, w_gate, w_up, w_down)


def _combine_kernel(route_ref, offs_ref, h_ref, y_ref, g_ref, out_ref, buf, sem, *, final_norm):
    base = pl.program_id(0) * TM

    def issue(t8, carry):
        t0 = pl.multiple_of(t8 * ISSUE_ROWS, ISSUE_ROWS)
        for r in range(ISSUE_ROWS):
            p = _sorted_row(route_ref, offs_ref, base + t0 + r)
            pltpu.make_async_copy(y_ref.at[pl.ds(p, 1)], buf.at[pl.ds(t0 + r, 1)], sem).start()
        return carry

    lax.fori_loop(0, TM // ISSUE_ROWS, issue, 0)
    pltpu.make_async_copy(y_ref.at[pl.ds(0, TM)], buf, sem).wait()
    out = h_ref[...] + buf[...]
    if final_norm:
        out = _rms(out) * g_ref[...]
    out_ref[...] = out


def _combine(route, offs, h, y, gain, final_norm):
    n = h.shape[0]
    grid_spec = pltpu.PrefetchScalarGridSpec(
        num_scalar_prefetch=2,
        grid=(n // TM,),
        in_specs=[pl.BlockSpec((TM, D_MODEL), lambda i, *_: (i, 0)),
                  pl.BlockSpec(memory_space=pl.ANY),
                  pl.BlockSpec((1, D_MODEL), lambda i, *_: (0, 0))],
        out_specs=pl.BlockSpec((TM, D_MODEL), lambda i, *_: (i, 0)),
        scratch_shapes=[pltpu.VMEM((TM, D_MODEL), F32), pltpu.SemaphoreType.DMA(())],
    )
    return pl.pallas_call(
        functools.partial(_combine_kernel, final_norm=final_norm),
        grid_spec=grid_spec,
        out_shape=jax.ShapeDtypeStruct((n, D_MODEL), F32),
        compiler_params=_cparams("arbitrary"),
        name="moe_combine",
    )(route, offs, h, y, gain)


def _moe_layer(a, w_out, res, moe_norm, w_group, b_group, w_router, b_router, w_gate, w_up, w_down,
               tril, out_gain, final_norm, a_transposed=False):
    n = res.shape[0]
    w_rt = jnp.zeros((D_MODEL, 128), F32).at[:, 0:MOE_EXPERTS].set(w_router)
    w_rt = w_rt.at[:, MOE_EXPERTS:MOE_EXPERTS + MOE_GROUPS].set(w_group)
    b_rt = jnp.zeros((1, 128), F32).at[0, 0:MOE_EXPERTS].set(b_router)
    b_rt = b_rt.at[0, MOE_EXPERTS:MOE_EXPERTS + MOE_GROUPS].set(b_group)
    h, xe, route8, counts = _mix_out(a, w_out, res, moe_norm.reshape(1, D_MODEL), w_rt, b_rt, tril, a_transposed)

    n_tiles = n // ROW_TILE + MOE_CLASSES
    cnt = counts[0].astype(jnp.int32)
    tiles_c = (cnt + ROW_TILE - 1) // ROW_TILE
    tile_end = jnp.cumsum(tiles_c)
    offs = (tile_end - tiles_c) * ROW_TILE
    n_used = tile_end[-1]
    tile_ids = jnp.minimum(jnp.arange(n_tiles, dtype=jnp.int32), n_used - 1)
    tile_cls = jnp.sum((tile_end[None, 0:MOE_CLASSES] <= tile_ids[:, None]).astype(jnp.int32), axis=1)
    tile_cls = jnp.clip(tile_cls, 0, MOE_CLASSES - 1)
    tile_lo = jnp.asarray(_PAIR_LO)[tile_cls]
    tile_hi = jnp.asarray(_PAIR_HI)[tile_cls]
    route = route8[LANE_ROUTE]

    n_used = n_used.reshape(1)
    xs = _dispatch(route, offs, cnt, n_used, xe, n_tiles * ROW_TILE)
    y = _experts(tile_lo, tile_hi, n_used, xs, w_gate.astype(BF16), w_up.astype(BF16),
                 w_down.astype(BF16))
    return _combine(route, offs, h, y, out_gain.reshape(1, D_MODEL), final_norm)


def _rope_tables(pos, width):
    half = ROPE_DIM // 2
    inv_freq = jnp.power(jnp.float32(ROPE_THETA), -jnp.arange(half, dtype=F32) * (2.0 / ROPE_DIM))
    ang = pos.astype(F32)[:, None] * inv_freq[None, :]
    cos, sin = jnp.cos(ang), jnp.sin(ang)
    t = pos.shape[0]
    rest = NSA_HEAD_DIM - ROPE_DIM
    cos_t = jnp.concatenate([cos, cos, jnp.ones((t, rest), F32)], axis=1)
    sin_a = jnp.concatenate([-sin, jnp.zeros((t, half + rest), F32)], axis=1)
    sin_b = jnp.concatenate([jnp.zeros((t, half), F32), sin, jnp.zeros((t, rest), F32)], axis=1)
    rep = width // NSA_HEAD_DIM
    return jnp.tile(cos_t, (1, rep)), jnp.tile(sin_a, (1, rep)), jnp.tile(sin_b, (1, rep))


def _rope(x, cos_t, sin_a, sin_b):
    half = ROPE_DIM // 2
    parts = []
    for c in range(x.shape[1] // 128):
        xc = x[:, c * 128:(c + 1) * 128]
        parts.append(xc * cos_t + pltpu.roll(xc, 128 - half, axis=1) * sin_a + pltpu.roll(xc, half, axis=1) * sin_b)
    return parts[0] if len(parts) == 1 else jnp.concatenate(parts, axis=1)


def _rope_rows(x, cos_t, sin_a, sin_b):
    half = ROPE_DIM // 2
    reps = x.shape[0] // NSA_HEAD_DIM
    tile = lambda t: jnp.concatenate([t] * reps, axis=0)
    up = jnp.concatenate([x[half:], x[:half]], axis=0)
    down = jnp.concatenate([x[-half:], x[:-half]], axis=0)
    return x * tile(cos_t) + up * tile(sin_a) + down * tile(sin_b)


def _nsa_proj_kernel(h_ref, gq_ref, gkv_ref, wqt_ref, wkv_ref, wvt_ref, cos_ref, sa_ref, sb_ref,
                     cost_ref, sat_ref, sbt_ref,
                     qt_ref, gatet_ref, kc0_ref, kc1_ref, vc0_ref, vc1_ref, ks_ref, vst_ref, kw_ref, vwt_ref):
    r = _rms(h_ref[...])
    hq = (r * gq_ref[...]).astype(BF16)
    hk = (r * gkv_ref[...]).astype(BF16)
    qt = lax.dot_general(wqt_ref[0:1024, :], hq, _NT, preferred_element_type=F32)
    q_scale = NSA_HEAD_DIM ** -0.5 * LOG2_E
    qt_ref[...] = (_rope_rows(qt, cost_ref[...], sat_ref[...], sbt_ref[...]) * q_scale).astype(BF16)
    gatet_ref[...] = _sigmoid(lax.dot_general(wqt_ref[1024:1152, :], hq, _NT, preferred_element_type=F32))
    kc0_ref[...] = jnp.dot(hk, wkv_ref[:, 0:128], preferred_element_type=F32)
    kc1_ref[...] = jnp.dot(hk, wkv_ref[:, 128:256], preferred_element_type=F32)
    vc0_ref[...] = jnp.dot(hk, wkv_ref[:, 256:384], preferred_element_type=F32)
    vc1_ref[...] = jnp.dot(hk, wkv_ref[:, 384:512], preferred_element_type=F32)

    tm = h_ref.shape[0]
    cos_t, sin_a, sin_b = cos_ref[...], sa_ref[...], sb_ref[...]
    lane = lax.broadcasted_iota(jnp.int32, (tm, NSA_HEAD_DIM), 1)
    pos = (pl.program_id(0) % (SEQ // tm)) * tm + lax.broadcasted_iota(jnp.int32, (tm, NSA_HEAD_DIM), 0)
    blk_onehot = jnp.where(lane == (pos >> 6), 1.0, 0.0).astype(BF16)
    zeros = jnp.zeros((tm, NSA_HEAD_DIM), BF16)

    def store_keys(ref, val, aux):
        for g in range(NSA_GROUPS):
            ref[0, g] = jnp.concatenate([val[:, g * 64:(g + 1) * 64].astype(BF16), aux], axis=1)

    store_keys(ks_ref, _rope(jnp.dot(hk, wkv_ref[:, 512:768], preferred_element_type=F32), cos_t, sin_a, sin_b),
               blk_onehot)
    store_keys(kw_ref, _rope(jnp.dot(hk, wkv_ref[:, 1024:1280], preferred_element_type=F32), cos_t, sin_a, sin_b),
               zeros)

    row = lax.broadcasted_iota(jnp.int32, (V_ROWS - NSA_HEAD_DIM, tm), 0)
    ones_row = jnp.where(row == 0, 1.0, 0.0).astype(BF16)

    def store_values(ref, val_t):
        for g in range(NSA_GROUPS):
            ref[0, g] = jnp.concatenate([val_t[g * 64:(g + 1) * 64, :].astype(BF16), ones_row], axis=0)

    store_values(vst_ref, lax.dot_general(wvt_ref[0:256, :], hk, _NT, preferred_element_type=F32))
    store_values(vwt_ref, lax.dot_general(wvt_ref[256:512, :], hk, _NT, preferred_element_type=F32))


def _nsa_proj(h, g_q, g_kv, w_qt, w_kv, w_vt, rope_tabs, rope_tabs_t, batch):
    n = h.shape[0]
    nblk = SEQ // TM
    row = lambda i: (i, 0)
    col = lambda i: (0, i)
    fixed = lambda i: (0, 0)
    tab = lambda i: (i % nblk, 0)
    tab_t = lambda i: (0, i % nblk)
    key_spec = pl.BlockSpec((1, NSA_GROUPS, TM, KV_LANES), lambda i: (i // nblk, 0, i % nblk, 0))
    key_shape = jax.ShapeDtypeStruct((batch, NSA_GROUPS, SEQ, KV_LANES), BF16)
    val_spec = pl.BlockSpec((1, NSA_GROUPS, V_ROWS, TM), lambda i: (i // nblk, 0, 0, i % nblk))
    val_shape = jax.ShapeDtypeStruct((batch, NSA_GROUPS, V_ROWS, SEQ), BF16)
    raw_spec = pl.BlockSpec((TM, 128), row)
    raw_shape = jax.ShapeDtypeStruct((n, 128), F32)
    return pl.pallas_call(
        _nsa_proj_kernel,
        grid=(n // TM,),
        in_specs=[pl.BlockSpec((TM, D_MODEL), row),
                  pl.BlockSpec((1, D_MODEL), fixed),
                  pl.BlockSpec((1, D_MODEL), fixed),
                  pl.BlockSpec((1152, D_MODEL), fixed),
                  pl.BlockSpec((D_MODEL, 1536), fixed),
                  pl.BlockSpec((512, D_MODEL), fixed),
                  pl.BlockSpec((TM, 128), tab), pl.BlockSpec((TM, 128), tab), pl.BlockSpec((TM, 128), tab),
                  pl.BlockSpec((NSA_HEAD_DIM, TM), tab_t), pl.BlockSpec((NSA_HEAD_DIM, TM), tab_t),
                  pl.BlockSpec((NSA_HEAD_DIM, TM), tab_t)],
        out_specs=[pl.BlockSpec((1024, TM), col),
                   pl.BlockSpec((128, TM), col),
                   raw_spec, raw_spec, raw_spec, raw_spec,
                   key_spec, val_spec, key_spec, val_spec],
        out_shape=[jax.ShapeDtypeStruct((1024, n), BF16),
                   jax.ShapeDtypeStruct((128, n), F32),
                   raw_shape, raw_shape, raw_shape, raw_shape,
                   key_shape, val_shape, key_shape, val_shape],
        compiler_params=_cparams("parallel"),
        name="nsa_proj",
    )(h, g_q, g_kv, w_qt, w_kv, w_vt, *rope_tabs, *rope_tabs_t)


HALF_BLOCKS = SEQ // CMP_STRIDE
CMP_K = CMP_STRIDE * 256


def _compress_kernel(rk0_ref, rk1_ref, rv0_ref, rv1_ref, w1k_ref, w1v_ref, pek_ref, pev_ref, w2k_ref, w2v_ref,
                     cos_ref, sa_ref, sb_ref, kc_ref, vc_ref):
    lane = lax.broadcasted_iota(jnp.int32, (HALF_BLOCKS, 512), 1)
    first_half = (lane & 127) < 64

    def comp(raw_refs, w1_ref, pe_ref, w2_ref):
        x = jnp.concatenate([r[pl.ds(l, HALF_BLOCKS, stride=CMP_STRIDE), :]
                             for l in range(CMP_STRIDE) for r in raw_refs], axis=1).astype(BF16)
        r = jnp.dot(x, w1_ref[...], preferred_element_type=F32)
        rpe = jnp.dot(pe_ref[...].astype(BF16), w1_ref[...], preferred_element_type=F32)
        r = r + jnp.where(first_half, rpe[0:1, :], rpe[1:2, :])
        nxt = pltpu.roll(r, HALF_BLOCKS - 1, axis=0)
        nxt = jnp.concatenate([pltpu.roll(nxt[:, c * 128:(c + 1) * 128], 64, axis=1) for c in range(4)], axis=1)
        pre = r + nxt
        act = pre * _sigmoid(pre)
        return jnp.dot(act.astype(BF16), w2_ref[...], preferred_element_type=F32)

    kc = comp((rk0_ref, rk1_ref), w1k_ref, pek_ref, w2k_ref)
    kc = _rope(kc, cos_ref[...], sa_ref[...], sb_ref[...])
    vc = comp((rv0_ref, rv1_ref), w1v_ref, pev_ref, w2v_ref)
    for g in range(NSA_GROUPS):
        kc_ref[0, g] = kc[:, g * 64:(g + 1) * 64]
        vc_ref[0, g] = vc[:, g * 64:(g + 1) * 64]


def _compress_weights(pe, w1, w2):
    dh = NSA_HEAD_DIM
    w1r = w1.reshape(2, CMP_STRIDE, dh, dh)
    big = jnp.zeros((CMP_STRIDE, NSA_GROUPS, dh, NSA_GROUPS, 2, dh), F32)
    for g in range(NSA_GROUPS):
        big = big.at[:, g, :, g, 0, :].set(w1r[0])
        big = big.at[:, g, :, g, 1, :].set(w1r[1])
    w1_big = big.reshape(CMP_K, 512).astype(BF16)
    pe_rows = jnp.zeros((8, CMP_STRIDE, NSA_GROUPS, dh), F32)
    pe_rows = pe_rows.at[0].set(jnp.broadcast_to(pe[0:CMP_STRIDE, None, :], (CMP_STRIDE, NSA_GROUPS, dh)))
    pe_rows = pe_rows.at[1].set(jnp.broadcast_to(pe[CMP_STRIDE:, None, :], (CMP_STRIDE, NSA_GROUPS, dh)))
    w2_bd = jnp.zeros((NSA_GROUPS, 2, dh, NSA_GROUPS, dh), F32)
    for g in range(NSA_GROUPS):
        w2_bd = w2_bd.at[g, 0, :, g, :].set(w2)
    return w1_big, pe_rows.reshape(8, CMP_K), w2_bd.reshape(512, 256).astype(BF16)


def _compress(raw_k, raw_v, wk, wv, cmp_tabs, batch):
    fixed = lambda b: (0, 0)
    raw_spec = pl.BlockSpec((SEQ, 128), lambda b: (b, 0))
    out_spec = pl.BlockSpec((1, NSA_GROUPS, N_CMP_PAD, NSA_HEAD_DIM), lambda b: (b, 0, 0, 0))
    out_shape = jax.ShapeDtypeStruct((batch, NSA_GROUPS, N_CMP_PAD, NSA_HEAD_DIM), F32)
    return pl.pallas_call(
        _compress_kernel,
        grid=(batch,),
        in_specs=[raw_spec, raw_spec, raw_spec, raw_spec,
                  pl.BlockSpec((CMP_K, 512), fixed), pl.BlockSpec((CMP_K, 512), fixed),
                  pl.BlockSpec((8, CMP_K), fixed), pl.BlockSpec((8, CMP_K), fixed),
                  pl.BlockSpec((512, 256), fixed), pl.BlockSpec((512, 256), fixed),
                  pl.BlockSpec((N_CMP_PAD, 128), fixed), pl.BlockSpec((N_CMP_PAD, 128), fixed),
                  pl.BlockSpec((N_CMP_PAD, 128), fixed)],
        out_specs=[out_spec, out_spec],
        out_shape=[out_shape, out_shape],
        compiler_params=_cparams("parallel"),
        name="nsa_compress",
    )(*raw_k, *raw_v, wk[0], wv[0], wk[1], wv[1], wk[2], wv[2], *cmp_tabs)


def _nsa_attn_kernel(qt_ref, gatet_ref, kc_ref, vc_ref, ks_ref, vst_ref, kw_ref, vwt_ref, ovl_ref, out_ref,
                     qx_sc, ms_sc, as_sc, aw_sc):
    qb = pl.program_id(2)
    q0 = qb * Q_BLOCK
    Q, HG, DH = Q_BLOCK, NSA_GROUP_SIZE, NSA_HEAD_DIM
    groups = range(ATTN_GROUPS)

    def heads(x):
        return jnp.concatenate([x] * HG, axis=1)

    n_row = lax.broadcasted_iota(jnp.int32, (N_CMP_PAD, Q), 0)
    t_lane = q0 + lax.broadcasted_iota(jnp.int32, (N_CMP_PAD, Q), 1)
    valid_c = heads((n_row * CMP_STRIDE + CMP_BLOCK - 1 <= t_lane) & (n_row < N_CMP_PAD - 1))
    j_row = lax.broadcasted_iota(jnp.int32, (N_SEL, Q), 0)
    cur = (q0 + lax.broadcasted_iota(jnp.int32, (N_SEL, Q), 1)) >> 6
    forced = (j_row == 0) | (j_row == cur) | (j_row == cur - 1)
    o_c = []
    for gi in groups:
        q4 = qt_ref[gi * HG * DH:(gi + 1) * HG * DH, :]
        q_t = jnp.concatenate([q4[h * DH:(h + 1) * DH, :] for h in range(HG)], axis=1)
        s_c = jnp.dot(kc_ref[0, gi].astype(BF16), q_t, preferred_element_type=F32)
        s_c = jnp.where(valid_c, s_c, NEG_INF)
        m_c = jnp.max(s_c, axis=0, keepdims=True)
        e_c = jnp.where(valid_c, jnp.exp2(s_c - m_c), 0.0)
        l_c = jnp.sum(e_c, axis=0, keepdims=True)
        p_c = e_c * jnp.where(l_c > 0.0, 1.0 / l_c, 0.0)
        o_c.append(jnp.dot(vc_ref[0, gi].T.astype(BF16), p_c.astype(BF16), preferred_element_type=F32))

        p_sum = p_c[:, 0:Q] + p_c[:, Q:2 * Q] + p_c[:, 2 * Q:3 * Q] + p_c[:, 3 * Q:4 * Q]
        imp_t = jnp.dot(ovl_ref[...], p_sum, precision=HIGHEST, preferred_element_type=F32)[0:N_SEL]
        imp_t = jnp.where(forced, FORCED_SCORE, imp_t)
        imp_t = jnp.where(j_row > cur, NEG_INF, imp_t)
        cnt = jnp.zeros((N_SEL, Q), F32)
        for i in range(N_SEL):
            ri = imp_t[i:i + 1, :]
            cnt = cnt + jnp.where(ri > imp_t, 1.0, jnp.where((ri == imp_t) & (j_row > i), 1.0, 0.0))
        sel_bias = jnp.where((cnt < SEL_TOPK) & (j_row <= cur), 0.0, NEG_INF).astype(BF16)

        qx_sc[gi] = jnp.concatenate([q_t, heads(sel_bias), jnp.zeros((KV_LANES - DH - N_SEL, HG * Q), BF16)],
                                    axis=0)

    def finish(acc):
        return acc[0:DH] / acc[DEN_ROW:DEN_ROW + 1]

    ms_sc[...] = jnp.full(ms_sc.shape, NEG_INF, F32)
    as_sc[...] = jnp.zeros(as_sc.shape, F32)
    n_kt = (q0 + Q + SLC_TILE - 1) // SLC_TILE

    def slc_tile(kt, bias):
        start = pl.multiple_of(kt * SLC_TILE, SLC_TILE)
        for gi in groups:
            s = jnp.dot(ks_ref[0, gi, pl.ds(start, SLC_TILE), :], qx_sc[gi], preferred_element_type=F32)
            if bias is not None:
                s = s + bias
            m_old = ms_sc[gi, 0:1, :]
            m_new = jnp.maximum(m_old, jnp.max(s, axis=0, keepdims=True))
            p = jnp.exp2(s - m_new).astype(BF16)
            pv = jnp.dot(vst_ref[0, gi, :, pl.ds(start, SLC_TILE)], p, preferred_element_type=F32)
            as_sc[gi] = as_sc[gi] * jnp.exp2(m_old - m_new) + pv
            ms_sc[gi] = jnp.broadcast_to(m_new, ms_sc.shape[1:])

    def slc_full(kt, carry):
        slc_tile(kt, None)
        return carry

    lax.fori_loop(0, n_kt - 1, slc_full, 0)
    c_s = lax.broadcasted_iota(jnp.int32, (SLC_TILE, Q), 0)
    r_s = lax.broadcasted_iota(jnp.int32, (SLC_TILE, Q), 1)
    slc_tile(n_kt - 1, heads(jnp.where((n_kt - 1) * SLC_TILE + c_s <= q0 + r_s, 0.0, NEG_INF)))

    c_w = lax.broadcasted_iota(jnp.int32, (Q, Q), 0)
    r_w = lax.broadcasted_iota(jnp.int32, (Q, Q), 1)
    n_slab = WIN_KEYS // Q

    def window(start, slab_bias):
        start = pl.multiple_of(start, Q)
        slab_bias = [None if b is None else heads(b) for b in slab_bias]
        for gi in groups:
            s = jnp.dot(kw_ref[0, gi, pl.ds(start, WIN_KEYS), :], qx_sc[gi], preferred_element_type=F32)
            slabs = []
            for j in range(n_slab):
                sj = s[j * Q:(j + 1) * Q]
                slabs.append(sj if slab_bias[j] is None else sj + slab_bias[j])
            top = slabs[0]
            for sj in slabs[1:]:
                top = jnp.maximum(top, sj)
            m = jnp.max(top, axis=0, keepdims=True)
            p = jnp.concatenate([jnp.exp2(sj - m) for sj in slabs], axis=0).astype(BF16)
            aw_sc[gi] = jnp.dot(vwt_ref[0, gi, :, pl.ds(start, WIN_KEYS)], p, preferred_element_type=F32)

    @pl.when(qb >= WINDOW // Q)
    def _():
        window(q0 - WINDOW, [jnp.where(c_w > r_w, 0.0, NEG_INF)] + [None] * (n_slab - 2)
               + [jnp.where(c_w <= r_w, 0.0, NEG_INF)])

    @pl.when(qb < WINDOW // Q)
    def _():
        window(0, [jnp.where(j * Q + c_w <= q0 + r_w, 0.0, NEG_INF) for j in range(n_slab)])

    for gi in groups:
        g = pl.program_id(1) * ATTN_GROUPS + gi

        def gate_row(branch):
            return jnp.concatenate([gatet_ref[pl.ds((g * HG + h) * 3 + branch, 1), :] for h in range(HG)], axis=1)

        o = gate_row(0) * o_c[gi] + gate_row(1) * finish(as_sc[gi]) + gate_row(2) * finish(aw_sc[gi])
        out_ref[gi * HG * DH:(gi + 1) * HG * DH, :] = jnp.concatenate(
            [o[:, h * Q:(h + 1) * Q] for h in range(HG)], axis=0).astype(BF16)


def _overlap_t():
    n = np.arange(N_CMP_PAD)
    j = np.arange(128)
    cmp_start = n * CMP_STRIDE
    cmp_end = cmp_start + CMP_BLOCK - 1
    sel_start = j * SEL_BLOCK
    ovl = ((cmp_start[None, :] <= sel_start[:, None] + SEL_BLOCK - 1) & (cmp_end[None, :] >= sel_start[:, None])
           & (j[:, None] < N_SEL) & (n[None, :] < N_CMP_PAD - 1))
    return jnp.asarray(ovl.astype(np.float32))


def _nsa_attn(q_t, gates_t, kc, vc, ks, vs_t, kw, vw_t, batch):
    n = q_t.shape[1]
    gs = ATTN_GROUPS
    qcol = lambda b, g, i: (g, b * N_QB + i)
    gcol = lambda b, g, i: (0, b * N_QB + i)
    kv = lambda b, g, i: (b, g, 0, 0)
    rows = gs * NSA_GROUP_SIZE * NSA_HEAD_DIM
    lanes = NSA_GROUP_SIZE * Q_BLOCK
    return pl.pallas_call(
        _nsa_attn_kernel,
        grid=(batch, NSA_GROUPS // gs, N_QB),
        in_specs=[pl.BlockSpec((rows, Q_BLOCK), qcol),
                  pl.BlockSpec((128, Q_BLOCK), gcol),
                  pl.BlockSpec((1, gs, N_CMP_PAD, NSA_HEAD_DIM), kv),
                  pl.BlockSpec((1, gs, N_CMP_PAD, NSA_HEAD_DIM), kv),
                  pl.BlockSpec((1, gs, SEQ, KV_LANES), kv),
                  pl.BlockSpec((1, gs, V_ROWS, SEQ), kv),
                  pl.BlockSpec((1, gs, SEQ, KV_LANES), kv),
                  pl.BlockSpec((1, gs, V_ROWS, SEQ), kv),
                  pl.BlockSpec((128, N_CMP_PAD), lambda b, g, i: (0, 0))],
        out_specs=pl.BlockSpec((rows, Q_BLOCK), qcol),
        out_shape=jax.ShapeDtypeStruct((NSA_HEADS * NSA_HEAD_DIM, n), BF16),
        scratch_shapes=[pltpu.VMEM((gs, KV_LANES, lanes), BF16),
                        pltpu.VMEM((gs, 8, lanes), F32),
                        pltpu.VMEM((gs, V_ROWS, lanes), F32),
                        pltpu.VMEM((gs, V_ROWS, lanes), F32)],
        compiler_params=_cparams("parallel", "parallel", "arbitrary"),
        name="nsa_attn",
    )(q_t, gates_t, kc, vc, ks, vs_t, kw, vw_t, _overlap_t())


def kernel(x, mlstm_norm, mlstm_w_in, mlstm_gate_bias, mlstm_head_norm, mlstm_w_out, kv_norm, kv_w, cmp_pe_k, cmp_w1_k, cmp_w2_k, cmp_pe_v, cmp_w1_v, cmp_w2_v, nsa_norm, nsa_w_q, nsa_w_out, moe_norm, moe_w_group, moe_b_group, moe_w_router, moe_b_router, moe_w_gate, moe_w_up, moe_w_down, final_norm):
    batch, seq, d = x.shape
    assert seq == SEQ and d == D_MODEL
    assert mlstm_norm.shape[0] == 1 and nsa_norm.shape[0] == 1 and moe_norm.shape[0] == 2
    n = batch * seq
    x2d = x.reshape(n, d)
    tril = jnp.tril(jnp.ones((TM, TM), F32)).astype(BF16)

    w_in = mlstm_w_in[0]
    w_gate = jnp.zeros((d, 128), F32).at[:, 0:8].set(w_in[:, 3072:3080])
    b_gate = jnp.zeros((1, 128), F32).at[0, 0:8].set(mlstm_gate_bias[0])
    q, k, v, o, gr = _mlstm_in(x2d, mlstm_norm[0].reshape(1, d), w_in[:, 0:3072].astype(BF16), w_gate, b_gate)
    hs = _mlstm_scan(q, k, v, o, gr, mlstm_head_norm[0].reshape(1, d), batch)
    h = _moe_layer(hs, mlstm_w_out[0].astype(BF16), x2d, moe_norm[0], moe_w_group[0], moe_b_group[0],
                   moe_w_router[0], moe_b_router[0], moe_w_gate[0], moe_w_up[0], moe_w_down[0],
                   tril, final_norm, False)

    w_qt = jnp.zeros((1152, d), F32).at[0:1072].set(nsa_w_q[0].T).astype(BF16)
    w_vt = jnp.concatenate([kv_w[:, 768:1024], kv_w[:, 1280:1536]], axis=1).T.astype(BF16)
    seq_tabs = _rope_tables(jnp.arange(SEQ), 128)
    seq_tabs_t = tuple(t[:, 0:NSA_HEAD_DIM].T for t in seq_tabs)
    q_t, gates_t, rk0, rk1, rv0, rv1, ks, vs_t, kw, vw_t = _nsa_proj(
        h, nsa_norm[0].reshape(1, d), kv_norm.reshape(1, d), w_qt, kv_w.astype(BF16), w_vt, seq_tabs, seq_tabs_t,
        batch)
    cmp_pos = jnp.arange(N_CMP_PAD) * CMP_STRIDE + CMP_BLOCK - 1
    kc, vc = _compress((rk0, rk1), (rv0, rv1), _compress_weights(cmp_pe_k, cmp_w1_k, cmp_w2_k),
                       _compress_weights(cmp_pe_v, cmp_w1_v, cmp_w2_v), _rope_tables(cmp_pos, 128), batch)
    att_t = _nsa_attn(q_t, gates_t, kc, vc, ks, vs_t, kw, vw_t, batch)
    out = _moe_layer(att_t, nsa_w_out[0].astype(BF16), h, moe_norm[1], moe_w_group[1], moe_b_group[1],
                     moe_w_router[1], moe_b_router[1], moe_w_gate[1], moe_w_up[1], moe_w_down[1],
                     tril, final_norm, True, a_transposed=True)
    return out.reshape(batch, seq, d)
```

```python
import functools

import numpy as np
import jax
import jax.numpy as jnp
from jax import lax
from jax.experimental import pallas as pl
from jax.experimental.pallas import tpu as pltpu

F32 = jnp.float32
BF16 = jnp.bfloat16
HIGHEST = lax.Precision.HIGHEST

D_MODEL = 1024
SEQ = 2048
RMS_EPS = 1e-6
NEG_INF = -1e30

MLSTM_HEADS = 4
MLSTM_V_DIM = 256
MLSTM_QK_DIM = 128
MLSTM_L = 256
GATE_SOFTCAP = 15.0

NSA_HEADS = 16
NSA_HEAD_DIM = 64
NSA_GROUPS = 4
NSA_GROUP_SIZE = 4
CMP_BLOCK = 32
CMP_STRIDE = 16
N_CMP_PAD = 128
SEL_BLOCK = 64
N_SEL = SEQ // SEL_BLOCK
SEL_TOPK = 16
WINDOW = 512
Q_BLOCK = 256
N_QB = SEQ // Q_BLOCK
WIN_KEYS = WINDOW + Q_BLOCK
SLC_TILE = 512
ATTN_GROUPS = 2
KV_LANES = 2 * NSA_HEAD_DIM
V_ROWS = NSA_HEAD_DIM + 16
DEN_ROW = NSA_HEAD_DIM
FORCED_SCORE = 1e6
ROPE_THETA = 500000.0
ROPE_DIM = 16

MOE_GROUPS = 4
MOE_PER_GROUP = 8
MOE_EXPERTS = 32
MOE_HIDDEN = 256
MOE_PAIRS = 28
MOE_CLASSES = MOE_GROUPS * MOE_PAIRS
ROW_TILE = 256
XE_W = D_MODEL + 128
LANE_WLO, LANE_WHI, LANE_ROUTE = 0, 1, 2
ROUTE_SHIFT = 16
ISSUE_ROWS = 8

TM = 512
VMEM_LIMIT = 56 * 1024 * 1024

_NT = (((1,), (1,)), ((), ()))
_TN = (((0,), (0,)), ((), ()))


def _cparams(*sem):
    return pltpu.CompilerParams(dimension_semantics=sem, vmem_limit_bytes=VMEM_LIMIT)


def _rms(x):
    return x * lax.rsqrt(jnp.mean(x * x, axis=-1, keepdims=True) + RMS_EPS)


def _sigmoid(x):
    return 1.0 / (1.0 + jnp.exp(-x))


LOG2_E = 1.4426950408889634


def _split_bf16(x):
    hi = x.astype(BF16)
    return hi, (x - hi.astype(F32)).astype(BF16)


def _dot_split(x, w_hi, w_lo):
    x_hi, x_lo = _split_bf16(x)
    return (jnp.dot(x_hi, w_hi, preferred_element_type=F32) + jnp.dot(x_hi, w_lo, preferred_element_type=F32)
            + jnp.dot(x_lo, w_hi, preferred_element_type=F32))


def _mlstm_in_kernel(x_ref, g_ref, wt_ref, wk_ref, wgh_ref, wgl_ref, bg_ref, qt_ref, k_ref, vt_ref, ot_ref, gr_ref):
    hn = _rms(x_ref[...]) * g_ref[...]
    hb = hn.astype(BF16)
    qt_ref[...] = lax.dot_general(wt_ref[0:512, :], hb, _NT, preferred_element_type=F32).astype(BF16)
    k = jnp.dot(hb, wk_ref[...], preferred_element_type=F32)
    k_ref[...] = (k * (MLSTM_QK_DIM ** -0.5)).astype(BF16)
    vt_ref[...] = lax.dot_general(wt_ref[1024:2048, :], hb, _NT, preferred_element_type=F32).astype(BF16)
    ot_ref[...] = _sigmoid(lax.dot_general(wt_ref[2048:3072, :], hb, _NT, preferred_element_type=F32))
    gates = _dot_split(hn, wgh_ref[...], wgl_ref[...]) + bg_ref[...]
    gates = GATE_SOFTCAP * jnp.tanh(gates / GATE_SOFTCAP)
    lane = lax.broadcasted_iota(jnp.int32, gates.shape, 1)
    log_f = jnp.minimum(gates, 0.0) - jnp.log1p(jnp.exp(-jnp.abs(gates)))
    lg = jnp.where(lane < MLSTM_HEADS, gates, log_f)
    gr_ref[...] = lg.T[0:8, :]


def _mlstm_in(x2d, norm_g, w_t, w_k, w_gate, b_gate):
    n = x2d.shape[0]
    w_gate_hi, w_gate_lo = _split_bf16(w_gate)
    row = lambda i: (i, 0)
    col = lambda i: (0, i)
    fixed = lambda i: (0, 0)
    return pl.pallas_call(
        _mlstm_in_kernel,
        grid=(n // TM,),
        in_specs=[pl.BlockSpec((TM, D_MODEL), row),
                  pl.BlockSpec((1, D_MODEL), fixed),
                  pl.BlockSpec((3072, D_MODEL), fixed),
                  pl.BlockSpec((D_MODEL, 512), fixed),
                  pl.BlockSpec((D_MODEL, 128), fixed),
                  pl.BlockSpec((D_MODEL, 128), fixed),
                  pl.BlockSpec((1, 128), fixed)],
        out_specs=[pl.BlockSpec((512, TM), col),
                   pl.BlockSpec((TM, 512), row),
                   pl.BlockSpec((1024, TM), col),
                   pl.BlockSpec((1024, TM), col),
                   pl.BlockSpec((8, TM), col)],
        out_shape=[jax.ShapeDtypeStruct((512, n), BF16),
                   jax.ShapeDtypeStruct((n, 512), BF16),
                   jax.ShapeDtypeStruct((1024, n), BF16),
                   jax.ShapeDtypeStruct((1024, n), F32),
                   jax.ShapeDtypeStruct((8, n), F32)],
        compiler_params=_cparams("parallel"),
        name="mlstm_in",
    )(x2d, norm_g, w_t, w_k, w_gate_hi, w_gate_lo, b_gate)


def _mlstm_scan_kernel(qt_ref, k_ref, vt_ref, ot_ref, gr_ref, hn_ref, out_ref, ct_ref, n_ref, m_ref):
    L = MLSTM_L

    @pl.when(pl.program_id(1) == 0)
    def _():
        ct_ref[...] = jnp.zeros_like(ct_ref)
        n_ref[...] = jnp.zeros_like(n_ref)
        m_ref[...] = jnp.zeros_like(m_ref)

    row = lax.broadcasted_iota(jnp.int32, (L, L), 0)
    col = lax.broadcasted_iota(jnp.int32, (L, L), 1)
    causal_t = row <= col
    tril = (col <= row).astype(F32)
    eye = (col == row).astype(F32)
    gr = gr_ref[...]
    gr_pad = jnp.concatenate([gr, jnp.zeros((120, L), F32)], axis=0)
    b_row = lax.dot_general(gr, tril, _NT, precision=HIGHEST, preferred_element_type=F32)
    b_col = lax.dot_general(tril, gr_pad, _NT, precision=HIGHEST, preferred_element_type=F32)
    g_col = lax.dot_general(eye, gr_pad, _NT, precision=HIGHEST, preferred_element_type=F32)
    n_hi, n_lo = _split_bf16(n_ref[...])

    for h in range(MLSTM_HEADS):
        src_col = g_col[:, h:h + 1] - b_col[:, 4 + h:5 + h]
        bf_row = b_row[4 + h:5 + h, :]
        m = m_ref[h:h + 1, 0:1]
        dmat = jnp.where(causal_t, bf_row + src_col, NEG_INF)
        m_inter = bf_row + m
        m_t = jnp.maximum(m_inter, jnp.max(dmat, axis=0, keepdims=True))
        w_intra = jnp.exp(dmat - m_t)
        w_inter = jnp.exp(m_inter - m_t)
        qt = qt_ref[h * 128:(h + 1) * 128, :]
        kh = k_ref[:, h * 128:(h + 1) * 128]
        vt = vt_ref[h * 256:(h + 1) * 256, :]
        s = jnp.dot(kh, qt, preferred_element_type=F32) * w_intra
        c_old = ct_ref[h]
        num = (jnp.dot(vt, s.astype(BF16), preferred_element_type=F32)
               + w_inter * jnp.dot(c_old.astype(BF16), qt, preferred_element_type=F32))
        qn = (jnp.dot(n_hi, qt, preferred_element_type=F32) + jnp.dot(n_lo, qt, preferred_element_type=F32))[h:h + 1]
        den = jnp.sum(s, axis=0, keepdims=True) + w_inter * qn
        hh = num / jnp.maximum(jnp.abs(den), jnp.exp(-m_t))
        b_end = bf_row[:, L - 1:L]
        g = b_end + src_col
        m_new = jnp.maximum(b_end + m, jnp.max(g, axis=0, keepdims=True))
        ws = jnp.exp(g - m_new)
        decay = jnp.exp(b_end + m - m_new)
        kf = kh.astype(F32) * ws
        ct_ref[h] = decay * c_old + jnp.dot(vt, kf.astype(BF16), preferred_element_type=F32)
        n_ref[h:h + 1, :] = decay * n_ref[h:h + 1, :] + jnp.sum(kf, axis=0, keepdims=True)
        m_ref[h:h + 1, :] = jnp.broadcast_to(m_new, (1, 128))
        rows = slice(h * 256, (h + 1) * 256)
        hnorm = hh * lax.rsqrt(jnp.mean(hh * hh, axis=0, keepdims=True) + RMS_EPS)
        gain = jnp.concatenate([hn_ref[rows, :]] * (L // 128), axis=1)
        out_ref[rows, :] = (hnorm * gain * ot_ref[rows, :]).astype(BF16)


def _mlstm_scan(q_t, k, v_t, o_t, gr, head_norm_cols, batch):
    n = k.shape[0]
    nblk = SEQ // MLSTM_L
    row = lambda b, j: (b * nblk + j, 0)
    col = lambda b, j: (0, b * nblk + j)
    return pl.pallas_call(
        _mlstm_scan_kernel,
        grid=(batch, nblk),
        in_specs=[pl.BlockSpec((512, MLSTM_L), col),
                  pl.BlockSpec((MLSTM_L, 512), row),
                  pl.BlockSpec((1024, MLSTM_L), col),
                  pl.BlockSpec((1024, MLSTM_L), col),
                  pl.BlockSpec((8, MLSTM_L), col),
                  pl.BlockSpec((1024, 128), lambda b, j: (0, 0))],
        out_specs=pl.BlockSpec((1024, MLSTM_L), col),
        out_shape=jax.ShapeDtypeStruct((1024, n), BF16),
        scratch_shapes=[pltpu.VMEM((MLSTM_HEADS, MLSTM_V_DIM, MLSTM_QK_DIM), F32),
                        pltpu.VMEM((8, 128), F32),
                        pltpu.VMEM((8, 128), F32)],
        compiler_params=_cparams("parallel", "arbitrary"),
        name="mlstm_scan",
    )(q_t, k, v_t, o_t, gr, head_norm_cols)


def _pair_tables():
    lo, hi = [], []
    for g in range(MOE_GROUPS):
        for a in range(MOE_PER_GROUP):
            for b in range(a + 1, MOE_PER_GROUP):
                lo.append(g * MOE_PER_GROUP + a)
                hi.append(g * MOE_PER_GROUP + b)
    return np.asarray(lo, np.int32), np.asarray(hi, np.int32)


_PAIR_LO, _PAIR_HI = _pair_tables()


def _mix_out_kernel(a_ref, w_ref, res_ref, g_ref, wrh_ref, wrl_ref, br_ref, tril_ref, h_ref, xe_ref, route_ref, cnt_ref,
                    run_ref, *, a_transposed):
    @pl.when(pl.program_id(0) == 0)
    def _():
        run_ref[...] = jnp.zeros_like(run_ref)

    dims = _TN if a_transposed else (((1,), (0,)), ((), ()))
    h = res_ref[...] + lax.dot_general(a_ref[...], w_ref[...], dims, preferred_element_type=F32)
    h_ref[...] = h
    hn = _rms(h) * g_ref[...]
    xe_ref[:, 0:D_MODEL] = hn

    logits = _dot_split(hn, wrh_ref[...], wrl_ref[...]) + br_ref[...]
    lane_i = lax.broadcasted_iota(jnp.int32, logits.shape, 1)
    lane = lane_i.astype(F32)
    ninf = -jnp.inf
    is_g = (lane_i >= MOE_EXPERTS) & (lane_i < MOE_EXPERTS + MOE_GROUPS)
    glog = jnp.where(is_g, logits, ninf)
    gmax = jnp.max(glog, axis=-1, keepdims=True)
    gidx = jnp.min(jnp.where(glog == gmax, lane - MOE_EXPERTS, 99.0), axis=-1, keepdims=True)
    pg_top = 1.0 / jnp.sum(jnp.exp(glog - gmax), axis=-1, keepdims=True)
    lane_grp = (lane_i >> 3).astype(F32)
    in_grp = (lane_i < MOE_EXPERTS) & (lane_grp == gidx)
    ev = jnp.where(in_grp, logits, ninf)
    v1 = jnp.max(ev, axis=-1, keepdims=True)
    i1 = jnp.min(jnp.where(ev == v1, lane, 999.0), axis=-1, keepdims=True)
    ev2 = jnp.where(lane == i1, ninf, ev)
    v2 = jnp.max(ev2, axis=-1, keepdims=True)
    i2 = jnp.min(jnp.where(ev2 == v2, lane, 999.0), axis=-1, keepdims=True)
    e2 = jnp.exp(v2 - v1)
    w1 = pg_top / (1.0 + e2)
    w2 = pg_top * e2 / (1.0 + e2)
    first_lo = i1 < i2
    w_lo = jnp.where(first_lo, w1, w2)
    w_hi = jnp.where(first_lo, w2, w1)
    a = jnp.minimum(i1, i2) - MOE_PER_GROUP * gidx
    b = jnp.maximum(i1, i2) - MOE_PER_GROUP * gidx
    cls = gidx * MOE_PAIRS + a * (15.0 - a) * 0.5 + (b - a - 1.0)

    onehot = lane == cls
    prefix = jnp.dot(tril_ref[...], onehot.astype(BF16), preferred_element_type=F32)
    run = run_ref[0:1, :]
    rank = jnp.sum(jnp.where(onehot, prefix - 1.0 + run, 0.0), axis=-1, keepdims=True)
    run_new = run + prefix[TM - 1:TM, :]
    run_ref[...] = jnp.broadcast_to(run_new, run_ref.shape)
    cnt_ref[...] = jnp.broadcast_to(run_new, cnt_ref.shape)

    route = cls * float(2 ** ROUTE_SHIFT) + rank
    meta = jnp.where(lane_i == LANE_WLO, w_lo,
                     jnp.where(lane_i == LANE_WHI, w_hi,
                               jnp.where(lane_i == LANE_ROUTE, route, 0.0)))
    xe_ref[:, D_MODEL:XE_W] = meta
    route_ref[...] = meta.T[0:8, :].astype(jnp.int32)


def _mix_out(a, w, res, g_moe, w_rt, b_rt, tril, a_transposed):
    n = res.shape[0]
    kdim = w.shape[0]
    row = lambda i: (i, 0)
    fixed = lambda i: (0, 0)
    a_spec = pl.BlockSpec((kdim, TM), lambda i: (0, i)) if a_transposed else pl.BlockSpec((TM, kdim), row)
    w_rt_hi, w_rt_lo = _split_bf16(w_rt)
    return pl.pallas_call(
        functools.partial(_mix_out_kernel, a_transposed=a_transposed),
        grid=(n // TM,),
        in_specs=[a_spec,
                  pl.BlockSpec((kdim, D_MODEL), fixed),
                  pl.BlockSpec((TM, D_MODEL), row),
                  pl.BlockSpec((1, D_MODEL), fixed),
                  pl.BlockSpec((D_MODEL, 128), fixed),
                  pl.BlockSpec((D_MODEL, 128), fixed),
                  pl.BlockSpec((1, 128), fixed),
                  pl.BlockSpec((TM, TM), fixed)],
        out_specs=[pl.BlockSpec((TM, D_MODEL), row),
                   pl.BlockSpec((TM, XE_W), row),
                   pl.BlockSpec((8, TM), lambda i: (0, i)),
                   pl.BlockSpec((8, 128), fixed)],
        out_shape=[jax.ShapeDtypeStruct((n, D_MODEL), F32),
                   jax.ShapeDtypeStruct((n, XE_W), F32),
                   jax.ShapeDtypeStruct((8, n), jnp.int32),
                   jax.ShapeDtypeStruct((8, 128), F32)],
        scratch_shapes=[pltpu.VMEM((8, 128), F32)],
        compiler_params=_cparams("arbitrary"),
        name="mix_out",
    )(a, w, res, g_moe, w_rt_hi, w_rt_lo, b_rt, tril)


def _sorted_row(route_ref, offs_ref, idx):
    r = route_ref[idx]
    return offs_ref[r >> ROUTE_SHIFT] + (r & (2 ** ROUTE_SHIFT - 1))


def _dispatch_kernel(route_ref, offs_ref, cnt_ref, nused_ref, xe_ref, xs_ref, zbuf, sem, zsem):
    i = pl.program_id(0)
    base = i * TM

    @pl.when(i == 0)
    def _():
        zbuf[...] = jnp.zeros_like(zbuf)

        def per_class(c, carry):
            cnt = cnt_ref[c]
            start = offs_ref[c] + cnt
            pad = (-cnt) & (ROW_TILE - 1)
            head = (-cnt) & 7
            blocks = (pad - head) >> 3

            def fill_row(r, inner):
                pltpu.make_async_copy(zbuf.at[pl.ds(0, 1)], xs_ref.at[pl.ds(start + r, 1)], zsem).start()
                return inner

            def fill_block(b, inner):
                row0 = pl.multiple_of(start + head + b * 8, 8)
                pltpu.make_async_copy(zbuf.at[pl.ds(0, 8)], xs_ref.at[pl.ds(row0, 8)], zsem).start()
                return inner

            def drain_row(r, inner):
                pltpu.make_async_copy(zbuf.at[pl.ds(0, 1)], xs_ref.at[pl.ds(0, 1)], zsem).wait()
                return inner

            def drain_block(b, inner):
                pltpu.make_async_copy(zbuf.at[pl.ds(0, 8)], xs_ref.at[pl.ds(0, 8)], zsem).wait()
                return inner

            lax.fori_loop(0, head, fill_row, 0)
            lax.fori_loop(0, blocks, fill_block, 0)
            lax.fori_loop(0, head, drain_row, 0)
            lax.fori_loop(0, blocks, drain_block, 0)
            return carry

        lax.fori_loop(0, MOE_CLASSES, per_class, 0)

        def tail(t, carry):
            row0 = pl.multiple_of(t * ROW_TILE, ROW_TILE)
            cp = pltpu.make_async_copy(zbuf, xs_ref.at[pl.ds(row0, ROW_TILE)], zsem)
            cp.start()
            cp.wait()
            return carry

        lax.fori_loop(nused_ref[0], xs_ref.shape[0] // ROW_TILE, tail, 0)

    def issue(t8, carry):
        t0 = pl.multiple_of(t8 * ISSUE_ROWS, ISSUE_ROWS)
        for r in range(ISSUE_ROWS):
            p = _sorted_row(route_ref, offs_ref, base + t0 + r)
            pltpu.make_async_copy(xe_ref.at[pl.ds(t0 + r, 1)], xs_ref.at[pl.ds(p, 1)], sem).start()
        return carry

    lax.fori_loop(0, TM // ISSUE_ROWS, issue, 0)
    pltpu.make_async_copy(xe_ref, xs_ref.at[pl.ds(0, TM)], sem).wait()


def _dispatch(route, offs, cnt, n_used, xe, n_rows):
    n = xe.shape[0]
    grid_spec = pltpu.PrefetchScalarGridSpec(
        num_scalar_prefetch=4,
        grid=(n // TM,),
        in_specs=[pl.BlockSpec((TM, XE_W), lambda i, *_: (i, 0))],
        out_specs=pl.BlockSpec(memory_space=pl.ANY),
        scratch_shapes=[pltpu.VMEM((ROW_TILE, XE_W), F32), pltpu.SemaphoreType.DMA(()),
                        pltpu.SemaphoreType.DMA(())],
    )
    return pl.pallas_call(
        _dispatch_kernel,
        grid_spec=grid_spec,
        out_shape=jax.ShapeDtypeStruct((n_rows, XE_W), F32),
        compiler_params=_cparams("arbitrary"),
        name="moe_dispatch",
    )(route, offs, cnt, n_used, xe)


def _experts_kernel(tlo_ref, thi_ref, nused_ref, xs_ref, wg_lo, wu_lo, wd_lo, wg_hi, wu_hi, wd_hi, y_ref):
    del tlo_ref, thi_ref

    @pl.when(pl.program_id(0) < nused_ref[0])
    def _():
        x = xs_ref[:, 0:D_MODEL].astype(BF16)

        def ffn(wg, wu, wd, w):
            a = jnp.dot(x, wg[0], preferred_element_type=F32)
            u = jnp.dot(x, wu[0], preferred_element_type=F32)
            hid = (a * _sigmoid(a)) * u * w
            return jnp.dot(hid.astype(BF16), wd[0], preferred_element_type=F32)

        w_lo = xs_ref[:, D_MODEL + LANE_WLO:D_MODEL + LANE_WLO + 1]
        w_hi = xs_ref[:, D_MODEL + LANE_WHI:D_MODEL + LANE_WHI + 1]
        y_ref[...] = ffn(wg_lo, wu_lo, wd_lo, w_lo) + ffn(wg_hi, wu_hi, wd_hi, w_hi)

    @pl.when(pl.program_id(0) >= nused_ref[0])
    def _():
        y_ref[...] = jnp.zeros_like(y_ref)


def _experts(tile_lo, tile_hi, n_used, xs, w_gate, w_up, w_down):
    n_tiles = xs.shape[0] // ROW_TILE
    rows = lambda i, tlo, thi, nu: (jnp.maximum(jnp.minimum(i, nu[0] - 1), 0), 0)
    lo = lambda i, tlo, thi, nu: (tlo[i], 0, 0)
    hi = lambda i, tlo, thi, nu: (thi[i], 0, 0)
    up_spec = lambda m: pl.BlockSpec((1, D_MODEL, MOE_HIDDEN), m)
    dn_spec = lambda m: pl.BlockSpec((1, MOE_HIDDEN, D_MODEL), m)
    grid_spec = pltpu.PrefetchScalarGridSpec(
        num_scalar_prefetch=3,
        grid=(n_tiles,),
        in_specs=[pl.BlockSpec((ROW_TILE, XE_W), rows),
                  up_spec(lo), up_spec(lo), dn_spec(lo),
                  up_spec(hi), up_spec(hi), dn_spec(hi)],
        out_specs=pl.BlockSpec((ROW_TILE, D_MODEL), lambda i, tlo, thi, nu: (i, 0)),
    )
    return pl.pallas_call(
        _experts_kernel,
        grid_spec=grid_spec,
        out_shape=jax.ShapeDtypeStruct((xs.shape[0], D_MODEL), F32),
        compiler_params=_cparams("arbitrary"),
        name="moe_experts",
    )(tile_lo, tile_hi, n_used, xs, w_gate, w_up, w_down, w_gate, w_up, w_down)


def _combine_kernel(route_ref, offs_ref, h_ref, y_ref, g_ref, out_ref, buf, sem, *, final_norm):
    base = pl.program_id(0) * TM

    def issue(t8, carry):
        t0 = pl.multiple_of(t8 * ISSUE_ROWS, ISSUE_ROWS)
        for r in range(ISSUE_ROWS):
            p = _sorted_row(route_ref, offs_ref, base + t0 + r)
            pltpu.make_async_copy(y_ref.at[pl.ds(p, 1)], buf.at[pl.ds(t0 + r, 1)], sem).start()
        return carry

    lax.fori_loop(0, TM // ISSUE_ROWS, issue, 0)
    pltpu.make_async_copy(y_ref.at[pl.ds(0, TM)], buf, sem).wait()
    out = h_ref[...] + buf[...]
    if final_norm:
        out = _rms(out) * g_ref[...]
    out_ref[...] = out


def _combine(route, offs, h, y, gain, final_norm):
    n = h.shape[0]
    grid_spec = pltpu.PrefetchScalarGridSpec(
        num_scalar_prefetch=2,
        grid=(n // TM,),
        in_specs=[pl.BlockSpec((TM, D_MODEL), lambda i, *_: (i, 0)),
                  pl.BlockSpec(memory_space=pl.ANY),
                  pl.BlockSpec((1, D_MODEL), lambda i, *_: (0, 0))],
        out_specs=pl.BlockSpec((TM, D_MODEL), lambda i, *_: (i, 0)),
        scratch_shapes=[pltpu.VMEM((TM, D_MODEL), F32), pltpu.SemaphoreType.DMA(())],
    )
    return pl.pallas_call(
        functools.partial(_combine_kernel, final_norm=final_norm),
        grid_spec=grid_spec,
        out_shape=jax.ShapeDtypeStruct((n, D_MODEL), F32),
        compiler_params=_cparams("arbitrary"),
        name="moe_combine",
    )(route, offs, h, y, gain)


def _moe_layer(a, w_out, res, moe_norm, w_group, b_group, w_router, b_router, w_gate, w_up, w_down,
               tril, out_gain, final_norm, a_transposed):
    n = res.shape[0]
    unused = 128 - MOE_EXPERTS - MOE_GROUPS
    w_rt = jnp.concatenate([w_router, w_group, jnp.zeros((D_MODEL, unused), F32)], axis=1)
    b_rt = jnp.concatenate([b_router, b_group, jnp.zeros((unused,), F32)]).reshape(1, 128)
    h, xe, route8, counts = _mix_out(a, w_out, res, moe_norm.reshape(1, D_MODEL), w_rt, b_rt, tril, a_transposed)

    n_tiles = n // ROW_TILE + MOE_CLASSES
    cnt = counts[0].astype(jnp.int32)
    tiles_c = (cnt + ROW_TILE - 1) // ROW_TILE
    tile_end = jnp.cumsum(tiles_c)
    offs = (tile_end - tiles_c) * ROW_TILE
    n_used = tile_end[-1]
    tile_ids = jnp.minimum(jnp.arange(n_tiles, dtype=jnp.int32), n_used - 1)
    tile_cls = jnp.sum((tile_end[None, 0:MOE_CLASSES] <= tile_ids[:, None]).astype(jnp.int32), axis=1)
    tile_cls = jnp.clip(tile_cls, 0, MOE_CLASSES - 1)
    tile_lo = jnp.asarray(_PAIR_LO)[tile_cls]
    tile_hi = jnp.asarray(_PAIR_HI)[tile_cls]
    route = route8[LANE_ROUTE]

    n_used = n_used.reshape(1)
    xs = _dispatch(route, offs, cnt, n_used, xe, n_tiles * ROW_TILE)
    y = _experts(tile_lo, tile_hi, n_used, xs, w_gate.astype(BF16), w_up.astype(BF16),
                 w_down.astype(BF16))
    return _combine(route, offs, h, y, out_gain.reshape(1, D_MODEL), final_norm)


def _rope_tables(pos, width):
    half = ROPE_DIM // 2
    inv_freq = jnp.power(jnp.float32(ROPE_THETA), -jnp.arange(half, dtype=F32) * (2.0 / ROPE_DIM))
    ang = pos.astype(F32)[:, None] * inv_freq[None, :]
    cos, sin = jnp.cos(ang), jnp.sin(ang)
    t = pos.shape[0]
    rest = NSA_HEAD_DIM - ROPE_DIM
    cos_t = jnp.concatenate([cos, cos, jnp.ones((t, rest), F32)], axis=1)
    sin_a = jnp.concatenate([-sin, jnp.zeros((t, half + rest), F32)], axis=1)
    sin_b = jnp.concatenate([jnp.zeros((t, half), F32), sin, jnp.zeros((t, rest), F32)], axis=1)
    rep = width // NSA_HEAD_DIM
    return jnp.tile(cos_t, (1, rep)), jnp.tile(sin_a, (1, rep)), jnp.tile(sin_b, (1, rep))


def _rope(x, cos_t, sin_a, sin_b):
    half = ROPE_DIM // 2
    parts = []
    for c in range(x.shape[1] // 128):
        xc = x[:, c * 128:(c + 1) * 128]
        parts.append(xc * cos_t + pltpu.roll(xc, 128 - half, axis=1) * sin_a + pltpu.roll(xc, half, axis=1) * sin_b)
    return parts[0] if len(parts) == 1 else jnp.concatenate(parts, axis=1)


def _rope_rows(x, cos_t, sin_a, sin_b):
    half = ROPE_DIM // 2
    reps = x.shape[0] // NSA_HEAD_DIM
    tile = lambda t: jnp.concatenate([t] * reps, axis=0)
    up = jnp.concatenate([x[half:], x[:half]], axis=0)
    down = jnp.concatenate([x[-half:], x[:-half]], axis=0)
    return x * tile(cos_t) + up * tile(sin_a) + down * tile(sin_b)


def _nsa_proj_kernel(h_ref, gq_ref, gkv_ref, wqt_ref, wkv_ref, wvt_ref, cos_ref, sa_ref, sb_ref,
                     cost_ref, sat_ref, sbt_ref,
                     qt_ref, gatet_ref, kc0_ref, kc1_ref, vc0_ref, vc1_ref, ks_ref, vst_ref, kw_ref, vwt_ref):
    r = _rms(h_ref[...])
    hq = (r * gq_ref[...]).astype(BF16)
    hk = (r * gkv_ref[...]).astype(BF16)
    qt = lax.dot_general(wqt_ref[0:1024, :], hq, _NT, preferred_element_type=F32)
    q_scale = NSA_HEAD_DIM ** -0.5 * LOG2_E
    qt_ref[...] = (_rope_rows(qt, cost_ref[...], sat_ref[...], sbt_ref[...]) * q_scale).astype(BF16)
    gatet_ref[...] = _sigmoid(lax.dot_general(wqt_ref[1024:1152, :], hq, _NT, preferred_element_type=F32))
    kc0_ref[...] = jnp.dot(hk, wkv_ref[:, 0:128], preferred_element_type=F32)
    kc1_ref[...] = jnp.dot(hk, wkv_ref[:, 128:256], preferred_element_type=F32)
    vc0_ref[...] = jnp.dot(hk, wkv_ref[:, 256:384], preferred_element_type=F32)
    vc1_ref[...] = jnp.dot(hk, wkv_ref[:, 384:512], preferred_element_type=F32)

    tm = h_ref.shape[0]
    cos_t, sin_a, sin_b = cos_ref[...], sa_ref[...], sb_ref[...]
    lane = lax.broadcasted_iota(jnp.int32, (tm, NSA_HEAD_DIM), 1)
    pos = (pl.program_id(0) % (SEQ // tm)) * tm + lax.broadcasted_iota(jnp.int32, (tm, NSA_HEAD_DIM), 0)
    blk_onehot = jnp.where(lane == (pos >> 6), 1.0, 0.0).astype(BF16)
    zeros = jnp.zeros((tm, NSA_HEAD_DIM), BF16)

    def store_keys(ref, val, aux):
        for g in range(NSA_GROUPS):
            ref[0, g] = jnp.concatenate([val[:, g * 64:(g + 1) * 64].astype(BF16), aux], axis=1)

    store_keys(ks_ref, _rope(jnp.dot(hk, wkv_ref[:, 512:768], preferred_element_type=F32), cos_t, sin_a, sin_b),
               blk_onehot)
    store_keys(kw_ref, _rope(jnp.dot(hk, wkv_ref[:, 1024:1280], preferred_element_type=F32), cos_t, sin_a, sin_b),
               zeros)

    row = lax.broadcasted_iota(jnp.int32, (V_ROWS - NSA_HEAD_DIM, tm), 0)
    ones_row = jnp.where(row == 0, 1.0, 0.0).astype(BF16)

    def store_values(ref, val_t):
        for g in range(NSA_GROUPS):
            ref[0, g] = jnp.concatenate([val_t[g * 64:(g + 1) * 64, :].astype(BF16), ones_row], axis=0)

    store_values(vst_ref, lax.dot_general(wvt_ref[0:256, :], hk, _NT, preferred_element_type=F32))
    store_values(vwt_ref, lax.dot_general(wvt_ref[256:512, :], hk, _NT, preferred_element_type=F32))


def _nsa_proj(h, g_q, g_kv, w_qt, w_kv, w_vt, rope_tabs, rope_tabs_t, batch):
    n = h.shape[0]
    nblk = SEQ // TM
    row = lambda i: (i, 0)
    col = lambda i: (0, i)
    fixed = lambda i: (0, 0)
    tab = lambda i: (i % nblk, 0)
    tab_t = lambda i: (0, i % nblk)
    key_spec = pl.BlockSpec((1, NSA_GROUPS, TM, KV_LANES), lambda i: (i // nblk, 0, i % nblk, 0))
    key_shape = jax.ShapeDtypeStruct((batch, NSA_GROUPS, SEQ, KV_LANES), BF16)
    val_spec = pl.BlockSpec((1, NSA_GROUPS, V_ROWS, TM), lambda i: (i // nblk, 0, 0, i % nblk))
    val_shape = jax.ShapeDtypeStruct((batch, NSA_GROUPS, V_ROWS, SEQ), BF16)
    raw_spec = pl.BlockSpec((TM, 128), row)
    raw_shape = jax.ShapeDtypeStruct((n, 128), F32)
    return pl.pallas_call(
        _nsa_proj_kernel,
        grid=(n // TM,),
        in_specs=[pl.BlockSpec((TM, D_MODEL), row),
                  pl.BlockSpec((1, D_MODEL), fixed),
                  pl.BlockSpec((1, D_MODEL), fixed),
                  pl.BlockSpec((1152, D_MODEL), fixed),
                  pl.BlockSpec((D_MODEL, 1536), fixed),
                  pl.BlockSpec((512, D_MODEL), fixed),
                  pl.BlockSpec((TM, 128), tab), pl.BlockSpec((TM, 128), tab), pl.BlockSpec((TM, 128), tab),
                  pl.BlockSpec((NSA_HEAD_DIM, TM), tab_t), pl.BlockSpec((NSA_HEAD_DIM, TM), tab_t),
                  pl.BlockSpec((NSA_HEAD_DIM, TM), tab_t)],
        out_specs=[pl.BlockSpec((1024, TM), col),
                   pl.BlockSpec((128, TM), col),
                   raw_spec, raw_spec, raw_spec, raw_spec,
                   key_spec, val_spec, key_spec, val_spec],
        out_shape=[jax.ShapeDtypeStruct((1024, n), BF16),
                   jax.ShapeDtypeStruct((128, n), F32),
                   raw_shape, raw_shape, raw_shape, raw_shape,
                   key_shape, val_shape, key_shape, val_shape],
        compiler_params=_cparams("parallel"),
        name="nsa_proj",
    )(h, g_q, g_kv, w_qt, w_kv, w_vt, *rope_tabs, *rope_tabs_t)


HALF_BLOCKS = SEQ // CMP_STRIDE
CMP_K = CMP_STRIDE * 256


def _compress_kernel(rk0_ref, rk1_ref, rv0_ref, rv1_ref, w1k_ref, w1v_ref, pek_ref, pev_ref, w2k_ref, w2v_ref,
                     cos_ref, sa_ref, sb_ref, kc_ref, vc_ref):
    lane = lax.broadcasted_iota(jnp.int32, (HALF_BLOCKS, 512), 1)
    first_half = (lane & 127) < 64

    def comp(raw_refs, w1_ref, pe_ref, w2_ref):
        x = jnp.concatenate([r[pl.ds(l, HALF_BLOCKS, stride=CMP_STRIDE), :]
                             for l in range(CMP_STRIDE) for r in raw_refs], axis=1).astype(BF16)
        r = jnp.dot(x, w1_ref[...], preferred_element_type=F32)
        rpe = jnp.dot(pe_ref[...].astype(BF16), w1_ref[...], preferred_element_type=F32)
        r = r + jnp.where(first_half, rpe[0:1, :], rpe[1:2, :])
        nxt = pltpu.roll(r, HALF_BLOCKS - 1, axis=0)
        nxt = jnp.concatenate([pltpu.roll(nxt[:, c * 128:(c + 1) * 128], 64, axis=1) for c in range(4)], axis=1)
        pre = r + nxt
        act = pre * _sigmoid(pre)
        return jnp.dot(act.astype(BF16), w2_ref[...], preferred_element_type=F32)

    kc = comp((rk0_ref, rk1_ref), w1k_ref, pek_ref, w2k_ref)
    kc = _rope(kc, cos_ref[...], sa_ref[...], sb_ref[...])
    vc = comp((rv0_ref, rv1_ref), w1v_ref, pev_ref, w2v_ref)
    for g in range(NSA_GROUPS):
        kc_ref[0, g] = kc[:, g * 64:(g + 1) * 64]
        vc_ref[0, g] = vc[:, g * 64:(g + 1) * 64]


def _compress_weights(pe, w1, w2):
    dh = NSA_HEAD_DIM
    w1r = w1.reshape(2, CMP_STRIDE, dh, dh)
    eye = jnp.eye(NSA_GROUPS, dtype=F32)
    w1_big = jnp.einsum("hldf,gk->lgdkhf", w1r, eye).reshape(CMP_K, 512).astype(BF16)
    pe_rows = jnp.broadcast_to(pe.reshape(2, CMP_STRIDE, 1, dh), (2, CMP_STRIDE, NSA_GROUPS, dh)).reshape(2, CMP_K)
    pe_rows = jnp.concatenate([pe_rows, jnp.zeros((6, CMP_K), F32)], axis=0)
    w2_halves = jnp.stack([w2, jnp.zeros_like(w2)])
    w2_bd = jnp.einsum("hdf,gk->ghdkf", w2_halves, eye).reshape(512, 256).astype(BF16)
    return w1_big, pe_rows, w2_bd


def _compress(raw_k, raw_v, wk, wv, cmp_tabs, batch):
    fixed = lambda b: (0, 0)
    raw_spec = pl.BlockSpec((SEQ, 128), lambda b: (b, 0))
    out_spec = pl.BlockSpec((1, NSA_GROUPS, N_CMP_PAD, NSA_HEAD_DIM), lambda b: (b, 0, 0, 0))
    out_shape = jax.ShapeDtypeStruct((batch, NSA_GROUPS, N_CMP_PAD, NSA_HEAD_DIM), F32)
    return pl.pallas_call(
        _compress_kernel,
        grid=(batch,),
        in_specs=[raw_spec, raw_spec, raw_spec, raw_spec,
                  pl.BlockSpec((CMP_K, 512), fixed), pl.BlockSpec((CMP_K, 512), fixed),
                  pl.BlockSpec((8, CMP_K), fixed), pl.BlockSpec((8, CMP_K), fixed),
                  pl.BlockSpec((512, 256), fixed), pl.BlockSpec((512, 256), fixed),
                  pl.BlockSpec((N_CMP_PAD, 128), fixed), pl.BlockSpec((N_CMP_PAD, 128), fixed),
                  pl.BlockSpec((N_CMP_PAD, 128), fixed)],
        out_specs=[out_spec, out_spec],
        out_shape=[out_shape, out_shape],
        compiler_params=_cparams("parallel"),
        name="nsa_compress",
    )(*raw_k, *raw_v, wk[0], wv[0], wk[1], wv[1], wk[2], wv[2], *cmp_tabs)


def _nsa_attn_kernel(qt_ref, gatet_ref, kc_ref, vc_ref, ks_ref, vst_ref, kw_ref, vwt_ref, ovl_ref, out_ref,
                     qx_sc, ms_sc, as_sc, aw_sc):
    qb = pl.program_id(2)
    q0 = qb * Q_BLOCK
    Q, HG, DH = Q_BLOCK, NSA_GROUP_SIZE, NSA_HEAD_DIM
    groups = range(ATTN_GROUPS)

    def heads(x):
        return jnp.concatenate([x] * HG, axis=1)

    n_row = lax.broadcasted_iota(jnp.int32, (N_CMP_PAD, Q), 0)
    t_lane = q0 + lax.broadcasted_iota(jnp.int32, (N_CMP_PAD, Q), 1)
    valid_c = heads((n_row * CMP_STRIDE + CMP_BLOCK - 1 <= t_lane) & (n_row < N_CMP_PAD - 1))
    j_row = lax.broadcasted_iota(jnp.int32, (N_SEL, Q), 0)
    cur = (q0 + lax.broadcasted_iota(jnp.int32, (N_SEL, Q), 1)) >> 6
    forced = (j_row == 0) | (j_row == cur) | (j_row == cur - 1)
    o_c = []
    for gi in groups:
        q4 = qt_ref[gi * HG * DH:(gi + 1) * HG * DH, :]
        q_t = jnp.concatenate([q4[h * DH:(h + 1) * DH, :] for h in range(HG)], axis=1)
        s_c = jnp.dot(kc_ref[0, gi].astype(BF16), q_t, preferred_element_type=F32)
        s_c = jnp.where(valid_c, s_c, NEG_INF)
        m_c = jnp.max(s_c, axis=0, keepdims=True)
        e_c = jnp.where(valid_c, jnp.exp2(s_c - m_c), 0.0)
        l_c = jnp.sum(e_c, axis=0, keepdims=True)
        p_c = e_c * jnp.where(l_c > 0.0, 1.0 / l_c, 0.0)
        o_c.append(jnp.dot(vc_ref[0, gi].T.astype(BF16), p_c.astype(BF16), preferred_element_type=F32))

        p_sum = p_c[:, 0:Q] + p_c[:, Q:2 * Q] + p_c[:, 2 * Q:3 * Q] + p_c[:, 3 * Q:4 * Q]
        imp_t = jnp.dot(ovl_ref[...], p_sum, precision=HIGHEST, preferred_element_type=F32)[0:N_SEL]
        imp_t = jnp.where(forced, FORCED_SCORE, imp_t)
        imp_t = jnp.where(j_row > cur, NEG_INF, imp_t)
        cnt = jnp.zeros((N_SEL, Q), F32)
        for i in range(N_SEL):
            ri = imp_t[i:i + 1, :]
            cnt = cnt + jnp.where(ri > imp_t, 1.0, jnp.where((ri == imp_t) & (j_row > i), 1.0, 0.0))
        sel_bias = jnp.where((cnt < SEL_TOPK) & (j_row <= cur), 0.0, NEG_INF).astype(BF16)

        qx_sc[gi] = jnp.concatenate([q_t, heads(sel_bias), jnp.zeros((KV_LANES - DH - N_SEL, HG * Q), BF16)],
                                    axis=0)

    def finish(acc):
        return acc[0:DH] / acc[DEN_ROW:DEN_ROW + 1]

    ms_sc[...] = jnp.full(ms_sc.shape, NEG_INF, F32)
    as_sc[...] = jnp.zeros(as_sc.shape, F32)
    n_kt = (q0 + Q + SLC_TILE - 1) // SLC_TILE

    def slc_tile(kt, bias):
        start = pl.multiple_of(kt * SLC_TILE, SLC_TILE)
        for gi in groups:
            s = jnp.dot(ks_ref[0, gi, pl.ds(start, SLC_TILE), :], qx_sc[gi], preferred_element_type=F32)
            if bias is not None:
                s = s + bias
            m_old = ms_sc[gi, 0:1, :]
            m_new = jnp.maximum(m_old, jnp.max(s, axis=0, keepdims=True))
            p = jnp.exp2(s - m_new).astype(BF16)
            pv = jnp.dot(vst_ref[0, gi, :, pl.ds(start, SLC_TILE)], p, preferred_element_type=F32)
            as_sc[gi] = as_sc[gi] * jnp.exp2(m_old - m_new) + pv
            ms_sc[gi] = jnp.broadcast_to(m_new, ms_sc.shape[1:])

    def slc_full(kt, carry):
        slc_tile(kt, None)
        return carry

    lax.fori_loop(0, n_kt - 1, slc_full, 0)
    c_s = lax.broadcasted_iota(jnp.int32, (SLC_TILE, Q), 0)
    r_s = lax.broadcasted_iota(jnp.int32, (SLC_TILE, Q), 1)
    slc_tile(n_kt - 1, heads(jnp.where((n_kt - 1) * SLC_TILE + c_s <= q0 + r_s, 0.0, NEG_INF)))

    c_w = lax.broadcasted_iota(jnp.int32, (Q, Q), 0)
    r_w = lax.broadcasted_iota(jnp.int32, (Q, Q), 1)
    n_slab = WIN_KEYS // Q

    def window(start, slab_bias):
        start = pl.multiple_of(start, Q)
        slab_bias = [None if b is None else heads(b) for b in slab_bias]
        for gi in groups:
            s = jnp.dot(kw_ref[0, gi, pl.ds(start, WIN_KEYS), :], qx_sc[gi], preferred_element_type=F32)
            slabs = []
            for j in range(n_slab):
                sj = s[j * Q:(j + 1) * Q]
                slabs.append(sj if slab_bias[j] is None else sj + slab_bias[j])
            top = slabs[0]
            for sj in slabs[1:]:
                top = jnp.maximum(top, sj)
            m = jnp.max(top, axis=0, keepdims=True)
            p = jnp.concatenate([jnp.exp2(sj - m) for sj in slabs], axis=0).astype(BF16)
            aw_sc[gi] = jnp.dot(vwt_ref[0, gi, :, pl.ds(start, WIN_KEYS)], p, preferred_element_type=F32)

    @pl.when(qb >= WINDOW // Q)
    def _():
        window(q0 - WINDOW, [jnp.where(c_w > r_w, 0.0, NEG_INF)] + [None] * (n_slab - 2)
               + [jnp.where(c_w <= r_w, 0.0, NEG_INF)])

    @pl.when(qb < WINDOW // Q)
    def _():
        window(0, [jnp.where(j * Q + c_w <= q0 + r_w, 0.0, NEG_INF) for j in range(n_slab)])

    for gi in groups:
        g = pl.program_id(1) * ATTN_GROUPS + gi

        def gate_row(branch):
            return jnp.concatenate([gatet_ref[pl.ds((g * HG + h) * 3 + branch, 1), :] for h in range(HG)], axis=1)

        o = gate_row(0) * o_c[gi] + gate_row(1) * finish(as_sc[gi]) + gate_row(2) * finish(aw_sc[gi])
        out_ref[gi * HG * DH:(gi + 1) * HG * DH, :] = jnp.concatenate(
            [o[:, h * Q:(h + 1) * Q] for h in range(HG)], axis=0).astype(BF16)


def _overlap_t():
    n = np.arange(N_CMP_PAD)
    j = np.arange(128)
    cmp_start = n * CMP_STRIDE
    cmp_end = cmp_start + CMP_BLOCK - 1
    sel_start = j * SEL_BLOCK
    ovl = ((cmp_start[None, :] <= sel_start[:, None] + SEL_BLOCK - 1) & (cmp_end[None, :] >= sel_start[:, None])
           & (j[:, None] < N_SEL) & (n[None, :] < N_CMP_PAD - 1))
    return jnp.asarray(ovl.astype(np.float32))


def _nsa_attn(q_t, gates_t, kc, vc, ks, vs_t, kw, vw_t, batch):
    n = q_t.shape[1]
    gs = ATTN_GROUPS
    qcol = lambda b, g, i: (g, b * N_QB + i)
    gcol = lambda b, g, i: (0, b * N_QB + i)
    kv = lambda b, g, i: (b, g, 0, 0)
    rows = gs * NSA_GROUP_SIZE * NSA_HEAD_DIM
    lanes = NSA_GROUP_SIZE * Q_BLOCK
    return pl.pallas_call(
        _nsa_attn_kernel,
        grid=(batch, NSA_GROUPS // gs, N_QB),
        in_specs=[pl.BlockSpec((rows, Q_BLOCK), qcol),
                  pl.BlockSpec((128, Q_BLOCK), gcol),
                  pl.BlockSpec((1, gs, N_CMP_PAD, NSA_HEAD_DIM), kv),
                  pl.BlockSpec((1, gs, N_CMP_PAD, NSA_HEAD_DIM), kv),
                  pl.BlockSpec((1, gs, SEQ, KV_LANES), kv),
                  pl.BlockSpec((1, gs, V_ROWS, SEQ), kv),
                  pl.BlockSpec((1, gs, SEQ, KV_LANES), kv),
                  pl.BlockSpec((1, gs, V_ROWS, SEQ), kv),
                  pl.BlockSpec((128, N_CMP_PAD), lambda b, g, i: (0, 0))],
        out_specs=pl.BlockSpec((rows, Q_BLOCK), qcol),
        out_shape=jax.ShapeDtypeStruct((NSA_HEADS * NSA_HEAD_DIM, n), BF16),
        scratch_shapes=[pltpu.VMEM((gs, KV_LANES, lanes), BF16),
                        pltpu.VMEM((gs, 8, lanes), F32),
                        pltpu.VMEM((gs, V_ROWS, lanes), F32),
                        pltpu.VMEM((gs, V_ROWS, lanes), F32)],
        compiler_params=_cparams("parallel", "parallel", "arbitrary"),
        name="nsa_attn",
    )(q_t, gates_t, kc, vc, ks, vs_t, kw, vw_t, _overlap_t())


def kernel(x, mlstm_norm, mlstm_w_in, mlstm_gate_bias, mlstm_head_norm, mlstm_w_out, kv_norm, kv_w, cmp_pe_k, cmp_w1_k, cmp_w2_k, cmp_pe_v, cmp_w1_v, cmp_w2_v, nsa_norm, nsa_w_q, nsa_w_out, moe_norm, moe_w_group, moe_b_group, moe_w_router, moe_b_router, moe_w_gate, moe_w_up, moe_w_down, final_norm):
    batch, seq, d = x.shape
    assert seq == SEQ and d == D_MODEL
    assert mlstm_norm.shape[0] == 1 and nsa_norm.shape[0] == 1 and moe_norm.shape[0] == 2
    n = batch * seq
    x2d = x.reshape(n, d)
    tril = jnp.tril(jnp.ones((TM, TM), F32)).astype(BF16)

    w_in = mlstm_w_in[0]
    w_gate = jnp.concatenate([w_in[:, 3072:3080], jnp.zeros((d, 120), F32)], axis=1)
    b_gate = jnp.concatenate([mlstm_gate_bias[0], jnp.zeros((120,), F32)]).reshape(1, 128)
    q_t, k, v_t, o_t, gr = _mlstm_in(x2d, mlstm_norm[0].reshape(1, d), w_in[:, 0:3072].T.astype(BF16),
                                     w_in[:, 512:1024].astype(BF16), w_gate, b_gate)
    head_norm_cols = jnp.broadcast_to(mlstm_head_norm[0].reshape(d, 1), (d, 128))
    hs_t = _mlstm_scan(q_t, k, v_t, o_t, gr, head_norm_cols, batch)
    h = _moe_layer(hs_t, mlstm_w_out[0].astype(BF16), x2d, moe_norm[0], moe_w_group[0], moe_b_group[0],
                   moe_w_router[0], moe_b_router[0], moe_w_gate[0], moe_w_up[0], moe_w_down[0],
                   tril, final_norm, False, a_transposed=True)

    w_qt = jnp.concatenate([nsa_w_q[0].T, jnp.zeros((80, d), F32)], axis=0).astype(BF16)
    w_vt = jnp.concatenate([kv_w[:, 768:1024], kv_w[:, 1280:1536]], axis=1).T.astype(BF16)
    seq_tabs = _rope_tables(jnp.arange(SEQ), 128)
    seq_tabs_t = tuple(t[:, 0:NSA_HEAD_DIM].T for t in seq_tabs)
    q_t, gates_t, rk0, rk1, rv0, rv1, ks, vs_t, kw, vw_t = _nsa_proj(
        h, nsa_norm[0].reshape(1, d), kv_norm.reshape(1, d), w_qt, kv_w.astype(BF16), w_vt, seq_tabs, seq_tabs_t,
        batch)
    cmp_pos = jnp.arange(N_CMP_PAD) * CMP_STRIDE + CMP_BLOCK - 1
    kc, vc = _compress((rk0, rk1), (rv0, rv1), _compress_weights(cmp_pe_k, cmp_w1_k, cmp_w2_k),
                       _compress_weights(cmp_pe_v, cmp_w1_v, cmp_w2_v), _rope_tables(cmp_pos, 128), batch)
    att_t = _nsa_attn(q_t, gates_t, kc, vc, ks, vs_t, kw, vw_t, batch)
    out = _moe_layer(att_t, nsa_w_out[0].astype(BF16), h, moe_norm[1], moe_w_group[1], moe_b_group[1],
                     moe_w_router[1], moe_b_router[1], moe_w_gate[1], moe_w_up[1], moe_w_down[1],
                     tril, final_norm, True, a_transposed=True)
    return out.reshape(batch, seq, d)
```

```python
import functools

import numpy as np
import jax
import jax.numpy as jnp
from jax import lax
from jax.experimental import pallas as pl
from jax.experimental.pallas import tpu as pltpu

F32 = jnp.float32
BF16 = jnp.bfloat16
HIGHEST = lax.Precision.HIGHEST

D_MODEL = 1024
SEQ = 2048
RMS_EPS = 1e-6
NEG_INF = -1e30

MLSTM_HEADS = 4
MLSTM_V_DIM = 256
MLSTM_QK_DIM = 128
MLSTM_L = 256
GATE_SOFTCAP = 15.0

NSA_HEADS = 16
NSA_HEAD_DIM = 64
NSA_GROUPS = 4
NSA_GROUP_SIZE = 4
CMP_BLOCK = 32
CMP_STRIDE = 16
N_CMP_PAD = 128
SEL_BLOCK = 64
N_SEL = SEQ // SEL_BLOCK
SEL_TOPK = 16
WINDOW = 512
Q_BLOCK = 256
N_QB = SEQ // Q_BLOCK
WIN_KEYS = WINDOW + Q_BLOCK
SLC_TILE = 512
ATTN_GROUPS = 2
KV_LANES = 2 * NSA_HEAD_DIM
V_ROWS = NSA_HEAD_DIM + 16
DEN_ROW = NSA_HEAD_DIM
FORCED_SCORE = 1e6
ROPE_THETA = 500000.0
ROPE_DIM = 16

MOE_GROUPS = 4
MOE_PER_GROUP = 8
MOE_EXPERTS = 32
MOE_HIDDEN = 256
MOE_PAIRS = 28
MOE_CLASSES = MOE_GROUPS * MOE_PAIRS
ROW_TILE = 256
XE_W = D_MODEL + 128
LANE_WLO, LANE_WHI, LANE_ROUTE = 0, 1, 2
ROUTE_SHIFT = 16
ROW_CHUNKS = D_MODEL // 128
ISSUE_ROWS = 8

TM = 512
VMEM_LIMIT = 56 * 1024 * 1024

_NT = (((1,), (1,)), ((), ()))
_TN = (((0,), (0,)), ((), ()))


def _cparams(*sem):
    return pltpu.CompilerParams(dimension_semantics=sem, vmem_limit_bytes=VMEM_LIMIT)


def _rms(x):
    return x * lax.rsqrt(jnp.mean(x * x, axis=-1, keepdims=True) + RMS_EPS)


def _sigmoid(x):
    return 1.0 / (1.0 + jnp.exp(-x))


LOG2_E = 1.4426950408889634


def _split_bf16(x):
    hi = x.astype(BF16)
    return hi, (x - hi.astype(F32)).astype(BF16)


def _dot_split(x, w_hi, w_lo):
    x_hi, x_lo = _split_bf16(x)
    return (jnp.dot(x_hi, w_hi, preferred_element_type=F32) + jnp.dot(x_hi, w_lo, preferred_element_type=F32)
            + jnp.dot(x_lo, w_hi, preferred_element_type=F32))


def _mlstm_in_kernel(x_ref, g_ref, wt_ref, wk_ref, wgh_ref, wgl_ref, bg_ref, qt_ref, k_ref, vt_ref, ot_ref, gr_ref):
    hn = _rms(x_ref[...]) * g_ref[...]
    hb = hn.astype(BF16)
    qt_ref[...] = lax.dot_general(wt_ref[0:512, :], hb, _NT, preferred_element_type=F32).astype(BF16)
    k = jnp.dot(hb, wk_ref[...], preferred_element_type=F32)
    k_ref[...] = (k * (MLSTM_QK_DIM ** -0.5)).astype(BF16)
    vt_ref[...] = lax.dot_general(wt_ref[1024:2048, :], hb, _NT, preferred_element_type=F32).astype(BF16)
    ot_ref[...] = _sigmoid(lax.dot_general(wt_ref[2048:3072, :], hb, _NT, preferred_element_type=F32))
    gates = _dot_split(hn, wgh_ref[...], wgl_ref[...]) + bg_ref[...]
    gates = GATE_SOFTCAP * jnp.tanh(gates / GATE_SOFTCAP)
    lane = lax.broadcasted_iota(jnp.int32, gates.shape, 1)
    log_f = jnp.minimum(gates, 0.0) - jnp.log1p(jnp.exp(-jnp.abs(gates)))
    lg = jnp.where(lane < MLSTM_HEADS, gates, log_f)
    gr_ref[...] = lg.T[0:8, :]


def _mlstm_in(x2d, norm_g, w_t, w_k, w_gate, b_gate):
    n = x2d.shape[0]
    w_gate_hi, w_gate_lo = _split_bf16(w_gate)
    row = lambda i: (i, 0)
    col = lambda i: (0, i)
    fixed = lambda i: (0, 0)
    return pl.pallas_call(
        _mlstm_in_kernel,
        grid=(n // TM,),
        in_specs=[pl.BlockSpec((TM, D_MODEL), row),
                  pl.BlockSpec((1, D_MODEL), fixed),
                  pl.BlockSpec((3072, D_MODEL), fixed),
                  pl.BlockSpec((D_MODEL, 512), fixed),
                  pl.BlockSpec((D_MODEL, 128), fixed),
                  pl.BlockSpec((D_MODEL, 128), fixed),
                  pl.BlockSpec((1, 128), fixed)],
        out_specs=[pl.BlockSpec((512, TM), col),
                   pl.BlockSpec((TM, 512), row),
                   pl.BlockSpec((1024, TM), col),
                   pl.BlockSpec((1024, TM), col),
                   pl.BlockSpec((8, TM), col)],
        out_shape=[jax.ShapeDtypeStruct((512, n), BF16),
                   jax.ShapeDtypeStruct((n, 512), BF16),
                   jax.ShapeDtypeStruct((1024, n), BF16),
                   jax.ShapeDtypeStruct((1024, n), F32),
                   jax.ShapeDtypeStruct((8, n), F32)],
        compiler_params=_cparams("parallel"),
        name="mlstm_in",
    )(x2d, norm_g, w_t, w_k, w_gate_hi, w_gate_lo, b_gate)


def _mlstm_scan_kernel(qt_ref, k_ref, vt_ref, ot_ref, gr_ref, hn_ref, out_ref, ct_ref, n_ref, m_ref):
    L = MLSTM_L

    @pl.when(pl.program_id(1) == 0)
    def _():
        ct_ref[...] = jnp.zeros_like(ct_ref)
        n_ref[...] = jnp.zeros_like(n_ref)
        m_ref[...] = jnp.zeros_like(m_ref)

    row = lax.broadcasted_iota(jnp.int32, (L, L), 0)
    col = lax.broadcasted_iota(jnp.int32, (L, L), 1)
    causal_t = row <= col
    tril = (col <= row).astype(F32)
    eye = (col == row).astype(F32)
    gr = gr_ref[...]
    gr_pad = jnp.concatenate([gr, jnp.zeros((120, L), F32)], axis=0)
    b_row = lax.dot_general(gr, tril, _NT, precision=HIGHEST, preferred_element_type=F32)
    b_col = lax.dot_general(tril, gr_pad, _NT, precision=HIGHEST, preferred_element_type=F32)
    g_col = lax.dot_general(eye, gr_pad, _NT, precision=HIGHEST, preferred_element_type=F32)
    n_hi, n_lo = _split_bf16(n_ref[...])

    for h in range(MLSTM_HEADS):
        src_col = g_col[:, h:h + 1] - b_col[:, 4 + h:5 + h]
        bf_row = b_row[4 + h:5 + h, :]
        m = m_ref[h:h + 1, 0:1]
        dmat = jnp.where(causal_t, bf_row + src_col, NEG_INF)
        m_inter = bf_row + m
        m_t = jnp.maximum(m_inter, jnp.max(dmat, axis=0, keepdims=True))
        w_intra = jnp.exp(dmat - m_t)
        w_inter = jnp.exp(m_inter - m_t)
        qt = qt_ref[h * 128:(h + 1) * 128, :]
        kh = k_ref[:, h * 128:(h + 1) * 128]
        vt = vt_ref[h * 256:(h + 1) * 256, :]
        s = jnp.dot(kh, qt, preferred_element_type=F32) * w_intra
        c_old = ct_ref[h]
        num = (jnp.dot(vt, s.astype(BF16), preferred_element_type=F32)
               + w_inter * jnp.dot(c_old.astype(BF16), qt, preferred_element_type=F32))
        qn = (jnp.dot(n_hi, qt, preferred_element_type=F32) + jnp.dot(n_lo, qt, preferred_element_type=F32))[h:h + 1]
        den = jnp.sum(s, axis=0, keepdims=True) + w_inter * qn
        hh = num / jnp.maximum(jnp.abs(den), jnp.exp(-m_t))
        b_end = bf_row[:, L - 1:L]
        g = b_end + src_col
        m_new = jnp.maximum(b_end + m, jnp.max(g, axis=0, keepdims=True))
        ws = jnp.exp(g - m_new)
        decay = jnp.exp(b_end + m - m_new)
        kf = kh.astype(F32) * ws
        ct_ref[h] = decay * c_old + jnp.dot(vt, kf.astype(BF16), preferred_element_type=F32)
        n_ref[h:h + 1, :] = decay * n_ref[h:h + 1, :] + jnp.sum(kf, axis=0, keepdims=True)
        m_ref[h:h + 1, :] = jnp.broadcast_to(m_new, (1, 128))
        rows = slice(h * 256, (h + 1) * 256)
        hnorm = hh * lax.rsqrt(jnp.mean(hh * hh, axis=0, keepdims=True) + RMS_EPS)
        gain = jnp.concatenate([hn_ref[rows, :]] * (L // 128), axis=1)
        out_ref[rows, :] = (hnorm * gain * ot_ref[rows, :]).astype(BF16)


def _mlstm_scan(q_t, k, v_t, o_t, gr, head_norm_cols, batch):
    n = k.shape[0]
    nblk = SEQ // MLSTM_L
    row = lambda b, j: (b * nblk + j, 0)
    col = lambda b, j: (0, b * nblk + j)
    return pl.pallas_call(
        _mlstm_scan_kernel,
        grid=(batch, nblk),
        in_specs=[pl.BlockSpec((512, MLSTM_L), col),
                  pl.BlockSpec((MLSTM_L, 512), row),
                  pl.BlockSpec((1024, MLSTM_L), col),
                  pl.BlockSpec((1024, MLSTM_L), col),
                  pl.BlockSpec((8, MLSTM_L), col),
                  pl.BlockSpec((1024, 128), lambda b, j: (0, 0))],
        out_specs=pl.BlockSpec((1024, MLSTM_L), col),
        out_shape=jax.ShapeDtypeStruct((1024, n), BF16),
        scratch_shapes=[pltpu.VMEM((MLSTM_HEADS, MLSTM_V_DIM, MLSTM_QK_DIM), F32),
                        pltpu.VMEM((8, 128), F32),
                        pltpu.VMEM((8, 128), F32)],
        compiler_params=_cparams("parallel", "arbitrary"),
        name="mlstm_scan",
    )(q_t, k, v_t, o_t, gr, head_norm_cols)


def _pair_tables():
    lo, hi = [], []
    for g in range(MOE_GROUPS):
        for a in range(MOE_PER_GROUP):
            for b in range(a + 1, MOE_PER_GROUP):
                lo.append(g * MOE_PER_GROUP + a)
                hi.append(g * MOE_PER_GROUP + b)
    return np.asarray(lo, np.int32), np.asarray(hi, np.int32)


_PAIR_LO, _PAIR_HI = _pair_tables()


def _mix_out_kernel(a_ref, w_ref, res_ref, g_ref, wrh_ref, wrl_ref, br_ref, tril_ref, h_ref, xe_ref, route_ref, cnt_ref,
                    run_ref, *, a_transposed):
    @pl.when(pl.program_id(0) == 0)
    def _():
        run_ref[...] = jnp.zeros_like(run_ref)

    dims = _TN if a_transposed else (((1,), (0,)), ((), ()))
    h = res_ref[...] + lax.dot_general(a_ref[...], w_ref[...], dims, preferred_element_type=F32)
    h_ref[...] = h
    hn = _rms(h) * g_ref[...]
    xe_ref[:, 0:D_MODEL] = hn

    logits = _dot_split(hn, wrh_ref[...], wrl_ref[...]) + br_ref[...]
    lane_i = lax.broadcasted_iota(jnp.int32, logits.shape, 1)
    lane = lane_i.astype(F32)
    ninf = -jnp.inf
    is_g = (lane_i >= MOE_EXPERTS) & (lane_i < MOE_EXPERTS + MOE_GROUPS)
    glog = jnp.where(is_g, logits, ninf)
    gmax = jnp.max(glog, axis=-1, keepdims=True)
    gidx = jnp.min(jnp.where(glog == gmax, lane - MOE_EXPERTS, 99.0), axis=-1, keepdims=True)
    pg_top = 1.0 / jnp.sum(jnp.exp(glog - gmax), axis=-1, keepdims=True)
    lane_grp = (lane_i >> 3).astype(F32)
    in_grp = (lane_i < MOE_EXPERTS) & (lane_grp == gidx)
    ev = jnp.where(in_grp, logits, ninf)
    v1 = jnp.max(ev, axis=-1, keepdims=True)
    i1 = jnp.min(jnp.where(ev == v1, lane, 999.0), axis=-1, keepdims=True)
    ev2 = jnp.where(lane == i1, ninf, ev)
    v2 = jnp.max(ev2, axis=-1, keepdims=True)
    i2 = jnp.min(jnp.where(ev2 == v2, lane, 999.0), axis=-1, keepdims=True)
    e2 = jnp.exp(v2 - v1)
    w1 = pg_top / (1.0 + e2)
    w2 = pg_top * e2 / (1.0 + e2)
    first_lo = i1 < i2
    w_lo = jnp.where(first_lo, w1, w2)
    w_hi = jnp.where(first_lo, w2, w1)
    a = jnp.minimum(i1, i2) - MOE_PER_GROUP * gidx
    b = jnp.maximum(i1, i2) - MOE_PER_GROUP * gidx
    cls = gidx * MOE_PAIRS + a * (15.0 - a) * 0.5 + (b - a - 1.0)

    onehot = lane == cls
    prefix = jnp.dot(tril_ref[...], onehot.astype(BF16), preferred_element_type=F32)
    run = run_ref[0:1, :]
    rank = jnp.sum(jnp.where(onehot, prefix - 1.0 + run, 0.0), axis=-1, keepdims=True)
    run_new = run + prefix[TM - 1:TM, :]
    run_ref[...] = jnp.broadcast_to(run_new, run_ref.shape)
    cnt_ref[...] = jnp.broadcast_to(run_new, cnt_ref.shape)

    route = cls * float(2 ** ROUTE_SHIFT) + rank
    meta = jnp.where(lane_i == LANE_WLO, w_lo,
                     jnp.where(lane_i == LANE_WHI, w_hi,
                               jnp.where(lane_i == LANE_ROUTE, route, 0.0)))
    xe_ref[:, D_MODEL:XE_W] = meta
    route_ref[...] = meta.T[0:8, :].astype(jnp.int32)


def _mix_out(a, w, res, g_moe, w_rt, b_rt, tril, a_transposed):
    n = res.shape[0]
    kdim = w.shape[0]
    row = lambda i: (i, 0)
    fixed = lambda i: (0, 0)
    a_spec = pl.BlockSpec((kdim, TM), lambda i: (0, i)) if a_transposed else pl.BlockSpec((TM, kdim), row)
    w_rt_hi, w_rt_lo = _split_bf16(w_rt)
    return pl.pallas_call(
        functools.partial(_mix_out_kernel, a_transposed=a_transposed),
        grid=(n // TM,),
        in_specs=[a_spec,
                  pl.BlockSpec((kdim, D_MODEL), fixed),
                  pl.BlockSpec((TM, D_MODEL), row),
                  pl.BlockSpec((1, D_MODEL), fixed),
                  pl.BlockSpec((D_MODEL, 128), fixed),
                  pl.BlockSpec((D_MODEL, 128), fixed),
                  pl.BlockSpec((1, 128), fixed),
                  pl.BlockSpec((TM, TM), fixed)],
        out_specs=[pl.BlockSpec((TM, D_MODEL), row),
                   pl.BlockSpec((TM, XE_W), row),
                   pl.BlockSpec((8, TM), lambda i: (0, i)),
                   pl.BlockSpec((8, 128), fixed)],
        out_shape=[jax.ShapeDtypeStruct((n, D_MODEL), F32),
                   jax.ShapeDtypeStruct((n, XE_W), F32),
                   jax.ShapeDtypeStruct((8, n), jnp.int32),
                   jax.ShapeDtypeStruct((8, 128), F32)],
        scratch_shapes=[pltpu.VMEM((8, 128), F32)],
        compiler_params=_cparams("arbitrary"),
        name="mix_out",
    )(a, w, res, g_moe, w_rt_hi, w_rt_lo, b_rt, tril)


def _sorted_row(route_ref, offs_ref, idx):
    r = route_ref[idx]
    return offs_ref[r >> ROUTE_SHIFT] + (r & (2 ** ROUTE_SHIFT - 1))


def _dispatch_kernel(route_ref, offs_ref, cnt_ref, nused_ref, xe_ref, xs_ref, zbuf, sem, zsem):
    i = pl.program_id(0)
    base = i * TM

    @pl.when(i == 0)
    def _():
        zbuf[...] = jnp.zeros_like(zbuf)

        def per_class(c, carry):
            cnt = cnt_ref[c]
            start = offs_ref[c] + cnt
            pad = (-cnt) & (ROW_TILE - 1)
            head = (-cnt) & 7
            blocks = (pad - head) >> 3

            def fill_row(r, inner):
                pltpu.make_async_copy(zbuf.at[pl.ds(0, 1)], xs_ref.at[pl.ds(start + r, 1)], zsem).start()
                return inner

            def fill_block(b, inner):
                row0 = pl.multiple_of(start + head + b * 8, 8)
                pltpu.make_async_copy(zbuf.at[pl.ds(0, 8)], xs_ref.at[pl.ds(row0, 8)], zsem).start()
                return inner

            def drain_row(r, inner):
                pltpu.make_async_copy(zbuf.at[pl.ds(0, 1)], xs_ref.at[pl.ds(0, 1)], zsem).wait()
                return inner

            def drain_block(b, inner):
                pltpu.make_async_copy(zbuf.at[pl.ds(0, 8)], xs_ref.at[pl.ds(0, 8)], zsem).wait()
                return inner

            lax.fori_loop(0, head, fill_row, 0)
            lax.fori_loop(0, blocks, fill_block, 0)
            lax.fori_loop(0, head, drain_row, 0)
            lax.fori_loop(0, blocks, drain_block, 0)
            return carry

        lax.fori_loop(0, MOE_CLASSES, per_class, 0)

        def tail(t, carry):
            row0 = pl.multiple_of(t * ROW_TILE, ROW_TILE)
            cp = pltpu.make_async_copy(zbuf, xs_ref.at[pl.ds(row0, ROW_TILE)], zsem)
            cp.start()
            cp.wait()
            return carry

        lax.fori_loop(nused_ref[0], xs_ref.shape[0] // ROW_TILE, tail, 0)

    def issue(t8, carry):
        t0 = pl.multiple_of(t8 * ISSUE_ROWS, ISSUE_ROWS)
        for r in range(ISSUE_ROWS):
            p = _sorted_row(route_ref, offs_ref, base + t0 + r)
            pltpu.make_async_copy(xe_ref.at[pl.ds(t0 + r, 1)], xs_ref.at[pl.ds(p, 1)], sem).start()
        return carry

    lax.fori_loop(0, TM // ISSUE_ROWS, issue, 0)
    pltpu.make_async_copy(xe_ref, xs_ref.at[pl.ds(0, TM)], sem).wait()


def _dispatch(route, offs, cnt, n_used, xe, n_rows):
    n = xe.shape[0]
    grid_spec = pltpu.PrefetchScalarGridSpec(
        num_scalar_prefetch=4,
        grid=(n // TM,),
        in_specs=[pl.BlockSpec((TM, XE_W), lambda i, *_: (i, 0))],
        out_specs=pl.BlockSpec(memory_space=pl.ANY),
        scratch_shapes=[pltpu.VMEM((ROW_TILE, XE_W), F32), pltpu.SemaphoreType.DMA(()),
                        pltpu.SemaphoreType.DMA(())],
    )
    return pl.pallas_call(
        _dispatch_kernel,
        grid_spec=grid_spec,
        out_shape=jax.ShapeDtypeStruct((n_rows, XE_W), F32),
        compiler_params=_cparams("arbitrary"),
        name="moe_dispatch",
    )(route, offs, cnt, n_used, xe)


def _experts_kernel(tlo_ref, thi_ref, nused_ref, xs_ref, wg_lo, wu_lo, wd_lo, wg_hi, wu_hi, wd_hi, y_ref):
    del tlo_ref, thi_ref

    @pl.when(pl.program_id(0) < nused_ref[0])
    def _():
        x = xs_ref[:, 0:D_MODEL].astype(BF16)

        def ffn(wg, wu, wd, w):
            a = jnp.dot(x, wg[0], preferred_element_type=F32)
            u = jnp.dot(x, wu[0], preferred_element_type=F32)
            hid = (a * _sigmoid(a)) * u * w
            return jnp.dot(hid.astype(BF16), wd[0], preferred_element_type=F32)

        w_lo = xs_ref[:, D_MODEL + LANE_WLO:D_MODEL + LANE_WLO + 1]
        w_hi = xs_ref[:, D_MODEL + LANE_WHI:D_MODEL + LANE_WHI + 1]
        y = ffn(wg_lo, wu_lo, wd_lo, w_lo) + ffn(wg_hi, wu_hi, wd_hi, w_hi)
        for j in range(ROW_CHUNKS):
            y_ref[pl.ds(j, ROW_TILE, stride=ROW_CHUNKS), :] = y[:, j * 128:(j + 1) * 128]

    @pl.when(pl.program_id(0) >= nused_ref[0])
    def _():
        y_ref[...] = jnp.zeros_like(y_ref)


def _experts(tile_lo, tile_hi, n_used, xs, w_gate, w_up, w_down):
    n_tiles = xs.shape[0] // ROW_TILE
    rows = lambda i, tlo, thi, nu: (jnp.maximum(jnp.minimum(i, nu[0] - 1), 0), 0)
    lo = lambda i, tlo, thi, nu: (tlo[i], 0, 0)
    hi = lambda i, tlo, thi, nu: (thi[i], 0, 0)
    up_spec = lambda m: pl.BlockSpec((1, D_MODEL, MOE_HIDDEN), m)
    dn_spec = lambda m: pl.BlockSpec((1, MOE_HIDDEN, D_MODEL), m)
    grid_spec = pltpu.PrefetchScalarGridSpec(
        num_scalar_prefetch=3,
        grid=(n_tiles,),
        in_specs=[pl.BlockSpec((ROW_TILE, XE_W), rows),
                  up_spec(lo), up_spec(lo), dn_spec(lo),
                  up_spec(hi), up_spec(hi), dn_spec(hi)],
        out_specs=pl.BlockSpec((ROW_TILE * ROW_CHUNKS, 128), lambda i, tlo, thi, nu: (i, 0)),
    )
    return pl.pallas_call(
        _experts_kernel,
        grid_spec=grid_spec,
        out_shape=jax.ShapeDtypeStruct((xs.shape[0] * ROW_CHUNKS, 128), F32),
        compiler_params=_cparams("arbitrary"),
        name="moe_experts",
    )(tile_lo, tile_hi, n_used, xs, w_gate, w_up, w_down, w_gate, w_up, w_down)


def _combine_kernel(route_ref, offs_ref, h_ref, y_ref, g_ref, out_ref, buf, sem, *, final_norm):
    base = pl.program_id(0) * TM

    def issue(t8, carry):
        t0 = pl.multiple_of(t8 * ISSUE_ROWS, ISSUE_ROWS)
        for r in range(ISSUE_ROWS):
            p = pl.multiple_of(_sorted_row(route_ref, offs_ref, base + t0 + r) * ROW_CHUNKS, ROW_CHUNKS)
            dst = pl.multiple_of((t0 + r) * ROW_CHUNKS, ROW_CHUNKS)
            pltpu.make_async_copy(y_ref.at[pl.ds(p, ROW_CHUNKS)], buf.at[pl.ds(dst, ROW_CHUNKS)], sem).start()
        return carry

    lax.fori_loop(0, TM // ISSUE_ROWS, issue, 0)
    pltpu.make_async_copy(y_ref.at[pl.ds(0, TM * ROW_CHUNKS)], buf, sem).wait()
    moe = jnp.concatenate([buf[pl.ds(j, TM, stride=ROW_CHUNKS), :] for j in range(ROW_CHUNKS)], axis=1)
    out = h_ref[...] + moe
    if final_norm:
        out = _rms(out) * g_ref[...]
    out_ref[...] = out


def _combine(route, offs, h, y, gain, final_norm):
    n = h.shape[0]
    grid_spec = pltpu.PrefetchScalarGridSpec(
        num_scalar_prefetch=2,
        grid=(n // TM,),
        in_specs=[pl.BlockSpec((TM, D_MODEL), lambda i, *_: (i, 0)),
                  pl.BlockSpec(memory_space=pl.ANY),
                  pl.BlockSpec((1, D_MODEL), lambda i, *_: (0, 0))],
        out_specs=pl.BlockSpec((TM, D_MODEL), lambda i, *_: (i, 0)),
        scratch_shapes=[pltpu.VMEM((TM * ROW_CHUNKS, 128), F32), pltpu.SemaphoreType.DMA(())],
    )
    return pl.pallas_call(
        functools.partial(_combine_kernel, final_norm=final_norm),
        grid_spec=grid_spec,
        out_shape=jax.ShapeDtypeStruct((n, D_MODEL), F32),
        compiler_params=_cparams("arbitrary"),
        name="moe_combine",
    )(route, offs, h, y, gain)


def _moe_layer(a, w_out, res, moe_norm, w_group, b_group, w_router, b_router, w_gate, w_up, w_down,
               tril, out_gain, final_norm, a_transposed):
    n = res.shape[0]
    unused = 128 - MOE_EXPERTS - MOE_GROUPS
    w_rt = jnp.concatenate([w_router, w_group, jnp.zeros((D_MODEL, unused), F32)], axis=1)
    b_rt = jnp.concatenate([b_router, b_group, jnp.zeros((unused,), F32)]).reshape(1, 128)
    h, xe, route8, counts = _mix_out(a, w_out, res, moe_norm.reshape(1, D_MODEL), w_rt, b_rt, tril, a_transposed)

    n_tiles = n // ROW_TILE + MOE_CLASSES
    cnt = counts[0].astype(jnp.int32)
    tiles_c = (cnt + ROW_TILE - 1) // ROW_TILE
    tile_end = jnp.cumsum(tiles_c)
    offs = (tile_end - tiles_c) * ROW_TILE
    n_used = tile_end[-1]
    tile_ids = jnp.minimum(jnp.arange(n_tiles, dtype=jnp.int32), n_used - 1)
    tile_cls = jnp.sum((tile_end[None, 0:MOE_CLASSES] <= tile_ids[:, None]).astype(jnp.int32), axis=1)
    tile_cls = jnp.clip(tile_cls, 0, MOE_CLASSES - 1)
    tile_lo = jnp.asarray(_PAIR_LO)[tile_cls]
    tile_hi = jnp.asarray(_PAIR_HI)[tile_cls]
    route = route8[LANE_ROUTE]

    n_used = n_used.reshape(1)
    xs = _dispatch(route, offs, cnt, n_used, xe, n_tiles * ROW_TILE)
    y = _experts(tile_lo, tile_hi, n_used, xs, w_gate.astype(BF16), w_up.astype(BF16),
                 w_down.astype(BF16))
    return _combine(route, offs, h, y, out_gain.reshape(1, D_MODEL), final_norm)


def _rope_tables(pos, width):
    half = ROPE_DIM // 2
    inv_freq = jnp.power(jnp.float32(ROPE_THETA), -jnp.arange(half, dtype=F32) * (2.0 / ROPE_DIM))
    ang = pos.astype(F32)[:, None] * inv_freq[None, :]
    cos, sin = jnp.cos(ang), jnp.sin(ang)
    t = pos.shape[0]
    rest = NSA_HEAD_DIM - ROPE_DIM
    cos_t = jnp.concatenate([cos, cos, jnp.ones((t, rest), F32)], axis=1)
    sin_a = jnp.concatenate([-sin, jnp.zeros((t, half + rest), F32)], axis=1)
    sin_b = jnp.concatenate([jnp.zeros((t, half), F32), sin, jnp.zeros((t, rest), F32)], axis=1)
    rep = width // NSA_HEAD_DIM
    return jnp.tile(cos_t, (1, rep)), jnp.tile(sin_a, (1, rep)), jnp.tile(sin_b, (1, rep))


def _rope(x, cos_t, sin_a, sin_b):
    half = ROPE_DIM // 2
    parts = []
    for c in range(x.shape[1] // 128):
        xc = x[:, c * 128:(c + 1) * 128]
        parts.append(xc * cos_t + pltpu.roll(xc, 128 - half, axis=1) * sin_a + pltpu.roll(xc, half, axis=1) * sin_b)
    return parts[0] if len(parts) == 1 else jnp.concatenate(parts, axis=1)


def _rope_rows(x, cos_t, sin_a, sin_b):
    half = ROPE_DIM // 2
    reps = x.shape[0] // NSA_HEAD_DIM
    tile = lambda t: jnp.concatenate([t] * reps, axis=0)
    up = jnp.concatenate([x[half:], x[:half]], axis=0)
    down = jnp.concatenate([x[-half:], x[:-half]], axis=0)
    return x * tile(cos_t) + up * tile(sin_a) + down * tile(sin_b)


def _nsa_proj_kernel(h_ref, gq_ref, gkv_ref, wqt_ref, wkv_ref, wvt_ref, cos_ref, sa_ref, sb_ref,
                     cost_ref, sat_ref, sbt_ref,
                     qt_ref, gatet_ref, kc0_ref, kc1_ref, vc0_ref, vc1_ref, ks_ref, vst_ref, kw_ref, vwt_ref):
    r = _rms(h_ref[...])
    hq = (r * gq_ref[...]).astype(BF16)
    hk = (r * gkv_ref[...]).astype(BF16)
    qt = lax.dot_general(wqt_ref[0:1024, :], hq, _NT, preferred_element_type=F32)
    q_scale = NSA_HEAD_DIM ** -0.5 * LOG2_E
    qt_ref[...] = (_rope_rows(qt, cost_ref[...], sat_ref[...], sbt_ref[...]) * q_scale).astype(BF16)
    gatet_ref[...] = _sigmoid(lax.dot_general(wqt_ref[1024:1152, :], hq, _NT, preferred_element_type=F32))
    kc0_ref[...] = jnp.dot(hk, wkv_ref[:, 0:128], preferred_element_type=F32)
    kc1_ref[...] = jnp.dot(hk, wkv_ref[:, 128:256], preferred_element_type=F32)
    vc0_ref[...] = jnp.dot(hk, wkv_ref[:, 256:384], preferred_element_type=F32)
    vc1_ref[...] = jnp.dot(hk, wkv_ref[:, 384:512], preferred_element_type=F32)

    tm = h_ref.shape[0]
    cos_t, sin_a, sin_b = cos_ref[...], sa_ref[...], sb_ref[...]
    lane = lax.broadcasted_iota(jnp.int32, (tm, NSA_HEAD_DIM), 1)
    pos = (pl.program_id(0) % (SEQ // tm)) * tm + lax.broadcasted_iota(jnp.int32, (tm, NSA_HEAD_DIM), 0)
    blk_onehot = jnp.where(lane == (pos >> 6), 1.0, 0.0).astype(BF16)
    zeros = jnp.zeros((tm, NSA_HEAD_DIM), BF16)

    def store_keys(ref, val, aux):
        for g in range(NSA_GROUPS):
            ref[0, g] = jnp.concatenate([val[:, g * 64:(g + 1) * 64].astype(BF16), aux], axis=1)

    store_keys(ks_ref, _rope(jnp.dot(hk, wkv_ref[:, 512:768], preferred_element_type=F32), cos_t, sin_a, sin_b),
               blk_onehot)
    store_keys(kw_ref, _rope(jnp.dot(hk, wkv_ref[:, 1024:1280], preferred_element_type=F32), cos_t, sin_a, sin_b),
               zeros)

    row = lax.broadcasted_iota(jnp.int32, (V_ROWS - NSA_HEAD_DIM, tm), 0)
    ones_row = jnp.where(row == 0, 1.0, 0.0).astype(BF16)

    def store_values(ref, val_t):
        for g in range(NSA_GROUPS):
            ref[0, g] = jnp.concatenate([val_t[g * 64:(g + 1) * 64, :].astype(BF16), ones_row], axis=0)

    store_values(vst_ref, lax.dot_general(wvt_ref[0:256, :], hk, _NT, preferred_element_type=F32))
    store_values(vwt_ref, lax.dot_general(wvt_ref[256:512, :], hk, _NT, preferred_element_type=F32))


def _nsa_proj(h, g_q, g_kv, w_qt, w_kv, w_vt, rope_tabs, rope_tabs_t, batch):
    n = h.shape[0]
    nblk = SEQ // TM
    row = lambda i: (i, 0)
    col = lambda i: (0, i)
    fixed = lambda i: (0, 0)
    tab = lambda i: (i % nblk, 0)
    tab_t = lambda i: (0, i % nblk)
    key_spec = pl.BlockSpec((1, NSA_GROUPS, TM, KV_LANES), lambda i: (i // nblk, 0, i % nblk, 0))
    key_shape = jax.ShapeDtypeStruct((batch, NSA_GROUPS, SEQ, KV_LANES), BF16)
    val_spec = pl.BlockSpec((1, NSA_GROUPS, V_ROWS, TM), lambda i: (i // nblk, 0, 0, i % nblk))
    val_shape = jax.ShapeDtypeStruct((batch, NSA_GROUPS, V_ROWS, SEQ), BF16)
    raw_spec = pl.BlockSpec((TM, 128), row)
    raw_shape = jax.ShapeDtypeStruct((n, 128), F32)
    return pl.pallas_call(
        _nsa_proj_kernel,
        grid=(n // TM,),
        in_specs=[pl.BlockSpec((TM, D_MODEL), row),
                  pl.BlockSpec((1, D_MODEL), fixed),
                  pl.BlockSpec((1, D_MODEL), fixed),
                  pl.BlockSpec((1152, D_MODEL), fixed),
                  pl.BlockSpec((D_MODEL, 1536), fixed),
                  pl.BlockSpec((512, D_MODEL), fixed),
                  pl.BlockSpec((TM, 128), tab), pl.BlockSpec((TM, 128), tab), pl.BlockSpec((TM, 128), tab),
                  pl.BlockSpec((NSA_HEAD_DIM, TM), tab_t), pl.BlockSpec((NSA_HEAD_DIM, TM), tab_t),
                  pl.BlockSpec((NSA_HEAD_DIM, TM), tab_t)],
        out_specs=[pl.BlockSpec((1024, TM), col),
                   pl.BlockSpec((128, TM), col),
                   raw_spec, raw_spec, raw_spec, raw_spec,
                   key_spec, val_spec, key_spec, val_spec],
        out_shape=[jax.ShapeDtypeStruct((1024, n), BF16),
                   jax.ShapeDtypeStruct((128, n), F32),
                   raw_shape, raw_shape, raw_shape, raw_shape,
                   key_shape, val_shape, key_shape, val_shape],
        compiler_params=_cparams("parallel"),
        name="nsa_proj",
    )(h, g_q, g_kv, w_qt, w_kv, w_vt, *rope_tabs, *rope_tabs_t)


HALF_BLOCKS = SEQ // CMP_STRIDE
CMP_K = CMP_STRIDE * 256


def _compress_kernel(rk0_ref, rk1_ref, rv0_ref, rv1_ref, w1k_ref, w1v_ref, pek_ref, pev_ref, w2k_ref, w2v_ref,
                     cos_ref, sa_ref, sb_ref, kc_ref, vc_ref):
    lane = lax.broadcasted_iota(jnp.int32, (HALF_BLOCKS, 512), 1)
    first_half = (lane & 127) < 64

    def comp(raw_refs, w1_ref, pe_ref, w2_ref):
        x = jnp.concatenate([r[pl.ds(l, HALF_BLOCKS, stride=CMP_STRIDE), :]
                             for l in range(CMP_STRIDE) for r in raw_refs], axis=1).astype(BF16)
        r = jnp.dot(x, w1_ref[...], preferred_element_type=F32)
        rpe = jnp.dot(pe_ref[...].astype(BF16), w1_ref[...], preferred_element_type=F32)
        r = r + jnp.where(first_half, rpe[0:1, :], rpe[1:2, :])
        nxt = pltpu.roll(r, HALF_BLOCKS - 1, axis=0)
        nxt = jnp.concatenate([pltpu.roll(nxt[:, c * 128:(c + 1) * 128], 64, axis=1) for c in range(4)], axis=1)
        pre = r + nxt
        act = pre * _sigmoid(pre)
        return jnp.dot(act.astype(BF16), w2_ref[...], preferred_element_type=F32)

    kc = comp((rk0_ref, rk1_ref), w1k_ref, pek_ref, w2k_ref)
    kc = _rope(kc, cos_ref[...], sa_ref[...], sb_ref[...])
    vc = comp((rv0_ref, rv1_ref), w1v_ref, pev_ref, w2v_ref)
    for g in range(NSA_GROUPS):
        kc_ref[0, g] = kc[:, g * 64:(g + 1) * 64]
        vc_ref[0, g] = vc[:, g * 64:(g + 1) * 64]


def _compress_weights(pe, w1, w2):
    dh = NSA_HEAD_DIM
    w1r = w1.reshape(2, CMP_STRIDE, dh, dh)
    eye = jnp.eye(NSA_GROUPS, dtype=F32)
    w1_big = jnp.einsum("hldf,gk->lgdkhf", w1r, eye).reshape(CMP_K, 512).astype(BF16)
    pe_rows = jnp.broadcast_to(pe.reshape(2, CMP_STRIDE, 1, dh), (2, CMP_STRIDE, NSA_GROUPS, dh)).reshape(2, CMP_K)
    pe_rows = jnp.concatenate([pe_rows, jnp.zeros((6, CMP_K), F32)], axis=0)
    w2_halves = jnp.stack([w2, jnp.zeros_like(w2)])
    w2_bd = jnp.einsum("hdf,gk->ghdkf", w2_halves, eye).reshape(512, 256).astype(BF16)
    return w1_big, pe_rows, w2_bd


def _compress(raw_k, raw_v, wk, wv, cmp_tabs, batch):
    fixed = lambda b: (0, 0)
    raw_spec = pl.BlockSpec((SEQ, 128), lambda b: (b, 0))
    out_spec = pl.BlockSpec((1, NSA_GROUPS, N_CMP_PAD, NSA_HEAD_DIM), lambda b: (b, 0, 0, 0))
    out_shape = jax.ShapeDtypeStruct((batch, NSA_GROUPS, N_CMP_PAD, NSA_HEAD_DIM), F32)
    return pl.pallas_call(
        _compress_kernel,
        grid=(batch,),
        in_specs=[raw_spec, raw_spec, raw_spec, raw_spec,
                  pl.BlockSpec((CMP_K, 512), fixed), pl.BlockSpec((CMP_K, 512), fixed),
                  pl.BlockSpec((8, CMP_K), fixed), pl.BlockSpec((8, CMP_K), fixed),
                  pl.BlockSpec((512, 256), fixed), pl.BlockSpec((512, 256), fixed),
                  pl.BlockSpec((N_CMP_PAD, 128), fixed), pl.BlockSpec((N_CMP_PAD, 128), fixed),
                  pl.BlockSpec((N_CMP_PAD, 128), fixed)],
        out_specs=[out_spec, out_spec],
        out_shape=[out_shape, out_shape],
        compiler_params=_cparams("parallel"),
        name="nsa_compress",
    )(*raw_k, *raw_v, wk[0], wv[0], wk[1], wv[1], wk[2], wv[2], *cmp_tabs)


def _nsa_attn_kernel(qt_ref, gatet_ref, kc_ref, vc_ref, ks_ref, vst_ref, kw_ref, vwt_ref, ovl_ref, out_ref,
                     qx_sc, ms_sc, as_sc, aw_sc):
    qb = pl.program_id(2)
    q0 = qb * Q_BLOCK
    Q, HG, DH = Q_BLOCK, NSA_GROUP_SIZE, NSA_HEAD_DIM
    groups = range(ATTN_GROUPS)

    def heads(x):
        return jnp.concatenate([x] * HG, axis=1)

    n_row = lax.broadcasted_iota(jnp.int32, (N_CMP_PAD, Q), 0)
    t_lane = q0 + lax.broadcasted_iota(jnp.int32, (N_CMP_PAD, Q), 1)
    valid_c = heads((n_row * CMP_STRIDE + CMP_BLOCK - 1 <= t_lane) & (n_row < N_CMP_PAD - 1))
    j_row = lax.broadcasted_iota(jnp.int32, (N_SEL, Q), 0)
    cur = (q0 + lax.broadcasted_iota(jnp.int32, (N_SEL, Q), 1)) >> 6
    forced = (j_row == 0) | (j_row == cur) | (j_row == cur - 1)
    o_c = []
    for gi in groups:
        q4 = qt_ref[gi * HG * DH:(gi + 1) * HG * DH, :]
        q_t = jnp.concatenate([q4[h * DH:(h + 1) * DH, :] for h in range(HG)], axis=1)
        s_c = jnp.dot(kc_ref[0, gi].astype(BF16), q_t, preferred_element_type=F32)
        s_c = jnp.where(valid_c, s_c, NEG_INF)
        m_c = jnp.max(s_c, axis=0, keepdims=True)
        e_c = jnp.where(valid_c, jnp.exp2(s_c - m_c), 0.0)
        l_c = jnp.sum(e_c, axis=0, keepdims=True)
        p_c = e_c * jnp.where(l_c > 0.0, 1.0 / l_c, 0.0)
        o_c.append(jnp.dot(vc_ref[0, gi].T.astype(BF16), p_c.astype(BF16), preferred_element_type=F32))

        p_sum = p_c[:, 0:Q] + p_c[:, Q:2 * Q] + p_c[:, 2 * Q:3 * Q] + p_c[:, 3 * Q:4 * Q]
        imp_t = jnp.dot(ovl_ref[...], p_sum, precision=HIGHEST, preferred_element_type=F32)[0:N_SEL]
        imp_t = jnp.where(forced, FORCED_SCORE, imp_t)
        imp_t = jnp.where(j_row > cur, NEG_INF, imp_t)
        cnt = jnp.zeros((N_SEL, Q), F32)
        for i in range(N_SEL):
            ri = imp_t[i:i + 1, :]
            cnt = cnt + jnp.where(ri > imp_t, 1.0, jnp.where((ri == imp_t) & (j_row > i), 1.0, 0.0))
        sel_bias = jnp.where((cnt < SEL_TOPK) & (j_row <= cur), 0.0, NEG_INF).astype(BF16)

        qx_sc[gi] = jnp.concatenate([q_t, heads(sel_bias), jnp.zeros((KV_LANES - DH - N_SEL, HG * Q), BF16)],
                                    axis=0)

    def finish(acc):
        return acc[0:DH] / acc[DEN_ROW:DEN_ROW + 1]

    ms_sc[...] = jnp.full(ms_sc.shape, NEG_INF, F32)
    as_sc[...] = jnp.zeros(as_sc.shape, F32)
    n_kt = (q0 + Q + SLC_TILE - 1) // SLC_TILE

    def slc_tile(kt, bias):
        start = pl.multiple_of(kt * SLC_TILE, SLC_TILE)
        for gi in groups:
            s = jnp.dot(ks_ref[0, gi, pl.ds(start, SLC_TILE), :], qx_sc[gi], preferred_element_type=F32)
            if bias is not None:
                s = s + bias
            m_old = ms_sc[gi, 0:1, :]
            m_new = jnp.maximum(m_old, jnp.max(s, axis=0, keepdims=True))
            p = jnp.exp2(s - m_new).astype(BF16)
            pv = jnp.dot(vst_ref[0, gi, :, pl.ds(start, SLC_TILE)], p, preferred_element_type=F32)
            as_sc[gi] = as_sc[gi] * jnp.exp2(m_old - m_new) + pv
            ms_sc[gi] = jnp.broadcast_to(m_new, ms_sc.shape[1:])

    def slc_full(kt, carry):
        slc_tile(kt, None)
        return carry

    lax.fori_loop(0, n_kt - 1, slc_full, 0)

    w0 = pl.multiple_of(jnp.maximum(q0 - WINDOW, 0), Q)
    s_w = [jnp.dot(kw_ref[0, gi, pl.ds(w0, WIN_KEYS), :], qx_sc[gi], preferred_element_type=F32) for gi in groups]

    c_s = lax.broadcasted_iota(jnp.int32, (SLC_TILE, Q), 0)
    r_s = lax.broadcasted_iota(jnp.int32, (SLC_TILE, Q), 1)
    slc_tile(n_kt - 1, heads(jnp.where((n_kt - 1) * SLC_TILE + c_s <= q0 + r_s, 0.0, NEG_INF)))

    c_w = lax.broadcasted_iota(jnp.int32, (Q, Q), 0)
    r_w = lax.broadcasted_iota(jnp.int32, (Q, Q), 1)
    n_slab = WIN_KEYS // Q
    slab_bias = []
    for j in range(n_slab):
        key = w0 + j * Q + c_w
        tok = q0 + r_w
        slab_bias.append(heads(jnp.where((key <= tok) & (key > tok - WINDOW), 0.0, NEG_INF)))
    for gi in groups:
        slabs = [s_w[gi][j * Q:(j + 1) * Q] + slab_bias[j] for j in range(n_slab)]
        top = slabs[0]
        for sj in slabs[1:]:
            top = jnp.maximum(top, sj)
        m = jnp.max(top, axis=0, keepdims=True)
        p = jnp.concatenate([jnp.exp2(sj - m) for sj in slabs], axis=0).astype(BF16)
        aw_sc[gi] = jnp.dot(vwt_ref[0, gi, :, pl.ds(w0, WIN_KEYS)], p, preferred_element_type=F32)

    for gi in groups:
        g = pl.program_id(1) * ATTN_GROUPS + gi

        def gate_row(branch):
            return jnp.concatenate([gatet_ref[pl.ds((g * HG + h) * 3 + branch, 1), :] for h in range(HG)], axis=1)

        o = gate_row(0) * o_c[gi] + gate_row(1) * finish(as_sc[gi]) + gate_row(2) * finish(aw_sc[gi])
        out_ref[gi * HG * DH:(gi + 1) * HG * DH, :] = jnp.concatenate(
            [o[:, h * Q:(h + 1) * Q] for h in range(HG)], axis=0).astype(BF16)


def _overlap_t():
    n = np.arange(N_CMP_PAD)
    j = np.arange(128)
    cmp_start = n * CMP_STRIDE
    cmp_end = cmp_start + CMP_BLOCK - 1
    sel_start = j * SEL_BLOCK
    ovl = ((cmp_start[None, :] <= sel_start[:, None] + SEL_BLOCK - 1) & (cmp_end[None, :] >= sel_start[:, None])
           & (j[:, None] < N_SEL) & (n[None, :] < N_CMP_PAD - 1))
    return jnp.asarray(ovl.astype(np.float32))


def _nsa_attn(q_t, gates_t, kc, vc, ks, vs_t, kw, vw_t, batch):
    n = q_t.shape[1]
    gs = ATTN_GROUPS
    qcol = lambda b, g, i: (g, b * N_QB + i)
    gcol = lambda b, g, i: (0, b * N_QB + i)
    kv = lambda b, g, i: (b, g, 0, 0)
    rows = gs * NSA_GROUP_SIZE * NSA_HEAD_DIM
    lanes = NSA_GROUP_SIZE * Q_BLOCK
    return pl.pallas_call(
        _nsa_attn_kernel,
        grid=(batch, NSA_GROUPS // gs, N_QB),
        in_specs=[pl.BlockSpec((rows, Q_BLOCK), qcol),
                  pl.BlockSpec((128, Q_BLOCK), gcol),
                  pl.BlockSpec((1, gs, N_CMP_PAD, NSA_HEAD_DIM), kv),
                  pl.BlockSpec((1, gs, N_CMP_PAD, NSA_HEAD_DIM), kv),
                  pl.BlockSpec((1, gs, SEQ, KV_LANES), kv),
                  pl.BlockSpec((1, gs, V_ROWS, SEQ), kv),
                  pl.BlockSpec((1, gs, SEQ, KV_LANES), kv),
                  pl.BlockSpec((1, gs, V_ROWS, SEQ), kv),
                  pl.BlockSpec((128, N_CMP_PAD), lambda b, g, i: (0, 0))],
        out_specs=pl.BlockSpec((rows, Q_BLOCK), qcol),
        out_shape=jax.ShapeDtypeStruct((NSA_HEADS * NSA_HEAD_DIM, n), BF16),
        scratch_shapes=[pltpu.VMEM((gs, KV_LANES, lanes), BF16),
                        pltpu.VMEM((gs, 8, lanes), F32),
                        pltpu.VMEM((gs, V_ROWS, lanes), F32),
                        pltpu.VMEM((gs, V_ROWS, lanes), F32)],
        compiler_params=_cparams("parallel", "parallel", "arbitrary"),
        name="nsa_attn",
    )(q_t, gates_t, kc, vc, ks, vs_t, kw, vw_t, _overlap_t())


def kernel(x, mlstm_norm, mlstm_w_in, mlstm_gate_bias, mlstm_head_norm, mlstm_w_out, kv_norm, kv_w, cmp_pe_k, cmp_w1_k, cmp_w2_k, cmp_pe_v, cmp_w1_v, cmp_w2_v, nsa_norm, nsa_w_q, nsa_w_out, moe_norm, moe_w_group, moe_b_group, moe_w_router, moe_b_router, moe_w_gate, moe_w_up, moe_w_down, final_norm):
    batch, seq, d = x.shape
    assert seq == SEQ and d == D_MODEL
    assert mlstm_norm.shape[0] == 1 and nsa_norm.shape[0] == 1 and moe_norm.shape[0] == 2
    n = batch * seq
    x2d = x.reshape(n, d)
    tril = jnp.tril(jnp.ones((TM, TM), F32)).astype(BF16)

    w_in = mlstm_w_in[0]
    w_gate = jnp.concatenate([w_in[:, 3072:3080], jnp.zeros((d, 120), F32)], axis=1)
    b_gate = jnp.concatenate([mlstm_gate_bias[0], jnp.zeros((120,), F32)]).reshape(1, 128)
    q_t, k, v_t, o_t, gr = _mlstm_in(x2d, mlstm_norm[0].reshape(1, d), w_in[:, 0:3072].T.astype(BF16),
                                     w_in[:, 512:1024].astype(BF16), w_gate, b_gate)
    head_norm_cols = jnp.broadcast_to(mlstm_head_norm[0].reshape(d, 1), (d, 128))
    hs_t = _mlstm_scan(q_t, k, v_t, o_t, gr, head_norm_cols, batch)
    h = _moe_layer(hs_t, mlstm_w_out[0].astype(BF16), x2d, moe_norm[0], moe_w_group[0], moe_b_group[0],
                   moe_w_router[0], moe_b_router[0], moe_w_gate[0], moe_w_up[0], moe_w_down[0],
                   tril, final_norm, False, a_transposed=True)

    w_qt = jnp.concatenate([nsa_w_q[0].T, jnp.zeros((80, d), F32)], axis=0).astype(BF16)
    w_vt = jnp.concatenate([kv_w[:, 768:1024], kv_w[:, 1280:1536]], axis=1).T.astype(BF16)
    seq_tabs = _rope_tables(jnp.arange(SEQ), 128)
    seq_tabs_t = tuple(t[:, 0:NSA_HEAD_DIM].T for t in seq_tabs)
    q_t, gates_t, rk0, rk1, rv0, rv1, ks, vs_t, kw, vw_t = _nsa_proj(
        h, nsa_norm[0].reshape(1, d), kv_norm.reshape(1, d), w_qt, kv_w.astype(BF16), w_vt, seq_tabs, seq_tabs_t,
        batch)
    cmp_pos = jnp.arange(N_CMP_PAD) * CMP_STRIDE + CMP_BLOCK - 1
    kc, vc = _compress((rk0, rk1), (rv0, rv1), _compress_weights(cmp_pe_k, cmp_w1_k, cmp_w2_k),
                       _compress_weights(cmp_pe_v, cmp_w1_v, cmp_w2_v), _rope_tables(cmp_pos, 128), batch)
    att_t = _nsa_attn(q_t, gates_t, kc, vc, ks, vs_t, kw, vw_t, batch)
    out = _moe_layer(att_t, nsa_w_out[0].astype(BF16), h, moe_norm[1], moe_w_group[1], moe_b_group[1],
                     moe_w_router[1], moe_b_router[1], moe_w_gate[1], moe_w_up[1], moe_w_down[1],
                     tril, final_norm, True, a_transposed=True)
    return out.reshape(batch, seq, d)
```

```python
import functools

import numpy as np
import jax
import jax.numpy as jnp
from jax import lax
from jax.experimental import pallas as pl
from jax.experimental.pallas import tpu as pltpu

F32 = jnp.float32
BF16 = jnp.bfloat16
HIGHEST = lax.Precision.HIGHEST

D_MODEL = 1024
SEQ = 2048
RMS_EPS = 1e-6
NEG_INF = -1e30

MLSTM_HEADS = 4
MLSTM_V_DIM = 256
MLSTM_QK_DIM = 128
MLSTM_L = 256
GATE_SOFTCAP = 15.0

NSA_HEADS = 16
NSA_HEAD_DIM = 64
NSA_GROUPS = 4
NSA_GROUP_SIZE = 4
CMP_BLOCK = 32
CMP_STRIDE = 16
N_CMP_PAD = 128
SEL_BLOCK = 64
N_SEL = SEQ // SEL_BLOCK
SEL_TOPK = 16
WINDOW = 512
Q_BLOCK = 256
N_QB = SEQ // Q_BLOCK
WIN_KEYS = WINDOW + Q_BLOCK
SLC_TILE = 512
ATTN_GROUPS = 4
KV_LANES = 2 * NSA_HEAD_DIM
V_ROWS = NSA_HEAD_DIM + 16
DEN_ROW = NSA_HEAD_DIM
FORCED_SCORE = 1e6
ROPE_THETA = 500000.0
ROPE_DIM = 16

MOE_GROUPS = 4
MOE_PER_GROUP = 8
MOE_EXPERTS = 32
MOE_HIDDEN = 256
MOE_PAIRS = 28
MOE_CLASSES = MOE_GROUPS * MOE_PAIRS
ROW_TILE = 256
XE_W = D_MODEL + 128
LANE_WLO, LANE_WHI, LANE_ROUTE = 0, 1, 2
ROUTE_SHIFT = 16
ROW_CHUNKS = D_MODEL // 128
ISSUE_ROWS = 8

TM = 512
VMEM_LIMIT = 56 * 1024 * 1024

_NT = (((1,), (1,)), ((), ()))
_TN = (((0,), (0,)), ((), ()))


def _cparams(*sem):
    return pltpu.CompilerParams(dimension_semantics=sem, vmem_limit_bytes=VMEM_LIMIT)


def _rms(x):
    return x * lax.rsqrt(jnp.mean(x * x, axis=-1, keepdims=True) + RMS_EPS)


def _sigmoid(x):
    return 1.0 / (1.0 + jnp.exp(-x))


LOG2_E = 1.4426950408889634


def _split_bf16(x):
    hi = x.astype(BF16)
    return hi, (x - hi.astype(F32)).astype(BF16)


def _dot_split(x, w_hi, w_lo):
    x_hi, x_lo = _split_bf16(x)
    return (jnp.dot(x_hi, w_hi, preferred_element_type=F32) + jnp.dot(x_hi, w_lo, preferred_element_type=F32)
            + jnp.dot(x_lo, w_hi, preferred_element_type=F32))


def _mlstm_in_kernel(x_ref, g_ref, wt_ref, wk_ref, wgh_ref, wgl_ref, bg_ref, qt_ref, k_ref, vt_ref, ot_ref, gr_ref):
    hn = _rms(x_ref[...]) * g_ref[...]
    hb = hn.astype(BF16)
    qt_ref[...] = lax.dot_general(wt_ref[0:512, :], hb, _NT, preferred_element_type=F32).astype(BF16)
    k = jnp.dot(hb, wk_ref[...], preferred_element_type=F32)
    k_ref[...] = (k * (MLSTM_QK_DIM ** -0.5)).astype(BF16)
    vt_ref[...] = lax.dot_general(wt_ref[1024:2048, :], hb, _NT, preferred_element_type=F32).astype(BF16)
    ot_ref[...] = _sigmoid(lax.dot_general(wt_ref[2048:3072, :], hb, _NT, preferred_element_type=F32))
    gates = _dot_split(hn, wgh_ref[...], wgl_ref[...]) + bg_ref[...]
    gates = GATE_SOFTCAP * jnp.tanh(gates / GATE_SOFTCAP)
    lane = lax.broadcasted_iota(jnp.int32, gates.shape, 1)
    log_f = jnp.minimum(gates, 0.0) - jnp.log1p(jnp.exp(-jnp.abs(gates)))
    lg = jnp.where(lane < MLSTM_HEADS, gates, log_f)
    gr_ref[...] = lg.T[0:8, :]


def _mlstm_in(x2d, norm_g, w_t, w_k, w_gate, b_gate):
    n = x2d.shape[0]
    w_gate_hi, w_gate_lo = _split_bf16(w_gate)
    row = lambda i: (i, 0)
    col = lambda i: (0, i)
    fixed = lambda i: (0, 0)
    return pl.pallas_call(
        _mlstm_in_kernel,
        grid=(n // TM,),
        in_specs=[pl.BlockSpec((TM, D_MODEL), row),
                  pl.BlockSpec((1, D_MODEL), fixed),
                  pl.BlockSpec((3072, D_MODEL), fixed),
                  pl.BlockSpec((D_MODEL, 512), fixed),
                  pl.BlockSpec((D_MODEL, 128), fixed),
                  pl.BlockSpec((D_MODEL, 128), fixed),
                  pl.BlockSpec((1, 128), fixed)],
        out_specs=[pl.BlockSpec((512, TM), col),
                   pl.BlockSpec((TM, 512), row),
                   pl.BlockSpec((1024, TM), col),
                   pl.BlockSpec((1024, TM), col),
                   pl.BlockSpec((8, TM), col)],
        out_shape=[jax.ShapeDtypeStruct((512, n), BF16),
                   jax.ShapeDtypeStruct((n, 512), BF16),
                   jax.ShapeDtypeStruct((1024, n), BF16),
                   jax.ShapeDtypeStruct((1024, n), F32),
                   jax.ShapeDtypeStruct((8, n), F32)],
        compiler_params=_cparams("parallel"),
        name="mlstm_in",
    )(x2d, norm_g, w_t, w_k, w_gate_hi, w_gate_lo, b_gate)


def _mlstm_scan_kernel(qt_ref, k_ref, vt_ref, ot_ref, gr_ref, hn_ref, out_ref, ct_ref, n_ref, m_ref):
    L = MLSTM_L

    @pl.when(pl.program_id(1) == 0)
    def _():
        ct_ref[...] = jnp.zeros_like(ct_ref)
        n_ref[...] = jnp.zeros_like(n_ref)
        m_ref[...] = jnp.zeros_like(m_ref)

    row = lax.broadcasted_iota(jnp.int32, (L, L), 0)
    col = lax.broadcasted_iota(jnp.int32, (L, L), 1)
    causal_t = row <= col
    tril = (col <= row).astype(F32)
    eye = (col == row).astype(F32)
    gr = gr_ref[...]
    gr_pad = jnp.concatenate([gr, jnp.zeros((120, L), F32)], axis=0)
    b_row = lax.dot_general(gr, tril, _NT, precision=HIGHEST, preferred_element_type=F32)
    b_col = lax.dot_general(tril, gr_pad, _NT, precision=HIGHEST, preferred_element_type=F32)
    g_col = lax.dot_general(eye, gr_pad, _NT, precision=HIGHEST, preferred_element_type=F32)
    n_hi, n_lo = _split_bf16(n_ref[...])

    for h in range(MLSTM_HEADS):
        src_col = g_col[:, h:h + 1] - b_col[:, 4 + h:5 + h]
        bf_row = b_row[4 + h:5 + h, :]
        m = m_ref[h:h + 1, 0:1]
        dmat = jnp.where(causal_t, bf_row + src_col, NEG_INF)
        m_inter = bf_row + m
        m_t = jnp.maximum(m_inter, jnp.max(dmat, axis=0, keepdims=True))
        w_intra = jnp.exp(dmat - m_t)
        w_inter = jnp.exp(m_inter - m_t)
        qt = qt_ref[h * 128:(h + 1) * 128, :]
        kh = k_ref[:, h * 128:(h + 1) * 128]
        vt = vt_ref[h * 256:(h + 1) * 256, :]
        s = jnp.dot(kh, qt, preferred_element_type=F32) * w_intra
        c_old = ct_ref[h]
        num = (jnp.dot(vt, s.astype(BF16), preferred_element_type=F32)
               + w_inter * jnp.dot(c_old.astype(BF16), qt, preferred_element_type=F32))
        qn = (jnp.dot(n_hi, qt, preferred_element_type=F32) + jnp.dot(n_lo, qt, preferred_element_type=F32))[h:h + 1]
        den = jnp.sum(s, axis=0, keepdims=True) + w_inter * qn
        hh = num / jnp.maximum(jnp.abs(den), jnp.exp(-m_t))
        b_end = bf_row[:, L - 1:L]
        g = b_end + src_col
        m_new = jnp.maximum(b_end + m, jnp.max(g, axis=0, keepdims=True))
        ws = jnp.exp(g - m_new)
        decay = jnp.exp(b_end + m - m_new)
        kf = kh.astype(F32) * ws
        ct_ref[h] = decay * c_old + jnp.dot(vt, kf.astype(BF16), preferred_element_type=F32)
        n_ref[h:h + 1, :] = decay * n_ref[h:h + 1, :] + jnp.sum(kf, axis=0, keepdims=True)
        m_ref[h:h + 1, :] = jnp.broadcast_to(m_new, (1, 128))
        rows = slice(h * 256, (h + 1) * 256)
        hnorm = hh * lax.rsqrt(jnp.mean(hh * hh, axis=0, keepdims=True) + RMS_EPS)
        gain = jnp.concatenate([hn_ref[rows, :]] * (L // 128), axis=1)
        out_ref[rows, :] = (hnorm * gain * ot_ref[rows, :]).astype(BF16)


def _mlstm_scan(q_t, k, v_t, o_t, gr, head_norm_cols, batch):
    n = k.shape[0]
    nblk = SEQ // MLSTM_L
    row = lambda b, j: (b * nblk + j, 0)
    col = lambda b, j: (0, b * nblk + j)
    return pl.pallas_call(
        _mlstm_scan_kernel,
        grid=(batch, nblk),
        in_specs=[pl.BlockSpec((512, MLSTM_L), col),
                  pl.BlockSpec((MLSTM_L, 512), row),
                  pl.BlockSpec((1024, MLSTM_L), col),
                  pl.BlockSpec((1024, MLSTM_L), col),
                  pl.BlockSpec((8, MLSTM_L), col),
                  pl.BlockSpec((1024, 128), lambda b, j: (0, 0))],
        out_specs=pl.BlockSpec((1024, MLSTM_L), col),
        out_shape=jax.ShapeDtypeStruct((1024, n), BF16),
        scratch_shapes=[pltpu.VMEM((MLSTM_HEADS, MLSTM_V_DIM, MLSTM_QK_DIM), F32),
                        pltpu.VMEM((8, 128), F32),
                        pltpu.VMEM((8, 128), F32)],
        compiler_params=_cparams("parallel", "arbitrary"),
        name="mlstm_scan",
    )(q_t, k, v_t, o_t, gr, head_norm_cols)


def _pair_tables():
    lo, hi = [], []
    for g in range(MOE_GROUPS):
        for a in range(MOE_PER_GROUP):
            for b in range(a + 1, MOE_PER_GROUP):
                lo.append(g * MOE_PER_GROUP + a)
                hi.append(g * MOE_PER_GROUP + b)
    return np.asarray(lo, np.int32), np.asarray(hi, np.int32)


_PAIR_LO, _PAIR_HI = _pair_tables()


def _mix_out_kernel(a_ref, w_ref, res_ref, g_ref, wrh_ref, wrl_ref, br_ref, tril_ref, h_ref, xe_ref, route_ref, cnt_ref,
                    run_ref, *, a_transposed):
    @pl.when(pl.program_id(0) == 0)
    def _():
        run_ref[...] = jnp.zeros_like(run_ref)

    dims = _TN if a_transposed else (((1,), (0,)), ((), ()))
    h = res_ref[...] + lax.dot_general(a_ref[...], w_ref[...], dims, preferred_element_type=F32)
    h_ref[...] = h
    hn = _rms(h) * g_ref[...]
    xe_ref[:, 0:D_MODEL] = hn

    logits = _dot_split(hn, wrh_ref[...], wrl_ref[...]) + br_ref[...]
    lane_i = lax.broadcasted_iota(jnp.int32, logits.shape, 1)
    lane = lane_i.astype(F32)
    ninf = -jnp.inf
    is_g = (lane_i >= MOE_EXPERTS) & (lane_i < MOE_EXPERTS + MOE_GROUPS)
    glog = jnp.where(is_g, logits, ninf)
    gmax = jnp.max(glog, axis=-1, keepdims=True)
    gidx = jnp.min(jnp.where(glog == gmax, lane - MOE_EXPERTS, 99.0), axis=-1, keepdims=True)
    pg_top = 1.0 / jnp.sum(jnp.exp(glog - gmax), axis=-1, keepdims=True)
    lane_grp = (lane_i >> 3).astype(F32)
    in_grp = (lane_i < MOE_EXPERTS) & (lane_grp == gidx)
    ev = jnp.where(in_grp, logits, ninf)
    v1 = jnp.max(ev, axis=-1, keepdims=True)
    i1 = jnp.min(jnp.where(ev == v1, lane, 999.0), axis=-1, keepdims=True)
    ev2 = jnp.where(lane == i1, ninf, ev)
    v2 = jnp.max(ev2, axis=-1, keepdims=True)
    i2 = jnp.min(jnp.where(ev2 == v2, lane, 999.0), axis=-1, keepdims=True)
    e2 = jnp.exp(v2 - v1)
    w1 = pg_top / (1.0 + e2)
    w2 = pg_top * e2 / (1.0 + e2)
    first_lo = i1 < i2
    w_lo = jnp.where(first_lo, w1, w2)
    w_hi = jnp.where(first_lo, w2, w1)
    a = jnp.minimum(i1, i2) - MOE_PER_GROUP * gidx
    b = jnp.maximum(i1, i2) - MOE_PER_GROUP * gidx
    cls = gidx * MOE_PAIRS + a * (15.0 - a) * 0.5 + (b - a - 1.0)

    onehot = lane == cls
    prefix = jnp.dot(tril_ref[...], onehot.astype(BF16), preferred_element_type=F32)
    run = run_ref[0:1, :]
    rank = jnp.sum(jnp.where(onehot, prefix - 1.0 + run, 0.0), axis=-1, keepdims=True)
    run_new = run + prefix[TM - 1:TM, :]
    run_ref[...] = jnp.broadcast_to(run_new, run_ref.shape)
    cnt_ref[...] = jnp.broadcast_to(run_new, cnt_ref.shape)

    route = cls * float(2 ** ROUTE_SHIFT) + rank
    meta = jnp.where(lane_i == LANE_WLO, w_lo,
                     jnp.where(lane_i == LANE_WHI, w_hi,
                               jnp.where(lane_i == LANE_ROUTE, route, 0.0)))
    xe_ref[:, D_MODEL:XE_W] = meta
    route_ref[...] = meta.T[LANE_ROUTE:LANE_ROUTE + 1, :].astype(jnp.int32)


def _mix_out(a, w, res, g_moe, w_rt, b_rt, tril, a_transposed):
    n = res.shape[0]
    kdim = w.shape[0]
    row = lambda i: (i, 0)
    fixed = lambda i: (0, 0)
    a_spec = pl.BlockSpec((kdim, TM), lambda i: (0, i)) if a_transposed else pl.BlockSpec((TM, kdim), row)
    w_rt_hi, w_rt_lo = _split_bf16(w_rt)
    return pl.pallas_call(
        functools.partial(_mix_out_kernel, a_transposed=a_transposed),
        grid=(n // TM,),
        in_specs=[a_spec,
                  pl.BlockSpec((kdim, D_MODEL), fixed),
                  pl.BlockSpec((TM, D_MODEL), row),
                  pl.BlockSpec((1, D_MODEL), fixed),
                  pl.BlockSpec((D_MODEL, 128), fixed),
                  pl.BlockSpec((D_MODEL, 128), fixed),
                  pl.BlockSpec((1, 128), fixed),
                  pl.BlockSpec((TM, TM), fixed)],
        out_specs=[pl.BlockSpec((TM, D_MODEL), row),
                   pl.BlockSpec((TM, XE_W), row),
                   pl.BlockSpec((1, TM), lambda i: (0, i)),
                   pl.BlockSpec((8, 128), fixed)],
        out_shape=[jax.ShapeDtypeStruct((n, D_MODEL), F32),
                   jax.ShapeDtypeStruct((n, XE_W), F32),
                   jax.ShapeDtypeStruct((1, n), jnp.int32),
                   jax.ShapeDtypeStruct((8, 128), F32)],
        scratch_shapes=[pltpu.VMEM((8, 128), F32)],
        compiler_params=_cparams("arbitrary"),
        name="mix_out",
    )(a, w, res, g_moe, w_rt_hi, w_rt_lo, b_rt, tril)


def _sorted_row(route_ref, offs_ref, idx):
    r = route_ref[idx]
    return offs_ref[r >> ROUTE_SHIFT] + (r & (2 ** ROUTE_SHIFT - 1))


def _dispatch_kernel(route_ref, offs_ref, cnt_ref, nused_ref, xe_ref, xs_ref, zbuf, sem, zsem):
    i = pl.program_id(0)
    base = i * TM

    @pl.when(i == 0)
    def _():
        zbuf[...] = jnp.zeros_like(zbuf)

        def per_class(c, carry):
            cnt = cnt_ref[c]
            start = offs_ref[c] + cnt
            pad = (-cnt) & (ROW_TILE - 1)
            head = (-cnt) & 7
            blocks = (pad - head) >> 3

            def fill_row(r, inner):
                pltpu.make_async_copy(zbuf.at[pl.ds(0, 1)], xs_ref.at[pl.ds(start + r, 1)], zsem).start()
                return inner

            def fill_block(b, inner):
                row0 = pl.multiple_of(start + head + b * 8, 8)
                pltpu.make_async_copy(zbuf.at[pl.ds(0, 8)], xs_ref.at[pl.ds(row0, 8)], zsem).start()
                return inner

            def drain_row(r, inner):
                pltpu.make_async_copy(zbuf.at[pl.ds(0, 1)], xs_ref.at[pl.ds(0, 1)], zsem).wait()
                return inner

            def drain_block(b, inner):
                pltpu.make_async_copy(zbuf.at[pl.ds(0, 8)], xs_ref.at[pl.ds(0, 8)], zsem).wait()
                return inner

            lax.fori_loop(0, head, fill_row, 0)
            lax.fori_loop(0, blocks, fill_block, 0)
            lax.fori_loop(0, head, drain_row, 0)
            lax.fori_loop(0, blocks, drain_block, 0)
            return carry

        lax.fori_loop(0, MOE_CLASSES, per_class, 0)

        def tail(t, carry):
            row0 = pl.multiple_of(t * ROW_TILE, ROW_TILE)
            cp = pltpu.make_async_copy(zbuf, xs_ref.at[pl.ds(row0, ROW_TILE)], zsem)
            cp.start()
            cp.wait()
            return carry

        lax.fori_loop(nused_ref[0], xs_ref.shape[0] // ROW_TILE, tail, 0)

    def issue(t8, carry):
        t0 = pl.multiple_of(t8 * ISSUE_ROWS, ISSUE_ROWS)
        for r in range(ISSUE_ROWS):
            p = _sorted_row(route_ref, offs_ref, base + t0 + r)
            pltpu.make_async_copy(xe_ref.at[pl.ds(t0 + r, 1)], xs_ref.at[pl.ds(p, 1)], sem).start()
        return carry

    lax.fori_loop(0, TM // ISSUE_ROWS, issue, 0)
    pltpu.make_async_copy(xe_ref, xs_ref.at[pl.ds(0, TM)], sem).wait()


def _dispatch(route, offs, cnt, n_used, xe, n_rows):
    n = xe.shape[0]
    grid_spec = pltpu.PrefetchScalarGridSpec(
        num_scalar_prefetch=4,
        grid=(n // TM,),
        in_specs=[pl.BlockSpec((TM, XE_W), lambda i, *_: (i, 0))],
        out_specs=pl.BlockSpec(memory_space=pl.ANY),
        scratch_shapes=[pltpu.VMEM((ROW_TILE, XE_W), F32), pltpu.SemaphoreType.DMA(()),
                        pltpu.SemaphoreType.DMA(())],
    )
    return pl.pallas_call(
        _dispatch_kernel,
        grid_spec=grid_spec,
        out_shape=jax.ShapeDtypeStruct((n_rows, XE_W), F32),
        compiler_params=_cparams("arbitrary"),
        name="moe_dispatch",
    )(route, offs, cnt, n_used, xe)


def _experts_kernel(tlo_ref, thi_ref, nused_ref, xs_ref, wg_lo, wu_lo, wd_lo, wg_hi, wu_hi, wd_hi, y_ref):
    del tlo_ref, thi_ref

    @pl.when(pl.program_id(0) < nused_ref[0])
    def _():
        x = xs_ref[:, 0:D_MODEL].astype(BF16)

        def ffn(wg, wu, wd, w):
            a = jnp.dot(x, wg[0], preferred_element_type=F32)
            u = jnp.dot(x, wu[0], preferred_element_type=F32)
            hid = (a * _sigmoid(a)) * u * w
            return jnp.dot(hid.astype(BF16), wd[0], preferred_element_type=F32)

        w_lo = xs_ref[:, D_MODEL + LANE_WLO:D_MODEL + LANE_WLO + 1]
        w_hi = xs_ref[:, D_MODEL + LANE_WHI:D_MODEL + LANE_WHI + 1]
        y = ffn(wg_lo, wu_lo, wd_lo, w_lo) + ffn(wg_hi, wu_hi, wd_hi, w_hi)
        for j in range(ROW_CHUNKS):
            y_ref[pl.ds(j, ROW_TILE, stride=ROW_CHUNKS), :] = y[:, j * 128:(j + 1) * 128]

    @pl.when(pl.program_id(0) >= nused_ref[0])
    def _():
        y_ref[...] = jnp.zeros_like(y_ref)


def _experts(tile_lo, tile_hi, n_used, xs, w_gate, w_up, w_down):
    n_tiles = xs.shape[0] // ROW_TILE
    rows = lambda i, tlo, thi, nu: (jnp.maximum(jnp.minimum(i, nu[0] - 1), 0), 0)
    lo = lambda i, tlo, thi, nu: (tlo[i], 0, 0)
    hi = lambda i, tlo, thi, nu: (thi[i], 0, 0)
    up_spec = lambda m: pl.BlockSpec((1, D_MODEL, MOE_HIDDEN), m)
    dn_spec = lambda m: pl.BlockSpec((1, MOE_HIDDEN, D_MODEL), m)
    grid_spec = pltpu.PrefetchScalarGridSpec(
        num_scalar_prefetch=3,
        grid=(n_tiles,),
        in_specs=[pl.BlockSpec((ROW_TILE, XE_W), rows),
                  up_spec(lo), up_spec(lo), dn_spec(lo),
                  up_spec(hi), up_spec(hi), dn_spec(hi)],
        out_specs=pl.BlockSpec((ROW_TILE * ROW_CHUNKS, 128), lambda i, tlo, thi, nu: (i, 0)),
    )
    return pl.pallas_call(
        _experts_kernel,
        grid_spec=grid_spec,
        out_shape=jax.ShapeDtypeStruct((xs.shape[0] * ROW_CHUNKS, 128), F32),
        compiler_params=_cparams("arbitrary"),
        name="moe_experts",
    )(tile_lo, tile_hi, n_used, xs, w_gate, w_up, w_down, w_gate, w_up, w_down)


def _combine_kernel(route_ref, offs_ref, h_ref, y_ref, g_ref, out_ref, buf, sem, *, final_norm):
    base = pl.program_id(0) * TM

    def issue(t8, carry):
        t0 = pl.multiple_of(t8 * ISSUE_ROWS, ISSUE_ROWS)
        for r in range(ISSUE_ROWS):
            p = pl.multiple_of(_sorted_row(route_ref, offs_ref, base + t0 + r) * ROW_CHUNKS, ROW_CHUNKS)
            dst = pl.multiple_of((t0 + r) * ROW_CHUNKS, ROW_CHUNKS)
            pltpu.make_async_copy(y_ref.at[pl.ds(p, ROW_CHUNKS)], buf.at[pl.ds(dst, ROW_CHUNKS)], sem).start()
        return carry

    lax.fori_loop(0, TM // ISSUE_ROWS, issue, 0)
    pltpu.make_async_copy(y_ref.at[pl.ds(0, TM * ROW_CHUNKS)], buf, sem).wait()
    moe = jnp.concatenate([buf[pl.ds(j, TM, stride=ROW_CHUNKS), :] for j in range(ROW_CHUNKS)], axis=1)
    out = h_ref[...] + moe
    if final_norm:
        out = _rms(out) * g_ref[...]
    out_ref[...] = out


def _combine(route, offs, h, y, gain, final_norm):
    n = h.shape[0]
    grid_spec = pltpu.PrefetchScalarGridSpec(
        num_scalar_prefetch=2,
        grid=(n // TM,),
        in_specs=[pl.BlockSpec((TM, D_MODEL), lambda i, *_: (i, 0)),
                  pl.BlockSpec(memory_space=pl.ANY),
                  pl.BlockSpec((1, D_MODEL), lambda i, *_: (0, 0))],
        out_specs=pl.BlockSpec((TM, D_MODEL), lambda i, *_: (i, 0)),
        scratch_shapes=[pltpu.VMEM((TM * ROW_CHUNKS, 128), F32), pltpu.SemaphoreType.DMA(())],
    )
    return pl.pallas_call(
        functools.partial(_combine_kernel, final_norm=final_norm),
        grid_spec=grid_spec,
        out_shape=jax.ShapeDtypeStruct((n, D_MODEL), F32),
        compiler_params=_cparams("arbitrary"),
        name="moe_combine",
    )(route, offs, h, y, gain)


def _moe_layer(a, w_out, res, moe_norm, w_group, b_group, w_router, b_router, w_gate, w_up, w_down,
               tril, out_gain, final_norm, a_transposed):
    n = res.shape[0]
    unused = 128 - MOE_EXPERTS - MOE_GROUPS
    w_rt = jnp.concatenate([w_router, w_group, jnp.zeros((D_MODEL, unused), F32)], axis=1)
    b_rt = jnp.concatenate([b_router, b_group, jnp.zeros((unused,), F32)]).reshape(1, 128)
    h, xe, route_row, counts = _mix_out(a, w_out, res, moe_norm.reshape(1, D_MODEL), w_rt, b_rt, tril, a_transposed)

    n_tiles = n // ROW_TILE + MOE_CLASSES
    cnt = counts[0].astype(jnp.int32)
    tiles_c = (cnt + ROW_TILE - 1) // ROW_TILE
    tile_end = jnp.cumsum(tiles_c)
    offs = (tile_end - tiles_c) * ROW_TILE
    n_used = tile_end[-1]
    tile_ids = jnp.minimum(jnp.arange(n_tiles, dtype=jnp.int32), n_used - 1)
    tile_cls = jnp.sum((tile_end[None, 0:MOE_CLASSES] <= tile_ids[:, None]).astype(jnp.int32), axis=1)
    tile_cls = jnp.clip(tile_cls, 0, MOE_CLASSES - 1)
    tile_lo = jnp.asarray(_PAIR_LO)[tile_cls]
    tile_hi = jnp.asarray(_PAIR_HI)[tile_cls]
    route = route_row.reshape(n)

    n_used = n_used.reshape(1)
    xs = _dispatch(route, offs, cnt, n_used, xe, n_tiles * ROW_TILE)
    y = _experts(tile_lo, tile_hi, n_used, xs, w_gate.astype(BF16), w_up.astype(BF16),
                 w_down.astype(BF16))
    return _combine(route, offs, h, y, out_gain.reshape(1, D_MODEL), final_norm)


def _rope_tables(pos, width):
    half = ROPE_DIM // 2
    inv_freq = jnp.power(jnp.float32(ROPE_THETA), -jnp.arange(half, dtype=F32) * (2.0 / ROPE_DIM))
    ang = pos.astype(F32)[:, None] * inv_freq[None, :]
    cos, sin = jnp.cos(ang), jnp.sin(ang)
    t = pos.shape[0]
    rest = NSA_HEAD_DIM - ROPE_DIM
    cos_t = jnp.concatenate([cos, cos, jnp.ones((t, rest), F32)], axis=1)
    sin_a = jnp.concatenate([-sin, jnp.zeros((t, half + rest), F32)], axis=1)
    sin_b = jnp.concatenate([jnp.zeros((t, half), F32), sin, jnp.zeros((t, rest), F32)], axis=1)
    rep = width // NSA_HEAD_DIM
    return jnp.tile(cos_t, (1, rep)), jnp.tile(sin_a, (1, rep)), jnp.tile(sin_b, (1, rep))


def _rope(x, cos_t, sin_a, sin_b):
    half = ROPE_DIM // 2
    parts = []
    for c in range(x.shape[1] // 128):
        xc = x[:, c * 128:(c + 1) * 128]
        parts.append(xc * cos_t + pltpu.roll(xc, 128 - half, axis=1) * sin_a + pltpu.roll(xc, half, axis=1) * sin_b)
    return parts[0] if len(parts) == 1 else jnp.concatenate(parts, axis=1)


def _rope_rows(x, cos_t, sin_a, sin_b):
    half = ROPE_DIM // 2
    reps = x.shape[0] // NSA_HEAD_DIM
    tile = lambda t: jnp.concatenate([t] * reps, axis=0)
    up = jnp.concatenate([x[half:], x[:half]], axis=0)
    down = jnp.concatenate([x[-half:], x[:-half]], axis=0)
    return x * tile(cos_t) + up * tile(sin_a) + down * tile(sin_b)


def _nsa_proj_kernel(h_ref, gq_ref, gkv_ref, wqt_ref, wkv_ref, wvt_ref, cos_ref, sa_ref, sb_ref,
                     cost_ref, sat_ref, sbt_ref,
                     qt_ref, gatet_ref, kc0_ref, kc1_ref, vc0_ref, vc1_ref, ks_ref, vst_ref, kw_ref, vwt_ref):
    r = _rms(h_ref[...])
    hq = (r * gq_ref[...]).astype(BF16)
    hk = (r * gkv_ref[...]).astype(BF16)
    qt = lax.dot_general(wqt_ref[0:1024, :], hq, _NT, preferred_element_type=F32)
    q_scale = NSA_HEAD_DIM ** -0.5 * LOG2_E
    qt_ref[...] = (_rope_rows(qt, cost_ref[...], sat_ref[...], sbt_ref[...]) * q_scale).astype(BF16)
    gatet_ref[...] = _sigmoid(lax.dot_general(wqt_ref[1024:1152, :], hq, _NT, preferred_element_type=F32))
    kc0_ref[...] = jnp.dot(hk, wkv_ref[:, 0:128], preferred_element_type=F32)
    kc1_ref[...] = jnp.dot(hk, wkv_ref[:, 128:256], preferred_element_type=F32)
    vc0_ref[...] = jnp.dot(hk, wkv_ref[:, 256:384], preferred_element_type=F32)
    vc1_ref[...] = jnp.dot(hk, wkv_ref[:, 384:512], preferred_element_type=F32)

    tm = h_ref.shape[0]
    cos_t, sin_a, sin_b = cos_ref[...], sa_ref[...], sb_ref[...]
    lane = lax.broadcasted_iota(jnp.int32, (tm, NSA_HEAD_DIM), 1)
    pos = (pl.program_id(0) % (SEQ // tm)) * tm + lax.broadcasted_iota(jnp.int32, (tm, NSA_HEAD_DIM), 0)
    blk_onehot = jnp.where(lane == (pos >> 6), 1.0, 0.0).astype(BF16)
    zeros = jnp.zeros((tm, NSA_HEAD_DIM), BF16)

    def store_keys(ref, val, aux):
        for g in range(NSA_GROUPS):
            ref[0, g] = jnp.concatenate([val[:, g * 64:(g + 1) * 64].astype(BF16), aux], axis=1)

    store_keys(ks_ref, _rope(jnp.dot(hk, wkv_ref[:, 512:768], preferred_element_type=F32), cos_t, sin_a, sin_b),
               blk_onehot)
    store_keys(kw_ref, _rope(jnp.dot(hk, wkv_ref[:, 1024:1280], preferred_element_type=F32), cos_t, sin_a, sin_b),
               zeros)

    row = lax.broadcasted_iota(jnp.int32, (V_ROWS - NSA_HEAD_DIM, tm), 0)
    ones_row = jnp.where(row == 0, 1.0, 0.0).astype(BF16)

    def store_values(ref, val_t):
        for g in range(NSA_GROUPS):
            ref[0, g] = jnp.concatenate([val_t[g * 64:(g + 1) * 64, :].astype(BF16), ones_row], axis=0)

    store_values(vst_ref, lax.dot_general(wvt_ref[0:256, :], hk, _NT, preferred_element_type=F32))
    store_values(vwt_ref, lax.dot_general(wvt_ref[256:512, :], hk, _NT, preferred_element_type=F32))


def _nsa_proj(h, g_q, g_kv, w_qt, w_kv, w_vt, rope_tabs, rope_tabs_t, batch):
    n = h.shape[0]
    nblk = SEQ // TM
    row = lambda i: (i, 0)
    col = lambda i: (0, i)
    fixed = lambda i: (0, 0)
    tab = lambda i: (i % nblk, 0)
    tab_t = lambda i: (0, i % nblk)
    key_spec = pl.BlockSpec((1, NSA_GROUPS, TM, KV_LANES), lambda i: (i // nblk, 0, i % nblk, 0))
    key_shape = jax.ShapeDtypeStruct((batch, NSA_GROUPS, SEQ, KV_LANES), BF16)
    val_spec = pl.BlockSpec((1, NSA_GROUPS, V_ROWS, TM), lambda i: (i // nblk, 0, 0, i % nblk))
    val_shape = jax.ShapeDtypeStruct((batch, NSA_GROUPS, V_ROWS, SEQ), BF16)
    raw_spec = pl.BlockSpec((TM, 128), row)
    raw_shape = jax.ShapeDtypeStruct((n, 128), F32)
    return pl.pallas_call(
        _nsa_proj_kernel,
        grid=(n // TM,),
        in_specs=[pl.BlockSpec((TM, D_MODEL), row),
                  pl.BlockSpec((1, D_MODEL), fixed),
                  pl.BlockSpec((1, D_MODEL), fixed),
                  pl.BlockSpec((1152, D_MODEL), fixed),
                  pl.BlockSpec((D_MODEL, 1536), fixed),
                  pl.BlockSpec((512, D_MODEL), fixed),
                  pl.BlockSpec((TM, 128), tab), pl.BlockSpec((TM, 128), tab), pl.BlockSpec((TM, 128), tab),
                  pl.BlockSpec((NSA_HEAD_DIM, TM), tab_t), pl.BlockSpec((NSA_HEAD_DIM, TM), tab_t),
                  pl.BlockSpec((NSA_HEAD_DIM, TM), tab_t)],
        out_specs=[pl.BlockSpec((1024, TM), col),
                   pl.BlockSpec((128, TM), col),
                   raw_spec, raw_spec, raw_spec, raw_spec,
                   key_spec, val_spec, key_spec, val_spec],
        out_shape=[jax.ShapeDtypeStruct((1024, n), BF16),
                   jax.ShapeDtypeStruct((128, n), F32),
                   raw_shape, raw_shape, raw_shape, raw_shape,
                   key_shape, val_shape, key_shape, val_shape],
        compiler_params=_cparams("parallel"),
        name="nsa_proj",
    )(h, g_q, g_kv, w_qt, w_kv, w_vt, *rope_tabs, *rope_tabs_t)


HALF_BLOCKS = SEQ // CMP_STRIDE
CMP_K = CMP_STRIDE * 256


def _compress_kernel(rk0_ref, rk1_ref, rv0_ref, rv1_ref, w1k_ref, w1v_ref, pek_ref, pev_ref, w2k_ref, w2v_ref,
                     cos_ref, sa_ref, sb_ref, kc_ref, vc_ref):
    lane = lax.broadcasted_iota(jnp.int32, (HALF_BLOCKS, 512), 1)
    first_half = (lane & 127) < 64

    def comp(raw_refs, w1_ref, pe_ref, w2_ref):
        x = jnp.concatenate([r[pl.ds(l, HALF_BLOCKS, stride=CMP_STRIDE), :]
                             for l in range(CMP_STRIDE) for r in raw_refs], axis=1).astype(BF16)
        r = jnp.dot(x, w1_ref[...], preferred_element_type=F32)
        rpe = jnp.dot(pe_ref[...].astype(BF16), w1_ref[...], preferred_element_type=F32)
        r = r + jnp.where(first_half, rpe[0:1, :], rpe[1:2, :])
        nxt = pltpu.roll(r, HALF_BLOCKS - 1, axis=0)
        nxt = jnp.concatenate([pltpu.roll(nxt[:, c * 128:(c + 1) * 128], 64, axis=1) for c in range(4)], axis=1)
        pre = r + nxt
        act = pre * _sigmoid(pre)
        return jnp.dot(act.astype(BF16), w2_ref[...], preferred_element_type=F32)

    kc = comp((rk0_ref, rk1_ref), w1k_ref, pek_ref, w2k_ref)
    kc = _rope(kc, cos_ref[...], sa_ref[...], sb_ref[...])
    vc = comp((rv0_ref, rv1_ref), w1v_ref, pev_ref, w2v_ref)
    for g in range(NSA_GROUPS):
        kc_ref[0, g] = kc[:, g * 64:(g + 1) * 64]
        vc_ref[0, g] = vc[:, g * 64:(g + 1) * 64]


def _compress_weights(pe, w1, w2):
    dh = NSA_HEAD_DIM
    w1r = w1.reshape(2, CMP_STRIDE, dh, dh)
    halves = jnp.concatenate([w1r[0], w1r[1]], axis=-1)
    rows = jnp.broadcast_to(halves[:, None], (CMP_STRIDE, NSA_GROUPS, dh, 2 * dh)).reshape(CMP_K, 2 * dh)
    row_group = (jnp.arange(CMP_K) // dh) % NSA_GROUPS
    w1_big = jnp.concatenate([jnp.where((row_group == g)[:, None], rows, 0.0) for g in range(NSA_GROUPS)],
                             axis=1).astype(BF16)
    pe_rows = jnp.broadcast_to(pe.reshape(2, CMP_STRIDE, 1, dh), (2, CMP_STRIDE, NSA_GROUPS, dh)).reshape(2, CMP_K)
    pe_rows = jnp.concatenate([pe_rows, jnp.zeros((6, CMP_K), F32)], axis=0)
    rows2 = jnp.tile(jnp.concatenate([w2, jnp.zeros_like(w2)], axis=0), (NSA_GROUPS, 1))
    row_group2 = jnp.arange(NSA_GROUPS * 2 * dh) // (2 * dh)
    w2_bd = jnp.concatenate([jnp.where((row_group2 == g)[:, None], rows2, 0.0) for g in range(NSA_GROUPS)],
                            axis=1).astype(BF16)
    return w1_big, pe_rows, w2_bd


def _compress(raw_k, raw_v, wk, wv, cmp_tabs, batch):
    fixed = lambda b: (0, 0)
    raw_spec = pl.BlockSpec((SEQ, 128), lambda b: (b, 0))
    out_spec = pl.BlockSpec((1, NSA_GROUPS, N_CMP_PAD, NSA_HEAD_DIM), lambda b: (b, 0, 0, 0))
    out_shape = jax.ShapeDtypeStruct((batch, NSA_GROUPS, N_CMP_PAD, NSA_HEAD_DIM), F32)
    return pl.pallas_call(
        _compress_kernel,
        grid=(batch,),
        in_specs=[raw_spec, raw_spec, raw_spec, raw_spec,
                  pl.BlockSpec((CMP_K, 512), fixed), pl.BlockSpec((CMP_K, 512), fixed),
                  pl.BlockSpec((8, CMP_K), fixed), pl.BlockSpec((8, CMP_K), fixed),
                  pl.BlockSpec((512, 256), fixed), pl.BlockSpec((512, 256), fixed),
                  pl.BlockSpec((N_CMP_PAD, 128), fixed), pl.BlockSpec((N_CMP_PAD, 128), fixed),
                  pl.BlockSpec((N_CMP_PAD, 128), fixed)],
        out_specs=[out_spec, out_spec],
        out_shape=[out_shape, out_shape],
        compiler_params=_cparams("parallel"),
        name="nsa_compress",
    )(*raw_k, *raw_v, wk[0], wv[0], wk[1], wv[1], wk[2], wv[2], *cmp_tabs)


def _nsa_attn_kernel(qt_ref, gatet_ref, kc_ref, vc_ref, ks_ref, vst_ref, kw_ref, vwt_ref, ovl_ref, out_ref,
                     qx_sc, ms_sc, as_sc, aw_sc):
    qb = pl.program_id(2)
    q0 = qb * Q_BLOCK
    Q, HG, DH = Q_BLOCK, NSA_GROUP_SIZE, NSA_HEAD_DIM
    groups = range(ATTN_GROUPS)

    def heads(x):
        return jnp.concatenate([x] * HG, axis=1)

    n_row = lax.broadcasted_iota(jnp.int32, (N_CMP_PAD, Q), 0)
    t_lane = q0 + lax.broadcasted_iota(jnp.int32, (N_CMP_PAD, Q), 1)
    valid_c = heads((n_row * CMP_STRIDE + CMP_BLOCK - 1 <= t_lane) & (n_row < N_CMP_PAD - 1))
    j_row = lax.broadcasted_iota(jnp.int32, (N_SEL, Q), 0)
    cur = (q0 + lax.broadcasted_iota(jnp.int32, (N_SEL, Q), 1)) >> 6
    forced = (j_row == 0) | (j_row == cur) | (j_row == cur - 1)
    o_c = []
    for gi in groups:
        q4 = qt_ref[gi * HG * DH:(gi + 1) * HG * DH, :]
        q_t = jnp.concatenate([q4[h * DH:(h + 1) * DH, :] for h in range(HG)], axis=1)
        s_c = jnp.dot(kc_ref[0, gi].astype(BF16), q_t, preferred_element_type=F32)
        s_c = jnp.where(valid_c, s_c, NEG_INF)
        m_c = jnp.max(s_c, axis=0, keepdims=True)
        e_c = jnp.where(valid_c, jnp.exp2(s_c - m_c), 0.0)
        l_c = jnp.sum(e_c, axis=0, keepdims=True)
        p_c = e_c * jnp.where(l_c > 0.0, 1.0 / l_c, 0.0)
        o_c.append(jnp.dot(vc_ref[0, gi].T.astype(BF16), p_c.astype(BF16), preferred_element_type=F32))

        p_sum = p_c[:, 0:Q] + p_c[:, Q:2 * Q] + p_c[:, 2 * Q:3 * Q] + p_c[:, 3 * Q:4 * Q]
        imp_t = jnp.dot(ovl_ref[...], p_sum, precision=HIGHEST, preferred_element_type=F32)[0:N_SEL]
        imp_t = jnp.where(forced, FORCED_SCORE, imp_t)
        imp_t = jnp.where(j_row > cur, NEG_INF, imp_t)
        cnt = jnp.zeros((N_SEL, Q), F32)
        for i in range(N_SEL):
            ri = imp_t[i:i + 1, :]
            cnt = cnt + jnp.where(ri > imp_t, 1.0, jnp.where((ri == imp_t) & (j_row > i), 1.0, 0.0))
        sel_bias = jnp.where((cnt < SEL_TOPK) & (j_row <= cur), 0.0, NEG_INF).astype(BF16)

        qx_sc[gi] = jnp.concatenate([q_t, heads(sel_bias), jnp.zeros((KV_LANES - DH - N_SEL, HG * Q), BF16)],
                                    axis=0)

    def finish(acc):
        return acc[0:DH] / acc[DEN_ROW:DEN_ROW + 1]

    ms_sc[...] = jnp.full(ms_sc.shape, NEG_INF, F32)
    as_sc[...] = jnp.zeros(as_sc.shape, F32)
    n_kt = (q0 + Q + SLC_TILE - 1) // SLC_TILE

    def slc_tile(kt, bias):
        start = pl.multiple_of(kt * SLC_TILE, SLC_TILE)
        for gi in groups:
            s = jnp.dot(ks_ref[0, gi, pl.ds(start, SLC_TILE), :], qx_sc[gi], preferred_element_type=F32)
            if bias is not None:
                s = s + bias
            m_old = ms_sc[gi, 0:1, :]
            m_new = jnp.maximum(m_old, jnp.max(s, axis=0, keepdims=True))
            p = jnp.exp2(s - m_new).astype(BF16)
            pv = jnp.dot(vst_ref[0, gi, :, pl.ds(start, SLC_TILE)], p, preferred_element_type=F32)
            as_sc[gi] = as_sc[gi] * jnp.exp2(m_old - m_new) + pv
            ms_sc[gi] = jnp.broadcast_to(m_new, ms_sc.shape[1:])

    def slc_full(kt, carry):
        slc_tile(kt, None)
        return carry

    lax.fori_loop(0, n_kt - 1, slc_full, 0)

    w0 = pl.multiple_of(jnp.maximum(q0 - WINDOW, 0), Q)
    s_w = [jnp.dot(kw_ref[0, gi, pl.ds(w0, WIN_KEYS), :], qx_sc[gi], preferred_element_type=F32) for gi in groups]

    c_s = lax.broadcasted_iota(jnp.int32, (SLC_TILE, Q), 0)
    r_s = lax.broadcasted_iota(jnp.int32, (SLC_TILE, Q), 1)
    slc_tile(n_kt - 1, heads(jnp.where((n_kt - 1) * SLC_TILE + c_s <= q0 + r_s, 0.0, NEG_INF)))

    c_w = lax.broadcasted_iota(jnp.int32, (Q, Q), 0)
    r_w = lax.broadcasted_iota(jnp.int32, (Q, Q), 1)
    n_slab = WIN_KEYS // Q
    slab_bias = []
    for j in range(n_slab):
        key = w0 + j * Q + c_w
        tok = q0 + r_w
        slab_bias.append(heads(jnp.where((key <= tok) & (key > tok - WINDOW), 0.0, NEG_INF)))
    for gi in groups:
        slabs = [s_w[gi][j * Q:(j + 1) * Q] + slab_bias[j] for j in range(n_slab)]
        top = slabs[0]
        for sj in slabs[1:]:
            top = jnp.maximum(top, sj)
        m = jnp.max(top, axis=0, keepdims=True)
        p = jnp.concatenate([jnp.exp2(sj - m) for sj in slabs], axis=0).astype(BF16)
        aw_sc[gi] = jnp.dot(vwt_ref[0, gi, :, pl.ds(w0, WIN_KEYS)], p, preferred_element_type=F32)

    for gi in groups:
        g = pl.program_id(1) * ATTN_GROUPS + gi

        def gate_row(branch):
            return jnp.concatenate([gatet_ref[pl.ds((g * HG + h) * 3 + branch, 1), :] for h in range(HG)], axis=1)

        o = gate_row(0) * o_c[gi] + gate_row(1) * finish(as_sc[gi]) + gate_row(2) * finish(aw_sc[gi])
        out_ref[gi * HG * DH:(gi + 1) * HG * DH, :] = jnp.concatenate(
            [o[:, h * Q:(h + 1) * Q] for h in range(HG)], axis=0).astype(BF16)


def _overlap_t():
    n = np.arange(N_CMP_PAD)
    j = np.arange(128)
    cmp_start = n * CMP_STRIDE
    cmp_end = cmp_start + CMP_BLOCK - 1
    sel_start = j * SEL_BLOCK
    ovl = ((cmp_start[None, :] <= sel_start[:, None] + SEL_BLOCK - 1) & (cmp_end[None, :] >= sel_start[:, None])
           & (j[:, None] < N_SEL) & (n[None, :] < N_CMP_PAD - 1))
    return jnp.asarray(ovl.astype(np.float32))


def _nsa_attn(q_t, gates_t, kc, vc, ks, vs_t, kw, vw_t, batch):
    n = q_t.shape[1]
    gs = ATTN_GROUPS
    qcol = lambda b, g, i: (g, b * N_QB + i)
    gcol = lambda b, g, i: (0, b * N_QB + i)
    kv = lambda b, g, i: (b, g, 0, 0)
    rows = gs * NSA_GROUP_SIZE * NSA_HEAD_DIM
    lanes = NSA_GROUP_SIZE * Q_BLOCK
    return pl.pallas_call(
        _nsa_attn_kernel,
        grid=(batch, NSA_GROUPS // gs, N_QB),
        in_specs=[pl.BlockSpec((rows, Q_BLOCK), qcol),
                  pl.BlockSpec((128, Q_BLOCK), gcol),
                  pl.BlockSpec((1, gs, N_CMP_PAD, NSA_HEAD_DIM), kv),
                  pl.BlockSpec((1, gs, N_CMP_PAD, NSA_HEAD_DIM), kv),
                  pl.BlockSpec((1, gs, SEQ, KV_LANES), kv),
                  pl.BlockSpec((1, gs, V_ROWS, SEQ), kv),
                  pl.BlockSpec((1, gs, SEQ, KV_LANES), kv),
                  pl.BlockSpec((1, gs, V_ROWS, SEQ), kv),
                  pl.BlockSpec((128, N_CMP_PAD), lambda b, g, i: (0, 0))],
        out_specs=pl.BlockSpec((rows, Q_BLOCK), qcol),
        out_shape=jax.ShapeDtypeStruct((NSA_HEADS * NSA_HEAD_DIM, n), BF16),
        scratch_shapes=[pltpu.VMEM((gs, KV_LANES, lanes), BF16),
                        pltpu.VMEM((gs, 8, lanes), F32),
                        pltpu.VMEM((gs, V_ROWS, lanes), F32),
                        pltpu.VMEM((gs, V_ROWS, lanes), F32)],
        compiler_params=_cparams("parallel", "parallel", "arbitrary"),
        name="nsa_attn",
    )(q_t, gates_t, kc, vc, ks, vs_t, kw, vw_t, _overlap_t())


def kernel(x, mlstm_norm, mlstm_w_in, mlstm_gate_bias, mlstm_head_norm, mlstm_w_out, kv_norm, kv_w, cmp_pe_k, cmp_w1_k, cmp_w2_k, cmp_pe_v, cmp_w1_v, cmp_w2_v, nsa_norm, nsa_w_q, nsa_w_out, moe_norm, moe_w_group, moe_b_group, moe_w_router, moe_b_router, moe_w_gate, moe_w_up, moe_w_down, final_norm):
    batch, seq, d = x.shape
    assert seq == SEQ and d == D_MODEL
    assert mlstm_norm.shape[0] == 1 and nsa_norm.shape[0] == 1 and moe_norm.shape[0] == 2
    n = batch * seq
    x2d = x.reshape(n, d)
    tril = jnp.tril(jnp.ones((TM, TM), F32)).astype(BF16)

    w_in = mlstm_w_in[0]
    w_gate = jnp.concatenate([w_in[:, 3072:3080], jnp.zeros((d, 120), F32)], axis=1)
    b_gate = jnp.concatenate([mlstm_gate_bias[0], jnp.zeros((120,), F32)]).reshape(1, 128)
    q_t, k, v_t, o_t, gr = _mlstm_in(x2d, mlstm_norm[0].reshape(1, d), w_in[:, 0:3072].T.astype(BF16),
                                     w_in[:, 512:1024].astype(BF16), w_gate, b_gate)
    head_norm_cols = jnp.broadcast_to(mlstm_head_norm[0].reshape(d, 1), (d, 128))
    hs_t = _mlstm_scan(q_t, k, v_t, o_t, gr, head_norm_cols, batch)
    h = _moe_layer(hs_t, mlstm_w_out[0].astype(BF16), x2d, moe_norm[0], moe_w_group[0], moe_b_group[0],
                   moe_w_router[0], moe_b_router[0], moe_w_gate[0], moe_w_up[0], moe_w_down[0],
                   tril, final_norm, False, a_transposed=True)

    w_qt = jnp.concatenate([nsa_w_q[0].T, jnp.zeros((80, d), F32)], axis=0).astype(BF16)
    w_vt = jnp.concatenate([kv_w[:, 768:1024], kv_w[:, 1280:1536]], axis=1).T.astype(BF16)
    seq_tabs = _rope_tables(jnp.arange(SEQ), 128)
    seq_tabs_t = tuple(t[:, 0:NSA_HEAD_DIM].T for t in seq_tabs)
    q_t, gates_t, rk0, rk1, rv0, rv1, ks, vs_t, kw, vw_t = _nsa_proj(
        h, nsa_norm[0].reshape(1, d), kv_norm.reshape(1, d), w_qt, kv_w.astype(BF16), w_vt, seq_tabs, seq_tabs_t,
        batch)
    cmp_pos = jnp.arange(N_CMP_PAD) * CMP_STRIDE + CMP_BLOCK - 1
    kc, vc = _compress((rk0, rk1), (rv0, rv1), _compress_weights(cmp_pe_k, cmp_w1_k, cmp_w2_k),
                       _compress_weights(cmp_pe_v, cmp_w1_v, cmp_w2_v), _rope_tables(cmp_pos, 128), batch)
    att_t = _nsa_attn(q_t, gates_t, kc, vc, ks, vs_t, kw, vw_t, batch)
    out = _moe_layer(att_t, nsa_w_out[0].astype(BF16), h, moe_norm[1], moe_w_group[1], moe_b_group[1],
                     moe_w_router[1], moe_b_router[1], moe_w_gate[1], moe_w_up[1], moe_w_down[1],
                     tril, final_norm, True, a_transposed=True)
    return out.reshape(batch, seq, d)
```

```python
import functools

import numpy as np
import jax
import jax.numpy as jnp
from jax import lax
from jax.experimental import pallas as pl
from jax.experimental.pallas import tpu as pltpu

F32 = jnp.float32
BF16 = jnp.bfloat16
HIGHEST = lax.Precision.HIGHEST

D_MODEL = 1024
SEQ = 2048
RMS_EPS = 1e-6
NEG_INF = -1e30

MLSTM_HEADS = 4
MLSTM_V_DIM = 256
MLSTM_QK_DIM = 128
MLSTM_L = 256
GATE_SOFTCAP = 15.0

NSA_HEADS = 16
NSA_HEAD_DIM = 64
NSA_GROUPS = 4
NSA_GROUP_SIZE = 4
CMP_BLOCK = 32
CMP_STRIDE = 16
N_CMP_PAD = 128
SEL_BLOCK = 64
N_SEL = SEQ // SEL_BLOCK
SEL_TOPK = 16
WINDOW = 512
Q_BLOCK = 256
N_QB = SEQ // Q_BLOCK
WIN_KEYS = WINDOW + Q_BLOCK
SLC_TILE = 512
ATTN_GROUPS = 4
KV_LANES = 2 * NSA_HEAD_DIM
V_ROWS = NSA_HEAD_DIM + 16
DEN_ROW = NSA_HEAD_DIM
FORCED_SCORE = 1e6
ROPE_THETA = 500000.0
ROPE_DIM = 16

MOE_GROUPS = 4
MOE_PER_GROUP = 8
MOE_EXPERTS = 32
MOE_HIDDEN = 256
MOE_PAIRS = 28
MOE_CLASSES = MOE_GROUPS * MOE_PAIRS
ROW_TILE = 256
XE_W = D_MODEL + 128
LANE_WLO, LANE_WHI, LANE_ROUTE = 0, 1, 2
ROUTE_SHIFT = 16
ROW_CHUNKS = D_MODEL // 128
ISSUE_ROWS = 8

TM = 512
VMEM_LIMIT = 56 * 1024 * 1024

_NT = (((1,), (1,)), ((), ()))
_TN = (((0,), (0,)), ((), ()))


def _cparams(*sem):
    return pltpu.CompilerParams(dimension_semantics=sem, vmem_limit_bytes=VMEM_LIMIT)


def _rms(x):
    return x * lax.rsqrt(jnp.mean(x * x, axis=-1, keepdims=True) + RMS_EPS)


def _sigmoid(x):
    return 1.0 / (1.0 + jnp.exp(-x))


LOG2_E = 1.4426950408889634


def _split_bf16(x):
    hi = x.astype(BF16)
    return hi, (x - hi.astype(F32)).astype(BF16)


def _dot_split(x, w_hi, w_lo):
    x_hi, x_lo = _split_bf16(x)
    return (jnp.dot(x_hi, w_hi, preferred_element_type=F32) + jnp.dot(x_hi, w_lo, preferred_element_type=F32)
            + jnp.dot(x_lo, w_hi, preferred_element_type=F32))


def _mlstm_in_kernel(x_ref, g_ref, wt_ref, wk_ref, wgh_ref, wgl_ref, bg_ref, qt_ref, k_ref, vt_ref, ot_ref, gr_ref):
    hn = _rms(x_ref[...]) * g_ref[...]
    hb = hn.astype(BF16)
    qt_ref[...] = lax.dot_general(wt_ref[0:512, :], hb, _NT, preferred_element_type=F32).astype(BF16)
    k = jnp.dot(hb, wk_ref[...], preferred_element_type=F32)
    k_ref[...] = (k * (MLSTM_QK_DIM ** -0.5)).astype(BF16)
    vt_ref[...] = lax.dot_general(wt_ref[1024:2048, :], hb, _NT, preferred_element_type=F32).astype(BF16)
    ot_ref[...] = _sigmoid(lax.dot_general(wt_ref[2048:3072, :], hb, _NT, preferred_element_type=F32))
    gates = _dot_split(hn, wgh_ref[...], wgl_ref[...]) + bg_ref[...]
    gates = GATE_SOFTCAP * jnp.tanh(gates / GATE_SOFTCAP)
    lane = lax.broadcasted_iota(jnp.int32, gates.shape, 1)
    log_f = jnp.minimum(gates, 0.0) - jnp.log1p(jnp.exp(-jnp.abs(gates)))
    lg = jnp.where(lane < MLSTM_HEADS, gates, log_f)
    gr_ref[...] = lg.T[0:8, :]


def _mlstm_in(x2d, norm_g, w_t, w_k, w_gate, b_gate):
    n = x2d.shape[0]
    w_gate_hi, w_gate_lo = _split_bf16(w_gate)
    row = lambda i: (i, 0)
    col = lambda i: (0, i)
    fixed = lambda i: (0, 0)
    return pl.pallas_call(
        _mlstm_in_kernel,
        grid=(n // TM,),
        in_specs=[pl.BlockSpec((TM, D_MODEL), row),
                  pl.BlockSpec((1, D_MODEL), fixed),
                  pl.BlockSpec((3072, D_MODEL), fixed),
                  pl.BlockSpec((D_MODEL, 512), fixed),
                  pl.BlockSpec((D_MODEL, 128), fixed),
                  pl.BlockSpec((D_MODEL, 128), fixed),
                  pl.BlockSpec((1, 128), fixed)],
        out_specs=[pl.BlockSpec((512, TM), col),
                   pl.BlockSpec((TM, 512), row),
                   pl.BlockSpec((1024, TM), col),
                   pl.BlockSpec((1024, TM), col),
                   pl.BlockSpec((8, TM), col)],
        out_shape=[jax.ShapeDtypeStruct((512, n), BF16),
                   jax.ShapeDtypeStruct((n, 512), BF16),
                   jax.ShapeDtypeStruct((1024, n), BF16),
                   jax.ShapeDtypeStruct((1024, n), F32),
                   jax.ShapeDtypeStruct((8, n), F32)],
        compiler_params=_cparams("parallel"),
        name="mlstm_in",
    )(x2d, norm_g, w_t, w_k, w_gate_hi, w_gate_lo, b_gate)


def _mlstm_scan_kernel(qt_ref, k_ref, vt_ref, ot_ref, gr_ref, hn_ref, out_ref, ct_ref, n_ref, m_ref):
    L = MLSTM_L

    @pl.when(pl.program_id(1) == 0)
    def _():
        ct_ref[...] = jnp.zeros_like(ct_ref)
        n_ref[...] = jnp.zeros_like(n_ref)
        m_ref[...] = jnp.zeros_like(m_ref)

    row = lax.broadcasted_iota(jnp.int32, (L, L), 0)
    col = lax.broadcasted_iota(jnp.int32, (L, L), 1)
    causal_t = row <= col
    tril = (col <= row).astype(F32)
    eye = (col == row).astype(F32)
    gr = gr_ref[...]
    gr_pad = jnp.concatenate([gr, jnp.zeros((120, L), F32)], axis=0)
    b_row = lax.dot_general(gr, tril, _NT, precision=HIGHEST, preferred_element_type=F32)
    b_col = lax.dot_general(tril, gr_pad, _NT, precision=HIGHEST, preferred_element_type=F32)
    g_col = lax.dot_general(eye, gr_pad, _NT, precision=HIGHEST, preferred_element_type=F32)
    n_hi, n_lo = _split_bf16(n_ref[...])

    for h in range(MLSTM_HEADS):
        src_col = g_col[:, h:h + 1] - b_col[:, 4 + h:5 + h]
        bf_row = b_row[4 + h:5 + h, :]
        m = m_ref[h:h + 1, 0:1]
        dmat = jnp.where(causal_t, bf_row + src_col, NEG_INF)
        m_inter = bf_row + m
        m_t = jnp.maximum(m_inter, jnp.max(dmat, axis=0, keepdims=True))
        w_intra = jnp.exp(dmat - m_t)
        w_inter = jnp.exp(m_inter - m_t)
        qt = qt_ref[h * 128:(h + 1) * 128, :]
        kh = k_ref[:, h * 128:(h + 1) * 128]
        vt = vt_ref[h * 256:(h + 1) * 256, :]
        s = jnp.dot(kh, qt, preferred_element_type=F32) * w_intra
        c_old = ct_ref[h]
        num = (jnp.dot(vt, s.astype(BF16), preferred_element_type=F32)
               + w_inter * jnp.dot(c_old.astype(BF16), qt, preferred_element_type=F32))
        qn = (jnp.dot(n_hi, qt, preferred_element_type=F32) + jnp.dot(n_lo, qt, preferred_element_type=F32))[h:h + 1]
        den = jnp.sum(s, axis=0, keepdims=True) + w_inter * qn
        hh = num / jnp.maximum(jnp.abs(den), jnp.exp(-m_t))
        b_end = bf_row[:, L - 1:L]
        g = b_end + src_col
        m_new = jnp.maximum(b_end + m, jnp.max(g, axis=0, keepdims=True))
        ws = jnp.exp(g - m_new)
        decay = jnp.exp(b_end + m - m_new)
        kf = kh.astype(F32) * ws
        ct_ref[h] = decay * c_old + jnp.dot(vt, kf.astype(BF16), preferred_element_type=F32)
        n_ref[h:h + 1, :] = decay * n_ref[h:h + 1, :] + jnp.sum(kf, axis=0, keepdims=True)
        m_ref[h:h + 1, :] = jnp.broadcast_to(m_new, (1, 128))
        rows = slice(h * 256, (h + 1) * 256)
        hnorm = hh * lax.rsqrt(jnp.mean(hh * hh, axis=0, keepdims=True) + RMS_EPS)
        gain = jnp.concatenate([hn_ref[rows, :]] * (L // 128), axis=1)
        out_ref[rows, :] = (hnorm * gain * ot_ref[rows, :]).astype(BF16)


def _mlstm_scan(q_t, k, v_t, o_t, gr, head_norm_cols, batch):
    n = k.shape[0]
    nblk = SEQ // MLSTM_L
    row = lambda b, j: (b * nblk + j, 0)
    col = lambda b, j: (0, b * nblk + j)
    return pl.pallas_call(
        _mlstm_scan_kernel,
        grid=(batch, nblk),
        in_specs=[pl.BlockSpec((512, MLSTM_L), col),
                  pl.BlockSpec((MLSTM_L, 512), row),
                  pl.BlockSpec((1024, MLSTM_L), col),
                  pl.BlockSpec((1024, MLSTM_L), col),
                  pl.BlockSpec((8, MLSTM_L), col),
                  pl.BlockSpec((1024, 128), lambda b, j: (0, 0))],
        out_specs=pl.BlockSpec((1024, MLSTM_L), col),
        out_shape=jax.ShapeDtypeStruct((1024, n), BF16),
        scratch_shapes=[pltpu.VMEM((MLSTM_HEADS, MLSTM_V_DIM, MLSTM_QK_DIM), F32),
                        pltpu.VMEM((8, 128), F32),
                        pltpu.VMEM((8, 128), F32)],
        compiler_params=_cparams("parallel", "arbitrary"),
        name="mlstm_scan",
    )(q_t, k, v_t, o_t, gr, head_norm_cols)


def _pair_tables():
    lo, hi = [], []
    for g in range(MOE_GROUPS):
        for a in range(MOE_PER_GROUP):
            for b in range(a + 1, MOE_PER_GROUP):
                lo.append(g * MOE_PER_GROUP + a)
                hi.append(g * MOE_PER_GROUP + b)
    return np.asarray(lo, np.int32), np.asarray(hi, np.int32)


_PAIR_LO, _PAIR_HI = _pair_tables()


def _mix_out_kernel(a_ref, w_ref, res_ref, g_ref, wrh_ref, wrl_ref, br_ref, tril_ref, h_ref, xe_ref, route_ref, cnt_ref,
                    run_ref, *, a_transposed):
    @pl.when(pl.program_id(0) == 0)
    def _():
        run_ref[...] = jnp.zeros_like(run_ref)

    dims = _TN if a_transposed else (((1,), (0,)), ((), ()))
    h = res_ref[...] + lax.dot_general(a_ref[...], w_ref[...], dims, preferred_element_type=F32)
    h_ref[...] = h
    hn = _rms(h) * g_ref[...]
    xe_ref[:, 0:D_MODEL] = hn

    logits = _dot_split(hn, wrh_ref[...], wrl_ref[...]) + br_ref[...]
    lane_i = lax.broadcasted_iota(jnp.int32, logits.shape, 1)
    lane = lane_i.astype(F32)
    ninf = -jnp.inf
    is_g = (lane_i >= MOE_EXPERTS) & (lane_i < MOE_EXPERTS + MOE_GROUPS)
    glog = jnp.where(is_g, logits, ninf)
    gmax = jnp.max(glog, axis=-1, keepdims=True)
    gidx = jnp.min(jnp.where(glog == gmax, lane - MOE_EXPERTS, 99.0), axis=-1, keepdims=True)
    pg_top = 1.0 / jnp.sum(jnp.exp(glog - gmax), axis=-1, keepdims=True)
    lane_grp = (lane_i >> 3).astype(F32)
    in_grp = (lane_i < MOE_EXPERTS) & (lane_grp == gidx)
    ev = jnp.where(in_grp, logits, ninf)
    v1 = jnp.max(ev, axis=-1, keepdims=True)
    i1 = jnp.min(jnp.where(ev == v1, lane, 999.0), axis=-1, keepdims=True)
    ev2 = jnp.where(lane == i1, ninf, ev)
    v2 = jnp.max(ev2, axis=-1, keepdims=True)
    i2 = jnp.min(jnp.where(ev2 == v2, lane, 999.0), axis=-1, keepdims=True)
    e2 = jnp.exp(v2 - v1)
    w1 = pg_top / (1.0 + e2)
    w2 = pg_top * e2 / (1.0 + e2)
    first_lo = i1 < i2
    w_lo = jnp.where(first_lo, w1, w2)
    w_hi = jnp.where(first_lo, w2, w1)
    a = jnp.minimum(i1, i2) - MOE_PER_GROUP * gidx
    b = jnp.maximum(i1, i2) - MOE_PER_GROUP * gidx
    cls = gidx * MOE_PAIRS + a * (15.0 - a) * 0.5 + (b - a - 1.0)

    onehot = lane == cls
    prefix = jnp.dot(tril_ref[...], onehot.astype(BF16), preferred_element_type=F32)
    run = run_ref[0:1, :]
    rank = jnp.sum(jnp.where(onehot, prefix - 1.0 + run, 0.0), axis=-1, keepdims=True)
    run_new = run + prefix[TM - 1:TM, :]
    run_ref[...] = jnp.broadcast_to(run_new, run_ref.shape)
    cnt_ref[...] = jnp.broadcast_to(run_new, cnt_ref.shape)

    route = cls * float(2 ** ROUTE_SHIFT) + rank
    meta = jnp.where(lane_i == LANE_WLO, w_lo,
                     jnp.where(lane_i == LANE_WHI, w_hi,
                               jnp.where(lane_i == LANE_ROUTE, route, 0.0)))
    xe_ref[:, D_MODEL:XE_W] = meta
    route_ref[...] = meta.T[LANE_ROUTE:LANE_ROUTE + 1, :].astype(jnp.int32)


def _mix_out(a, w, res, g_moe, w_rt, b_rt, tril, a_transposed):
    n = res.shape[0]
    kdim = w.shape[0]
    row = lambda i: (i, 0)
    fixed = lambda i: (0, 0)
    a_spec = pl.BlockSpec((kdim, TM), lambda i: (0, i)) if a_transposed else pl.BlockSpec((TM, kdim), row)
    w_rt_hi, w_rt_lo = _split_bf16(w_rt)
    return pl.pallas_call(
        functools.partial(_mix_out_kernel, a_transposed=a_transposed),
        grid=(n // TM,),
        in_specs=[a_spec,
                  pl.BlockSpec((kdim, D_MODEL), fixed),
                  pl.BlockSpec((TM, D_MODEL), row),
                  pl.BlockSpec((1, D_MODEL), fixed),
                  pl.BlockSpec((D_MODEL, 128), fixed),
                  pl.BlockSpec((D_MODEL, 128), fixed),
                  pl.BlockSpec((1, 128), fixed),
                  pl.BlockSpec((TM, TM), fixed)],
        out_specs=[pl.BlockSpec((TM, D_MODEL), row),
                   pl.BlockSpec((TM, XE_W), row),
                   pl.BlockSpec((1, TM), lambda i: (0, i)),
                   pl.BlockSpec((8, 128), fixed)],
        out_shape=[jax.ShapeDtypeStruct((n, D_MODEL), F32),
                   jax.ShapeDtypeStruct((n, XE_W), F32),
                   jax.ShapeDtypeStruct((1, n), jnp.int32),
                   jax.ShapeDtypeStruct((8, 128), F32)],
        scratch_shapes=[pltpu.VMEM((8, 128), F32)],
        compiler_params=_cparams("arbitrary"),
        name="mix_out",
    )(a, w, res, g_moe, w_rt_hi, w_rt_lo, b_rt, tril)


def _sorted_row(route_ref, offs_ref, idx):
    r = route_ref[idx]
    return offs_ref[r >> ROUTE_SHIFT] + (r & (2 ** ROUTE_SHIFT - 1))


def _dispatch_kernel(route_ref, offs_ref, cnt_ref, nused_ref, xe_ref, xs_ref, zbuf, sem, zsem):
    i = pl.program_id(0)
    base = i * TM

    @pl.when(i == 0)
    def _():
        zbuf[...] = jnp.zeros_like(zbuf)

        def per_class(c, carry):
            cnt = cnt_ref[c]
            start = offs_ref[c] + cnt
            pad = (-cnt) & (ROW_TILE - 1)
            head = (-cnt) & 7
            blocks = (pad - head) >> 3

            def fill_row(r, inner):
                pltpu.make_async_copy(zbuf.at[pl.ds(0, 1)], xs_ref.at[pl.ds(start + r, 1)], zsem).start()
                return inner

            def fill_block(b, inner):
                row0 = pl.multiple_of(start + head + b * 8, 8)
                pltpu.make_async_copy(zbuf.at[pl.ds(0, 8)], xs_ref.at[pl.ds(row0, 8)], zsem).start()
                return inner

            def drain_row(r, inner):
                pltpu.make_async_copy(zbuf.at[pl.ds(0, 1)], xs_ref.at[pl.ds(0, 1)], zsem).wait()
                return inner

            def drain_block(b, inner):
                pltpu.make_async_copy(zbuf.at[pl.ds(0, 8)], xs_ref.at[pl.ds(0, 8)], zsem).wait()
                return inner

            lax.fori_loop(0, head, fill_row, 0)
            lax.fori_loop(0, blocks, fill_block, 0)
            lax.fori_loop(0, head, drain_row, 0)
            lax.fori_loop(0, blocks, drain_block, 0)
            return carry

        lax.fori_loop(0, MOE_CLASSES, per_class, 0)

        def tail(t, carry):
            row0 = pl.multiple_of(t * ROW_TILE, ROW_TILE)
            cp = pltpu.make_async_copy(zbuf, xs_ref.at[pl.ds(row0, ROW_TILE)], zsem)
            cp.start()
            cp.wait()
            return carry

        lax.fori_loop(nused_ref[0], xs_ref.shape[0] // ROW_TILE, tail, 0)

    def issue(t8, carry):
        t0 = pl.multiple_of(t8 * ISSUE_ROWS, ISSUE_ROWS)
        for r in range(ISSUE_ROWS):
            p = _sorted_row(route_ref, offs_ref, base + t0 + r)
            pltpu.make_async_copy(xe_ref.at[pl.ds(t0 + r, 1)], xs_ref.at[pl.ds(p, 1)], sem).start()
        return carry

    lax.fori_loop(0, TM // ISSUE_ROWS, issue, 0)
    pltpu.make_async_copy(xe_ref, xs_ref.at[pl.ds(0, TM)], sem).wait()


def _dispatch(route, offs, cnt, n_used, xe, n_rows):
    n = xe.shape[0]
    grid_spec = pltpu.PrefetchScalarGridSpec(
        num_scalar_prefetch=4,
        grid=(n // TM,),
        in_specs=[pl.BlockSpec((TM, XE_W), lambda i, *_: (i, 0))],
        out_specs=pl.BlockSpec(memory_space=pl.ANY),
        scratch_shapes=[pltpu.VMEM((ROW_TILE, XE_W), F32), pltpu.SemaphoreType.DMA(()),
                        pltpu.SemaphoreType.DMA(())],
    )
    return pl.pallas_call(
        _dispatch_kernel,
        grid_spec=grid_spec,
        out_shape=jax.ShapeDtypeStruct((n_rows, XE_W), F32),
        compiler_params=_cparams("arbitrary"),
        name="moe_dispatch",
    )(route, offs, cnt, n_used, xe)


def _experts_kernel(tlo_ref, thi_ref, nused_ref, xs_ref, wg_lo, wu_lo, wd_lo, wg_hi, wu_hi, wd_hi, y_ref):
    del tlo_ref, thi_ref

    @pl.when(pl.program_id(0) < nused_ref[0])
    def _():
        x = xs_ref[:, 0:D_MODEL].astype(BF16)

        def ffn(wg, wu, wd, w):
            a = jnp.dot(x, wg[0], preferred_element_type=F32)
            u = jnp.dot(x, wu[0], preferred_element_type=F32)
            hid = (a * _sigmoid(a)) * u * w
            return jnp.dot(hid.astype(BF16), wd[0], preferred_element_type=F32)

        w_lo = xs_ref[:, D_MODEL + LANE_WLO:D_MODEL + LANE_WLO + 1]
        w_hi = xs_ref[:, D_MODEL + LANE_WHI:D_MODEL + LANE_WHI + 1]
        y = ffn(wg_lo, wu_lo, wd_lo, w_lo) + ffn(wg_hi, wu_hi, wd_hi, w_hi)
        for j in range(ROW_CHUNKS):
            y_ref[pl.ds(j, ROW_TILE, stride=ROW_CHUNKS), :] = y[:, j * 128:(j + 1) * 128]

    @pl.when(pl.program_id(0) >= nused_ref[0])
    def _():
        y_ref[...] = jnp.zeros_like(y_ref)


def _experts(tile_lo, tile_hi, n_used, xs, w_gate, w_up, w_down):
    n_tiles = xs.shape[0] // ROW_TILE
    rows = lambda i, tlo, thi, nu: (jnp.maximum(jnp.minimum(i, nu[0] - 1), 0), 0)
    lo = lambda i, tlo, thi, nu: (tlo[i], 0, 0)
    hi = lambda i, tlo, thi, nu: (thi[i], 0, 0)
    up_spec = lambda m: pl.BlockSpec((1, D_MODEL, MOE_HIDDEN), m)
    dn_spec = lambda m: pl.BlockSpec((1, MOE_HIDDEN, D_MODEL), m)
    grid_spec = pltpu.PrefetchScalarGridSpec(
        num_scalar_prefetch=3,
        grid=(n_tiles,),
        in_specs=[pl.BlockSpec((ROW_TILE, XE_W), rows),
                  up_spec(lo), up_spec(lo), dn_spec(lo),
                  up_spec(hi), up_spec(hi), dn_spec(hi)],
        out_specs=pl.BlockSpec((ROW_TILE * ROW_CHUNKS, 128), lambda i, tlo, thi, nu: (i, 0)),
    )
    return pl.pallas_call(
        _experts_kernel,
        grid_spec=grid_spec,
        out_shape=jax.ShapeDtypeStruct((xs.shape[0] * ROW_CHUNKS, 128), F32),
        compiler_params=_cparams("arbitrary"),
        name="moe_experts",
    )(tile_lo, tile_hi, n_used, xs, w_gate, w_up, w_down, w_gate, w_up, w_down)


def _combine_kernel(route_ref, offs_ref, h_ref, y_ref, g_ref, out_ref, buf, sem, *, final_norm):
    base = pl.program_id(0) * TM

    def issue(t8, carry):
        t0 = pl.multiple_of(t8 * ISSUE_ROWS, ISSUE_ROWS)
        for r in range(ISSUE_ROWS):
            p = pl.multiple_of(_sorted_row(route_ref, offs_ref, base + t0 + r) * ROW_CHUNKS, ROW_CHUNKS)
            dst = pl.multiple_of((t0 + r) * ROW_CHUNKS, ROW_CHUNKS)
            pltpu.make_async_copy(y_ref.at[pl.ds(p, ROW_CHUNKS)], buf.at[pl.ds(dst, ROW_CHUNKS)], sem).start()
        return carry

    lax.fori_loop(0, TM // ISSUE_ROWS, issue, 0)
    pltpu.make_async_copy(y_ref.at[pl.ds(0, TM * ROW_CHUNKS)], buf, sem).wait()
    moe = jnp.concatenate([buf[pl.ds(j, TM, stride=ROW_CHUNKS), :] for j in range(ROW_CHUNKS)], axis=1)
    out = h_ref[...] + moe
    if final_norm:
        out = _rms(out) * g_ref[...]
    out_ref[...] = out


def _combine(route, offs, h, y, gain, final_norm):
    n = h.shape[0]
    grid_spec = pltpu.PrefetchScalarGridSpec(
        num_scalar_prefetch=2,
        grid=(n // TM,),
        in_specs=[pl.BlockSpec((TM, D_MODEL), lambda i, *_: (i, 0)),
                  pl.BlockSpec(memory_space=pl.ANY),
                  pl.BlockSpec((1, D_MODEL), lambda i, *_: (0, 0))],
        out_specs=pl.BlockSpec((TM, D_MODEL), lambda i, *_: (i, 0)),
        scratch_shapes=[pltpu.VMEM((TM * ROW_CHUNKS, 128), F32), pltpu.SemaphoreType.DMA(())],
    )
    return pl.pallas_call(
        functools.partial(_combine_kernel, final_norm=final_norm),
        grid_spec=grid_spec,
        out_shape=jax.ShapeDtypeStruct((n, D_MODEL), F32),
        compiler_params=_cparams("arbitrary"),
        name="moe_combine",
    )(route, offs, h, y, gain)


def _moe_layer(a, w_out, res, moe_norm, w_group, b_group, w_router, b_router, w_gate, w_up, w_down,
               tril, out_gain, final_norm, a_transposed):
    n = res.shape[0]
    unused = 128 - MOE_EXPERTS - MOE_GROUPS
    w_rt = jnp.concatenate([w_router, w_group, jnp.zeros((D_MODEL, unused), F32)], axis=1)
    b_rt = jnp.concatenate([b_router, b_group, jnp.zeros((unused,), F32)]).reshape(1, 128)
    h, xe, route_row, counts = _mix_out(a, w_out, res, moe_norm.reshape(1, D_MODEL), w_rt, b_rt, tril, a_transposed)

    n_tiles = n // ROW_TILE + MOE_CLASSES
    cnt = counts[0].astype(jnp.int32)
    tiles_c = (cnt + ROW_TILE - 1) // ROW_TILE
    tile_end = jnp.cumsum(tiles_c)
    offs = (tile_end - tiles_c) * ROW_TILE
    n_used = tile_end[-1]
    tile_ids = jnp.minimum(jnp.arange(n_tiles, dtype=jnp.int32), n_used - 1)
    tile_cls = jnp.sum((tile_end[None, 0:MOE_CLASSES] <= tile_ids[:, None]).astype(jnp.int32), axis=1)
    tile_cls = jnp.clip(tile_cls, 0, MOE_CLASSES - 1)
    tile_lo = jnp.asarray(_PAIR_LO)[tile_cls]
    tile_hi = jnp.asarray(_PAIR_HI)[tile_cls]
    route = route_row.reshape(n)

    n_used = n_used.reshape(1)
    xs = _dispatch(route, offs, cnt, n_used, xe, n_tiles * ROW_TILE)
    y = _experts(tile_lo, tile_hi, n_used, xs, w_gate.astype(BF16), w_up.astype(BF16),
                 w_down.astype(BF16))
    if out_gain is None:
        return h, y, route, offs
    return _combine(route, offs, h, y, out_gain.reshape(1, D_MODEL), final_norm)


def _rope_tables(pos, width):
    half = ROPE_DIM // 2
    inv_freq = jnp.power(jnp.float32(ROPE_THETA), -jnp.arange(half, dtype=F32) * (2.0 / ROPE_DIM))
    ang = pos.astype(F32)[:, None] * inv_freq[None, :]
    cos, sin = jnp.cos(ang), jnp.sin(ang)
    t = pos.shape[0]
    rest = NSA_HEAD_DIM - ROPE_DIM
    cos_t = jnp.concatenate([cos, cos, jnp.ones((t, rest), F32)], axis=1)
    sin_a = jnp.concatenate([-sin, jnp.zeros((t, half + rest), F32)], axis=1)
    sin_b = jnp.concatenate([jnp.zeros((t, half), F32), sin, jnp.zeros((t, rest), F32)], axis=1)
    rep = width // NSA_HEAD_DIM
    return jnp.tile(cos_t, (1, rep)), jnp.tile(sin_a, (1, rep)), jnp.tile(sin_b, (1, rep))


def _rope(x, cos_t, sin_a, sin_b):
    half = ROPE_DIM // 2
    parts = []
    for c in range(x.shape[1] // 128):
        xc = x[:, c * 128:(c + 1) * 128]
        parts.append(xc * cos_t + pltpu.roll(xc, 128 - half, axis=1) * sin_a + pltpu.roll(xc, half, axis=1) * sin_b)
    return parts[0] if len(parts) == 1 else jnp.concatenate(parts, axis=1)


def _rope_rows(x, cos_t, sin_a, sin_b):
    half = ROPE_DIM // 2
    reps = x.shape[0] // NSA_HEAD_DIM
    tile = lambda t: jnp.concatenate([t] * reps, axis=0)
    up = jnp.concatenate([x[half:], x[:half]], axis=0)
    down = jnp.concatenate([x[-half:], x[:-half]], axis=0)
    return x * tile(cos_t) + up * tile(sin_a) + down * tile(sin_b)


def _nsa_proj_kernel(route_ref, offs_ref, h_ref, y_ref, gq_ref, gkv_ref, wqt_ref, wkv_ref, wvt_ref,
                     cos_ref, sa_ref, sb_ref, cost_ref, sat_ref, sbt_ref,
                     h2_ref, qt_ref, gatet_ref, kc0_ref, kc1_ref, vc0_ref, vc1_ref, ks_ref, vst_ref, kw_ref, vwt_ref,
                     gbuf, sems):
    i = pl.program_id(0)

    def gather(tile, slot):
        base = tile * TM

        def issue(t8, carry):
            t0 = pl.multiple_of(t8 * ISSUE_ROWS, ISSUE_ROWS)
            for r in range(ISSUE_ROWS):
                p = pl.multiple_of(_sorted_row(route_ref, offs_ref, base + t0 + r) * ROW_CHUNKS, ROW_CHUNKS)
                dst = pl.multiple_of((t0 + r) * ROW_CHUNKS, ROW_CHUNKS)
                pltpu.make_async_copy(y_ref.at[pl.ds(p, ROW_CHUNKS)], gbuf.at[slot, pl.ds(dst, ROW_CHUNKS)],
                                      sems.at[slot]).start()
            return carry

        lax.fori_loop(0, TM // ISSUE_ROWS, issue, 0)

    @pl.when(i == 0)
    def _():
        gather(0, 0)

    @pl.when(i + 1 < pl.num_programs(0))
    def _():
        gather(i + 1, (i + 1) % 2)

    slot = i % 2
    pltpu.make_async_copy(y_ref.at[pl.ds(0, TM * ROW_CHUNKS)], gbuf.at[slot], sems.at[slot]).wait()
    moe = jnp.concatenate([gbuf[slot, pl.ds(j, TM, stride=ROW_CHUNKS), :] for j in range(ROW_CHUNKS)], axis=1)
    h2 = h_ref[...] + moe
    h2_ref[...] = h2
    r = _rms(h2)
    hq = (r * gq_ref[...]).astype(BF16)
    hk = (r * gkv_ref[...]).astype(BF16)
    qt = lax.dot_general(wqt_ref[0:1024, :], hq, _NT, preferred_element_type=F32)
    q_scale = NSA_HEAD_DIM ** -0.5 * LOG2_E
    qt_ref[...] = (_rope_rows(qt, cost_ref[...], sat_ref[...], sbt_ref[...]) * q_scale).astype(BF16)
    gatet_ref[...] = _sigmoid(lax.dot_general(wqt_ref[1024:1152, :], hq, _NT, preferred_element_type=F32))
    kc0_ref[...] = jnp.dot(hk, wkv_ref[:, 0:128], preferred_element_type=F32)
    kc1_ref[...] = jnp.dot(hk, wkv_ref[:, 128:256], preferred_element_type=F32)
    vc0_ref[...] = jnp.dot(hk, wkv_ref[:, 256:384], preferred_element_type=F32)
    vc1_ref[...] = jnp.dot(hk, wkv_ref[:, 384:512], preferred_element_type=F32)

    tm = TM
    cos_t, sin_a, sin_b = cos_ref[...], sa_ref[...], sb_ref[...]
    lane = lax.broadcasted_iota(jnp.int32, (tm, NSA_HEAD_DIM), 1)
    pos = (pl.program_id(0) % (SEQ // tm)) * tm + lax.broadcasted_iota(jnp.int32, (tm, NSA_HEAD_DIM), 0)
    blk_onehot = jnp.where(lane == (pos >> 6), 1.0, 0.0).astype(BF16)
    zeros = jnp.zeros((tm, NSA_HEAD_DIM), BF16)

    def store_keys(ref, val, aux):
        for g in range(NSA_GROUPS):
            ref[0, g] = jnp.concatenate([val[:, g * 64:(g + 1) * 64].astype(BF16), aux], axis=1)

    store_keys(ks_ref, _rope(jnp.dot(hk, wkv_ref[:, 512:768], preferred_element_type=F32), cos_t, sin_a, sin_b),
               blk_onehot)
    store_keys(kw_ref, _rope(jnp.dot(hk, wkv_ref[:, 1024:1280], preferred_element_type=F32), cos_t, sin_a, sin_b),
               zeros)

    row = lax.broadcasted_iota(jnp.int32, (V_ROWS - NSA_HEAD_DIM, tm), 0)
    ones_row = jnp.where(row == 0, 1.0, 0.0).astype(BF16)

    def store_values(ref, val_t):
        for g in range(NSA_GROUPS):
            ref[0, g] = jnp.concatenate([val_t[g * 64:(g + 1) * 64, :].astype(BF16), ones_row], axis=0)

    store_values(vst_ref, lax.dot_general(wvt_ref[0:256, :], hk, _NT, preferred_element_type=F32))
    store_values(vwt_ref, lax.dot_general(wvt_ref[256:512, :], hk, _NT, preferred_element_type=F32))


def _nsa_proj(route, offs, h, y, g_q, g_kv, w_qt, w_kv, w_vt, rope_tabs, rope_tabs_t, batch):
    n = h.shape[0]
    nblk = SEQ // TM
    row = lambda i, *_: (i, 0)
    col = lambda i, *_: (0, i)
    fixed = lambda i, *_: (0, 0)
    tab = lambda i, *_: (i % nblk, 0)
    tab_t = lambda i, *_: (0, i % nblk)
    key_spec = pl.BlockSpec((1, NSA_GROUPS, TM, KV_LANES), lambda i, *_: (i // nblk, 0, i % nblk, 0))
    key_shape = jax.ShapeDtypeStruct((batch, NSA_GROUPS, SEQ, KV_LANES), BF16)
    val_spec = pl.BlockSpec((1, NSA_GROUPS, V_ROWS, TM), lambda i, *_: (i // nblk, 0, 0, i % nblk))
    val_shape = jax.ShapeDtypeStruct((batch, NSA_GROUPS, V_ROWS, SEQ), BF16)
    raw_spec = pl.BlockSpec((TM, 128), row)
    raw_shape = jax.ShapeDtypeStruct((n, 128), F32)
    grid_spec = pltpu.PrefetchScalarGridSpec(
        num_scalar_prefetch=2,
        grid=(n // TM,),
        in_specs=[pl.BlockSpec((TM, D_MODEL), row),
                  pl.BlockSpec(memory_space=pl.ANY),
                  pl.BlockSpec((1, D_MODEL), fixed),
                  pl.BlockSpec((1, D_MODEL), fixed),
                  pl.BlockSpec((1152, D_MODEL), fixed),
                  pl.BlockSpec((D_MODEL, 1536), fixed),
                  pl.BlockSpec((512, D_MODEL), fixed),
                  pl.BlockSpec((TM, 128), tab), pl.BlockSpec((TM, 128), tab), pl.BlockSpec((TM, 128), tab),
                  pl.BlockSpec((NSA_HEAD_DIM, TM), tab_t), pl.BlockSpec((NSA_HEAD_DIM, TM), tab_t),
                  pl.BlockSpec((NSA_HEAD_DIM, TM), tab_t)],
        out_specs=[pl.BlockSpec((TM, D_MODEL), row),
                   pl.BlockSpec((1024, TM), col),
                   pl.BlockSpec((128, TM), col),
                   raw_spec, raw_spec, raw_spec, raw_spec,
                   key_spec, val_spec, key_spec, val_spec],
        scratch_shapes=[pltpu.VMEM((2, TM * ROW_CHUNKS, 128), F32), pltpu.SemaphoreType.DMA((2,))],
    )
    return pl.pallas_call(
        _nsa_proj_kernel,
        grid_spec=grid_spec,
        out_shape=[jax.ShapeDtypeStruct((n, D_MODEL), F32),
                   jax.ShapeDtypeStruct((1024, n), BF16),
                   jax.ShapeDtypeStruct((128, n), F32),
                   raw_shape, raw_shape, raw_shape, raw_shape,
                   key_shape, val_shape, key_shape, val_shape],
        compiler_params=_cparams("arbitrary"),
        name="nsa_proj",
    )(route, offs, h, y, g_q, g_kv, w_qt, w_kv, w_vt, *rope_tabs, *rope_tabs_t)


HALF_BLOCKS = SEQ // CMP_STRIDE
CMP_K = CMP_STRIDE * 256


def _compress_kernel(rk0_ref, rk1_ref, rv0_ref, rv1_ref, w1k_ref, w1v_ref, pek_ref, pev_ref, w2k_ref, w2v_ref,
                     cos_ref, sa_ref, sb_ref, kc_ref, vc_ref):
    lane = lax.broadcasted_iota(jnp.int32, (HALF_BLOCKS, 512), 1)
    first_half = (lane & 127) < 64

    def comp(raw_refs, w1_ref, pe_ref, w2_ref):
        x = jnp.concatenate([r[pl.ds(l, HALF_BLOCKS, stride=CMP_STRIDE), :]
                             for l in range(CMP_STRIDE) for r in raw_refs], axis=1).astype(BF16)
        r = jnp.dot(x, w1_ref[...], preferred_element_type=F32)
        rpe = jnp.dot(pe_ref[...].astype(BF16), w1_ref[...], preferred_element_type=F32)
        r = r + jnp.where(first_half, rpe[0:1, :], rpe[1:2, :])
        nxt = pltpu.roll(r, HALF_BLOCKS - 1, axis=0)
        nxt = jnp.concatenate([pltpu.roll(nxt[:, c * 128:(c + 1) * 128], 64, axis=1) for c in range(4)], axis=1)
        pre = r + nxt
        act = pre * _sigmoid(pre)
        return jnp.dot(act.astype(BF16), w2_ref[...], preferred_element_type=F32)

    kc = comp((rk0_ref, rk1_ref), w1k_ref, pek_ref, w2k_ref)
    kc = _rope(kc, cos_ref[...], sa_ref[...], sb_ref[...])
    vc = comp((rv0_ref, rv1_ref), w1v_ref, pev_ref, w2v_ref)
    for g in range(NSA_GROUPS):
        kc_ref[0, g] = kc[:, g * 64:(g + 1) * 64]
        vc_ref[0, g] = vc[:, g * 64:(g + 1) * 64]


def _compress_weights(pe, w1, w2):
    dh = NSA_HEAD_DIM
    w1r = w1.reshape(2, CMP_STRIDE, dh, dh)
    halves = jnp.concatenate([w1r[0], w1r[1]], axis=-1)
    rows = jnp.broadcast_to(halves[:, None], (CMP_STRIDE, NSA_GROUPS, dh, 2 * dh)).reshape(CMP_K, 2 * dh)
    row_group = (jnp.arange(CMP_K) // dh) % NSA_GROUPS
    w1_big = jnp.concatenate([jnp.where((row_group == g)[:, None], rows, 0.0) for g in range(NSA_GROUPS)],
                             axis=1).astype(BF16)
    pe_rows = jnp.broadcast_to(pe.reshape(2, CMP_STRIDE, 1, dh), (2, CMP_STRIDE, NSA_GROUPS, dh)).reshape(2, CMP_K)
    pe_rows = jnp.concatenate([pe_rows, jnp.zeros((6, CMP_K), F32)], axis=0)
    rows2 = jnp.tile(jnp.concatenate([w2, jnp.zeros_like(w2)], axis=0), (NSA_GROUPS, 1))
    row_group2 = jnp.arange(NSA_GROUPS * 2 * dh) // (2 * dh)
    w2_bd = jnp.concatenate([jnp.where((row_group2 == g)[:, None], rows2, 0.0) for g in range(NSA_GROUPS)],
                            axis=1).astype(BF16)
    return w1_big, pe_rows, w2_bd


def _compress(raw_k, raw_v, wk, wv, cmp_tabs, batch):
    fixed = lambda b: (0, 0)
    raw_spec = pl.BlockSpec((SEQ, 128), lambda b: (b, 0))
    out_spec = pl.BlockSpec((1, NSA_GROUPS, N_CMP_PAD, NSA_HEAD_DIM), lambda b: (b, 0, 0, 0))
    out_shape = jax.ShapeDtypeStruct((batch, NSA_GROUPS, N_CMP_PAD, NSA_HEAD_DIM), F32)
    return pl.pallas_call(
        _compress_kernel,
        grid=(batch,),
        in_specs=[raw_spec, raw_spec, raw_spec, raw_spec,
                  pl.BlockSpec((CMP_K, 512), fixed), pl.BlockSpec((CMP_K, 512), fixed),
                  pl.BlockSpec((8, CMP_K), fixed), pl.BlockSpec((8, CMP_K), fixed),
                  pl.BlockSpec((512, 256), fixed), pl.BlockSpec((512, 256), fixed),
                  pl.BlockSpec((N_CMP_PAD, 128), fixed), pl.BlockSpec((N_CMP_PAD, 128), fixed),
                  pl.BlockSpec((N_CMP_PAD, 128), fixed)],
        out_specs=[out_spec, out_spec],
        out_shape=[out_shape, out_shape],
        compiler_params=_cparams("parallel"),
        name="nsa_compress",
    )(*raw_k, *raw_v, wk[0], wv[0], wk[1], wv[1], wk[2], wv[2], *cmp_tabs)


def _nsa_attn_kernel(qt_ref, gatet_ref, kc_ref, vc_ref, ks_ref, vst_ref, kw_ref, vwt_ref, ovl_ref, out_ref,
                     qx_sc, ms_sc, as_sc, aw_sc):
    qb = pl.program_id(2)
    q0 = qb * Q_BLOCK
    Q, HG, DH = Q_BLOCK, NSA_GROUP_SIZE, NSA_HEAD_DIM
    groups = range(ATTN_GROUPS)

    def heads(x):
        return jnp.concatenate([x] * HG, axis=1)

    n_row = lax.broadcasted_iota(jnp.int32, (N_CMP_PAD, Q), 0)
    t_lane = q0 + lax.broadcasted_iota(jnp.int32, (N_CMP_PAD, Q), 1)
    valid_c = heads((n_row * CMP_STRIDE + CMP_BLOCK - 1 <= t_lane) & (n_row < N_CMP_PAD - 1))
    j_row = lax.broadcasted_iota(jnp.int32, (N_SEL, Q), 0)
    cur = (q0 + lax.broadcasted_iota(jnp.int32, (N_SEL, Q), 1)) >> 6
    forced = (j_row == 0) | (j_row == cur) | (j_row == cur - 1)
    o_c = []
    for gi in groups:
        q4 = qt_ref[gi * HG * DH:(gi + 1) * HG * DH, :]
        q_t = jnp.concatenate([q4[h * DH:(h + 1) * DH, :] for h in range(HG)], axis=1)
        s_c = jnp.dot(kc_ref[0, gi].astype(BF16), q_t, preferred_element_type=F32)
        s_c = jnp.where(valid_c, s_c, NEG_INF)
        m_c = jnp.max(s_c, axis=0, keepdims=True)
        e_c = jnp.where(valid_c, jnp.exp2(s_c - m_c), 0.0)
        l_c = jnp.sum(e_c, axis=0, keepdims=True)
        p_c = e_c * jnp.where(l_c > 0.0, 1.0 / l_c, 0.0)
        o_c.append(jnp.dot(vc_ref[0, gi].T.astype(BF16), p_c.astype(BF16), preferred_element_type=F32))

        p_sum = p_c[:, 0:Q] + p_c[:, Q:2 * Q] + p_c[:, 2 * Q:3 * Q] + p_c[:, 3 * Q:4 * Q]
        imp_t = jnp.dot(ovl_ref[...], p_sum, precision=HIGHEST, preferred_element_type=F32)[0:N_SEL]
        imp_t = jnp.where(forced, FORCED_SCORE, imp_t)
        imp_t = jnp.where(j_row > cur, NEG_INF, imp_t)
        cnt = jnp.zeros((N_SEL, Q), F32)
        for i in range(N_SEL):
            ri = imp_t[i:i + 1, :]
            cnt = cnt + jnp.where(ri > imp_t, 1.0, jnp.where((ri == imp_t) & (j_row > i), 1.0, 0.0))
        sel_bias = jnp.where((cnt < SEL_TOPK) & (j_row <= cur), 0.0, NEG_INF).astype(BF16)

        qx_sc[gi] = jnp.concatenate([q_t, heads(sel_bias), jnp.zeros((KV_LANES - DH - N_SEL, HG * Q), BF16)],
                                    axis=0)

    def finish(acc):
        return acc[0:DH] / acc[DEN_ROW:DEN_ROW + 1]

    ms_sc[...] = jnp.full(ms_sc.shape, NEG_INF, F32)
    as_sc[...] = jnp.zeros(as_sc.shape, F32)
    n_kt = (q0 + Q + SLC_TILE - 1) // SLC_TILE

    def slc_tile(kt, bias):
        start = pl.multiple_of(kt * SLC_TILE, SLC_TILE)
        for gi in groups:
            s = jnp.dot(ks_ref[0, gi, pl.ds(start, SLC_TILE), :], qx_sc[gi], preferred_element_type=F32)
            if bias is not None:
                s = s + bias
            m_old = ms_sc[gi, 0:1, :]
            m_new = jnp.maximum(m_old, jnp.max(s, axis=0, keepdims=True))
            p = jnp.exp2(s - m_new).astype(BF16)
            pv = jnp.dot(vst_ref[0, gi, :, pl.ds(start, SLC_TILE)], p, preferred_element_type=F32)
            as_sc[gi] = as_sc[gi] * jnp.exp2(m_old - m_new) + pv
            ms_sc[gi] = jnp.broadcast_to(m_new, ms_sc.shape[1:])

    def slc_full(kt, carry):
        slc_tile(kt, None)
        return carry

    lax.fori_loop(0, n_kt - 1, slc_full, 0)

    w0 = pl.multiple_of(jnp.maximum(q0 - WINDOW, 0), Q)
    s_w = [jnp.dot(kw_ref[0, gi, pl.ds(w0, WIN_KEYS), :], qx_sc[gi], preferred_element_type=F32) for gi in groups]

    c_s = lax.broadcasted_iota(jnp.int32, (SLC_TILE, Q), 0)
    r_s = lax.broadcasted_iota(jnp.int32, (SLC_TILE, Q), 1)
    slc_tile(n_kt - 1, heads(jnp.where((n_kt - 1) * SLC_TILE + c_s <= q0 + r_s, 0.0, NEG_INF)))

    c_w = lax.broadcasted_iota(jnp.int32, (Q, Q), 0)
    r_w = lax.broadcasted_iota(jnp.int32, (Q, Q), 1)
    n_slab = WIN_KEYS // Q
    slab_bias = []
    for j in range(n_slab):
        key = w0 + j * Q + c_w
        tok = q0 + r_w
        slab_bias.append(heads(jnp.where((key <= tok) & (key > tok - WINDOW), 0.0, NEG_INF)))
    for gi in groups:
        slabs = [s_w[gi][j * Q:(j + 1) * Q] + slab_bias[j] for j in range(n_slab)]
        top = slabs[0]
        for sj in slabs[1:]:
            top = jnp.maximum(top, sj)
        m = jnp.max(top, axis=0, keepdims=True)
        p = jnp.concatenate([jnp.exp2(sj - m) for sj in slabs], axis=0).astype(BF16)
        aw_sc[gi] = jnp.dot(vwt_ref[0, gi, :, pl.ds(w0, WIN_KEYS)], p, preferred_element_type=F32)

    for gi in groups:
        g = pl.program_id(1) * ATTN_GROUPS + gi

        def gate_row(branch):
            return jnp.concatenate([gatet_ref[pl.ds((g * HG + h) * 3 + branch, 1), :] for h in range(HG)], axis=1)

        o = gate_row(0) * o_c[gi] + gate_row(1) * finish(as_sc[gi]) + gate_row(2) * finish(aw_sc[gi])
        out_ref[gi * HG * DH:(gi + 1) * HG * DH, :] = jnp.concatenate(
            [o[:, h * Q:(h + 1) * Q] for h in range(HG)], axis=0).astype(BF16)


def _overlap_t():
    n = np.arange(N_CMP_PAD)
    j = np.arange(128)
    cmp_start = n * CMP_STRIDE
    cmp_end = cmp_start + CMP_BLOCK - 1
    sel_start = j * SEL_BLOCK
    ovl = ((cmp_start[None, :] <= sel_start[:, None] + SEL_BLOCK - 1) & (cmp_end[None, :] >= sel_start[:, None])
           & (j[:, None] < N_SEL) & (n[None, :] < N_CMP_PAD - 1))
    return jnp.asarray(ovl.astype(np.float32))


def _nsa_attn(q_t, gates_t, kc, vc, ks, vs_t, kw, vw_t, batch):
    n = q_t.shape[1]
    gs = ATTN_GROUPS
    qcol = lambda b, g, i: (g, b * N_QB + i)
    gcol = lambda b, g, i: (0, b * N_QB + i)
    kv = lambda b, g, i: (b, g, 0, 0)
    rows = gs * NSA_GROUP_SIZE * NSA_HEAD_DIM
    lanes = NSA_GROUP_SIZE * Q_BLOCK
    return pl.pallas_call(
        _nsa_attn_kernel,
        grid=(batch, NSA_GROUPS // gs, N_QB),
        in_specs=[pl.BlockSpec((rows, Q_BLOCK), qcol),
                  pl.BlockSpec((128, Q_BLOCK), gcol),
                  pl.BlockSpec((1, gs, N_CMP_PAD, NSA_HEAD_DIM), kv),
                  pl.BlockSpec((1, gs, N_CMP_PAD, NSA_HEAD_DIM), kv),
                  pl.BlockSpec((1, gs, SEQ, KV_LANES), kv),
                  pl.BlockSpec((1, gs, V_ROWS, SEQ), kv),
                  pl.BlockSpec((1, gs, SEQ, KV_LANES), kv),
                  pl.BlockSpec((1, gs, V_ROWS, SEQ), kv),
                  pl.BlockSpec((128, N_CMP_PAD), lambda b, g, i: (0, 0))],
        out_specs=pl.BlockSpec((rows, Q_BLOCK), qcol),
        out_shape=jax.ShapeDtypeStruct((NSA_HEADS * NSA_HEAD_DIM, n), BF16),
        scratch_shapes=[pltpu.VMEM((gs, KV_LANES, lanes), BF16),
                        pltpu.VMEM((gs, 8, lanes), F32),
                        pltpu.VMEM((gs, V_ROWS, lanes), F32),
                        pltpu.VMEM((gs, V_ROWS, lanes), F32)],
        compiler_params=_cparams("parallel", "parallel", "arbitrary"),
        name="nsa_attn",
    )(q_t, gates_t, kc, vc, ks, vs_t, kw, vw_t, _overlap_t())


def kernel(x, mlstm_norm, mlstm_w_in, mlstm_gate_bias, mlstm_head_norm, mlstm_w_out, kv_norm, kv_w, cmp_pe_k, cmp_w1_k, cmp_w2_k, cmp_pe_v, cmp_w1_v, cmp_w2_v, nsa_norm, nsa_w_q, nsa_w_out, moe_norm, moe_w_group, moe_b_group, moe_w_router, moe_b_router, moe_w_gate, moe_w_up, moe_w_down, final_norm):
    batch, seq, d = x.shape
    assert seq == SEQ and d == D_MODEL
    assert mlstm_norm.shape[0] == 1 and nsa_norm.shape[0] == 1 and moe_norm.shape[0] == 2
    n = batch * seq
    x2d = x.reshape(n, d)
    tril = jnp.tril(jnp.ones((TM, TM), F32)).astype(BF16)

    w_in = mlstm_w_in[0]
    w_gate = jnp.concatenate([w_in[:, 3072:3080], jnp.zeros((d, 120), F32)], axis=1)
    b_gate = jnp.concatenate([mlstm_gate_bias[0], jnp.zeros((120,), F32)]).reshape(1, 128)
    q_t, k, v_t, o_t, gr = _mlstm_in(x2d, mlstm_norm[0].reshape(1, d), w_in[:, 0:3072].T.astype(BF16),
                                     w_in[:, 512:1024].astype(BF16), w_gate, b_gate)
    head_norm_cols = jnp.broadcast_to(mlstm_head_norm[0].reshape(d, 1), (d, 128))
    hs_t = _mlstm_scan(q_t, k, v_t, o_t, gr, head_norm_cols, batch)
    h_pre, y, route, offs = _moe_layer(hs_t, mlstm_w_out[0].astype(BF16), x2d, moe_norm[0], moe_w_group[0],
                                       moe_b_group[0], moe_w_router[0], moe_b_router[0], moe_w_gate[0],
                                       moe_w_up[0], moe_w_down[0], tril, None, False, a_transposed=True)

    w_qt = jnp.concatenate([nsa_w_q[0].T, jnp.zeros((80, d), F32)], axis=0).astype(BF16)
    w_vt = jnp.concatenate([kv_w[:, 768:1024], kv_w[:, 1280:1536]], axis=1).T.astype(BF16)
    seq_tabs = _rope_tables(jnp.arange(SEQ), 128)
    seq_tabs_t = tuple(t[:, 0:NSA_HEAD_DIM].T for t in seq_tabs)
    h, q_t, gates_t, rk0, rk1, rv0, rv1, ks, vs_t, kw, vw_t = _nsa_proj(
        route, offs, h_pre, y, nsa_norm[0].reshape(1, d), kv_norm.reshape(1, d), w_qt, kv_w.astype(BF16), w_vt,
        seq_tabs, seq_tabs_t, batch)
    cmp_pos = jnp.arange(N_CMP_PAD) * CMP_STRIDE + CMP_BLOCK - 1
    kc, vc = _compress((rk0, rk1), (rv0, rv1), _compress_weights(cmp_pe_k, cmp_w1_k, cmp_w2_k),
                       _compress_weights(cmp_pe_v, cmp_w1_v, cmp_w2_v), _rope_tables(cmp_pos, 128), batch)
    att_t = _nsa_attn(q_t, gates_t, kc, vc, ks, vs_t, kw, vw_t, batch)
    out = _moe_layer(att_t, nsa_w_out[0].astype(BF16), h, moe_norm[1], moe_w_group[1], moe_b_group[1],
                     moe_w_router[1], moe_b_router[1], moe_w_gate[1], moe_w_up[1], moe_w_down[1],
                     tril, final_norm, True, a_transposed=True)
    return out.reshape(batch, seq, d)
```

```python
import functools

import numpy as np
import jax
import jax.numpy as jnp
from jax import lax
from jax.experimental import pallas as pl
from jax.experimental.pallas import tpu as pltpu

F32 = jnp.float32
BF16 = jnp.bfloat16
HIGHEST = lax.Precision.HIGHEST

D_MODEL = 1024
SEQ = 2048
RMS_EPS = 1e-6
NEG_INF = -1e30

MLSTM_HEADS = 4
MLSTM_V_DIM = 256
MLSTM_QK_DIM = 128
MLSTM_L = 256
GATE_SOFTCAP = 15.0

NSA_HEADS = 16
NSA_HEAD_DIM = 64
NSA_GROUPS = 4
NSA_GROUP_SIZE = 4
CMP_BLOCK = 32
CMP_STRIDE = 16
N_CMP_PAD = 128
SEL_BLOCK = 64
N_SEL = SEQ // SEL_BLOCK
SEL_TOPK = 16
WINDOW = 512
Q_BLOCK = 256
N_QB = SEQ // Q_BLOCK
WIN_KEYS = WINDOW + Q_BLOCK
SLC_TILE = 512
ATTN_GROUPS = 4
KV_LANES = 2 * NSA_HEAD_DIM
V_ROWS = NSA_HEAD_DIM + 16
DEN_ROW = NSA_HEAD_DIM
FORCED_SCORE = 1e6
ROPE_THETA = 500000.0
ROPE_DIM = 16

MOE_GROUPS = 4
MOE_PER_GROUP = 8
MOE_EXPERTS = 32
MOE_HIDDEN = 256
MOE_PAIRS = 28
MOE_CLASSES = MOE_GROUPS * MOE_PAIRS
ROW_TILE = 256
XE_W = D_MODEL + 128
LANE_WLO, LANE_WHI, LANE_ROUTE = 0, 1, 2
ROUTE_SHIFT = 16
ROW_CHUNKS = D_MODEL // 128
ISSUE_ROWS = 8

TM = 512
VMEM_LIMIT = 56 * 1024 * 1024

_NT = (((1,), (1,)), ((), ()))
_TN = (((0,), (0,)), ((), ()))


def _cparams(*sem):
    return pltpu.CompilerParams(dimension_semantics=sem, vmem_limit_bytes=VMEM_LIMIT)


def _rms(x):
    return x * lax.rsqrt(jnp.mean(x * x, axis=-1, keepdims=True) + RMS_EPS)


def _sigmoid(x):
    return 1.0 / (1.0 + jnp.exp(-x))


LOG2_E = 1.4426950408889634


def _split_bf16(x):
    hi = x.astype(BF16)
    return hi, (x - hi.astype(F32)).astype(BF16)


def _dot_split(x, w_hi, w_lo):
    x_hi, x_lo = _split_bf16(x)
    return (jnp.dot(x_hi, w_hi, preferred_element_type=F32) + jnp.dot(x_hi, w_lo, preferred_element_type=F32)
            + jnp.dot(x_lo, w_hi, preferred_element_type=F32))


def _mlstm_kernel(x_ref, g_ref, wt_ref, wk_ref, wgh_ref, wgl_ref, bg_ref, hn_ref, out_ref, ct_ref, n_ref, m_ref):
    L = MLSTM_L

    @pl.when(pl.program_id(1) == 0)
    def _():
        ct_ref[...] = jnp.zeros_like(ct_ref)
        n_ref[...] = jnp.zeros_like(n_ref)
        m_ref[...] = jnp.zeros_like(m_ref)

    hn = _rms(x_ref[...]) * g_ref[...]
    hb = hn.astype(BF16)
    q_t = lax.dot_general(wt_ref[0:512, :], hb, _NT, preferred_element_type=F32).astype(BF16)
    k_all = (jnp.dot(hb, wk_ref[...], preferred_element_type=F32) * (MLSTM_QK_DIM ** -0.5)).astype(BF16)
    v_t = lax.dot_general(wt_ref[1024:2048, :], hb, _NT, preferred_element_type=F32).astype(BF16)
    o_t = _sigmoid(lax.dot_general(wt_ref[2048:3072, :], hb, _NT, preferred_element_type=F32))
    gates = _dot_split(hn, wgh_ref[...], wgl_ref[...]) + bg_ref[...]
    gates = GATE_SOFTCAP * jnp.tanh(gates / GATE_SOFTCAP)
    lane = lax.broadcasted_iota(jnp.int32, gates.shape, 1)
    log_f = jnp.minimum(gates, 0.0) - jnp.log1p(jnp.exp(-jnp.abs(gates)))
    lg = jnp.where(lane < MLSTM_HEADS, gates, log_f)
    gr_pad = lg.T
    gr = gr_pad[0:8, :]

    row = lax.broadcasted_iota(jnp.int32, (L, L), 0)
    col = lax.broadcasted_iota(jnp.int32, (L, L), 1)
    causal_t = row <= col
    tril = (col <= row).astype(F32)
    b_row =lax.dot_general(gr, tril, _NT, precision=HIGHEST, preferred_element_type=F32)
    b_col = lax.dot_general(tril, gr_pad, _NT, precision=HIGHEST, preferred_element_type=F32)
    n_hi, n_lo = _split_bf16(n_ref[...])

    for h in range(MLSTM_HEADS):
        src_col = lg[:, h:h + 1] - b_col[:, 4 + h:5 + h]
        bf_row = b_row[4 + h:5 + h, :]
        m = m_ref[h:h + 1, 0:1]
        dmat = jnp.where(causal_t, bf_row + src_col, NEG_INF)
        m_inter = bf_row + m
        m_t = jnp.maximum(m_inter, jnp.max(dmat, axis=0, keepdims=True))
        w_intra = jnp.exp(dmat - m_t)
        w_inter = jnp.exp(m_inter - m_t)
        qt = q_t[h * 128:(h + 1) * 128, :]
        kh = k_all[:, h * 128:(h + 1) * 128]
        vt = v_t[h * 256:(h + 1) * 256, :]
        s = jnp.dot(kh, qt, preferred_element_type=F32) * w_intra
        c_old = ct_ref[h]
        num = (jnp.dot(vt, s.astype(BF16), preferred_element_type=F32)
               + w_inter * jnp.dot(c_old.astype(BF16), qt, preferred_element_type=F32))
        qn = (jnp.dot(n_hi, qt, preferred_element_type=F32) + jnp.dot(n_lo, qt, preferred_element_type=F32))[h:h + 1]
        den = jnp.sum(s, axis=0, keepdims=True) + w_inter * qn
        hh = num / jnp.maximum(jnp.abs(den), jnp.exp(-m_t))
        b_end = bf_row[:, L - 1:L]
        g = b_end + src_col
        m_new = jnp.maximum(b_end + m, jnp.max(g, axis=0, keepdims=True))
        ws = jnp.exp(g - m_new)
        decay = jnp.exp(b_end + m - m_new)
        kf = kh.astype(F32) * ws
        ct_ref[h] = decay * c_old + jnp.dot(vt, kf.astype(BF16), preferred_element_type=F32)
        n_ref[h:h + 1, :] = decay * n_ref[h:h + 1, :] + jnp.sum(kf, axis=0, keepdims=True)
        m_ref[h:h + 1, :] = jnp.broadcast_to(m_new, (1, 128))
        rows = slice(h * 256, (h + 1) * 256)
        hnorm = hh * lax.rsqrt(jnp.mean(hh * hh, axis=0, keepdims=True) + RMS_EPS)
        gain = jnp.concatenate([hn_ref[rows, :]] * (L // 128), axis=1)
        out_ref[rows, :] = (hnorm * gain * o_t[rows, :]).astype(BF16)


def _mlstm(x2d, norm_g, w_t, w_k, w_gate, b_gate, head_norm_cols, batch):
    n = x2d.shape[0]
    nblk = SEQ // MLSTM_L
    w_gate_hi, w_gate_lo = _split_bf16(w_gate)
    fixed = lambda b, j: (0, 0)
    return pl.pallas_call(
        _mlstm_kernel,
        grid=(batch, nblk),
        in_specs=[pl.BlockSpec((MLSTM_L, D_MODEL), lambda b, j: (b * nblk + j, 0)),
                  pl.BlockSpec((1, D_MODEL), fixed),
                  pl.BlockSpec((3072, D_MODEL), fixed),
                  pl.BlockSpec((D_MODEL, 512), fixed),
                  pl.BlockSpec((D_MODEL, 128), fixed),
                  pl.BlockSpec((D_MODEL, 128), fixed),
                  pl.BlockSpec((1, 128), fixed),
                  pl.BlockSpec((1024, 128), fixed)],
        out_specs=pl.BlockSpec((1024, MLSTM_L), lambda b, j: (0, b * nblk + j)),
        out_shape=jax.ShapeDtypeStruct((1024, n), BF16),
        scratch_shapes=[pltpu.VMEM((MLSTM_HEADS, MLSTM_V_DIM, MLSTM_QK_DIM), F32),
                        pltpu.VMEM((8, 128), F32),
                        pltpu.VMEM((8, 128), F32)],
        compiler_params=_cparams("parallel", "arbitrary"),
        name="mlstm",
    )(x2d, norm_g, w_t, w_k, w_gate_hi, w_gate_lo, b_gate, head_norm_cols)


def _pair_tables():
    lo, hi = [], []
    for g in range(MOE_GROUPS):
        for a in range(MOE_PER_GROUP):
            for b in range(a + 1, MOE_PER_GROUP):
                lo.append(g * MOE_PER_GROUP + a)
                hi.append(g * MOE_PER_GROUP + b)
    return np.asarray(lo, np.int32), np.asarray(hi, np.int32)


_PAIR_LO, _PAIR_HI = _pair_tables()


def _mix_out_kernel(a_ref, w_ref, res_ref, g_ref, wrh_ref, wrl_ref, br_ref, tril_ref, h_ref, xe_ref, route_ref, cnt_ref,
                    run_ref, *, a_transposed):
    @pl.when(pl.program_id(0) == 0)
    def _():
        run_ref[...] = jnp.zeros_like(run_ref)

    dims = _TN if a_transposed else (((1,), (0,)), ((), ()))
    h = res_ref[...] + lax.dot_general(a_ref[...], w_ref[...], dims, preferred_element_type=F32)
    h_ref[...] = h
    hn = _rms(h) * g_ref[...]
    xe_ref[:, 0:D_MODEL] = hn

    logits = _dot_split(hn, wrh_ref[...], wrl_ref[...]) + br_ref[...]
    lane_i = lax.broadcasted_iota(jnp.int32, logits.shape, 1)
    lane = lane_i.astype(F32)
    ninf = -jnp.inf
    is_g = (lane_i >= MOE_EXPERTS) & (lane_i < MOE_EXPERTS + MOE_GROUPS)
    glog = jnp.where(is_g, logits, ninf)
    gmax = jnp.max(glog, axis=-1, keepdims=True)
    gidx = jnp.min(jnp.where(glog == gmax, lane - MOE_EXPERTS, 99.0), axis=-1, keepdims=True)
    pg_top = 1.0 / jnp.sum(jnp.exp(glog - gmax), axis=-1, keepdims=True)
    lane_grp = (lane_i >> 3).astype(F32)
    in_grp = (lane_i < MOE_EXPERTS) & (lane_grp == gidx)
    ev = jnp.where(in_grp, logits, ninf)
    v1 = jnp.max(ev, axis=-1, keepdims=True)
    i1 = jnp.min(jnp.where(ev == v1, lane, 999.0), axis=-1, keepdims=True)
    ev2 = jnp.where(lane == i1, ninf, ev)
    v2 = jnp.max(ev2, axis=-1, keepdims=True)
    i2 = jnp.min(jnp.where(ev2 == v2, lane, 999.0), axis=-1, keepdims=True)
    e2 = jnp.exp(v2 - v1)
    w1 = pg_top / (1.0 + e2)
    w2 = pg_top * e2 / (1.0 + e2)
    first_lo = i1 < i2
    w_lo = jnp.where(first_lo, w1, w2)
    w_hi = jnp.where(first_lo, w2, w1)
    a = jnp.minimum(i1, i2) - MOE_PER_GROUP * gidx
    b = jnp.maximum(i1, i2) - MOE_PER_GROUP * gidx
    cls = gidx * MOE_PAIRS + a * (15.0 - a) * 0.5 + (b - a - 1.0)

    onehot = lane == cls
    prefix = jnp.dot(tril_ref[...], onehot.astype(BF16), preferred_element_type=F32)
    run = run_ref[0:1, :]
    rank = jnp.sum(jnp.where(onehot, prefix - 1.0 + run, 0.0), axis=-1, keepdims=True)
    run_new = run + prefix[TM - 1:TM, :]
    run_ref[...] = jnp.broadcast_to(run_new, run_ref.shape)
    cnt_ref[...] = jnp.broadcast_to(run_new, cnt_ref.shape)

    route = cls * float(2 ** ROUTE_SHIFT) + rank
    meta = jnp.where(lane_i == LANE_WLO, w_lo,
                     jnp.where(lane_i == LANE_WHI, w_hi,
                               jnp.where(lane_i == LANE_ROUTE, route, 0.0)))
    xe_ref[:, D_MODEL:XE_W] = meta
    route_ref[...] = meta.T[LANE_ROUTE:LANE_ROUTE + 1, :].astype(jnp.int32)


def _mix_out(a, w, res, g_moe, w_rt, b_rt, tril, a_transposed):
    n = res.shape[0]
    kdim = w.shape[0]
    row = lambda i: (i, 0)
    fixed = lambda i: (0, 0)
    a_spec = pl.BlockSpec((kdim, TM), lambda i: (0, i)) if a_transposed else pl.BlockSpec((TM, kdim), row)
    w_rt_hi, w_rt_lo = _split_bf16(w_rt)
    return pl.pallas_call(
        functools.partial(_mix_out_kernel, a_transposed=a_transposed),
        grid=(n // TM,),
        in_specs=[a_spec,
                  pl.BlockSpec((kdim, D_MODEL), fixed),
                  pl.BlockSpec((TM, D_MODEL), row),
                  pl.BlockSpec((1, D_MODEL), fixed),
                  pl.BlockSpec((D_MODEL, 128), fixed),
                  pl.BlockSpec((D_MODEL, 128), fixed),
                  pl.BlockSpec((1, 128), fixed),
                  pl.BlockSpec((TM, TM), fixed)],
        out_specs=[pl.BlockSpec((TM, D_MODEL), row),
                   pl.BlockSpec((TM, XE_W), row),
                   pl.BlockSpec((1, TM), lambda i: (0, i)),
                   pl.BlockSpec((8, 128), fixed)],
        out_shape=[jax.ShapeDtypeStruct((n, D_MODEL), F32),
                   jax.ShapeDtypeStruct((n, XE_W), F32),
                   jax.ShapeDtypeStruct((1, n), jnp.int32),
                   jax.ShapeDtypeStruct((8, 128), F32)],
        scratch_shapes=[pltpu.VMEM((8, 128), F32)],
        compiler_params=_cparams("arbitrary"),
        name="mix_out",
    )(a, w, res, g_moe, w_rt_hi, w_rt_lo, b_rt, tril)


def _sorted_row(route_ref, offs_ref, idx):
    r = route_ref[idx]
    return offs_ref[r >> ROUTE_SHIFT] + (r & (2 ** ROUTE_SHIFT - 1))


def _dispatch_kernel(route_ref, offs_ref, cnt_ref, nused_ref, xe_ref, xs_ref, zbuf, sem, zsem):
    i = pl.program_id(0)
    base = i * TM

    @pl.when(i == 0)
    def _():
        zbuf[...] = jnp.zeros_like(zbuf)

        def per_class(c, carry):
            cnt = cnt_ref[c]
            start = offs_ref[c] + cnt
            pad = (-cnt) & (ROW_TILE - 1)
            head = (-cnt) & 7
            blocks = (pad - head) >> 3

            def fill_row(r, inner):
                pltpu.make_async_copy(zbuf.at[pl.ds(0, 1)], xs_ref.at[pl.ds(start + r, 1)], zsem).start()
                return inner

            def fill_block(b, inner):
                row0 = pl.multiple_of(start + head + b * 8, 8)
                pltpu.make_async_copy(zbuf.at[pl.ds(0, 8)], xs_ref.at[pl.ds(row0, 8)], zsem).start()
                return inner

            def drain_row(r, inner):
                pltpu.make_async_copy(zbuf.at[pl.ds(0, 1)], xs_ref.at[pl.ds(0, 1)], zsem).wait()
                return inner

            def drain_block(b, inner):
                pltpu.make_async_copy(zbuf.at[pl.ds(0, 8)], xs_ref.at[pl.ds(0, 8)], zsem).wait()
                return inner

            lax.fori_loop(0, head, fill_row, 0)
            lax.fori_loop(0, blocks, fill_block, 0)
            lax.fori_loop(0, head, drain_row, 0)
            lax.fori_loop(0, blocks, drain_block, 0)
            return carry

        lax.fori_loop(0, MOE_CLASSES, per_class, 0)

        def tail(t, carry):
            row0 = pl.multiple_of(t * ROW_TILE, ROW_TILE)
            cp = pltpu.make_async_copy(zbuf, xs_ref.at[pl.ds(row0, ROW_TILE)], zsem)
            cp.start()
            cp.wait()
            return carry

        lax.fori_loop(nused_ref[0], xs_ref.shape[0] // ROW_TILE, tail, 0)

    def issue(t8, carry):
        t0 = pl.multiple_of(t8 * ISSUE_ROWS, ISSUE_ROWS)
        for r in range(ISSUE_ROWS):
            p = _sorted_row(route_ref, offs_ref, base + t0 + r)
            pltpu.make_async_copy(xe_ref.at[pl.ds(t0 + r, 1)], xs_ref.at[pl.ds(p, 1)], sem).start()
        return carry

    lax.fori_loop(0, TM // ISSUE_ROWS, issue, 0)
    pltpu.make_async_copy(xe_ref, xs_ref.at[pl.ds(0, TM)], sem).wait()


def _dispatch(route, offs, cnt, n_used, xe, n_rows):
    n = xe.shape[0]
    grid_spec = pltpu.PrefetchScalarGridSpec(
        num_scalar_prefetch=4,
        grid=(n // TM,),
        in_specs=[pl.BlockSpec((TM, XE_W), lambda i, *_: (i, 0))],
        out_specs=pl.BlockSpec(memory_space=pl.ANY),
        scratch_shapes=[pltpu.VMEM((ROW_TILE, XE_W), F32), pltpu.SemaphoreType.DMA(()),
                        pltpu.SemaphoreType.DMA(())],
    )
    return pl.pallas_call(
        _dispatch_kernel,
        grid_spec=grid_spec,
        out_shape=jax.ShapeDtypeStruct((n_rows, XE_W), F32),
        compiler_params=_cparams("arbitrary"),
        name="moe_dispatch",
    )(route, offs, cnt, n_used, xe)


def _experts_kernel(tlo_ref, thi_ref, nused_ref, xs_ref, wg_lo, wu_lo, wd_lo, wg_hi, wu_hi, wd_hi, y_ref):
    del tlo_ref, thi_ref

    @pl.when(pl.program_id(0) < nused_ref[0])
    def _():
        x = xs_ref[:, 0:D_MODEL].astype(BF16)

        def ffn(wg, wu, wd, w):
            a = jnp.dot(x, wg[0], preferred_element_type=F32)
            u = jnp.dot(x, wu[0], preferred_element_type=F32)
            hid = (a * _sigmoid(a)) * u * w
            return jnp.dot(hid.astype(BF16), wd[0], preferred_element_type=F32)

        w_lo = xs_ref[:, D_MODEL + LANE_WLO:D_MODEL + LANE_WLO + 1]
        w_hi = xs_ref[:, D_MODEL + LANE_WHI:D_MODEL + LANE_WHI + 1]
        y = ffn(wg_lo, wu_lo, wd_lo, w_lo) + ffn(wg_hi, wu_hi, wd_hi, w_hi)
        for j in range(ROW_CHUNKS):
            y_ref[pl.ds(j, ROW_TILE, stride=ROW_CHUNKS), :] = y[:, j * 128:(j + 1) * 128]

    @pl.when(pl.program_id(0) >= nused_ref[0])
    def _():
        y_ref[...] = jnp.zeros_like(y_ref)


def _experts(tile_lo, tile_hi, n_used, xs, w_gate, w_up, w_down):
    n_tiles = xs.shape[0] // ROW_TILE
    rows = lambda i, tlo, thi, nu: (jnp.maximum(jnp.minimum(i, nu[0] - 1), 0), 0)
    lo = lambda i, tlo, thi, nu: (tlo[i], 0, 0)
    hi = lambda i, tlo, thi, nu: (thi[i], 0, 0)
    up_spec = lambda m: pl.BlockSpec((1, D_MODEL, MOE_HIDDEN), m)
    dn_spec = lambda m: pl.BlockSpec((1, MOE_HIDDEN, D_MODEL), m)
    grid_spec = pltpu.PrefetchScalarGridSpec(
        num_scalar_prefetch=3,
        grid=(n_tiles,),
        in_specs=[pl.BlockSpec((ROW_TILE, XE_W), rows),
                  up_spec(lo), up_spec(lo), dn_spec(lo),
                  up_spec(hi), up_spec(hi), dn_spec(hi)],
        out_specs=pl.BlockSpec((ROW_TILE * ROW_CHUNKS, 128), lambda i, tlo, thi, nu: (i, 0)),
    )
    return pl.pallas_call(
        _experts_kernel,
        grid_spec=grid_spec,
        out_shape=jax.ShapeDtypeStruct((xs.shape[0] * ROW_CHUNKS, 128), F32),
        compiler_params=_cparams("arbitrary"),
        name="moe_experts",
    )(tile_lo, tile_hi, n_used, xs, w_gate, w_up, w_down, w_gate, w_up, w_down)


def _combine_kernel(route_ref, offs_ref, h_ref, y_ref, g_ref, out_ref, buf, sem, *, final_norm):
    base = pl.program_id(0) * TM

    def issue(t8, carry):
        t0 = pl.multiple_of(t8 * ISSUE_ROWS, ISSUE_ROWS)
        for r in range(ISSUE_ROWS):
            p = pl.multiple_of(_sorted_row(route_ref, offs_ref, base + t0 + r) * ROW_CHUNKS, ROW_CHUNKS)
            dst = pl.multiple_of((t0 + r) * ROW_CHUNKS, ROW_CHUNKS)
            pltpu.make_async_copy(y_ref.at[pl.ds(p, ROW_CHUNKS)], buf.at[pl.ds(dst, ROW_CHUNKS)], sem).start()
        return carry

    lax.fori_loop(0, TM // ISSUE_ROWS, issue, 0)
    pltpu.make_async_copy(y_ref.at[pl.ds(0, TM * ROW_CHUNKS)], buf, sem).wait()
    moe = jnp.concatenate([buf[pl.ds(j, TM, stride=ROW_CHUNKS), :] for j in range(ROW_CHUNKS)], axis=1)
    out = h_ref[...] + moe
    if final_norm:
        out = _rms(out) * g_ref[...]
    out_ref[...] = out


def _combine(route, offs, h, y, gain, final_norm):
    n = h.shape[0]
    grid_spec = pltpu.PrefetchScalarGridSpec(
        num_scalar_prefetch=2,
        grid=(n // TM,),
        in_specs=[pl.BlockSpec((TM, D_MODEL), lambda i, *_: (i, 0)),
                  pl.BlockSpec(memory_space=pl.ANY),
                  pl.BlockSpec((1, D_MODEL), lambda i, *_: (0, 0))],
        out_specs=pl.BlockSpec((TM, D_MODEL), lambda i, *_: (i, 0)),
        scratch_shapes=[pltpu.VMEM((TM * ROW_CHUNKS, 128), F32), pltpu.SemaphoreType.DMA(())],
    )
    return pl.pallas_call(
        functools.partial(_combine_kernel, final_norm=final_norm),
        grid_spec=grid_spec,
        out_shape=jax.ShapeDtypeStruct((n, D_MODEL), F32),
        compiler_params=_cparams("arbitrary"),
        name="moe_combine",
    )(route, offs, h, y, gain)


def _moe_layer(a, w_out, res, moe_norm, w_group, b_group, w_router, b_router, w_gate, w_up, w_down,
               tril, out_gain, final_norm, a_transposed):
    n = res.shape[0]
    unused = 128 - MOE_EXPERTS - MOE_GROUPS
    w_rt = jnp.concatenate([w_router, w_group, jnp.zeros((D_MODEL, unused), F32)], axis=1)
    b_rt = jnp.concatenate([b_router, b_group, jnp.zeros((unused,), F32)]).reshape(1, 128)
    h, xe, route_row, counts = _mix_out(a, w_out, res, moe_norm.reshape(1, D_MODEL), w_rt, b_rt, tril, a_transposed)

    n_tiles = n // ROW_TILE + MOE_CLASSES
    cnt = counts[0].astype(jnp.int32)
    tiles_c = (cnt + ROW_TILE - 1) // ROW_TILE
    tile_end = jnp.cumsum(tiles_c)
    offs = (tile_end - tiles_c) * ROW_TILE
    n_used = tile_end[-1]
    tile_ids = jnp.minimum(jnp.arange(n_tiles, dtype=jnp.int32), n_used - 1)
    tile_cls = jnp.sum((tile_end[None, 0:MOE_CLASSES] <= tile_ids[:, None]).astype(jnp.int32), axis=1)
    tile_cls = jnp.clip(tile_cls, 0, MOE_CLASSES - 1)
    tile_lo = jnp.asarray(_PAIR_LO)[tile_cls]
    tile_hi = jnp.asarray(_PAIR_HI)[tile_cls]
    route = route_row.reshape(n)

    n_used = n_used.reshape(1)
    xs = _dispatch(route, offs, cnt, n_used, xe, n_tiles * ROW_TILE)
    y = _experts(tile_lo, tile_hi, n_used, xs, w_gate.astype(BF16), w_up.astype(BF16),
                 w_down.astype(BF16))
    if out_gain is None:
        return h, y, route, offs
    return _combine(route, offs, h, y, out_gain.reshape(1, D_MODEL), final_norm)


def _rope_tables(pos, width):
    half = ROPE_DIM // 2
    inv_freq = jnp.power(jnp.float32(ROPE_THETA), -jnp.arange(half, dtype=F32) * (2.0 / ROPE_DIM))
    ang = pos.astype(F32)[:, None] * inv_freq[None, :]
    cos, sin = jnp.cos(ang), jnp.sin(ang)
    t = pos.shape[0]
    rest = NSA_HEAD_DIM - ROPE_DIM
    cos_t = jnp.concatenate([cos, cos, jnp.ones((t, rest), F32)], axis=1)
    sin_a = jnp.concatenate([-sin, jnp.zeros((t, half + rest), F32)], axis=1)
    sin_b = jnp.concatenate([jnp.zeros((t, half), F32), sin, jnp.zeros((t, rest), F32)], axis=1)
    rep = width // NSA_HEAD_DIM
    return jnp.tile(cos_t, (1, rep)), jnp.tile(sin_a, (1, rep)), jnp.tile(sin_b, (1, rep))


def _rope(x, cos_t, sin_a, sin_b):
    half = ROPE_DIM // 2
    parts = []
    for c in range(x.shape[1] // 128):
        xc = x[:, c * 128:(c + 1) * 128]
        parts.append(xc * cos_t + pltpu.roll(xc, 128 - half, axis=1) * sin_a + pltpu.roll(xc, half, axis=1) * sin_b)
    return parts[0] if len(parts) == 1 else jnp.concatenate(parts, axis=1)


def _rope_rows(x, cos_t, sin_a, sin_b):
    half = ROPE_DIM // 2
    reps = x.shape[0] // NSA_HEAD_DIM
    tile = lambda t: jnp.concatenate([t] * reps, axis=0)
    up = jnp.concatenate([x[half:], x[:half]], axis=0)
    down = jnp.concatenate([x[-half:], x[:-half]], axis=0)
    return x * tile(cos_t) + up * tile(sin_a) + down * tile(sin_b)


def _nsa_proj_kernel(route_ref, offs_ref, h_ref, y_ref, gq_ref, gkv_ref, wqt_ref, wkv_ref, wvt_ref,
                     cos_ref, sa_ref, sb_ref, cost_ref, sat_ref, sbt_ref,
                     h2_ref, qt_ref, gatet_ref, kc0_ref, kc1_ref, vc0_ref, vc1_ref, ks_ref, vst_ref, kw_ref, vwt_ref,
                     gbuf, sems):
    i = pl.program_id(0)

    def gather(tile, slot):
        base = tile * TM

        def issue(t8, carry):
            t0 = pl.multiple_of(t8 * ISSUE_ROWS, ISSUE_ROWS)
            for r in range(ISSUE_ROWS):
                p = pl.multiple_of(_sorted_row(route_ref, offs_ref, base + t0 + r) * ROW_CHUNKS, ROW_CHUNKS)
                dst = pl.multiple_of((t0 + r) * ROW_CHUNKS, ROW_CHUNKS)
                pltpu.make_async_copy(y_ref.at[pl.ds(p, ROW_CHUNKS)], gbuf.at[slot, pl.ds(dst, ROW_CHUNKS)],
                                      sems.at[slot]).start()
            return carry

        lax.fori_loop(0, TM // ISSUE_ROWS, issue, 0)

    @pl.when(i == 0)
    def _():
        gather(0, 0)

    @pl.when(i + 1 < pl.num_programs(0))
    def _():
        gather(i + 1, (i + 1) % 2)

    slot = i % 2
    pltpu.make_async_copy(y_ref.at[pl.ds(0, TM * ROW_CHUNKS)], gbuf.at[slot], sems.at[slot]).wait()
    moe = jnp.concatenate([gbuf[slot, pl.ds(j, TM, stride=ROW_CHUNKS), :] for j in range(ROW_CHUNKS)], axis=1)
    h2 = h_ref[...] + moe
    h2_ref[...] = h2
    r = _rms(h2)
    hq = (r * gq_ref[...]).astype(BF16)
    hk = (r * gkv_ref[...]).astype(BF16)
    qt = lax.dot_general(wqt_ref[0:1024, :], hq, _NT, preferred_element_type=F32)
    q_scale = NSA_HEAD_DIM ** -0.5 * LOG2_E
    qt_ref[...] = (_rope_rows(qt, cost_ref[...], sat_ref[...], sbt_ref[...]) * q_scale).astype(BF16)
    gatet_ref[...] = _sigmoid(lax.dot_general(wqt_ref[1024:1152, :], hq, _NT, preferred_element_type=F32))
    kc0_ref[...] = jnp.dot(hk, wkv_ref[:, 0:128], preferred_element_type=F32)
    kc1_ref[...] = jnp.dot(hk, wkv_ref[:, 128:256], preferred_element_type=F32)
    vc0_ref[...] = jnp.dot(hk, wkv_ref[:, 256:384], preferred_element_type=F32)
    vc1_ref[...] = jnp.dot(hk, wkv_ref[:, 384:512], preferred_element_type=F32)

    tm = TM
    cos_t, sin_a, sin_b = cos_ref[...], sa_ref[...], sb_ref[...]
    lane = lax.broadcasted_iota(jnp.int32, (tm, NSA_HEAD_DIM), 1)
    pos = (pl.program_id(0) % (SEQ // tm)) * tm + lax.broadcasted_iota(jnp.int32, (tm, NSA_HEAD_DIM), 0)
    blk_onehot = jnp.where(lane == (pos >> 6), 1.0, 0.0).astype(BF16)
    zeros = jnp.zeros((tm, NSA_HEAD_DIM), BF16)

    def store_keys(ref, val, aux):
        for g in range(NSA_GROUPS):
            ref[0, g] = jnp.concatenate([val[:, g * 64:(g + 1) * 64].astype(BF16), aux], axis=1)

    store_keys(ks_ref, _rope(jnp.dot(hk, wkv_ref[:, 512:768], preferred_element_type=F32), cos_t, sin_a, sin_b),
               blk_onehot)
    store_keys(kw_ref, _rope(jnp.dot(hk, wkv_ref[:, 1024:1280], preferred_element_type=F32), cos_t, sin_a, sin_b),
               zeros)

    row = lax.broadcasted_iota(jnp.int32, (V_ROWS - NSA_HEAD_DIM, tm), 0)
    ones_row = jnp.where(row == 0, 1.0, 0.0).astype(BF16)

    def store_values(ref, val_t):
        for g in range(NSA_GROUPS):
            ref[0, g] = jnp.concatenate([val_t[g * 64:(g + 1) * 64, :].astype(BF16), ones_row], axis=0)

    store_values(vst_ref, lax.dot_general(wvt_ref[0:256, :], hk, _NT, preferred_element_type=F32))
    store_values(vwt_ref, lax.dot_general(wvt_ref[256:512, :], hk, _NT, preferred_element_type=F32))


def _nsa_proj(route, offs, h, y, g_q, g_kv, w_qt, w_kv, w_vt, rope_tabs, rope_tabs_t, batch):
    n = h.shape[0]
    nblk = SEQ // TM
    row = lambda i, *_: (i, 0)
    col = lambda i, *_: (0, i)
    fixed = lambda i, *_: (0, 0)
    tab = lambda i, *_: (i % nblk, 0)
    tab_t = lambda i, *_: (0, i % nblk)
    key_spec = pl.BlockSpec((1, NSA_GROUPS, TM, KV_LANES), lambda i, *_: (i // nblk, 0, i % nblk, 0))
    key_shape = jax.ShapeDtypeStruct((batch, NSA_GROUPS, SEQ, KV_LANES), BF16)
    val_spec = pl.BlockSpec((1, NSA_GROUPS, V_ROWS, TM), lambda i, *_: (i // nblk, 0, 0, i % nblk))
    val_shape = jax.ShapeDtypeStruct((batch, NSA_GROUPS, V_ROWS, SEQ), BF16)
    raw_spec = pl.BlockSpec((TM, 128), row)
    raw_shape = jax.ShapeDtypeStruct((n, 128), F32)
    grid_spec = pltpu.PrefetchScalarGridSpec(
        num_scalar_prefetch=2,
        grid=(n // TM,),
        in_specs=[pl.BlockSpec((TM, D_MODEL), row),
                  pl.BlockSpec(memory_space=pl.ANY),
                  pl.BlockSpec((1, D_MODEL), fixed),
                  pl.BlockSpec((1, D_MODEL), fixed),
                  pl.BlockSpec((1152, D_MODEL), fixed),
                  pl.BlockSpec((D_MODEL, 1536), fixed),
                  pl.BlockSpec((512, D_MODEL), fixed),
                  pl.BlockSpec((TM, 128), tab), pl.BlockSpec((TM, 128), tab), pl.BlockSpec((TM, 128), tab),
                  pl.BlockSpec((NSA_HEAD_DIM, TM), tab_t), pl.BlockSpec((NSA_HEAD_DIM, TM), tab_t),
                  pl.BlockSpec((NSA_HEAD_DIM, TM), tab_t)],
        out_specs=[pl.BlockSpec((TM, D_MODEL), row),
                   pl.BlockSpec((1024, TM), col),
                   pl.BlockSpec((128, TM), col),
                   raw_spec, raw_spec, raw_spec, raw_spec,
                   key_spec, val_spec, key_spec, val_spec],
        scratch_shapes=[pltpu.VMEM((2, TM * ROW_CHUNKS, 128), F32), pltpu.SemaphoreType.DMA((2,))],
    )
    return pl.pallas_call(
        _nsa_proj_kernel,
        grid_spec=grid_spec,
        out_shape=[jax.ShapeDtypeStruct((n, D_MODEL), F32),
                   jax.ShapeDtypeStruct((1024, n), BF16),
                   jax.ShapeDtypeStruct((128, n), F32),
                   raw_shape, raw_shape, raw_shape, raw_shape,
                   key_shape, val_shape, key_shape, val_shape],
        compiler_params=_cparams("arbitrary"),
        name="nsa_proj",
    )(route, offs, h, y, g_q, g_kv, w_qt, w_kv, w_vt, *rope_tabs, *rope_tabs_t)


HALF_BLOCKS = SEQ // CMP_STRIDE
CMP_K = CMP_STRIDE * 256


def _compress_kernel(rk0_ref, rk1_ref, rv0_ref, rv1_ref, w1k_ref, w1v_ref, pek_ref, pev_ref, w2k_ref, w2v_ref,
                     cos_ref, sa_ref, sb_ref, kc_ref, vc_ref):
    lane = lax.broadcasted_iota(jnp.int32, (HALF_BLOCKS, 512), 1)
    first_half = (lane & 127) < 64

    def comp(raw_refs, w1_ref, pe_ref, w2_ref):
        x = jnp.concatenate([r[pl.ds(l, HALF_BLOCKS, stride=CMP_STRIDE), :]
                             for l in range(CMP_STRIDE) for r in raw_refs], axis=1).astype(BF16)
        r = jnp.dot(x, w1_ref[...], preferred_element_type=F32)
        rpe = jnp.dot(pe_ref[...].astype(BF16), w1_ref[...], preferred_element_type=F32)
        r = r + jnp.where(first_half, rpe[0:1, :], rpe[1:2, :])
        nxt = pltpu.roll(r, HALF_BLOCKS - 1, axis=0)
        nxt = jnp.concatenate([pltpu.roll(nxt[:, c * 128:(c + 1) * 128], 64, axis=1) for c in range(4)], axis=1)
        pre = r + nxt
        act = pre * _sigmoid(pre)
        return jnp.dot(act.astype(BF16), w2_ref[...], preferred_element_type=F32)

    kc = comp((rk0_ref, rk1_ref), w1k_ref, pek_ref, w2k_ref)
    kc = _rope(kc, cos_ref[...], sa_ref[...], sb_ref[...])
    vc = comp((rv0_ref, rv1_ref), w1v_ref, pev_ref, w2v_ref)
    for g in range(NSA_GROUPS):
        kc_ref[0, g] = kc[:, g * 64:(g + 1) * 64]
        vc_ref[0, g] = vc[:, g * 64:(g + 1) * 64]


def _compress_weights(pe, w1, w2):
    dh = NSA_HEAD_DIM
    w1r = w1.reshape(2, CMP_STRIDE, dh, dh)
    halves = jnp.concatenate([w1r[0], w1r[1]], axis=-1)
    rows = jnp.broadcast_to(halves[:, None], (CMP_STRIDE, NSA_GROUPS, dh, 2 * dh)).reshape(CMP_K, 2 * dh)
    row_group = (jnp.arange(CMP_K) // dh) % NSA_GROUPS
    w1_big = jnp.concatenate([jnp.where((row_group == g)[:, None], rows, 0.0) for g in range(NSA_GROUPS)],
                             axis=1).astype(BF16)
    pe_rows = jnp.broadcast_to(pe.reshape(2, CMP_STRIDE, 1, dh), (2, CMP_STRIDE, NSA_GROUPS, dh)).reshape(2, CMP_K)
    pe_rows = jnp.concatenate([pe_rows, jnp.zeros((6, CMP_K), F32)], axis=0)
    rows2 = jnp.tile(jnp.concatenate([w2, jnp.zeros_like(w2)], axis=0), (NSA_GROUPS, 1))
    row_group2 = jnp.arange(NSA_GROUPS * 2 * dh) // (2 * dh)
    w2_bd = jnp.concatenate([jnp.where((row_group2 == g)[:, None], rows2, 0.0) for g in range(NSA_GROUPS)],
                            axis=1).astype(BF16)
    return w1_big, pe_rows, w2_bd


def _compress(raw_k, raw_v, wk, wv, cmp_tabs, batch):
    fixed = lambda b: (0, 0)
    raw_spec = pl.BlockSpec((SEQ, 128), lambda b: (b, 0))
    out_spec = pl.BlockSpec((1, NSA_GROUPS, N_CMP_PAD, NSA_HEAD_DIM), lambda b: (b, 0, 0, 0))
    out_shape = jax.ShapeDtypeStruct((batch, NSA_GROUPS, N_CMP_PAD, NSA_HEAD_DIM), F32)
    return pl.pallas_call(
        _compress_kernel,
        grid=(batch,),
        in_specs=[raw_spec, raw_spec, raw_spec, raw_spec,
                  pl.BlockSpec((CMP_K, 512), fixed), pl.BlockSpec((CMP_K, 512), fixed),
                  pl.BlockSpec((8, CMP_K), fixed), pl.BlockSpec((8, CMP_K), fixed),
                  pl.BlockSpec((512, 256), fixed), pl.BlockSpec((512, 256), fixed),
                  pl.BlockSpec((N_CMP_PAD, 128), fixed), pl.BlockSpec((N_CMP_PAD, 128), fixed),
                  pl.BlockSpec((N_CMP_PAD, 128), fixed)],
        out_specs=[out_spec, out_spec],
        out_shape=[out_shape, out_shape],
        compiler_params=_cparams("parallel"),
        name="nsa_compress",
    )(*raw_k, *raw_v, wk[0], wv[0], wk[1], wv[1], wk[2], wv[2], *cmp_tabs)


def _nsa_attn_kernel(qt_ref, gatet_ref, kc_ref, vc_ref, ks_ref, vst_ref, kw_ref, vwt_ref, ovl_ref, out_ref,
                     qx_sc, ms_sc, as_sc, aw_sc):
    qb = pl.program_id(2)
    q0 = qb * Q_BLOCK
    Q, HG, DH = Q_BLOCK, NSA_GROUP_SIZE, NSA_HEAD_DIM
    groups = range(ATTN_GROUPS)

    def heads(x):
        return jnp.concatenate([x] * HG, axis=1)

    n_row = lax.broadcasted_iota(jnp.int32, (N_CMP_PAD, Q), 0)
    t_lane = q0 + lax.broadcasted_iota(jnp.int32, (N_CMP_PAD, Q), 1)
    valid_c = heads((n_row * CMP_STRIDE + CMP_BLOCK - 1 <= t_lane) & (n_row < N_CMP_PAD - 1))
    j_row = lax.broadcasted_iota(jnp.int32, (N_SEL, Q), 0)
    cur = (q0 + lax.broadcasted_iota(jnp.int32, (N_SEL, Q), 1)) >> 6
    forced = (j_row == 0) | (j_row == cur) | (j_row == cur - 1)
    o_c = []
    for gi in groups:
        q4 = qt_ref[gi * HG * DH:(gi + 1) * HG * DH, :]
        q_t = jnp.concatenate([q4[h * DH:(h + 1) * DH, :] for h in range(HG)], axis=1)
        s_c = jnp.dot(kc_ref[0, gi].astype(BF16), q_t, preferred_element_type=F32)
        s_c = jnp.where(valid_c, s_c, NEG_INF)
        m_c = jnp.max(s_c, axis=0, keepdims=True)
        e_c = jnp.where(valid_c, jnp.exp2(s_c - m_c), 0.0)
        l_c = jnp.sum(e_c, axis=0, keepdims=True)
        p_c = e_c * jnp.where(l_c > 0.0, 1.0 / l_c, 0.0)
        o_c.append(jnp.dot(vc_ref[0, gi].T.astype(BF16), p_c.astype(BF16), preferred_element_type=F32))

        p_sum = p_c[:, 0:Q] + p_c[:, Q:2 * Q] + p_c[:, 2 * Q:3 * Q] + p_c[:, 3 * Q:4 * Q]
        imp_t = jnp.dot(ovl_ref[...], p_sum, precision=HIGHEST, preferred_element_type=F32)[0:N_SEL]
        imp_t = jnp.where(forced, FORCED_SCORE, imp_t)
        imp_t = jnp.where(j_row > cur, NEG_INF, imp_t)
        cnt = jnp.zeros((N_SEL, Q), F32)
        for i in range(N_SEL):
            ri = imp_t[i:i + 1, :]
            cnt = cnt + jnp.where(ri > imp_t, 1.0, jnp.where((ri == imp_t) & (j_row > i), 1.0, 0.0))
        sel_bias = jnp.where((cnt < SEL_TOPK) & (j_row <= cur), 0.0, NEG_INF).astype(BF16)

        qx_sc[gi] = jnp.concatenate([q_t, heads(sel_bias), jnp.zeros((KV_LANES - DH - N_SEL, HG * Q), BF16)],
                                    axis=0)

    def finish(acc):
        return acc[0:DH] / acc[DEN_ROW:DEN_ROW + 1]

    ms_sc[...] = jnp.full(ms_sc.shape, NEG_INF, F32)
    as_sc[...] = jnp.zeros(as_sc.shape, F32)
    n_kt = (q0 + Q + SLC_TILE - 1) // SLC_TILE

    def slc_tile(kt, bias):
        start = pl.multiple_of(kt * SLC_TILE, SLC_TILE)
        for gi in groups:
            s = jnp.dot(ks_ref[0, gi, pl.ds(start, SLC_TILE), :], qx_sc[gi], preferred_element_type=F32)
            if bias is not None:
                s = s + bias
            m_old = ms_sc[gi, 0:1, :]
            m_new = jnp.maximum(m_old, jnp.max(s, axis=0, keepdims=True))
            p = jnp.exp2(s - m_new).astype(BF16)
            pv = jnp.dot(vst_ref[0, gi, :, pl.ds(start, SLC_TILE)], p, preferred_element_type=F32)
            as_sc[gi] = as_sc[gi] * jnp.exp2(m_old - m_new) + pv
            ms_sc[gi] = jnp.broadcast_to(m_new, ms_sc.shape[1:])

    def slc_full(kt, carry):
        slc_tile(kt, None)
        return carry

    lax.fori_loop(0, n_kt - 1, slc_full, 0)

    w0 = pl.multiple_of(jnp.maximum(q0 - WINDOW, 0), Q)
    s_w = [jnp.dot(kw_ref[0, gi, pl.ds(w0, WIN_KEYS), :], qx_sc[gi], preferred_element_type=F32) for gi in groups]

    c_s = lax.broadcasted_iota(jnp.int32, (SLC_TILE, Q), 0)
    r_s = lax.broadcasted_iota(jnp.int32, (SLC_TILE, Q), 1)
    slc_tile(n_kt - 1, heads(jnp.where((n_kt - 1) * SLC_TILE + c_s <= q0 + r_s, 0.0, NEG_INF)))

    c_w = lax.broadcasted_iota(jnp.int32, (Q, Q), 0)
    r_w = lax.broadcasted_iota(jnp.int32, (Q, Q), 1)
    n_slab = WIN_KEYS // Q
    slab_bias = []
    for j in range(n_slab):
        key = w0 + j * Q + c_w
        tok = q0 + r_w
        slab_bias.append(heads(jnp.where((key <= tok) & (key > tok - WINDOW), 0.0, NEG_INF)))
    for gi in groups:
        slabs = [s_w[gi][j * Q:(j + 1) * Q] + slab_bias[j] for j in range(n_slab)]
        top = slabs[0]
        for sj in slabs[1:]:
            top = jnp.maximum(top, sj)
        m = jnp.max(top, axis=0, keepdims=True)
        p = jnp.concatenate([jnp.exp2(sj - m) for sj in slabs], axis=0).astype(BF16)
        aw_sc[gi] = jnp.dot(vwt_ref[0, gi, :, pl.ds(w0, WIN_KEYS)], p, preferred_element_type=F32)

    for gi in groups:
        g = pl.program_id(1) * ATTN_GROUPS + gi

        def gate_row(branch):
            return jnp.concatenate([gatet_ref[pl.ds((g * HG + h) * 3 + branch, 1), :] for h in range(HG)], axis=1)

        o = gate_row(0) * o_c[gi] + gate_row(1) * finish(as_sc[gi]) + gate_row(2) * finish(aw_sc[gi])
        out_ref[gi * HG * DH:(gi + 1) * HG * DH, :] = jnp.concatenate(
            [o[:, h * Q:(h + 1) * Q] for h in range(HG)], axis=0).astype(BF16)


def _overlap_t():
    n = np.arange(N_CMP_PAD)
    j = np.arange(128)
    cmp_start = n * CMP_STRIDE
    cmp_end = cmp_start + CMP_BLOCK - 1
    sel_start = j * SEL_BLOCK
    ovl = ((cmp_start[None, :] <= sel_start[:, None] + SEL_BLOCK - 1) & (cmp_end[None, :] >= sel_start[:, None])
           & (j[:, None] < N_SEL) & (n[None, :] < N_CMP_PAD - 1))
    return jnp.asarray(ovl.astype(np.float32))


def _nsa_attn(q_t, gates_t, kc, vc, ks, vs_t, kw, vw_t, batch):
    n = q_t.shape[1]
    gs = ATTN_GROUPS
    qcol = lambda b, g, i: (g, b * N_QB + i)
    gcol = lambda b, g, i: (0, b * N_QB + i)
    kv = lambda b, g, i: (b, g, 0, 0)
    rows = gs * NSA_GROUP_SIZE * NSA_HEAD_DIM
    lanes = NSA_GROUP_SIZE * Q_BLOCK
    return pl.pallas_call(
        _nsa_attn_kernel,
        grid=(batch, NSA_GROUPS // gs, N_QB),
        in_specs=[pl.BlockSpec((rows, Q_BLOCK), qcol),
                  pl.BlockSpec((128, Q_BLOCK), gcol),
                  pl.BlockSpec((1, gs, N_CMP_PAD, NSA_HEAD_DIM), kv),
                  pl.BlockSpec((1, gs, N_CMP_PAD, NSA_HEAD_DIM), kv),
                  pl.BlockSpec((1, gs, SEQ, KV_LANES), kv),
                  pl.BlockSpec((1, gs, V_ROWS, SEQ), kv),
                  pl.BlockSpec((1, gs, SEQ, KV_LANES), kv),
                  pl.BlockSpec((1, gs, V_ROWS, SEQ), kv),
                  pl.BlockSpec((128, N_CMP_PAD), lambda b, g, i: (0, 0))],
        out_specs=pl.BlockSpec((rows, Q_BLOCK), qcol),
        out_shape=jax.ShapeDtypeStruct((NSA_HEADS * NSA_HEAD_DIM, n), BF16),
        scratch_shapes=[pltpu.VMEM((gs, KV_LANES, lanes), BF16),
                        pltpu.VMEM((gs, 8, lanes), F32),
                        pltpu.VMEM((gs, V_ROWS, lanes), F32),
                        pltpu.VMEM((gs, V_ROWS, lanes), F32)],
        compiler_params=_cparams("parallel", "parallel", "arbitrary"),
        name="nsa_attn",
    )(q_t, gates_t, kc, vc, ks, vs_t, kw, vw_t, _overlap_t())


def kernel(x, mlstm_norm, mlstm_w_in, mlstm_gate_bias, mlstm_head_norm, mlstm_w_out, kv_norm, kv_w, cmp_pe_k, cmp_w1_k, cmp_w2_k, cmp_pe_v, cmp_w1_v, cmp_w2_v, nsa_norm, nsa_w_q, nsa_w_out, moe_norm, moe_w_group, moe_b_group, moe_w_router, moe_b_router, moe_w_gate, moe_w_up, moe_w_down, final_norm):
    batch, seq, d = x.shape
    assert seq == SEQ and d == D_MODEL
    assert mlstm_norm.shape[0] == 1 and nsa_norm.shape[0] == 1 and moe_norm.shape[0] == 2
    n = batch * seq
    x2d = x.reshape(n, d)
    tril = jnp.tril(jnp.ones((TM, TM), F32)).astype(BF16)

    w_in = mlstm_w_in[0]
    w_gate = jnp.concatenate([w_in[:, 3072:3080], jnp.zeros((d, 120), F32)], axis=1)
    b_gate = jnp.concatenate([mlstm_gate_bias[0], jnp.zeros((120,), F32)]).reshape(1, 128)
    head_norm_cols = jnp.broadcast_to(mlstm_head_norm[0].reshape(d, 1), (d, 128))
    hs_t = _mlstm(x2d, mlstm_norm[0].reshape(1, d), w_in[:, 0:3072].T.astype(BF16), w_in[:, 512:1024].astype(BF16),
                  w_gate, b_gate, head_norm_cols, batch)
    h_pre, y, route, offs = _moe_layer(hs_t, mlstm_w_out[0].astype(BF16), x2d, moe_norm[0], moe_w_group[0],
                                       moe_b_group[0], moe_w_router[0], moe_b_router[0], moe_w_gate[0],
                                       moe_w_up[0], moe_w_down[0], tril, None, False, a_transposed=True)

    w_qt = jnp.concatenate([nsa_w_q[0].T, jnp.zeros((80, d), F32)], axis=0).astype(BF16)
    w_vt = jnp.concatenate([kv_w[:, 768:1024], kv_w[:, 1280:1536]], axis=1).T.astype(BF16)
    seq_tabs = _rope_tables(jnp.arange(SEQ), 128)
    seq_tabs_t = tuple(t[:, 0:NSA_HEAD_DIM].T for t in seq_tabs)
    h, q_t, gates_t, rk0, rk1, rv0, rv1, ks, vs_t, kw, vw_t = _nsa_proj(
        route, offs, h_pre, y, nsa_norm[0].reshape(1, d), kv_norm.reshape(1, d), w_qt, kv_w.astype(BF16), w_vt,
        seq_tabs, seq_tabs_t, batch)
    cmp_pos = jnp.arange(N_CMP_PAD) * CMP_STRIDE + CMP_BLOCK - 1
    kc, vc = _compress((rk0, rk1), (rv0, rv1), _compress_weights(cmp_pe_k, cmp_w1_k, cmp_w2_k),
                       _compress_weights(cmp_pe_v, cmp_w1_v, cmp_w2_v), _rope_tables(cmp_pos, 128), batch)
    att_t = _nsa_attn(q_t, gates_t, kc, vc, ks, vs_t, kw, vw_t, batch)
    out = _moe_layer(att_t, nsa_w_out[0].astype(BF16), h, moe_norm[1], moe_w_group[1], moe_b_group[1],
                     moe_w_router[1], moe_b_router[1], moe_w_gate[1], moe_w_up[1], moe_w_down[1],
                     tril, final_norm, True, a_transposed=True)
    return out.reshape(batch, seq, d)
```

```python
import functools

import numpy as np
import jax
import jax.numpy as jnp
from jax import lax
from jax.experimental import pallas as pl
from jax.experimental.pallas import tpu as pltpu

F32 = jnp.float32
BF16 = jnp.bfloat16
HIGHEST = lax.Precision.HIGHEST

D_MODEL = 1024
SEQ = 2048
RMS_EPS = 1e-6
NEG_INF = -1e30

MLSTM_HEADS = 4
MLSTM_V_DIM = 256
MLSTM_QK_DIM = 128
MLSTM_L = 256
GATE_SOFTCAP = 15.0

NSA_HEADS = 16
NSA_HEAD_DIM = 64
NSA_GROUPS = 4
NSA_GROUP_SIZE = 4
CMP_BLOCK = 32
CMP_STRIDE = 16
N_CMP_PAD = 128
SEL_BLOCK = 64
N_SEL = SEQ // SEL_BLOCK
SEL_TOPK = 16
WINDOW = 512
Q_BLOCK = 256
N_QB = SEQ // Q_BLOCK
WIN_KEYS = WINDOW + Q_BLOCK
SLC_TILE = 512
ATTN_GROUPS = 4
KV_LANES = 2 * NSA_HEAD_DIM
V_ROWS = NSA_HEAD_DIM + 16
DEN_ROW = NSA_HEAD_DIM
FORCED_SCORE = 1e6
ROPE_THETA = 500000.0
ROPE_DIM = 16

MOE_GROUPS = 4
MOE_PER_GROUP = 8
MOE_EXPERTS = 32
MOE_HIDDEN = 256
MOE_PAIRS = 28
MOE_CLASSES = MOE_GROUPS * MOE_PAIRS
ROW_TILE = 256
XE_W = D_MODEL + 128
LANE_WLO, LANE_WHI, LANE_ROUTE = 0, 1, 2
ROUTE_SHIFT = 16
ROUTE_ROWS = 40
ROW_CHUNKS = D_MODEL // 128
ISSUE_ROWS = 8

TM = 512
VMEM_LIMIT = 56 * 1024 * 1024

_NT = (((1,), (1,)), ((), ()))
_TN = (((0,), (0,)), ((), ()))


def _cparams(*sem):
    return pltpu.CompilerParams(dimension_semantics=sem, vmem_limit_bytes=VMEM_LIMIT)


def _rms(x):
    return x * lax.rsqrt(jnp.mean(x * x, axis=-1, keepdims=True) + RMS_EPS)


def _sigmoid(x):
    return 1.0 / (1.0 + jnp.exp(-x))


LOG2_E = 1.4426950408889634


def _split_bf16(x):
    hi = x.astype(BF16)
    return hi, (x - hi.astype(F32)).astype(BF16)


def _dot_split(x, w_hi, w_lo):
    x_hi, x_lo = _split_bf16(x)
    return (jnp.dot(x_hi, w_hi, preferred_element_type=F32) + jnp.dot(x_hi, w_lo, preferred_element_type=F32)
            + jnp.dot(x_lo, w_hi, preferred_element_type=F32))


def _mlstm_kernel(x_ref, g_ref, wt_ref, wk_ref, wgh_ref, wgl_ref, bg_ref, hn_ref, out_ref, ct_ref, n_ref, m_ref):
    L = MLSTM_L

    @pl.when(pl.program_id(1) == 0)
    def _():
        ct_ref[...] = jnp.zeros_like(ct_ref)
        n_ref[...] = jnp.zeros_like(n_ref)
        m_ref[...] = jnp.zeros_like(m_ref)

    hn = _rms(x_ref[...]) * g_ref[...]
    hb = hn.astype(BF16)
    q_t = lax.dot_general(wt_ref[0:512, :], hb, _NT, preferred_element_type=F32).astype(BF16)
    k_all = (jnp.dot(hb, wk_ref[...], preferred_element_type=F32) * (MLSTM_QK_DIM ** -0.5)).astype(BF16)
    v_t = lax.dot_general(wt_ref[1024:2048, :], hb, _NT, preferred_element_type=F32).astype(BF16)
    o_t = _sigmoid(lax.dot_general(wt_ref[2048:3072, :], hb, _NT, preferred_element_type=F32))
    gates = _dot_split(hn, wgh_ref[...], wgl_ref[...]) + bg_ref[...]
    gates = GATE_SOFTCAP * jnp.tanh(gates / GATE_SOFTCAP)
    lane = lax.broadcasted_iota(jnp.int32, gates.shape, 1)
    log_f = jnp.minimum(gates, 0.0) - jnp.log1p(jnp.exp(-jnp.abs(gates)))
    lg = jnp.where(lane < MLSTM_HEADS, gates, log_f)
    gr_pad = lg.T
    gr = gr_pad[0:8, :]

    row = lax.broadcasted_iota(jnp.int32, (L, L), 0)
    col = lax.broadcasted_iota(jnp.int32, (L, L), 1)
    causal_t = row <= col
    tril = (col <= row).astype(F32)
    b_row = lax.dot_general(gr, tril, _NT, precision=HIGHEST, preferred_element_type=F32)
    b_col = lax.dot_general(tril, gr_pad, _NT, precision=HIGHEST, preferred_element_type=F32)
    n_hi, n_lo = _split_bf16(n_ref[...])

    for h in range(MLSTM_HEADS):
        src_col = lg[:, h:h + 1] - b_col[:, 4 + h:5 + h]
        bf_row = b_row[4 + h:5 + h, :]
        m = m_ref[h:h + 1, 0:1]
        dmat = jnp.where(causal_t, bf_row + src_col, NEG_INF)
        m_inter = bf_row + m
        m_t = jnp.maximum(m_inter, jnp.max(dmat, axis=0, keepdims=True))
        w_intra = jnp.exp(dmat - m_t)
        w_inter = jnp.exp(m_inter - m_t)
        qt = q_t[h * 128:(h + 1) * 128, :]
        kh = k_all[:, h * 128:(h + 1) * 128]
        vt = v_t[h * 256:(h + 1) * 256, :]
        s = jnp.dot(kh, qt, preferred_element_type=F32) * w_intra
        c_old = ct_ref[h]
        num = (jnp.dot(vt, s.astype(BF16), preferred_element_type=F32)
               + w_inter * jnp.dot(c_old.astype(BF16), qt, preferred_element_type=F32))
        qn = (jnp.dot(n_hi, qt, preferred_element_type=F32) + jnp.dot(n_lo, qt, preferred_element_type=F32))[h:h + 1]
        den = jnp.sum(s, axis=0, keepdims=True) + w_inter * qn
        hh = num / jnp.maximum(jnp.abs(den), jnp.exp(-m_t))
        b_end = bf_row[:, L - 1:L]
        g = b_end + src_col
        m_new = jnp.maximum(b_end + m, jnp.max(g, axis=0, keepdims=True))
        ws = jnp.exp(g - m_new)
        decay = jnp.exp(b_end + m - m_new)
        kf = kh.astype(F32) * ws
        ct_ref[h] = decay * c_old + jnp.dot(vt, kf.astype(BF16), preferred_element_type=F32)
        n_ref[h:h + 1, :] = decay * n_ref[h:h + 1, :] + jnp.sum(kf, axis=0, keepdims=True)
        m_ref[h:h + 1, :] = jnp.broadcast_to(m_new, (1, 128))
        rows = slice(h * 256, (h + 1) * 256)
        hnorm = hh * lax.rsqrt(jnp.mean(hh * hh, axis=0, keepdims=True) + RMS_EPS)
        gain = jnp.concatenate([hn_ref[rows, :]] * (L // 128), axis=1)
        out_ref[rows, :] = (hnorm * gain * o_t[rows, :]).astype(BF16)


def _mlstm(x2d, norm_g, w_t, w_k, w_gate, b_gate, head_norm_cols, batch):
    n = x2d.shape[0]
    nblk = SEQ // MLSTM_L
    w_gate_hi, w_gate_lo = _split_bf16(w_gate)
    fixed = lambda b, j: (0, 0)
    return pl.pallas_call(
        _mlstm_kernel,
        grid=(batch, nblk),
        in_specs=[pl.BlockSpec((MLSTM_L, D_MODEL), lambda b, j: (b * nblk + j, 0)),
                  pl.BlockSpec((1, D_MODEL), fixed),
                  pl.BlockSpec((3072, D_MODEL), fixed),
                  pl.BlockSpec((D_MODEL, 512), fixed),
                  pl.BlockSpec((D_MODEL, 128), fixed),
                  pl.BlockSpec((D_MODEL, 128), fixed),
                  pl.BlockSpec((1, 128), fixed),
                  pl.BlockSpec((1024, 128), fixed)],
        out_specs=pl.BlockSpec((1024, MLSTM_L), lambda b, j: (0, b * nblk + j)),
        out_shape=jax.ShapeDtypeStruct((1024, n), BF16),
        scratch_shapes=[pltpu.VMEM((MLSTM_HEADS, MLSTM_V_DIM, MLSTM_QK_DIM), F32),
                        pltpu.VMEM((8, 128), F32),
                        pltpu.VMEM((8, 128), F32)],
        compiler_params=_cparams("parallel", "arbitrary"),
        name="mlstm",
    )(x2d, norm_g, w_t, w_k, w_gate_hi, w_gate_lo, b_gate, head_norm_cols)


def _pair_tables():
    lo, hi = [], []
    for g in range(MOE_GROUPS):
        for a in range(MOE_PER_GROUP):
            for b in range(a + 1, MOE_PER_GROUP):
                lo.append(g * MOE_PER_GROUP + a)
                hi.append(g * MOE_PER_GROUP + b)
    return np.asarray(lo, np.int32), np.asarray(hi, np.int32)


_PAIR_LO, _PAIR_HI = _pair_tables()


def _mix_out_kernel(a_ref, w_ref, res_ref, g_ref, wrh_ref, wrl_ref, br_ref, tril_ref, h_ref, xe_ref, route_ref, cnt_ref,
                    run_ref, *, a_transposed):
    @pl.when(pl.program_id(0) == 0)
    def _():
        run_ref[...] = jnp.zeros_like(run_ref)

    dims = _TN if a_transposed else (((1,), (0,)), ((), ()))
    h = res_ref[...] + lax.dot_general(a_ref[...], w_ref[...], dims, preferred_element_type=F32)
    h_ref[...] = h
    hn = _rms(h) * g_ref[...]
    xe_ref[:, 0:D_MODEL] = hn

    logits = _dot_split(hn, wrh_ref[...], wrl_ref[...]) + br_ref[...]
    lt = logits.T[0:ROUTE_ROWS, :]
    r_i = lax.broadcasted_iota(jnp.int32, lt.shape, 0)
    r = r_i.astype(F32)
    ninf = -jnp.inf
    is_g = (r_i >= MOE_EXPERTS) & (r_i < MOE_EXPERTS + MOE_GROUPS)
    glog = jnp.where(is_g, lt, ninf)
    gmax = jnp.max(glog, axis=0, keepdims=True)
    gidx = jnp.min(jnp.where(glog == gmax, r - MOE_EXPERTS, 99.0), axis=0, keepdims=True)
    pg_top = 1.0 / jnp.sum(jnp.exp(glog - gmax), axis=0, keepdims=True)
    in_grp = (r_i < MOE_EXPERTS) & ((r_i >> 3).astype(F32) == gidx)
    ev = jnp.where(in_grp, lt, ninf)
    v1 = jnp.max(ev, axis=0, keepdims=True)
    i1 = jnp.min(jnp.where(ev == v1, r, 999.0), axis=0, keepdims=True)
    ev2 = jnp.where(r == i1, ninf, ev)
    v2 = jnp.max(ev2, axis=0, keepdims=True)
    i2 = jnp.min(jnp.where(ev2 == v2, r, 999.0), axis=0, keepdims=True)
    e2 = jnp.exp(v2 - v1)
    w1 = pg_top / (1.0 + e2)
    w2 = pg_top * e2 / (1.0 + e2)
    first_lo = i1 < i2
    w_lo = jnp.where(first_lo, w1, w2)
    w_hi = jnp.where(first_lo, w2, w1)
    a = jnp.minimum(i1, i2) - MOE_PER_GROUP * gidx
    b = jnp.maximum(i1, i2) - MOE_PER_GROUP * gidx
    cls = gidx * MOE_PAIRS + a * (15.0 - a) * 0.5 + (b - a - 1.0)

    onehot = lax.broadcasted_iota(jnp.int32, (128, TM), 0).astype(F32) == cls
    prefix = lax.dot_general(onehot.astype(BF16), tril_ref[...], _NT, preferred_element_type=F32)
    run = run_ref[...]
    rank = jnp.sum(jnp.where(onehot, prefix - 1.0 + jnp.concatenate([run] * (TM // 128), axis=1), 0.0),
                   axis=0, keepdims=True)
    run_new = run + jnp.sum(jnp.where(onehot, 1.0, 0.0), axis=1, keepdims=True)
    run_ref[...] = run_new
    cnt_ref[...] = run_new.T[0:8, :]

    route = cls * float(2 ** ROUTE_SHIFT) + rank
    route_ref[...] = route.astype(jnp.int32)
    meta_rows = jnp.concatenate([w_lo, w_hi, route, jnp.zeros((128 - 3, TM), F32)], axis=0)
    xe_ref[:, D_MODEL:XE_W] = meta_rows.T


def _mix_out(a, w, res, g_moe, w_rt, b_rt, tril, a_transposed):
    n = res.shape[0]
    kdim = w.shape[0]
    row = lambda i: (i, 0)
    fixed = lambda i: (0, 0)
    a_spec = pl.BlockSpec((kdim, TM), lambda i: (0, i)) if a_transposed else pl.BlockSpec((TM, kdim), row)
    w_rt_hi, w_rt_lo = _split_bf16(w_rt)
    return pl.pallas_call(
        functools.partial(_mix_out_kernel, a_transposed=a_transposed),
        grid=(n // TM,),
        in_specs=[a_spec,
                  pl.BlockSpec((kdim, D_MODEL), fixed),
                  pl.BlockSpec((TM, D_MODEL), row),
                  pl.BlockSpec((1, D_MODEL), fixed),
                  pl.BlockSpec((D_MODEL, 128), fixed),
                  pl.BlockSpec((D_MODEL, 128), fixed),
                  pl.BlockSpec((1, 128), fixed),
                  pl.BlockSpec((TM, TM), fixed)],
        out_specs=[pl.BlockSpec((TM, D_MODEL), row),
                   pl.BlockSpec((TM, XE_W), row),
                   pl.BlockSpec((1, TM), lambda i: (0, i)),
                   pl.BlockSpec((8, 128), fixed)],
        out_shape=[jax.ShapeDtypeStruct((n, D_MODEL), F32),
                   jax.ShapeDtypeStruct((n, XE_W), F32),
                   jax.ShapeDtypeStruct((1, n), jnp.int32),
                   jax.ShapeDtypeStruct((8, 128), F32)],
        scratch_shapes=[pltpu.VMEM((128, 128), F32)],
        compiler_params=_cparams("arbitrary"),
        name="mix_out",
    )(a, w, res, g_moe, w_rt_hi, w_rt_lo, b_rt, tril)


def _sorted_row(route_ref, offs_ref, idx):
    r = route_ref[idx]
    return offs_ref[r >> ROUTE_SHIFT] + (r & (2 ** ROUTE_SHIFT - 1))


def _dispatch_kernel(route_ref, offs_ref, cnt_ref, nused_ref, xe_ref, xs_ref, zbuf, sem, zsem):
    i = pl.program_id(0)
    base = i * TM

    @pl.when(i == 0)
    def _():
        zbuf[...] = jnp.zeros_like(zbuf)

        def per_class(c, carry):
            cnt = cnt_ref[c]
            start = offs_ref[c] + cnt
            pad = (-cnt) & (ROW_TILE - 1)
            head = (-cnt) & 7
            blocks = (pad - head) >> 3

            def fill_row(r, inner):
                pltpu.make_async_copy(zbuf.at[pl.ds(0, 1)], xs_ref.at[pl.ds(start + r, 1)], zsem).start()
                return inner

            def fill_block(b, inner):
                row0 = pl.multiple_of(start + head + b * 8, 8)
                pltpu.make_async_copy(zbuf.at[pl.ds(0, 8)], xs_ref.at[pl.ds(row0, 8)], zsem).start()
                return inner

            def drain_row(r, inner):
                pltpu.make_async_copy(zbuf.at[pl.ds(0, 1)], xs_ref.at[pl.ds(0, 1)], zsem).wait()
                return inner

            def drain_block(b, inner):
                pltpu.make_async_copy(zbuf.at[pl.ds(0, 8)], xs_ref.at[pl.ds(0, 8)], zsem).wait()
                return inner

            lax.fori_loop(0, head, fill_row, 0)
            lax.fori_loop(0, blocks, fill_block, 0)
            lax.fori_loop(0, head, drain_row, 0)
            lax.fori_loop(0, blocks, drain_block, 0)
            return carry

        lax.fori_loop(0, MOE_CLASSES, per_class, 0)

        def tail(t, carry):
            row0 = pl.multiple_of(t * ROW_TILE, ROW_TILE)
            cp = pltpu.make_async_copy(zbuf, xs_ref.at[pl.ds(row0, ROW_TILE)], zsem)
            cp.start()
            cp.wait()
            return carry

        lax.fori_loop(nused_ref[0], xs_ref.shape[0] // ROW_TILE, tail, 0)

    def issue(t8, carry):
        t0 = pl.multiple_of(t8 * ISSUE_ROWS, ISSUE_ROWS)
        for r in range(ISSUE_ROWS):
            p = _sorted_row(route_ref, offs_ref, base + t0 + r)
            pltpu.make_async_copy(xe_ref.at[pl.ds(t0 + r, 1)], xs_ref.at[pl.ds(p, 1)], sem).start()
        return carry

    lax.fori_loop(0, TM // ISSUE_ROWS, issue, 0)
    pltpu.make_async_copy(xe_ref, xs_ref.at[pl.ds(0, TM)], sem).wait()


def _dispatch(route, offs, cnt, n_used, xe, n_rows):
    n = xe.shape[0]
    grid_spec = pltpu.PrefetchScalarGridSpec(
        num_scalar_prefetch=4,
        grid=(n // TM,),
        in_specs=[pl.BlockSpec((TM, XE_W), lambda i, *_: (i, 0))],
        out_specs=pl.BlockSpec(memory_space=pl.ANY),
        scratch_shapes=[pltpu.VMEM((ROW_TILE, XE_W), F32), pltpu.SemaphoreType.DMA(()),
                        pltpu.SemaphoreType.DMA(())],
    )
    return pl.pallas_call(
        _dispatch_kernel,
        grid_spec=grid_spec,
        out_shape=jax.ShapeDtypeStruct((n_rows, XE_W), F32),
        compiler_params=_cparams("arbitrary"),
        name="moe_dispatch",
    )(route, offs, cnt, n_used, xe)


def _experts_kernel(tlo_ref, thi_ref, nused_ref, xs_ref, wg_lo, wu_lo, wd_lo, wg_hi, wu_hi, wd_hi, y_ref):
    del tlo_ref, thi_ref

    @pl.when(pl.program_id(0) < nused_ref[0])
    def _():
        x = xs_ref[:, 0:D_MODEL].astype(BF16)

        def ffn(wg, wu, wd, w):
            a = jnp.dot(x, wg[0], preferred_element_type=F32)
            u = jnp.dot(x, wu[0], preferred_element_type=F32)
            hid = (a * _sigmoid(a)) * u * w
            return jnp.dot(hid.astype(BF16), wd[0], preferred_element_type=F32)

        w_lo = xs_ref[:, D_MODEL + LANE_WLO:D_MODEL + LANE_WLO + 1]
        w_hi = xs_ref[:, D_MODEL + LANE_WHI:D_MODEL + LANE_WHI + 1]
        y = ffn(wg_lo, wu_lo, wd_lo, w_lo) + ffn(wg_hi, wu_hi, wd_hi, w_hi)
        for j in range(ROW_CHUNKS):
            y_ref[pl.ds(j, ROW_TILE, stride=ROW_CHUNKS), :] = y[:, j * 128:(j + 1) * 128]

    @pl.when(pl.program_id(0) >= nused_ref[0])
    def _():
        y_ref[...] = jnp.zeros_like(y_ref)


def _experts(tile_lo, tile_hi, n_used, xs, w_gate, w_up, w_down):
    n_tiles = xs.shape[0] // ROW_TILE
    rows = lambda i, tlo, thi, nu: (jnp.maximum(jnp.minimum(i, nu[0] - 1), 0), 0)
    lo = lambda i, tlo, thi, nu: (tlo[i], 0, 0)
    hi = lambda i, tlo, thi, nu: (thi[i], 0, 0)
    up_spec = lambda m: pl.BlockSpec((1, D_MODEL, MOE_HIDDEN), m)
    dn_spec = lambda m: pl.BlockSpec((1, MOE_HIDDEN, D_MODEL), m)
    grid_spec = pltpu.PrefetchScalarGridSpec(
        num_scalar_prefetch=3,
        grid=(n_tiles,),
        in_specs=[pl.BlockSpec((ROW_TILE, XE_W), rows),
                  up_spec(lo), up_spec(lo), dn_spec(lo),
                  up_spec(hi), up_spec(hi), dn_spec(hi)],
        out_specs=pl.BlockSpec((ROW_TILE * ROW_CHUNKS, 128), lambda i, tlo, thi, nu: (i, 0)),
    )
    return pl.pallas_call(
        _experts_kernel,
        grid_spec=grid_spec,
        out_shape=jax.ShapeDtypeStruct((xs.shape[0] * ROW_CHUNKS, 128), F32),
        compiler_params=_cparams("arbitrary"),
        name="moe_experts",
    )(tile_lo, tile_hi, n_used, xs, w_gate, w_up, w_down, w_gate, w_up, w_down)


def _combine_kernel(route_ref, offs_ref, h_ref, y_ref, g_ref, out_ref, buf, sem, *, final_norm):
    base = pl.program_id(0) * TM

    def issue(t8, carry):
        t0 = pl.multiple_of(t8 * ISSUE_ROWS, ISSUE_ROWS)
        for r in range(ISSUE_ROWS):
            p = pl.multiple_of(_sorted_row(route_ref, offs_ref, base + t0 + r) * ROW_CHUNKS, ROW_CHUNKS)
            dst = pl.multiple_of((t0 + r) * ROW_CHUNKS, ROW_CHUNKS)
            pltpu.make_async_copy(y_ref.at[pl.ds(p, ROW_CHUNKS)], buf.at[pl.ds(dst, ROW_CHUNKS)], sem).start()
        return carry

    lax.fori_loop(0, TM // ISSUE_ROWS, issue, 0)
    pltpu.make_async_copy(y_ref.at[pl.ds(0, TM * ROW_CHUNKS)], buf, sem).wait()
    moe = jnp.concatenate([buf[pl.ds(j, TM, stride=ROW_CHUNKS), :] for j in range(ROW_CHUNKS)], axis=1)
    out = h_ref[...] + moe
    if final_norm:
        out = _rms(out) * g_ref[...]
    out_ref[...] = out


def _combine(route, offs, h, y, gain, final_norm):
    n = h.shape[0]
    grid_spec = pltpu.PrefetchScalarGridSpec(
        num_scalar_prefetch=2,
        grid=(n // TM,),
        in_specs=[pl.BlockSpec((TM, D_MODEL), lambda i, *_: (i, 0)),
                  pl.BlockSpec(memory_space=pl.ANY),
                  pl.BlockSpec((1, D_MODEL), lambda i, *_: (0, 0))],
        out_specs=pl.BlockSpec((TM, D_MODEL), lambda i, *_: (i, 0)),
        scratch_shapes=[pltpu.VMEM((TM * ROW_CHUNKS, 128), F32), pltpu.SemaphoreType.DMA(())],
    )
    return pl.pallas_call(
        functools.partial(_combine_kernel, final_norm=final_norm),
        grid_spec=grid_spec,
        out_shape=jax.ShapeDtypeStruct((n, D_MODEL), F32),
        compiler_params=_cparams("arbitrary"),
        name="moe_combine",
    )(route, offs, h, y, gain)


def _moe_layer(a, w_out, res, moe_norm, w_group, b_group, w_router, b_router, w_gate, w_up, w_down,
               tril, out_gain, final_norm, a_transposed):
    n = res.shape[0]
    unused = 128 - MOE_EXPERTS - MOE_GROUPS
    w_rt = jnp.concatenate([w_router, w_group, jnp.zeros((D_MODEL, unused), F32)], axis=1)
    b_rt = jnp.concatenate([b_router, b_group, jnp.zeros((unused,), F32)]).reshape(1, 128)
    h, xe, route_row, counts = _mix_out(a, w_out, res, moe_norm.reshape(1, D_MODEL), w_rt, b_rt, tril, a_transposed)

    n_tiles = n // ROW_TILE + MOE_CLASSES
    cnt = counts[0].astype(jnp.int32)
    tiles_c = (cnt + ROW_TILE - 1) // ROW_TILE
    tile_end = jnp.cumsum(tiles_c)
    offs = (tile_end - tiles_c) * ROW_TILE
    n_used = tile_end[-1]
    tile_ids = jnp.minimum(jnp.arange(n_tiles, dtype=jnp.int32), n_used - 1)
    tile_cls = jnp.sum((tile_end[None, 0:MOE_CLASSES] <= tile_ids[:, None]).astype(jnp.int32), axis=1)
    tile_cls = jnp.clip(tile_cls, 0, MOE_CLASSES - 1)
    tile_lo = jnp.asarray(_PAIR_LO)[tile_cls]
    tile_hi = jnp.asarray(_PAIR_HI)[tile_cls]
    route = route_row.reshape(n)

    n_used = n_used.reshape(1)
    xs = _dispatch(route, offs, cnt, n_used, xe, n_tiles * ROW_TILE)
    y = _experts(tile_lo, tile_hi, n_used, xs, w_gate.astype(BF16), w_up.astype(BF16),
                 w_down.astype(BF16))
    if out_gain is None:
        return h, y, route, offs
    return _combine(route, offs, h, y, out_gain.reshape(1, D_MODEL), final_norm)


def _rope_tables(pos, width):
    half = ROPE_DIM // 2
    inv_freq = jnp.power(jnp.float32(ROPE_THETA), -jnp.arange(half, dtype=F32) * (2.0 / ROPE_DIM))
    ang = pos.astype(F32)[:, None] * inv_freq[None, :]
    cos, sin = jnp.cos(ang), jnp.sin(ang)
    t = pos.shape[0]
    rest = NSA_HEAD_DIM - ROPE_DIM
    cos_t = jnp.concatenate([cos, cos, jnp.ones((t, rest), F32)], axis=1)
    sin_a = jnp.concatenate([-sin, jnp.zeros((t, half + rest), F32)], axis=1)
    sin_b = jnp.concatenate([jnp.zeros((t, half), F32), sin, jnp.zeros((t, rest), F32)], axis=1)
    rep = width // NSA_HEAD_DIM
    return jnp.tile(cos_t, (1, rep)), jnp.tile(sin_a, (1, rep)), jnp.tile(sin_b, (1, rep))


def _rope(x, cos_t, sin_a, sin_b):
    half = ROPE_DIM // 2
    parts = []
    for c in range(x.shape[1] // 128):
        xc = x[:, c * 128:(c + 1) * 128]
        parts.append(xc * cos_t + pltpu.roll(xc, 128 - half, axis=1) * sin_a + pltpu.roll(xc, half, axis=1) * sin_b)
    return parts[0] if len(parts) == 1 else jnp.concatenate(parts, axis=1)


def _rope_rows(x, cos_t, sin_a, sin_b):
    half = ROPE_DIM // 2
    reps = x.shape[0] // NSA_HEAD_DIM
    tile = lambda t: jnp.concatenate([t] * reps, axis=0)
    up = jnp.concatenate([x[half:], x[:half]], axis=0)
    down = jnp.concatenate([x[-half:], x[:-half]], axis=0)
    return x * tile(cos_t) + up * tile(sin_a) + down * tile(sin_b)


def _nsa_proj_kernel(route_ref, offs_ref, h_ref, y_ref, gq_ref, gkv_ref, wqt_ref, wkv_ref, wvt_ref,
                     cos_ref, sa_ref, sb_ref, cost_ref, sat_ref, sbt_ref,
                     h2_ref, qt_ref, gatet_ref, kc0_ref, kc1_ref, vc0_ref, vc1_ref, ks_ref, vst_ref, kw_ref, vwt_ref,
                     gbuf, sems):
    i = pl.program_id(0)

    def gather(tile, slot):
        base = tile * TM

        def issue(t8, carry):
            t0 = pl.multiple_of(t8 * ISSUE_ROWS, ISSUE_ROWS)
            for r in range(ISSUE_ROWS):
                p = pl.multiple_of(_sorted_row(route_ref, offs_ref, base + t0 + r) * ROW_CHUNKS, ROW_CHUNKS)
                dst = pl.multiple_of((t0 + r) * ROW_CHUNKS, ROW_CHUNKS)
                pltpu.make_async_copy(y_ref.at[pl.ds(p, ROW_CHUNKS)], gbuf.at[slot, pl.ds(dst, ROW_CHUNKS)],
                                      sems.at[slot]).start()
            return carry

        lax.fori_loop(0, TM // ISSUE_ROWS, issue, 0)

    @pl.when(i == 0)
    def _():
        gather(0, 0)

    @pl.when(i + 1 < pl.num_programs(0))
    def _():
        gather(i + 1, (i + 1) % 2)

    slot = i % 2
    pltpu.make_async_copy(y_ref.at[pl.ds(0, TM * ROW_CHUNKS)], gbuf.at[slot], sems.at[slot]).wait()
    moe = jnp.concatenate([gbuf[slot, pl.ds(j, TM, stride=ROW_CHUNKS), :] for j in range(ROW_CHUNKS)], axis=1)
    h2 = h_ref[...] + moe
    h2_ref[...] = h2
    r = _rms(h2)
    hq = (r * gq_ref[...]).astype(BF16)
    hk = (r * gkv_ref[...]).astype(BF16)
    qt = lax.dot_general(wqt_ref[0:1024, :], hq, _NT, preferred_element_type=F32)
    q_scale = NSA_HEAD_DIM ** -0.5 * LOG2_E
    qt_ref[...] = (_rope_rows(qt, cost_ref[...], sat_ref[...], sbt_ref[...]) * q_scale).astype(BF16)
    gatet_ref[...] = _sigmoid(lax.dot_general(wqt_ref[1024:1152, :], hq, _NT, preferred_element_type=F32))
    kc0_ref[...] = jnp.dot(hk, wkv_ref[:, 0:128], preferred_element_type=F32)
    kc1_ref[...] = jnp.dot(hk, wkv_ref[:, 128:256], preferred_element_type=F32)
    vc0_ref[...] = jnp.dot(hk, wkv_ref[:, 256:384], preferred_element_type=F32)
    vc1_ref[...] = jnp.dot(hk, wkv_ref[:, 384:512], preferred_element_type=F32)

    tm = TM
    cos_t, sin_a, sin_b = cos_ref[...], sa_ref[...], sb_ref[...]
    lane = lax.broadcasted_iota(jnp.int32, (tm, NSA_HEAD_DIM), 1)
    pos = (pl.program_id(0) % (SEQ // tm)) * tm + lax.broadcasted_iota(jnp.int32, (tm, NSA_HEAD_DIM), 0)
    blk_onehot = jnp.where(lane == (pos >> 6), 1.0, 0.0).astype(BF16)
    zeros = jnp.zeros((tm, NSA_HEAD_DIM), BF16)

    def store_keys(ref, val, aux):
        for g in range(NSA_GROUPS):
            ref[0, g] = jnp.concatenate([val[:, g * 64:(g + 1) * 64].astype(BF16), aux], axis=1)

    store_keys(ks_ref, _rope(jnp.dot(hk, wkv_ref[:, 512:768], preferred_element_type=F32), cos_t, sin_a, sin_b),
               blk_onehot)
    store_keys(kw_ref, _rope(jnp.dot(hk, wkv_ref[:, 1024:1280], preferred_element_type=F32), cos_t, sin_a, sin_b),
               zeros)

    row = lax.broadcasted_iota(jnp.int32, (V_ROWS - NSA_HEAD_DIM, tm), 0)
    ones_row = jnp.where(row == 0, 1.0, 0.0).astype(BF16)

    def store_values(ref, val_t):
        for g in range(NSA_GROUPS):
            ref[0, g] = jnp.concatenate([val_t[g * 64:(g + 1) * 64, :].astype(BF16), ones_row], axis=0)

    store_values(vst_ref, lax.dot_general(wvt_ref[0:256, :], hk, _NT, preferred_element_type=F32))
    store_values(vwt_ref, lax.dot_general(wvt_ref[256:512, :], hk, _NT, preferred_element_type=F32))


def _nsa_proj(route, offs, h, y, g_q, g_kv, w_qt, w_kv, w_vt, rope_tabs, rope_tabs_t, batch):
    n = h.shape[0]
    nblk = SEQ // TM
    row = lambda i, *_: (i, 0)
    col = lambda i, *_: (0, i)
    fixed = lambda i, *_: (0, 0)
    tab = lambda i, *_: (i % nblk, 0)
    tab_t = lambda i, *_: (0, i % nblk)
    key_spec = pl.BlockSpec((1, NSA_GROUPS, TM, KV_LANES), lambda i, *_: (i // nblk, 0, i % nblk, 0))
    key_shape = jax.ShapeDtypeStruct((batch, NSA_GROUPS, SEQ, KV_LANES), BF16)
    val_spec = pl.BlockSpec((1, NSA_GROUPS, V_ROWS, TM), lambda i, *_: (i // nblk, 0, 0, i % nblk))
    val_shape = jax.ShapeDtypeStruct((batch, NSA_GROUPS, V_ROWS, SEQ), BF16)
    raw_spec = pl.BlockSpec((TM, 128), row)
    raw_shape = jax.ShapeDtypeStruct((n, 128), F32)
    grid_spec = pltpu.PrefetchScalarGridSpec(
        num_scalar_prefetch=2,
        grid=(n // TM,),
        in_specs=[pl.BlockSpec((TM, D_MODEL), row),
                  pl.BlockSpec(memory_space=pl.ANY),
                  pl.BlockSpec((1, D_MODEL), fixed),
                  pl.BlockSpec((1, D_MODEL), fixed),
                  pl.BlockSpec((1152, D_MODEL), fixed),
                  pl.BlockSpec((D_MODEL, 1536), fixed),
                  pl.BlockSpec((512, D_MODEL), fixed),
                  pl.BlockSpec((TM, 128), tab), pl.BlockSpec((TM, 128), tab), pl.BlockSpec((TM, 128), tab),
                  pl.BlockSpec((NSA_HEAD_DIM, TM), tab_t), pl.BlockSpec((NSA_HEAD_DIM, TM), tab_t),
                  pl.BlockSpec((NSA_HEAD_DIM, TM), tab_t)],
        out_specs=[pl.BlockSpec((TM, D_MODEL), row),
                   pl.BlockSpec((1024, TM), col),
                   pl.BlockSpec((128, TM), col),
                   raw_spec, raw_spec, raw_spec, raw_spec,
                   key_spec, val_spec, key_spec, val_spec],
        scratch_shapes=[pltpu.VMEM((2, TM * ROW_CHUNKS, 128), F32), pltpu.SemaphoreType.DMA((2,))],
    )
    return pl.pallas_call(
        _nsa_proj_kernel,
        grid_spec=grid_spec,
        out_shape=[jax.ShapeDtypeStruct((n, D_MODEL), F32),
                   jax.ShapeDtypeStruct((1024, n), BF16),
                   jax.ShapeDtypeStruct((128, n), F32),
                   raw_shape, raw_shape, raw_shape, raw_shape,
                   key_shape, val_shape, key_shape, val_shape],
        compiler_params=_cparams("arbitrary"),
        name="nsa_proj",
    )(route, offs, h, y, g_q, g_kv, w_qt, w_kv, w_vt, *rope_tabs, *rope_tabs_t)


HALF_BLOCKS = SEQ // CMP_STRIDE
CMP_K = CMP_STRIDE * 256


def _compress_kernel(rk0_ref, rk1_ref, rv0_ref, rv1_ref, w1k_ref, w1v_ref, pek_ref, pev_ref, w2k_ref, w2v_ref,
                     cos_ref, sa_ref, sb_ref, kc_ref, vc_ref):
    lane = lax.broadcasted_iota(jnp.int32, (HALF_BLOCKS, 512), 1)
    first_half = (lane & 127) < 64

    def comp(raw_refs, w1_ref, pe_ref, w2_ref):
        x = jnp.concatenate([r[pl.ds(l, HALF_BLOCKS, stride=CMP_STRIDE), :]
                             for l in range(CMP_STRIDE) for r in raw_refs], axis=1).astype(BF16)
        r = jnp.dot(x, w1_ref[...], preferred_element_type=F32)
        rpe = jnp.dot(pe_ref[...].astype(BF16), w1_ref[...], preferred_element_type=F32)
        r = r + jnp.where(first_half, rpe[0:1, :], rpe[1:2, :])
        nxt = pltpu.roll(r, HALF_BLOCKS - 1, axis=0)
        nxt = jnp.concatenate([pltpu.roll(nxt[:, c * 128:(c + 1) * 128], 64, axis=1) for c in range(4)], axis=1)
        pre = r + nxt
        act = pre * _sigmoid(pre)
        return jnp.dot(act.astype(BF16), w2_ref[...], preferred_element_type=F32)

    kc = comp((rk0_ref, rk1_ref), w1k_ref, pek_ref, w2k_ref)
    kc = _rope(kc, cos_ref[...], sa_ref[...], sb_ref[...])
    vc = comp((rv0_ref, rv1_ref), w1v_ref, pev_ref, w2v_ref)
    for g in range(NSA_GROUPS):
        kc_ref[0, g] = kc[:, g * 64:(g + 1) * 64]
        vc_ref[0, g] = vc[:, g * 64:(g + 1) * 64]


def _compress_weights(pe, w1, w2):
    dh = NSA_HEAD_DIM
    w1r = w1.reshape(2, CMP_STRIDE, dh, dh)
    halves = jnp.concatenate([w1r[0], w1r[1]], axis=-1)
    rows = jnp.broadcast_to(halves[:, None], (CMP_STRIDE, NSA_GROUPS, dh, 2 * dh)).reshape(CMP_K, 2 * dh)
    row_group = (jnp.arange(CMP_K) // dh) % NSA_GROUPS
    w1_big = jnp.concatenate([jnp.where((row_group == g)[:, None], rows, 0.0) for g in range(NSA_GROUPS)],
                             axis=1).astype(BF16)
    pe_rows = jnp.broadcast_to(pe.reshape(2, CMP_STRIDE, 1, dh), (2, CMP_STRIDE, NSA_GROUPS, dh)).reshape(2, CMP_K)
    pe_rows = jnp.concatenate([pe_rows, jnp.zeros((6, CMP_K), F32)], axis=0)
    rows2 = jnp.tile(jnp.concatenate([w2, jnp.zeros_like(w2)], axis=0), (NSA_GROUPS, 1))
    row_group2 = jnp.arange(NSA_GROUPS * 2 * dh) // (2 * dh)
    w2_bd = jnp.concatenate([jnp.where((row_group2 == g)[:, None], rows2, 0.0) for g in range(NSA_GROUPS)],
                            axis=1).astype(BF16)
    return w1_big, pe_rows, w2_bd


def _compress(raw_k, raw_v, wk, wv, cmp_tabs, batch):
    fixed = lambda b: (0, 0)
    raw_spec = pl.BlockSpec((SEQ, 128), lambda b: (b, 0))
    out_spec = pl.BlockSpec((1, NSA_GROUPS, N_CMP_PAD, NSA_HEAD_DIM), lambda b: (b, 0, 0, 0))
    out_shape = jax.ShapeDtypeStruct((batch, NSA_GROUPS, N_CMP_PAD, NSA_HEAD_DIM), F32)
    return pl.pallas_call(
        _compress_kernel,
        grid=(batch,),
        in_specs=[raw_spec, raw_spec, raw_spec, raw_spec,
                  pl.BlockSpec((CMP_K, 512), fixed), pl.BlockSpec((CMP_K, 512), fixed),
                  pl.BlockSpec((8, CMP_K), fixed), pl.BlockSpec((8, CMP_K), fixed),
                  pl.BlockSpec((512, 256), fixed), pl.BlockSpec((512, 256), fixed),
                  pl.BlockSpec((N_CMP_PAD, 128), fixed), pl.BlockSpec((N_CMP_PAD, 128), fixed),
                  pl.BlockSpec((N_CMP_PAD, 128), fixed)],
        out_specs=[out_spec, out_spec],
        out_shape=[out_shape, out_shape],
        compiler_params=_cparams("parallel"),
        name="nsa_compress",
    )(*raw_k, *raw_v, wk[0], wv[0], wk[1], wv[1], wk[2], wv[2], *cmp_tabs)


def _nsa_attn_kernel(qt_ref, gatet_ref, kc_ref, vc_ref, ks_ref, vst_ref, kw_ref, vwt_ref, ovl_ref, out_ref,
                     qx_sc, ms_sc, as_sc, aw_sc):
    qb = pl.program_id(2)
    q0 = qb * Q_BLOCK
    Q, HG, DH = Q_BLOCK, NSA_GROUP_SIZE, NSA_HEAD_DIM
    groups = range(ATTN_GROUPS)

    def heads(x):
        return jnp.concatenate([x] * HG, axis=1)

    n_row = lax.broadcasted_iota(jnp.int32, (N_CMP_PAD, Q), 0)
    t_lane = q0 + lax.broadcasted_iota(jnp.int32, (N_CMP_PAD, Q), 1)
    valid_c = heads((n_row * CMP_STRIDE + CMP_BLOCK - 1 <= t_lane) & (n_row < N_CMP_PAD - 1))
    j_row = lax.broadcasted_iota(jnp.int32, (N_SEL, Q), 0)
    cur = (q0 + lax.broadcasted_iota(jnp.int32, (N_SEL, Q), 1)) >> 6
    forced = (j_row == 0) | (j_row == cur) | (j_row == cur - 1)
    o_c = []
    for gi in groups:
        q4 = qt_ref[gi * HG * DH:(gi + 1) * HG * DH, :]
        q_t = jnp.concatenate([q4[h * DH:(h + 1) * DH, :] for h in range(HG)], axis=1)
        s_c = jnp.dot(kc_ref[0, gi].astype(BF16), q_t, preferred_element_type=F32)
        s_c = jnp.where(valid_c, s_c, NEG_INF)
        m_c = jnp.max(s_c, axis=0, keepdims=True)
        e_c = jnp.where(valid_c, jnp.exp2(s_c - m_c), 0.0)
        l_c = jnp.sum(e_c, axis=0, keepdims=True)
        p_c = e_c * jnp.where(l_c > 0.0, 1.0 / l_c, 0.0)
        o_c.append(jnp.dot(vc_ref[0, gi].T.astype(BF16), p_c.astype(BF16), preferred_element_type=F32))

        p_sum = p_c[:, 0:Q] + p_c[:, Q:2 * Q] + p_c[:, 2 * Q:3 * Q] + p_c[:, 3 * Q:4 * Q]
        imp_t = jnp.dot(ovl_ref[...], p_sum, precision=HIGHEST, preferred_element_type=F32)[0:N_SEL]
        imp_t = jnp.where(forced, FORCED_SCORE, imp_t)
        imp_t = jnp.where(j_row > cur, NEG_INF, imp_t)
        cnt = jnp.zeros((N_SEL, Q), F32)
        for i in range(N_SEL):
            ri = imp_t[i:i + 1, :]
            cnt = cnt + jnp.where(ri > imp_t, 1.0, jnp.where((ri == imp_t) & (j_row > i), 1.0, 0.0))
        sel_bias = jnp.where((cnt < SEL_TOPK) & (j_row <= cur), 0.0, NEG_INF).astype(BF16)

        qx_sc[gi] = jnp.concatenate([q_t, heads(sel_bias), jnp.zeros((KV_LANES - DH - N_SEL, HG * Q), BF16)],
                                    axis=0)

    def finish(acc):
        return acc[0:DH] / acc[DEN_ROW:DEN_ROW + 1]

    ms_sc[...] = jnp.full(ms_sc.shape, NEG_INF, F32)
    as_sc[...] = jnp.zeros(as_sc.shape, F32)
    n_kt = (q0 + Q + SLC_TILE - 1) // SLC_TILE

    def slc_tile(kt, bias):
        start = pl.multiple_of(kt * SLC_TILE, SLC_TILE)
        for gi in groups:
            s = jnp.dot(ks_ref[0, gi, pl.ds(start, SLC_TILE), :], qx_sc[gi], preferred_element_type=F32)
            if bias is not None:
                s = s + bias
            m_old = ms_sc[gi, 0:1, :]
            m_new = jnp.maximum(m_old, jnp.max(s, axis=0, keepdims=True))
            p = jnp.exp2(s - m_new).astype(BF16)
            pv = jnp.dot(vst_ref[0, gi, :, pl.ds(start, SLC_TILE)], p, preferred_element_type=F32)
            as_sc[gi] = as_sc[gi] * jnp.exp2(m_old - m_new) + pv
            ms_sc[gi] = jnp.broadcast_to(m_new, ms_sc.shape[1:])

    def slc_full(kt, carry):
        slc_tile(kt, None)
        return carry

    lax.fori_loop(0, n_kt - 1, slc_full, 0)

    w0 = pl.multiple_of(jnp.maximum(q0 - WINDOW, 0), Q)
    s_w = [jnp.dot(kw_ref[0, gi, pl.ds(w0, WIN_KEYS), :], qx_sc[gi], preferred_element_type=F32) for gi in groups]

    c_s = lax.broadcasted_iota(jnp.int32, (SLC_TILE, Q), 0)
    r_s = lax.broadcasted_iota(jnp.int32, (SLC_TILE, Q), 1)
    slc_tile(n_kt - 1, heads(jnp.where((n_kt - 1) * SLC_TILE + c_s <= q0 + r_s, 0.0, NEG_INF)))

    c_w = lax.broadcasted_iota(jnp.int32, (Q, Q), 0)
    r_w = lax.broadcasted_iota(jnp.int32, (Q, Q), 1)
    n_slab = WIN_KEYS // Q
    slab_bias = []
    for j in range(n_slab):
        key = w0 + j * Q + c_w
        tok = q0 + r_w
        slab_bias.append(heads(jnp.where((key <= tok) & (key > tok - WINDOW), 0.0, NEG_INF)))
    for gi in groups:
        slabs = [s_w[gi][j * Q:(j + 1) * Q] + slab_bias[j] for j in range(n_slab)]
        top = slabs[0]
        for sj in slabs[1:]:
            top = jnp.maximum(top, sj)
        m = jnp.max(top, axis=0, keepdims=True)
        p = jnp.concatenate([jnp.exp2(sj - m) for sj in slabs], axis=0).astype(BF16)
        aw_sc[gi] = jnp.dot(vwt_ref[0, gi, :, pl.ds(w0, WIN_KEYS)], p, preferred_element_type=F32)

    for gi in groups:
        g = pl.program_id(1) * ATTN_GROUPS + gi

        def gate_row(branch):
            return jnp.concatenate([gatet_ref[pl.ds((g * HG + h) * 3 + branch, 1), :] for h in range(HG)], axis=1)

        o = gate_row(0) * o_c[gi] + gate_row(1) * finish(as_sc[gi]) + gate_row(2) * finish(aw_sc[gi])
        out_ref[gi * HG * DH:(gi + 1) * HG * DH, :] = jnp.concatenate(
            [o[:, h * Q:(h + 1) * Q] for h in range(HG)], axis=0).astype(BF16)


def _overlap_t():
    n = np.arange(N_CMP_PAD)
    j = np.arange(128)
    cmp_start = n * CMP_STRIDE
    cmp_end = cmp_start + CMP_BLOCK - 1
    sel_start = j * SEL_BLOCK
    ovl = ((cmp_start[None, :] <= sel_start[:, None] + SEL_BLOCK - 1) & (cmp_end[None, :] >= sel_start[:, None])
           & (j[:, None] < N_SEL) & (n[None, :] < N_CMP_PAD - 1))
    return jnp.asarray(ovl.astype(np.float32))


def _nsa_attn(q_t, gates_t, kc, vc, ks, vs_t, kw, vw_t, batch):
    n = q_t.shape[1]
    gs = ATTN_GROUPS
    qcol = lambda b, g, i: (g, b * N_QB + i)
    gcol = lambda b, g, i: (0, b * N_QB + i)
    kv = lambda b, g, i: (b, g, 0, 0)
    rows = gs * NSA_GROUP_SIZE * NSA_HEAD_DIM
    lanes = NSA_GROUP_SIZE * Q_BLOCK
    return pl.pallas_call(
        _nsa_attn_kernel,
        grid=(batch, NSA_GROUPS // gs, N_QB),
        in_specs=[pl.BlockSpec((rows, Q_BLOCK), qcol),
                  pl.BlockSpec((128, Q_BLOCK), gcol),
                  pl.BlockSpec((1, gs, N_CMP_PAD, NSA_HEAD_DIM), kv),
                  pl.BlockSpec((1, gs, N_CMP_PAD, NSA_HEAD_DIM), kv),
                  pl.BlockSpec((1, gs, SEQ, KV_LANES), kv),
                  pl.BlockSpec((1, gs, V_ROWS, SEQ), kv),
                  pl.BlockSpec((1, gs, SEQ, KV_LANES), kv),
                  pl.BlockSpec((1, gs, V_ROWS, SEQ), kv),
                  pl.BlockSpec((128, N_CMP_PAD), lambda b, g, i: (0, 0))],
        out_specs=pl.BlockSpec((rows, Q_BLOCK), qcol),
        out_shape=jax.ShapeDtypeStruct((NSA_HEADS * NSA_HEAD_DIM, n), BF16),
        scratch_shapes=[pltpu.VMEM((gs, KV_LANES, lanes), BF16),
                        pltpu.VMEM((gs, 8, lanes), F32),
                        pltpu.VMEM((gs, V_ROWS, lanes), F32),
                        pltpu.VMEM((gs, V_ROWS, lanes), F32)],
        compiler_params=_cparams("parallel", "parallel", "arbitrary"),
        name="nsa_attn",
    )(q_t, gates_t, kc, vc, ks, vs_t, kw, vw_t, _overlap_t())


def kernel(x, mlstm_norm, mlstm_w_in, mlstm_gate_bias, mlstm_head_norm, mlstm_w_out, kv_norm, kv_w, cmp_pe_k, cmp_w1_k, cmp_w2_k, cmp_pe_v, cmp_w1_v, cmp_w2_v, nsa_norm, nsa_w_q, nsa_w_out, moe_norm, moe_w_group, moe_b_group, moe_w_router, moe_b_router, moe_w_gate, moe_w_up, moe_w_down, final_norm):
    batch, seq, d = x.shape
    assert seq == SEQ and d == D_MODEL
    assert mlstm_norm.shape[0] == 1 and nsa_norm.shape[0] == 1 and moe_norm.shape[0] == 2
    n = batch * seq
    x2d = x.reshape(n, d)
    tril = jnp.tril(jnp.ones((TM, TM), F32)).astype(BF16)

    w_in = mlstm_w_in[0]
    w_gate = jnp.concatenate([w_in[:, 3072:3080], jnp.zeros((d, 120), F32)], axis=1)
    b_gate = jnp.concatenate([mlstm_gate_bias[0], jnp.zeros((120,), F32)]).reshape(1, 128)
    head_norm_cols = jnp.broadcast_to(mlstm_head_norm[0].reshape(d, 1), (d, 128))
    hs_t = _mlstm(x2d, mlstm_norm[0].reshape(1, d), w_in[:, 0:3072].T.astype(BF16), w_in[:, 512:1024].astype(BF16),
                  w_gate, b_gate, head_norm_cols, batch)
    h_pre, y, route, offs = _moe_layer(hs_t, mlstm_w_out[0].astype(BF16), x2d, moe_norm[0], moe_w_group[0],
                                       moe_b_group[0], moe_w_router[0], moe_b_router[0], moe_w_gate[0],
                                       moe_w_up[0], moe_w_down[0], tril, None, False, a_transposed=True)

    w_qt = jnp.concatenate([nsa_w_q[0].T, jnp.zeros((80, d), F32)], axis=0).astype(BF16)
    w_vt = jnp.concatenate([kv_w[:, 768:1024], kv_w[:, 1280:1536]], axis=1).T.astype(BF16)
    seq_tabs = _rope_tables(jnp.arange(SEQ), 128)
    seq_tabs_t = tuple(t[:, 0:NSA_HEAD_DIM].T for t in seq_tabs)
    h, q_t, gates_t, rk0, rk1, rv0, rv1, ks, vs_t, kw, vw_t = _nsa_proj(
        route, offs, h_pre, y, nsa_norm[0].reshape(1, d), kv_norm.reshape(1, d), w_qt, kv_w.astype(BF16), w_vt,
        seq_tabs, seq_tabs_t, batch)
    cmp_pos = jnp.arange(N_CMP_PAD) * CMP_STRIDE + CMP_BLOCK - 1
    kc, vc = _compress((rk0, rk1), (rv0, rv1), _compress_weights(cmp_pe_k, cmp_w1_k, cmp_w2_k),
                       _compress_weights(cmp_pe_v, cmp_w1_v, cmp_w2_v), _rope_tables(cmp_pos, 128), batch)
    att_t = _nsa_attn(q_t, gates_t, kc, vc, ks, vs_t, kw, vw_t, batch)
    out = _moe_layer(att_t, nsa_w_out[0].astype(BF16), h, moe_norm[1], moe_w_group[1], moe_b_group[1],
                     moe_w_router[1], moe_b_router[1], moe_w_gate[1], moe_w_up[1], moe_w_down[1],
                     tril, final_norm, True, a_transposed=True)
    return out.reshape(batch, seq, d)
```

```python
import numpy as np
import jax
import jax.numpy as jnp
from jax import lax
from jax.experimental import pallas as pl
from jax.experimental.pallas import tpu as pltpu

F32 = jnp.float32
BF16 = jnp.bfloat16
HIGHEST = lax.Precision.HIGHEST

D_MODEL = 1024
SEQ = 2048
RMS_EPS = 1e-6
NEG_INF = -1e30

MLSTM_HEADS = 4
MLSTM_V_DIM = 256
MLSTM_QK_DIM = 128
MLSTM_L = 256
GATE_SOFTCAP = 15.0

NSA_HEADS = 16
NSA_HEAD_DIM = 64
NSA_GROUPS = 4
NSA_GROUP_SIZE = 4
CMP_BLOCK = 32
CMP_STRIDE = 16
N_CMP_PAD = 128
SEL_BLOCK = 64
N_SEL = SEQ // SEL_BLOCK
SEL_TOPK = 16
WINDOW = 512
Q_BLOCK = 256
N_QB = SEQ // Q_BLOCK
WIN_KEYS = WINDOW + Q_BLOCK
SLC_TILE = 512
ATTN_GROUPS = 4
KV_LANES = 2 * NSA_HEAD_DIM
V_ROWS = NSA_HEAD_DIM + 16
DEN_ROW = NSA_HEAD_DIM
FORCED_SCORE = 1e6
ROPE_THETA = 500000.0
ROPE_DIM = 16

MOE_GROUPS = 4
MOE_PER_GROUP = 8
MOE_EXPERTS = 32
MOE_HIDDEN = 256
MOE_PAIRS = 28
MOE_CLASSES = MOE_GROUPS * MOE_PAIRS
ROW_TILE = 256
XE_W = D_MODEL + 128
LANE_WLO, LANE_WHI, LANE_ROUTE = 0, 1, 2
ROUTE_SHIFT = 16
ROUTE_ROWS = 40
ROW_CHUNKS = D_MODEL // 128
ISSUE_ROWS = 8

TM = 512
VMEM_LIMIT = 56 * 1024 * 1024

_NT = (((1,), (1,)), ((), ()))
_TN = (((0,), (0,)), ((), ()))


def _cparams(*sem):
    return pltpu.CompilerParams(dimension_semantics=sem, vmem_limit_bytes=VMEM_LIMIT)


def _rms(x):
    return x * lax.rsqrt(jnp.mean(x * x, axis=-1, keepdims=True) + RMS_EPS)


def _sigmoid(x):
    return 1.0 / (1.0 + jnp.exp(-x))


LOG2_E = 1.4426950408889634


def _split_bf16(x):
    hi = x.astype(BF16)
    return hi, (x - hi.astype(F32)).astype(BF16)


def _dot_split(x, w_hi, w_lo):
    x_hi, x_lo = _split_bf16(x)
    return (jnp.dot(x_hi, w_hi, preferred_element_type=F32) + jnp.dot(x_hi, w_lo, preferred_element_type=F32)
            + jnp.dot(x_lo, w_hi, preferred_element_type=F32))


def _mlstm_kernel(x_ref, g_ref, wt_ref, wk_ref, wgh_ref, wgl_ref, bg_ref, hn_ref, out_ref, ct_ref, n_ref, m_ref):
    L = MLSTM_L

    @pl.when(pl.program_id(1) == 0)
    def _():
        ct_ref[...] = jnp.zeros_like(ct_ref)
        n_ref[...] = jnp.zeros_like(n_ref)
        m_ref[...] = jnp.zeros_like(m_ref)

    hn = _rms(x_ref[...]) * g_ref[...]
    hb = hn.astype(BF16)
    q_t = lax.dot_general(wt_ref[0:512, :], hb, _NT, preferred_element_type=F32).astype(BF16)
    k_all = (jnp.dot(hb, wk_ref[...], preferred_element_type=F32) * (MLSTM_QK_DIM ** -0.5)).astype(BF16)
    v_t = lax.dot_general(wt_ref[1024:2048, :], hb, _NT, preferred_element_type=F32).astype(BF16)
    o_t = _sigmoid(lax.dot_general(wt_ref[2048:3072, :], hb, _NT, preferred_element_type=F32))
    gates = _dot_split(hn, wgh_ref[...], wgl_ref[...]) + bg_ref[...]
    gates = GATE_SOFTCAP * jnp.tanh(gates / GATE_SOFTCAP)
    lane = lax.broadcasted_iota(jnp.int32, gates.shape, 1)
    log_f = jnp.minimum(gates, 0.0) - jnp.log1p(jnp.exp(-jnp.abs(gates)))
    lg = jnp.where(lane < MLSTM_HEADS, gates, log_f)
    gr_pad = lg.T
    gr = gr_pad[0:8, :]

    row = lax.broadcasted_iota(jnp.int32, (L, L), 0)
    col = lax.broadcasted_iota(jnp.int32, (L, L), 1)
    causal_t = row <= col
    tril = (col <= row).astype(F32)
    b_row = lax.dot_general(gr, tril, _NT, precision=HIGHEST, preferred_element_type=F32)
    b_col = lax.dot_general(tril, gr_pad, _NT, precision=HIGHEST, preferred_element_type=F32)
    n_hi, n_lo = _split_bf16(n_ref[...])

    for h in range(MLSTM_HEADS):
        src_col = lg[:, h:h + 1] - b_col[:, 4 + h:5 + h]
        bf_row = b_row[4 + h:5 + h, :]
        m = m_ref[h:h + 1, 0:1]
        dmat = jnp.where(causal_t, bf_row + src_col, NEG_INF)
        m_inter = bf_row + m
        m_t = jnp.maximum(m_inter, jnp.max(dmat, axis=0, keepdims=True))
        w_intra = jnp.exp(dmat - m_t)
        w_inter = jnp.exp(m_inter - m_t)
        qt = q_t[h * 128:(h + 1) * 128, :]
        kh = k_all[:, h * 128:(h + 1) * 128]
        vt = v_t[h * 256:(h + 1) * 256, :]
        s = jnp.dot(kh, qt, preferred_element_type=F32) * w_intra
        c_old = ct_ref[h]
        num = (jnp.dot(vt, s.astype(BF16), preferred_element_type=F32)
               + w_inter * jnp.dot(c_old.astype(BF16), qt, preferred_element_type=F32))
        qn = (jnp.dot(n_hi, qt, preferred_element_type=F32) + jnp.dot(n_lo, qt, preferred_element_type=F32))[h:h + 1]
        den = jnp.sum(s, axis=0, keepdims=True) + w_inter * qn
        hh = num / jnp.maximum(jnp.abs(den), jnp.exp(-m_t))
        b_end = bf_row[:, L - 1:L]
        g = b_end + src_col
        m_new = jnp.maximum(b_end + m, jnp.max(g, axis=0, keepdims=True))
        ws = jnp.exp(g - m_new)
        decay = jnp.exp(b_end + m - m_new)
        kf = kh.astype(F32) * ws
        ct_ref[h] = decay * c_old + jnp.dot(vt, kf.astype(BF16), preferred_element_type=F32)
        n_ref[h:h + 1, :] = decay * n_ref[h:h + 1, :] + jnp.sum(kf, axis=0, keepdims=True)
        m_ref[h:h + 1, :] = jnp.broadcast_to(m_new, (1, 128))
        rows = slice(h * 256, (h + 1) * 256)
        hnorm = hh * lax.rsqrt(jnp.mean(hh * hh, axis=0, keepdims=True) + RMS_EPS)
        gain = jnp.concatenate([hn_ref[rows, :]] * (L // 128), axis=1)
        out_ref[rows, :] = (hnorm * gain * o_t[rows, :]).astype(BF16)


def _mlstm(x2d, norm_g, w_t, w_k, w_gate, b_gate, head_norm_cols, batch):
    n = x2d.shape[0]
    nblk = SEQ // MLSTM_L
    w_gate_hi, w_gate_lo = _split_bf16(w_gate)
    fixed = lambda b, j: (0, 0)
    return pl.pallas_call(
        _mlstm_kernel,
        grid=(batch, nblk),
        in_specs=[pl.BlockSpec((MLSTM_L, D_MODEL), lambda b, j: (b * nblk + j, 0)),
                  pl.BlockSpec((1, D_MODEL), fixed),
                  pl.BlockSpec((3072, D_MODEL), fixed),
                  pl.BlockSpec((D_MODEL, 512), fixed),
                  pl.BlockSpec((D_MODEL, 128), fixed),
                  pl.BlockSpec((D_MODEL, 128), fixed),
                  pl.BlockSpec((1, 128), fixed),
                  pl.BlockSpec((1024, 128), fixed)],
        out_specs=pl.BlockSpec((1024, MLSTM_L), lambda b, j: (0, b * nblk + j)),
        out_shape=jax.ShapeDtypeStruct((1024, n), BF16),
        scratch_shapes=[pltpu.VMEM((MLSTM_HEADS, MLSTM_V_DIM, MLSTM_QK_DIM), F32),
                        pltpu.VMEM((8, 128), F32),
                        pltpu.VMEM((8, 128), F32)],
        compiler_params=_cparams("parallel", "arbitrary"),
        name="mlstm",
    )(x2d, norm_g, w_t, w_k, w_gate_hi, w_gate_lo, b_gate, head_norm_cols)


def _pair_tables():
    lo, hi = [], []
    for g in range(MOE_GROUPS):
        for a in range(MOE_PER_GROUP):
            for b in range(a + 1, MOE_PER_GROUP):
                lo.append(g * MOE_PER_GROUP + a)
                hi.append(g * MOE_PER_GROUP + b)
    return np.asarray(lo, np.int32), np.asarray(hi, np.int32)


_PAIR_LO, _PAIR_HI = _pair_tables()


def _mix_out_kernel(a_ref, w_ref, res_ref, g_ref, wrh_ref, wrl_ref, br_ref, tril_ref, h_ref, xe_ref, route_ref, cnt_ref,
                    run_ref):
    @pl.when(pl.program_id(0) == 0)
    def _():
        run_ref[...] = jnp.zeros_like(run_ref)

    h = res_ref[...] + lax.dot_general(a_ref[...], w_ref[...], _TN, preferred_element_type=F32)
    h_ref[...] = h
    hn = _rms(h) * g_ref[...]
    xe_ref[:, 0:D_MODEL] = hn

    logits = _dot_split(hn, wrh_ref[...], wrl_ref[...]) + br_ref[...]
    lt = logits.T[0:ROUTE_ROWS, :]
    r_i = lax.broadcasted_iota(jnp.int32, lt.shape, 0)
    r = r_i.astype(F32)
    ninf = -jnp.inf
    is_g = (r_i >= MOE_EXPERTS) & (r_i < MOE_EXPERTS + MOE_GROUPS)
    glog = jnp.where(is_g, lt, ninf)
    gmax = jnp.max(glog, axis=0, keepdims=True)
    gidx = jnp.min(jnp.where(glog == gmax, r - MOE_EXPERTS, 99.0), axis=0, keepdims=True)
    pg_top = 1.0 / jnp.sum(jnp.exp(glog - gmax), axis=0, keepdims=True)
    in_grp = (r_i < MOE_EXPERTS) & ((r_i >> 3).astype(F32) == gidx)
    ev = jnp.where(in_grp, lt, ninf)
    v1 = jnp.max(ev, axis=0, keepdims=True)
    i1 = jnp.min(jnp.where(ev == v1, r, 999.0), axis=0, keepdims=True)
    ev2 = jnp.where(r == i1, ninf, ev)
    v2 = jnp.max(ev2, axis=0, keepdims=True)
    i2 = jnp.min(jnp.where(ev2 == v2, r, 999.0), axis=0, keepdims=True)
    e2 = jnp.exp(v2 - v1)
    w1 = pg_top / (1.0 + e2)
    w2 = pg_top * e2 / (1.0 + e2)
    first_lo = i1 < i2
    w_lo = jnp.where(first_lo, w1, w2)
    w_hi = jnp.where(first_lo, w2, w1)
    a = jnp.minimum(i1, i2) - MOE_PER_GROUP * gidx
    b = jnp.maximum(i1, i2) - MOE_PER_GROUP * gidx
    cls = gidx * MOE_PAIRS + a * (15.0 - a) * 0.5 + (b - a - 1.0)

    onehot = lax.broadcasted_iota(jnp.int32, (128, TM), 0).astype(F32) == cls
    prefix = lax.dot_general(onehot.astype(BF16), tril_ref[...], _NT, preferred_element_type=F32)
    run = run_ref[...]
    rank = jnp.sum(jnp.where(onehot, prefix - 1.0 + jnp.concatenate([run] * (TM // 128), axis=1), 0.0),
                   axis=0, keepdims=True)
    run_new = run + jnp.sum(jnp.where(onehot, 1.0, 0.0), axis=1, keepdims=True)
    run_ref[...] = run_new
    cnt_ref[...] = run_new.T[0:8, :]

    route = cls * float(2 ** ROUTE_SHIFT) + rank
    route_ref[...] = route.astype(jnp.int32)
    lanes = {LANE_WLO: w_lo, LANE_WHI: w_hi, LANE_ROUTE: route}
    meta_rows = jnp.concatenate([lanes[l] for l in range(len(lanes))]
                                + [jnp.zeros((128 - len(lanes), TM), F32)], axis=0)
    xe_ref[:, D_MODEL:XE_W] = meta_rows.T


def _mix_out(a, w, res, g_moe, w_rt, b_rt, tril):
    n = res.shape[0]
    kdim = w.shape[0]
    row = lambda i: (i, 0)
    fixed = lambda i: (0, 0)
    a_spec = pl.BlockSpec((kdim, TM), lambda i: (0, i))
    w_rt_hi, w_rt_lo = _split_bf16(w_rt)
    return pl.pallas_call(
        _mix_out_kernel,
        grid=(n // TM,),
        in_specs=[a_spec,
                  pl.BlockSpec((kdim, D_MODEL), fixed),
                  pl.BlockSpec((TM, D_MODEL), row),
                  pl.BlockSpec((1, D_MODEL), fixed),
                  pl.BlockSpec((D_MODEL, 128), fixed),
                  pl.BlockSpec((D_MODEL, 128), fixed),
                  pl.BlockSpec((1, 128), fixed),
                  pl.BlockSpec((TM, TM), fixed)],
        out_specs=[pl.BlockSpec((TM, D_MODEL), row),
                   pl.BlockSpec((TM, XE_W), row),
                   pl.BlockSpec((1, TM), lambda i: (0, i)),
                   pl.BlockSpec((8, 128), fixed)],
        out_shape=[jax.ShapeDtypeStruct((n, D_MODEL), F32),
                   jax.ShapeDtypeStruct((n, XE_W), F32),
                   jax.ShapeDtypeStruct((1, n), jnp.int32),
                   jax.ShapeDtypeStruct((8, 128), F32)],
        scratch_shapes=[pltpu.VMEM((128, 128), F32)],
        compiler_params=_cparams("arbitrary"),
        name="mix_out",
    )(a, w, res, g_moe, w_rt_hi, w_rt_lo, b_rt, tril)


def _sorted_row(route_ref, offs_ref, idx):
    r = route_ref[idx]
    return offs_ref[r >> ROUTE_SHIFT] + (r & (2 ** ROUTE_SHIFT - 1))


def _dispatch_kernel(route_ref, offs_ref, cnt_ref, nused_ref, xe_ref, xs_ref, zbuf, sem, zsem):
    i = pl.program_id(0)
    base = i * TM

    @pl.when(i == 0)
    def _():
        zbuf[...] = jnp.zeros_like(zbuf)

        def per_class(c, carry):
            cnt = cnt_ref[c]
            start = offs_ref[c] + cnt
            pad = (-cnt) & (ROW_TILE - 1)
            head = (-cnt) & 7
            blocks = (pad - head) >> 3

            def fill_row(r, inner):
                pltpu.make_async_copy(zbuf.at[pl.ds(0, 1)], xs_ref.at[pl.ds(start + r, 1)], zsem).start()
                return inner

            def fill_block(b, inner):
                row0 = pl.multiple_of(start + head + b * 8, 8)
                pltpu.make_async_copy(zbuf.at[pl.ds(0, 8)], xs_ref.at[pl.ds(row0, 8)], zsem).start()
                return inner

            def drain_row(r, inner):
                pltpu.make_async_copy(zbuf.at[pl.ds(0, 1)], xs_ref.at[pl.ds(0, 1)], zsem).wait()
                return inner

            def drain_block(b, inner):
                pltpu.make_async_copy(zbuf.at[pl.ds(0, 8)], xs_ref.at[pl.ds(0, 8)], zsem).wait()
                return inner

            lax.fori_loop(0, head, fill_row, 0)
            lax.fori_loop(0, blocks, fill_block, 0)
            lax.fori_loop(0, head, drain_row, 0)
            lax.fori_loop(0, blocks, drain_block, 0)
            return carry

        lax.fori_loop(0, MOE_CLASSES, per_class, 0)

        def tail(t, carry):
            row0 = pl.multiple_of(t * ROW_TILE, ROW_TILE)
            cp = pltpu.make_async_copy(zbuf, xs_ref.at[pl.ds(row0, ROW_TILE)], zsem)
            cp.start()
            cp.wait()
            return carry

        lax.fori_loop(nused_ref[0], xs_ref.shape[0] // ROW_TILE, tail, 0)

    def issue(t8, carry):
        t0 = pl.multiple_of(t8 * ISSUE_ROWS, ISSUE_ROWS)
        for r in range(ISSUE_ROWS):
            p = _sorted_row(route_ref, offs_ref, base + t0 + r)
            pltpu.make_async_copy(xe_ref.at[pl.ds(t0 + r, 1)], xs_ref.at[pl.ds(p, 1)], sem).start()
        return carry

    lax.fori_loop(0, TM // ISSUE_ROWS, issue, 0)
    pltpu.make_async_copy(xe_ref, xs_ref.at[pl.ds(0, TM)], sem).wait()


def _dispatch(route, offs, cnt, n_used, xe, n_rows):
    n = xe.shape[0]
    grid_spec = pltpu.PrefetchScalarGridSpec(
        num_scalar_prefetch=4,
        grid=(n // TM,),
        in_specs=[pl.BlockSpec((TM, XE_W), lambda i, *_: (i, 0))],
        out_specs=pl.BlockSpec(memory_space=pl.ANY),
        scratch_shapes=[pltpu.VMEM((ROW_TILE, XE_W), F32), pltpu.SemaphoreType.DMA(()),
                        pltpu.SemaphoreType.DMA(())],
    )
    return pl.pallas_call(
        _dispatch_kernel,
        grid_spec=grid_spec,
        out_shape=jax.ShapeDtypeStruct((n_rows, XE_W), F32),
        compiler_params=_cparams("arbitrary"),
        name="moe_dispatch",
    )(route, offs, cnt, n_used, xe)


def _experts_kernel(tlo_ref, thi_ref, nused_ref, xs_ref, wg_lo, wu_lo, wd_lo, wg_hi, wu_hi, wd_hi, y_ref):
    del tlo_ref, thi_ref

    @pl.when(pl.program_id(0) < nused_ref[0])
    def _():
        x = xs_ref[:, 0:D_MODEL].astype(BF16)

        def ffn(wg, wu, wd, w):
            a = jnp.dot(x, wg[0], preferred_element_type=F32)
            u = jnp.dot(x, wu[0], preferred_element_type=F32)
            hid = (a * _sigmoid(a)) * u * w
            return jnp.dot(hid.astype(BF16), wd[0], preferred_element_type=F32)

        w_lo = xs_ref[:, D_MODEL + LANE_WLO:D_MODEL + LANE_WLO + 1]
        w_hi = xs_ref[:, D_MODEL + LANE_WHI:D_MODEL + LANE_WHI + 1]
        y = ffn(wg_lo, wu_lo, wd_lo, w_lo) + ffn(wg_hi, wu_hi, wd_hi, w_hi)
        for j in range(ROW_CHUNKS):
            y_ref[pl.ds(j, ROW_TILE, stride=ROW_CHUNKS), :] = y[:, j * 128:(j + 1) * 128]

    @pl.when(pl.program_id(0) >= nused_ref[0])
    def _():
        y_ref[...] = jnp.zeros_like(y_ref)


def _experts(tile_lo, tile_hi, n_used, xs, w_gate, w_up, w_down, layer):
    n_tiles = xs.shape[0] // ROW_TILE
    rows = lambda i, tlo, thi, nu: (jnp.maximum(jnp.minimum(i, nu[0] - 1), 0), 0)
    lo = lambda i, tlo, thi, nu: (layer, tlo[i], 0, 0)
    hi = lambda i, tlo, thi, nu: (layer, thi[i], 0, 0)
    up_spec = lambda m: pl.BlockSpec((None, 1, D_MODEL, MOE_HIDDEN), m)
    dn_spec = lambda m: pl.BlockSpec((None, 1, MOE_HIDDEN, D_MODEL), m)
    grid_spec = pltpu.PrefetchScalarGridSpec(
        num_scalar_prefetch=3,
        grid=(n_tiles,),
        in_specs=[pl.BlockSpec((ROW_TILE, XE_W), rows),
                  up_spec(lo), up_spec(lo), dn_spec(lo),
                  up_spec(hi), up_spec(hi), dn_spec(hi)],
        out_specs=pl.BlockSpec((ROW_TILE * ROW_CHUNKS, 128), lambda i, tlo, thi, nu: (i, 0)),
    )
    return pl.pallas_call(
        _experts_kernel,
        grid_spec=grid_spec,
        out_shape=jax.ShapeDtypeStruct((xs.shape[0] * ROW_CHUNKS, 128), F32),
        compiler_params=_cparams("arbitrary"),
        name="moe_experts",
    )(tile_lo, tile_hi, n_used, xs, w_gate, w_up, w_down, w_gate, w_up, w_down)


def _combine_kernel(route_ref, offs_ref, h_ref, y_ref, g_ref, out_ref, buf, sem):
    base = pl.program_id(0) * TM

    def issue(t8, carry):
        t0 = pl.multiple_of(t8 * ISSUE_ROWS, ISSUE_ROWS)
        for r in range(ISSUE_ROWS):
            p = pl.multiple_of(_sorted_row(route_ref, offs_ref, base + t0 + r) * ROW_CHUNKS, ROW_CHUNKS)
            dst = pl.multiple_of((t0 + r) * ROW_CHUNKS, ROW_CHUNKS)
            pltpu.make_async_copy(y_ref.at[pl.ds(p, ROW_CHUNKS)], buf.at[pl.ds(dst, ROW_CHUNKS)], sem).start()
        return carry

    lax.fori_loop(0, TM // ISSUE_ROWS, issue, 0)
    pltpu.make_async_copy(y_ref.at[pl.ds(0, TM * ROW_CHUNKS)], buf, sem).wait()
    moe = jnp.concatenate([buf[pl.ds(j, TM, stride=ROW_CHUNKS), :] for j in range(ROW_CHUNKS)], axis=1)
    out_ref[...] = _rms(h_ref[...] + moe) * g_ref[...]


def _combine(route, offs, h, y, gain):
    n = h.shape[0]
    grid_spec = pltpu.PrefetchScalarGridSpec(
        num_scalar_prefetch=2,
        grid=(n // TM,),
        in_specs=[pl.BlockSpec((TM, D_MODEL), lambda i, *_: (i, 0)),
                  pl.BlockSpec(memory_space=pl.ANY),
                  pl.BlockSpec((1, D_MODEL), lambda i, *_: (0, 0))],
        out_specs=pl.BlockSpec((TM, D_MODEL), lambda i, *_: (i, 0)),
        scratch_shapes=[pltpu.VMEM((TM * ROW_CHUNKS, 128), F32), pltpu.SemaphoreType.DMA(())],
    )
    return pl.pallas_call(
        _combine_kernel,
        grid_spec=grid_spec,
        out_shape=jax.ShapeDtypeStruct((n, D_MODEL), F32),
        compiler_params=_cparams("arbitrary"),
        name="moe_combine",
    )(route, offs, h, y, gain)


def _moe_layer(a, w_out, res, moe_norm, w_group, b_group, w_router, b_router, w_gate, w_up, w_down, layer,
               tril, final_gain):
    n = res.shape[0]
    unused = 128 - MOE_EXPERTS - MOE_GROUPS
    w_rt = jnp.concatenate([w_router, w_group, jnp.zeros((D_MODEL, unused), F32)], axis=1)
    b_rt = jnp.concatenate([b_router, b_group, jnp.zeros((unused,), F32)]).reshape(1, 128)
    h, xe, route_row, counts = _mix_out(a, w_out, res, moe_norm.reshape(1, D_MODEL), w_rt, b_rt, tril)

    n_tiles = n // ROW_TILE + MOE_CLASSES
    cnt = counts[0].astype(jnp.int32)
    tiles_c = (cnt + ROW_TILE - 1) // ROW_TILE
    tile_end = jnp.cumsum(tiles_c)
    offs = (tile_end - tiles_c) * ROW_TILE
    n_used = tile_end[-1]
    tile_ids = jnp.minimum(jnp.arange(n_tiles, dtype=jnp.int32), n_used - 1)
    tile_cls = jnp.sum((tile_end[None, 0:MOE_CLASSES] <= tile_ids[:, None]).astype(jnp.int32), axis=1)
    tile_cls = jnp.clip(tile_cls, 0, MOE_CLASSES - 1)
    tile_lo = jnp.asarray(_PAIR_LO)[tile_cls]
    tile_hi = jnp.asarray(_PAIR_HI)[tile_cls]
    route = route_row.reshape(n)

    n_used = n_used.reshape(1)
    xs = _dispatch(route, offs, cnt, n_used, xe, n_tiles * ROW_TILE)
    y = _experts(tile_lo, tile_hi, n_used, xs, w_gate, w_up, w_down, layer)
    if final_gain is None:
        return h, y, route, offs
    return _combine(route, offs, h, y, final_gain.reshape(1, D_MODEL))


def _rope_tables(pos, width):
    half = ROPE_DIM // 2
    inv_freq = jnp.power(jnp.float32(ROPE_THETA), -jnp.arange(half, dtype=F32) * (2.0 / ROPE_DIM))
    ang = pos.astype(F32)[:, None] * inv_freq[None, :]
    cos, sin = jnp.cos(ang), jnp.sin(ang)
    t = pos.shape[0]
    rest = NSA_HEAD_DIM - ROPE_DIM
    cos_t = jnp.concatenate([cos, cos, jnp.ones((t, rest), F32)], axis=1)
    sin_a = jnp.concatenate([-sin, jnp.zeros((t, half + rest), F32)], axis=1)
    sin_b = jnp.concatenate([jnp.zeros((t, half), F32), sin, jnp.zeros((t, rest), F32)], axis=1)
    rep = width // NSA_HEAD_DIM
    return jnp.tile(cos_t, (1, rep)), jnp.tile(sin_a, (1, rep)), jnp.tile(sin_b, (1, rep))


def _rope(x, cos_t, sin_a, sin_b):
    half = ROPE_DIM // 2
    parts = []
    for c in range(x.shape[1] // 128):
        xc = x[:, c * 128:(c + 1) * 128]
        parts.append(xc * cos_t + pltpu.roll(xc, 128 - half, axis=1) * sin_a + pltpu.roll(xc, half, axis=1) * sin_b)
    return parts[0] if len(parts) == 1 else jnp.concatenate(parts, axis=1)


def _rope_rows(x, cos_t, sin_a, sin_b):
    half = ROPE_DIM // 2
    reps = x.shape[0] // NSA_HEAD_DIM
    tile = lambda t: jnp.concatenate([t] * reps, axis=0)
    up = jnp.concatenate([x[half:], x[:half]], axis=0)
    down = jnp.concatenate([x[-half:], x[:-half]], axis=0)
    return x * tile(cos_t) + up * tile(sin_a) + down * tile(sin_b)


def _nsa_proj_kernel(route_ref, offs_ref, h_ref, y_ref, gq_ref, gkv_ref, wqt_ref, wkv_ref, wvt_ref,
                     cos_ref, sa_ref, sb_ref, cost_ref, sat_ref, sbt_ref,
                     h2_ref, qt_ref, gatet_ref, kc0_ref, kc1_ref, vc0_ref, vc1_ref, ks_ref, vst_ref, kw_ref, vwt_ref,
                     gbuf, sems):
    i = pl.program_id(0)

    def gather(tile, slot):
        base = tile * TM

        def issue(t8, carry):
            t0 = pl.multiple_of(t8 * ISSUE_ROWS, ISSUE_ROWS)
            for r in range(ISSUE_ROWS):
                p = pl.multiple_of(_sorted_row(route_ref, offs_ref, base + t0 + r) * ROW_CHUNKS, ROW_CHUNKS)
                dst = pl.multiple_of((t0 + r) * ROW_CHUNKS, ROW_CHUNKS)
                pltpu.make_async_copy(y_ref.at[pl.ds(p, ROW_CHUNKS)], gbuf.at[slot, pl.ds(dst, ROW_CHUNKS)],
                                      sems.at[slot]).start()
            return carry

        lax.fori_loop(0, TM // ISSUE_ROWS, issue, 0)

    @pl.when(i == 0)
    def _():
        gather(0, 0)

    @pl.when(i + 1 < pl.num_programs(0))
    def _():
        gather(i + 1, (i + 1) % 2)

    slot = i % 2
    pltpu.make_async_copy(y_ref.at[pl.ds(0, TM * ROW_CHUNKS)], gbuf.at[slot], sems.at[slot]).wait()
    moe = jnp.concatenate([gbuf[slot, pl.ds(j, TM, stride=ROW_CHUNKS), :] for j in range(ROW_CHUNKS)], axis=1)
    h2 = h_ref[...] + moe
    h2_ref[...] = h2
    r = _rms(h2)
    hq = (r * gq_ref[...]).astype(BF16)
    hk = (r * gkv_ref[...]).astype(BF16)
    qt = lax.dot_general(wqt_ref[0:1024, :], hq, _NT, preferred_element_type=F32)
    q_scale = NSA_HEAD_DIM ** -0.5 * LOG2_E
    qt_ref[...] = (_rope_rows(qt, cost_ref[...], sat_ref[...], sbt_ref[...]) * q_scale).astype(BF16)
    gatet_ref[...] = _sigmoid(lax.dot_general(wqt_ref[1024:1152, :], hq, _NT, preferred_element_type=F32))
    kc0_ref[...] = jnp.dot(hk, wkv_ref[:, 0:128], preferred_element_type=F32)
    kc1_ref[...] = jnp.dot(hk, wkv_ref[:, 128:256], preferred_element_type=F32)
    vc0_ref[...] = jnp.dot(hk, wkv_ref[:, 256:384], preferred_element_type=F32)
    vc1_ref[...] = jnp.dot(hk, wkv_ref[:, 384:512], preferred_element_type=F32)

    tm = TM
    cos_t, sin_a, sin_b = cos_ref[...], sa_ref[...], sb_ref[...]
    lane = lax.broadcasted_iota(jnp.int32, (tm, NSA_HEAD_DIM), 1)
    pos = (pl.program_id(0) % (SEQ // tm)) * tm + lax.broadcasted_iota(jnp.int32, (tm, NSA_HEAD_DIM), 0)
    blk_onehot = jnp.where(lane == (pos >> 6), 1.0, 0.0).astype(BF16)
    zeros = jnp.zeros((tm, NSA_HEAD_DIM), BF16)

    def store_keys(ref, val, aux):
        for g in range(NSA_GROUPS):
            ref[0, g] = jnp.concatenate([val[:, g * 64:(g + 1) * 64].astype(BF16), aux], axis=1)

    store_keys(ks_ref, _rope(jnp.dot(hk, wkv_ref[:, 512:768], preferred_element_type=F32), cos_t, sin_a, sin_b),
               blk_onehot)
    store_keys(kw_ref, _rope(jnp.dot(hk, wkv_ref[:, 1024:1280], preferred_element_type=F32), cos_t, sin_a, sin_b),
               zeros)

    row = lax.broadcasted_iota(jnp.int32, (V_ROWS - NSA_HEAD_DIM, tm), 0)
    ones_row = jnp.where(row == 0, 1.0, 0.0).astype(BF16)

    def store_values(ref, val_t):
        for g in range(NSA_GROUPS):
            ref[0, g] = jnp.concatenate([val_t[g * 64:(g + 1) * 64, :].astype(BF16), ones_row], axis=0)

    store_values(vst_ref, lax.dot_general(wvt_ref[0:256, :], hk, _NT, preferred_element_type=F32))
    store_values(vwt_ref, lax.dot_general(wvt_ref[256:512, :], hk, _NT, preferred_element_type=F32))


def _nsa_proj(route, offs, h, y, g_q, g_kv, w_qt, w_kv, w_vt, rope_tabs, rope_tabs_t, batch):
    n = h.shape[0]
    nblk = SEQ // TM
    row = lambda i, *_: (i, 0)
    col = lambda i, *_: (0, i)
    fixed = lambda i, *_: (0, 0)
    tab = lambda i, *_: (i % nblk, 0)
    tab_t = lambda i, *_: (0, i % nblk)
    key_spec = pl.BlockSpec((1, NSA_GROUPS, TM, KV_LANES), lambda i, *_: (i // nblk, 0, i % nblk, 0))
    key_shape = jax.ShapeDtypeStruct((batch, NSA_GROUPS, SEQ, KV_LANES), BF16)
    val_spec = pl.BlockSpec((1, NSA_GROUPS, V_ROWS, TM), lambda i, *_: (i // nblk, 0, 0, i % nblk))
    val_shape = jax.ShapeDtypeStruct((batch, NSA_GROUPS, V_ROWS, SEQ), BF16)
    raw_spec = pl.BlockSpec((TM, 128), row)
    raw_shape = jax.ShapeDtypeStruct((n, 128), F32)
    grid_spec = pltpu.PrefetchScalarGridSpec(
        num_scalar_prefetch=2,
        grid=(n // TM,),
        in_specs=[pl.BlockSpec((TM, D_MODEL), row),
                  pl.BlockSpec(memory_space=pl.ANY),
                  pl.BlockSpec((1, D_MODEL), fixed),
                  pl.BlockSpec((1, D_MODEL), fixed),
                  pl.BlockSpec((1152, D_MODEL), fixed),
                  pl.BlockSpec((D_MODEL, 1536), fixed),
                  pl.BlockSpec((512, D_MODEL), fixed),
                  pl.BlockSpec((TM, 128), tab), pl.BlockSpec((TM, 128), tab), pl.BlockSpec((TM, 128), tab),
                  pl.BlockSpec((NSA_HEAD_DIM, TM), tab_t), pl.BlockSpec((NSA_HEAD_DIM, TM), tab_t),
                  pl.BlockSpec((NSA_HEAD_DIM, TM), tab_t)],
        out_specs=[pl.BlockSpec((TM, D_MODEL), row),
                   pl.BlockSpec((1024, TM), col),
                   pl.BlockSpec((128, TM), col),
                   raw_spec, raw_spec, raw_spec, raw_spec,
                   key_spec, val_spec, key_spec, val_spec],
        scratch_shapes=[pltpu.VMEM((2, TM * ROW_CHUNKS, 128), F32), pltpu.SemaphoreType.DMA((2,))],
    )
    return pl.pallas_call(
        _nsa_proj_kernel,
        grid_spec=grid_spec,
        out_shape=[jax.ShapeDtypeStruct((n, D_MODEL), F32),
                   jax.ShapeDtypeStruct((1024, n), BF16),
                   jax.ShapeDtypeStruct((128, n), F32),
                   raw_shape, raw_shape, raw_shape, raw_shape,
                   key_shape, val_shape, key_shape, val_shape],
        compiler_params=_cparams("arbitrary"),
        name="nsa_proj",
    )(route, offs, h, y, g_q, g_kv, w_qt, w_kv, w_vt, *rope_tabs, *rope_tabs_t)


HALF_BLOCKS = SEQ // CMP_STRIDE
CMP_K = CMP_STRIDE * 256


def _compress_kernel(rk0_ref, rk1_ref, rv0_ref, rv1_ref, w1k_ref, w1v_ref, pek_ref, pev_ref, w2k_ref, w2v_ref,
                     cos_ref, sa_ref, sb_ref, kc_ref, vc_ref):
    lane = lax.broadcasted_iota(jnp.int32, (HALF_BLOCKS, 512), 1)
    first_half = (lane & 127) < 64

    def comp(raw_refs, w1_ref, pe_ref, w2_ref):
        x = jnp.concatenate([r[pl.ds(l, HALF_BLOCKS, stride=CMP_STRIDE), :]
                             for l in range(CMP_STRIDE) for r in raw_refs], axis=1).astype(BF16)
        r = jnp.dot(x, w1_ref[...], preferred_element_type=F32)
        rpe = jnp.dot(pe_ref[...].astype(BF16), w1_ref[...], preferred_element_type=F32)
        r = r + jnp.where(first_half, rpe[0:1, :], rpe[1:2, :])
        nxt = pltpu.roll(r, HALF_BLOCKS - 1, axis=0)
        nxt = jnp.concatenate([pltpu.roll(nxt[:, c * 128:(c + 1) * 128], 64, axis=1) for c in range(4)], axis=1)
        pre = r + nxt
        act = pre * _sigmoid(pre)
        return jnp.dot(act.astype(BF16), w2_ref[...], preferred_element_type=F32)

    kc = comp((rk0_ref, rk1_ref), w1k_ref, pek_ref, w2k_ref)
    kc = _rope(kc, cos_ref[...], sa_ref[...], sb_ref[...])
    vc = comp((rv0_ref, rv1_ref), w1v_ref, pev_ref, w2v_ref)
    for g in range(NSA_GROUPS):
        kc_ref[0, g] = kc[:, g * 64:(g + 1) * 64]
        vc_ref[0, g] = vc[:, g * 64:(g + 1) * 64]


def _compress_weights(pe, w1, w2):
    dh = NSA_HEAD_DIM
    w1r = w1.reshape(2, CMP_STRIDE, dh, dh)
    halves = jnp.concatenate([w1r[0], w1r[1]], axis=-1)
    rows = jnp.broadcast_to(halves[:, None], (CMP_STRIDE, NSA_GROUPS, dh, 2 * dh)).reshape(CMP_K, 2 * dh)
    row_group = (jnp.arange(CMP_K) // dh) % NSA_GROUPS
    w1_big = jnp.concatenate([jnp.where((row_group == g)[:, None], rows, 0.0) for g in range(NSA_GROUPS)],
                             axis=1).astype(BF16)
    pe_rows = jnp.broadcast_to(pe.reshape(2, CMP_STRIDE, 1, dh), (2, CMP_STRIDE, NSA_GROUPS, dh)).reshape(2, CMP_K)
    pe_rows = jnp.concatenate([pe_rows, jnp.zeros((6, CMP_K), F32)], axis=0)
    rows2 = jnp.tile(jnp.concatenate([w2, jnp.zeros_like(w2)], axis=0), (NSA_GROUPS, 1))
    row_group2 = jnp.arange(NSA_GROUPS * 2 * dh) // (2 * dh)
    w2_bd = jnp.concatenate([jnp.where((row_group2 == g)[:, None], rows2, 0.0) for g in range(NSA_GROUPS)],
                            axis=1).astype(BF16)
    return w1_big, pe_rows, w2_bd


def _compress(raw_k, raw_v, wk, wv, cmp_tabs, batch):
    fixed = lambda b: (0, 0)
    raw_spec = pl.BlockSpec((SEQ, 128), lambda b: (b, 0))
    out_spec = pl.BlockSpec((1, NSA_GROUPS, N_CMP_PAD, NSA_HEAD_DIM), lambda b: (b, 0, 0, 0))
    out_shape = jax.ShapeDtypeStruct((batch, NSA_GROUPS, N_CMP_PAD, NSA_HEAD_DIM), F32)
    return pl.pallas_call(
        _compress_kernel,
        grid=(batch,),
        in_specs=[raw_spec, raw_spec, raw_spec, raw_spec,
                  pl.BlockSpec((CMP_K, 512), fixed), pl.BlockSpec((CMP_K, 512), fixed),
                  pl.BlockSpec((8, CMP_K), fixed), pl.BlockSpec((8, CMP_K), fixed),
                  pl.BlockSpec((512, 256), fixed), pl.BlockSpec((512, 256), fixed),
                  pl.BlockSpec((N_CMP_PAD, 128), fixed), pl.BlockSpec((N_CMP_PAD, 128), fixed),
                  pl.BlockSpec((N_CMP_PAD, 128), fixed)],
        out_specs=[out_spec, out_spec],
        out_shape=[out_shape, out_shape],
        compiler_params=_cparams("parallel"),
        name="nsa_compress",
    )(*raw_k, *raw_v, wk[0], wv[0], wk[1], wv[1], wk[2], wv[2], *cmp_tabs)


def _nsa_attn_kernel(qt_ref, gatet_ref, kc_ref, vc_ref, ks_ref, vst_ref, kw_ref, vwt_ref, ovl_ref, out_ref,
                     qx_sc, ms_sc, as_sc, aw_sc):
    qb = pl.program_id(2)
    q0 = qb * Q_BLOCK
    Q, HG, DH = Q_BLOCK, NSA_GROUP_SIZE, NSA_HEAD_DIM
    groups = range(ATTN_GROUPS)

    def heads(x):
        return jnp.concatenate([x] * HG, axis=1)

    n_row = lax.broadcasted_iota(jnp.int32, (N_CMP_PAD, Q), 0)
    t_lane = q0 + lax.broadcasted_iota(jnp.int32, (N_CMP_PAD, Q), 1)
    valid_c = heads((n_row * CMP_STRIDE + CMP_BLOCK - 1 <= t_lane) & (n_row < N_CMP_PAD - 1))
    j_row = lax.broadcasted_iota(jnp.int32, (N_SEL, Q), 0)
    cur = (q0 + lax.broadcasted_iota(jnp.int32, (N_SEL, Q), 1)) >> 6
    forced = (j_row == 0) | (j_row == cur) | (j_row == cur - 1)
    o_c = []
    for gi in groups:
        q4 = qt_ref[gi * HG * DH:(gi + 1) * HG * DH, :]
        q_t = jnp.concatenate([q4[h * DH:(h + 1) * DH, :] for h in range(HG)], axis=1)
        s_c = jnp.dot(kc_ref[0, gi].astype(BF16), q_t, preferred_element_type=F32)
        s_c = jnp.where(valid_c, s_c, NEG_INF)
        m_c = jnp.max(s_c, axis=0, keepdims=True)
        e_c = jnp.where(valid_c, jnp.exp2(s_c - m_c), 0.0)
        l_c = jnp.sum(e_c, axis=0, keepdims=True)
        p_c = e_c * jnp.where(l_c > 0.0, 1.0 / l_c, 0.0)
        o_c.append(jnp.dot(vc_ref[0, gi].T.astype(BF16), p_c.astype(BF16), preferred_element_type=F32))

        p_sum = p_c[:, 0:Q] + p_c[:, Q:2 * Q] + p_c[:, 2 * Q:3 * Q] + p_c[:, 3 * Q:4 * Q]
        imp_t = jnp.dot(ovl_ref[...], p_sum, precision=HIGHEST, preferred_element_type=F32)[0:N_SEL]
        imp_t = jnp.where(forced, FORCED_SCORE, imp_t)
        imp_t = jnp.where(j_row > cur, NEG_INF, imp_t)
        cnt = jnp.zeros((N_SEL, Q), F32)
        for i in range(N_SEL):
            ri = imp_t[i:i + 1, :]
            cnt = cnt + jnp.where(ri > imp_t, 1.0, jnp.where((ri == imp_t) & (j_row > i), 1.0, 0.0))
        sel_bias = jnp.where((cnt < SEL_TOPK) & (j_row <= cur), 0.0, NEG_INF).astype(BF16)

        qx_sc[gi] = jnp.concatenate([q_t, heads(sel_bias), jnp.zeros((KV_LANES - DH - N_SEL, HG * Q), BF16)],
                                    axis=0)

    def finish(acc):
        return acc[0:DH] / acc[DEN_ROW:DEN_ROW + 1]

    ms_sc[...] = jnp.full(ms_sc.shape, NEG_INF, F32)
    as_sc[...] = jnp.zeros(as_sc.shape, F32)
    n_kt = (q0 + Q + SLC_TILE - 1) // SLC_TILE

    def slc_tile(kt, bias):
        start = pl.multiple_of(kt * SLC_TILE, SLC_TILE)
        for gi in groups:
            s = jnp.dot(ks_ref[0, gi, pl.ds(start, SLC_TILE), :], qx_sc[gi], preferred_element_type=F32)
            if bias is not None:
                s = s + bias
            m_old = ms_sc[gi, 0:1, :]
            m_new = jnp.maximum(m_old, jnp.max(s, axis=0, keepdims=True))
            p = jnp.exp2(s - m_new).astype(BF16)
            pv = jnp.dot(vst_ref[0, gi, :, pl.ds(start, SLC_TILE)], p, preferred_element_type=F32)
            as_sc[gi] = as_sc[gi] * jnp.exp2(m_old - m_new) + pv
            ms_sc[gi] = jnp.broadcast_to(m_new, ms_sc.shape[1:])

    def slc_full(kt, carry):
        slc_tile(kt, None)
        return carry

    lax.fori_loop(0, n_kt - 1, slc_full, 0)

    w0 = pl.multiple_of(jnp.maximum(q0 - WINDOW, 0), Q)
    s_w = [jnp.dot(kw_ref[0, gi, pl.ds(w0, WIN_KEYS), :], qx_sc[gi], preferred_element_type=F32) for gi in groups]

    c_s = lax.broadcasted_iota(jnp.int32, (SLC_TILE, Q), 0)
    r_s = lax.broadcasted_iota(jnp.int32, (SLC_TILE, Q), 1)
    slc_tile(n_kt - 1, heads(jnp.where((n_kt - 1) * SLC_TILE + c_s <= q0 + r_s, 0.0, NEG_INF)))

    c_w = lax.broadcasted_iota(jnp.int32, (Q, Q), 0)
    r_w = lax.broadcasted_iota(jnp.int32, (Q, Q), 1)
    n_slab = WIN_KEYS // Q
    slab_bias = []
    for j in range(n_slab):
        key = w0 + j * Q + c_w
        tok = q0 + r_w
        slab_bias.append(heads(jnp.where((key <= tok) & (key > tok - WINDOW), 0.0, NEG_INF)))
    for gi in groups:
        slabs = [s_w[gi][j * Q:(j + 1) * Q] + slab_bias[j] for j in range(n_slab)]
        top = slabs[0]
        for sj in slabs[1:]:
            top = jnp.maximum(top, sj)
        m = jnp.max(top, axis=0, keepdims=True)
        p = jnp.concatenate([jnp.exp2(sj - m) for sj in slabs], axis=0).astype(BF16)
        aw_sc[gi] = jnp.dot(vwt_ref[0, gi, :, pl.ds(w0, WIN_KEYS)], p, preferred_element_type=F32)

    for gi in groups:
        g = pl.program_id(1) * ATTN_GROUPS + gi

        def gate_row(branch):
            return jnp.concatenate([gatet_ref[pl.ds((g * HG + h) * 3 + branch, 1), :] for h in range(HG)], axis=1)

        o = gate_row(0) * o_c[gi] + gate_row(1) * finish(as_sc[gi]) + gate_row(2) * finish(aw_sc[gi])
        out_ref[gi * HG * DH:(gi + 1) * HG * DH, :] = jnp.concatenate(
            [o[:, h * Q:(h + 1) * Q] for h in range(HG)], axis=0).astype(BF16)


def _overlap_t():
    n = np.arange(N_CMP_PAD)
    j = np.arange(128)
    cmp_start = n * CMP_STRIDE
    cmp_end = cmp_start + CMP_BLOCK - 1
    sel_start = j * SEL_BLOCK
    ovl = ((cmp_start[None, :] <= sel_start[:, None] + SEL_BLOCK - 1) & (cmp_end[None, :] >= sel_start[:, None])
           & (j[:, None] < N_SEL) & (n[None, :] < N_CMP_PAD - 1))
    return jnp.asarray(ovl.astype(np.float32))


def _nsa_attn(q_t, gates_t, kc, vc, ks, vs_t, kw, vw_t, batch):
    n = q_t.shape[1]
    gs = ATTN_GROUPS
    qcol = lambda b, g, i: (g, b * N_QB + i)
    gcol = lambda b, g, i: (0, b * N_QB + i)
    kv = lambda b, g, i: (b, g, 0, 0)
    rows = gs * NSA_GROUP_SIZE * NSA_HEAD_DIM
    lanes = NSA_GROUP_SIZE * Q_BLOCK
    return pl.pallas_call(
        _nsa_attn_kernel,
        grid=(batch, NSA_GROUPS // gs, N_QB),
        in_specs=[pl.BlockSpec((rows, Q_BLOCK), qcol),
                  pl.BlockSpec((128, Q_BLOCK), gcol),
                  pl.BlockSpec((1, gs, N_CMP_PAD, NSA_HEAD_DIM), kv),
                  pl.BlockSpec((1, gs, N_CMP_PAD, NSA_HEAD_DIM), kv),
                  pl.BlockSpec((1, gs, SEQ, KV_LANES), kv),
                  pl.BlockSpec((1, gs, V_ROWS, SEQ), kv),
                  pl.BlockSpec((1, gs, SEQ, KV_LANES), kv),
                  pl.BlockSpec((1, gs, V_ROWS, SEQ), kv),
                  pl.BlockSpec((128, N_CMP_PAD), lambda b, g, i: (0, 0))],
        out_specs=pl.BlockSpec((rows, Q_BLOCK), qcol),
        out_shape=jax.ShapeDtypeStruct((NSA_HEADS * NSA_HEAD_DIM, n), BF16),
        scratch_shapes=[pltpu.VMEM((gs, KV_LANES, lanes), BF16),
                        pltpu.VMEM((gs, 8, lanes), F32),
                        pltpu.VMEM((gs, V_ROWS, lanes), F32),
                        pltpu.VMEM((gs, V_ROWS, lanes), F32)],
        compiler_params=_cparams("parallel", "parallel", "arbitrary"),
        name="nsa_attn",
    )(q_t, gates_t, kc, vc, ks, vs_t, kw, vw_t, _overlap_t())


def kernel(x, mlstm_norm, mlstm_w_in, mlstm_gate_bias, mlstm_head_norm, mlstm_w_out, kv_norm, kv_w, cmp_pe_k, cmp_w1_k, cmp_w2_k, cmp_pe_v, cmp_w1_v, cmp_w2_v, nsa_norm, nsa_w_q, nsa_w_out, moe_norm, moe_w_group, moe_b_group, moe_w_router, moe_b_router, moe_w_gate, moe_w_up, moe_w_down, final_norm):
    batch, seq, d = x.shape
    assert seq == SEQ and d == D_MODEL
    assert mlstm_norm.shape[0] == 1 and nsa_norm.shape[0] == 1 and moe_norm.shape[0] == 2
    n = batch * seq
    x2d = x.reshape(n, d)
    tril = jnp.tril(jnp.ones((TM, TM), F32)).astype(BF16)

    w_in = mlstm_w_in[0]
    w_gate = jnp.concatenate([w_in[:, 3072:3080], jnp.zeros((d, 120), F32)], axis=1)
    b_gate = jnp.concatenate([mlstm_gate_bias[0], jnp.zeros((120,), F32)]).reshape(1, 128)
    head_norm_cols = jnp.broadcast_to(mlstm_head_norm[0].reshape(d, 1), (d, 128))
    hs_t = _mlstm(x2d, mlstm_norm[0].reshape(1, d), w_in[:, 0:3072].T.astype(BF16), w_in[:, 512:1024].astype(BF16),
                  w_gate, b_gate, head_norm_cols, batch)
    w_gate_all, w_up_all, w_down_all = moe_w_gate.astype(BF16), moe_w_up.astype(BF16), moe_w_down.astype(BF16)
    h_pre, y, route, offs = _moe_layer(hs_t, mlstm_w_out[0].astype(BF16), x2d, moe_norm[0], moe_w_group[0],
                                       moe_b_group[0], moe_w_router[0], moe_b_router[0], w_gate_all, w_up_all,
                                       w_down_all, 0, tril, None)

    w_qt = jnp.concatenate([nsa_w_q[0].T, jnp.zeros((80, d), F32)], axis=0).astype(BF16)
    w_vt = jnp.concatenate([kv_w[:, 768:1024], kv_w[:, 1280:1536]], axis=1).T.astype(BF16)
    seq_tabs = _rope_tables(jnp.arange(SEQ), 128)
    seq_tabs_t = tuple(t[:, 0:NSA_HEAD_DIM].T for t in seq_tabs)
    h, q_t, gates_t, rk0, rk1, rv0, rv1, ks, vs_t, kw, vw_t = _nsa_proj(
        route, offs, h_pre, y, nsa_norm[0].reshape(1, d), kv_norm.reshape(1, d), w_qt, kv_w.astype(BF16), w_vt,
        seq_tabs, seq_tabs_t, batch)
    cmp_pos = jnp.arange(N_CMP_PAD) * CMP_STRIDE + CMP_BLOCK - 1
    kc, vc = _compress((rk0, rk1), (rv0, rv1), _compress_weights(cmp_pe_k, cmp_w1_k, cmp_w2_k),
                       _compress_weights(cmp_pe_v, cmp_w1_v, cmp_w2_v), _rope_tables(cmp_pos, 128), batch)
    att_t = _nsa_attn(q_t, gates_t, kc, vc, ks, vs_t, kw, vw_t, batch)
    out = _moe_layer(att_t, nsa_w_out[0].astype(BF16), h, moe_norm[1], moe_w_group[1], moe_b_group[1],
                     moe_w_router[1], moe_b_router[1], w_gate_all, w_up_all, w_down_all, 1,
                     tril, final_norm)
    return out.reshape(batch, seq, d)
```

```python
import numpy as np
import jax
import jax.numpy as jnp
from jax import lax
from jax.experimental import pallas as pl
from jax.experimental.pallas import tpu as pltpu

F32 = jnp.float32
BF16 = jnp.bfloat16
HIGHEST = lax.Precision.HIGHEST

D_MODEL = 1024
SEQ = 2048
RMS_EPS = 1e-6
NEG_INF = -1e30

MLSTM_HEADS = 4
MLSTM_V_DIM = 256
MLSTM_QK_DIM = 128
MLSTM_L = 256
GATE_SOFTCAP = 15.0

NSA_HEADS = 16
NSA_HEAD_DIM = 64
NSA_GROUPS = 4
NSA_GROUP_SIZE = 4
CMP_BLOCK = 32
CMP_STRIDE = 16
N_CMP_PAD = 128
SEL_BLOCK = 64
N_SEL = SEQ // SEL_BLOCK
SEL_TOPK = 16
WINDOW = 512
Q_BLOCK = 256
N_QB = SEQ // Q_BLOCK
WIN_KEYS = WINDOW + Q_BLOCK
SLC_TILE = 512
ATTN_GROUPS = 4
KV_LANES = 2 * NSA_HEAD_DIM
V_ROWS = NSA_HEAD_DIM + 16
DEN_ROW = NSA_HEAD_DIM
FORCED_SCORE = 1e6
ROPE_THETA = 500000.0
ROPE_DIM = 16

MOE_GROUPS = 4
MOE_PER_GROUP = 8
MOE_EXPERTS = 32
MOE_HIDDEN = 256
MOE_PAIRS = 28
MOE_CLASSES = MOE_GROUPS * MOE_PAIRS
ROW_TILE = 256
XE_W = D_MODEL + 128
LANE_WLO, LANE_WHI, LANE_ROUTE = 0, 1, 2
ROUTE_SHIFT = 16
ROUTE_ROWS = 40
ROW_CHUNKS = D_MODEL // 128
ISSUE_ROWS = 8

TM = 512
VMEM_LIMIT = 56 * 1024 * 1024

_NT = (((1,), (1,)), ((), ()))
_TN = (((0,), (0,)), ((), ()))


def _cparams(*sem):
    return pltpu.CompilerParams(dimension_semantics=sem, vmem_limit_bytes=VMEM_LIMIT)


def _rms(x):
    return x * lax.rsqrt(jnp.mean(x * x, axis=-1, keepdims=True) + RMS_EPS)


def _sigmoid(x):
    return 1.0 / (1.0 + jnp.exp(-x))


LOG2_E = 1.4426950408889634


def _split_bf16(x):
    hi = x.astype(BF16)
    return hi, (x - hi.astype(F32)).astype(BF16)


def _dot_split(x, w_hi, w_lo):
    x_hi, x_lo = _split_bf16(x)
    return (jnp.dot(x_hi, w_hi, preferred_element_type=F32) + jnp.dot(x_hi, w_lo, preferred_element_type=F32)
            + jnp.dot(x_lo, w_hi, preferred_element_type=F32))


def _mlstm_kernel(x_ref, g_ref, wt_ref, wk_ref, wgh_ref, wgl_ref, bg_ref, hn_ref, out_ref, ct_ref, n_ref, m_ref):
    L = MLSTM_L

    @pl.when(pl.program_id(1) == 0)
    def _():
        ct_ref[...] = jnp.zeros_like(ct_ref)
        n_ref[...] = jnp.zeros_like(n_ref)
        m_ref[...] = jnp.zeros_like(m_ref)

    hn = _rms(x_ref[...]) * g_ref[...]
    hb = hn.astype(BF16)
    q_t = lax.dot_general(wt_ref[0:512, :], hb, _NT, preferred_element_type=F32).astype(BF16)
    k_all = (jnp.dot(hb, wk_ref[...], preferred_element_type=F32) * (MLSTM_QK_DIM ** -0.5)).astype(BF16)
    v_t = lax.dot_general(wt_ref[1024:2048, :], hb, _NT, preferred_element_type=F32).astype(BF16)
    o_t = _sigmoid(lax.dot_general(wt_ref[2048:3072, :], hb, _NT, preferred_element_type=F32))
    gates = _dot_split(hn, wgh_ref[...], wgl_ref[...]) + bg_ref[...]
    gates = GATE_SOFTCAP * jnp.tanh(gates / GATE_SOFTCAP)
    lane = lax.broadcasted_iota(jnp.int32, gates.shape, 1)
    log_f = jnp.minimum(gates, 0.0) - jnp.log1p(jnp.exp(-jnp.abs(gates)))
    lg = jnp.where(lane < MLSTM_HEADS, gates, log_f)
    gr_pad = lg.T
    gr = gr_pad[0:8, :]

    row = lax.broadcasted_iota(jnp.int32, (L, L), 0)
    col = lax.broadcasted_iota(jnp.int32, (L, L), 1)
    causal_t = row <= col
    tril = (col <= row).astype(F32)
    b_row = lax.dot_general(gr, tril, _NT, precision=HIGHEST, preferred_element_type=F32)
    b_col = lax.dot_general(tril, gr_pad, _NT, precision=HIGHEST, preferred_element_type=F32)
    n_hi, n_lo = _split_bf16(n_ref[...])

    for h in range(MLSTM_HEADS):
        src_col = lg[:, h:h + 1] - b_col[:, 4 + h:5 + h]
        bf_row = b_row[4 + h:5 + h, :]
        m = m_ref[h:h + 1, 0:1]
        dmat = jnp.where(causal_t, bf_row + src_col, NEG_INF)
        m_inter = bf_row + m
        m_t = jnp.maximum(m_inter, jnp.max(dmat, axis=0, keepdims=True))
        w_intra = jnp.exp(dmat - m_t)
        w_inter = jnp.exp(m_inter - m_t)
        qt = q_t[h * 128:(h + 1) * 128, :]
        kh = k_all[:, h * 128:(h + 1) * 128]
        vt = v_t[h * 256:(h + 1) * 256, :]
        s = jnp.dot(kh, qt, preferred_element_type=F32) * w_intra
        c_old = ct_ref[h]
        num = (jnp.dot(vt, s.astype(BF16), preferred_element_type=F32)
               + w_inter * jnp.dot(c_old.astype(BF16), qt, preferred_element_type=F32))
        qn = (jnp.dot(n_hi, qt, preferred_element_type=F32) + jnp.dot(n_lo, qt, preferred_element_type=F32))[h:h + 1]
        den = jnp.sum(s, axis=0, keepdims=True) + w_inter * qn
        hh = num / jnp.maximum(jnp.abs(den), jnp.exp(-m_t))
        b_end = bf_row[:, L - 1:L]
        g = b_end + src_col
        m_new = jnp.maximum(b_end + m, jnp.max(g, axis=0, keepdims=True))
        ws = jnp.exp(g - m_new)
        decay = jnp.exp(b_end + m - m_new)
        kf = kh.astype(F32) * ws
        ct_ref[h] = decay * c_old + jnp.dot(vt, kf.astype(BF16), preferred_element_type=F32)
        n_ref[h:h + 1, :] = decay * n_ref[h:h + 1, :] + jnp.sum(kf, axis=0, keepdims=True)
        m_ref[h:h + 1, :] = jnp.broadcast_to(m_new, (1, 128))
        rows = slice(h * 256, (h + 1) * 256)
        hnorm = hh * lax.rsqrt(jnp.mean(hh * hh, axis=0, keepdims=True) + RMS_EPS)
        gain = jnp.concatenate([hn_ref[rows, :]] * (L // 128), axis=1)
        out_ref[rows, :] = (hnorm * gain * o_t[rows, :]).astype(BF16)


def _mlstm(x2d, norm_g, w_t, w_k, w_gate, b_gate, head_norm_cols, batch):
    n = x2d.shape[0]
    nblk = SEQ // MLSTM_L
    w_gate_hi, w_gate_lo = _split_bf16(w_gate)
    fixed = lambda b, j: (0, 0)
    return pl.pallas_call(
        _mlstm_kernel,
        grid=(batch, nblk),
        in_specs=[pl.BlockSpec((MLSTM_L, D_MODEL), lambda b, j: (b * nblk + j, 0)),
                  pl.BlockSpec((1, D_MODEL), fixed),
                  pl.BlockSpec((3072, D_MODEL), fixed),
                  pl.BlockSpec((D_MODEL, 512), fixed),
                  pl.BlockSpec((D_MODEL, 128), fixed),
                  pl.BlockSpec((D_MODEL, 128), fixed),
                  pl.BlockSpec((1, 128), fixed),
                  pl.BlockSpec((1024, 128), fixed)],
        out_specs=pl.BlockSpec((1024, MLSTM_L), lambda b, j: (0, b * nblk + j)),
        out_shape=jax.ShapeDtypeStruct((1024, n), BF16),
        scratch_shapes=[pltpu.VMEM((MLSTM_HEADS, MLSTM_V_DIM, MLSTM_QK_DIM), F32),
                        pltpu.VMEM((8, 128), F32),
                        pltpu.VMEM((8, 128), F32)],
        compiler_params=_cparams("parallel", "arbitrary"),
        name="mlstm",
    )(x2d, norm_g, w_t, w_k, w_gate_hi, w_gate_lo, b_gate, head_norm_cols)


def _pair_tables():
    lo, hi = [], []
    for g in range(MOE_GROUPS):
        for a in range(MOE_PER_GROUP):
            for b in range(a + 1, MOE_PER_GROUP):
                lo.append(g * MOE_PER_GROUP + a)
                hi.append(g * MOE_PER_GROUP + b)
    return np.asarray(lo, np.int32), np.asarray(hi, np.int32)


_PAIR_LO, _PAIR_HI = _pair_tables()


def _mix_out_kernel(a_ref, w_ref, res_ref, g_ref, wrh_ref, wrl_ref, br_ref, tril_ref, h_ref, xe_ref, route_ref, cnt_ref,
                    run_ref):
    @pl.when(pl.program_id(0) == 0)
    def _():
        run_ref[...] = jnp.zeros_like(run_ref)

    h = res_ref[...] + lax.dot_general(a_ref[...], w_ref[...], _TN, preferred_element_type=F32)
    h_ref[...] = h
    hn = _rms(h) * g_ref[...]
    xe_ref[:, 0:D_MODEL] = hn

    logits = _dot_split(hn, wrh_ref[...], wrl_ref[...]) + br_ref[...]
    lt = logits.T[0:ROUTE_ROWS, :]
    r_i = lax.broadcasted_iota(jnp.int32, lt.shape, 0)
    r = r_i.astype(F32)
    ninf = -jnp.inf
    is_g = (r_i >= MOE_EXPERTS) & (r_i < MOE_EXPERTS + MOE_GROUPS)
    glog = jnp.where(is_g, lt, ninf)
    gmax = jnp.max(glog, axis=0, keepdims=True)
    gidx = jnp.min(jnp.where(glog == gmax, r - MOE_EXPERTS, 99.0), axis=0, keepdims=True)
    pg_top = 1.0 / jnp.sum(jnp.exp(glog - gmax), axis=0, keepdims=True)
    in_grp = (r_i < MOE_EXPERTS) & ((r_i >> 3).astype(F32) == gidx)
    ev = jnp.where(in_grp, lt, ninf)
    v1 = jnp.max(ev, axis=0, keepdims=True)
    i1 = jnp.min(jnp.where(ev == v1, r, 999.0), axis=0, keepdims=True)
    ev2 = jnp.where(r == i1, ninf, ev)
    v2 = jnp.max(ev2, axis=0, keepdims=True)
    i2 = jnp.min(jnp.where(ev2 == v2, r, 999.0), axis=0, keepdims=True)
    e2 = jnp.exp(v2 - v1)
    w1 = pg_top / (1.0 + e2)
    w2 = pg_top * e2 / (1.0 + e2)
    first_lo = i1 < i2
    w_lo = jnp.where(first_lo, w1, w2)
    w_hi = jnp.where(first_lo, w2, w1)
    a = jnp.minimum(i1, i2) - MOE_PER_GROUP * gidx
    b = jnp.maximum(i1, i2) - MOE_PER_GROUP * gidx
    cls = gidx * MOE_PAIRS + a * (15.0 - a) * 0.5 + (b - a - 1.0)

    onehot = lax.broadcasted_iota(jnp.int32, (128, TM), 0).astype(F32) == cls
    prefix = lax.dot_general(onehot.astype(BF16), tril_ref[...], _NT, preferred_element_type=F32)
    run = run_ref[...]
    rank = jnp.sum(jnp.where(onehot, prefix - 1.0 + jnp.concatenate([run] * (TM // 128), axis=1), 0.0),
                   axis=0, keepdims=True)
    run_new = run + jnp.sum(jnp.where(onehot, 1.0, 0.0), axis=1, keepdims=True)
    run_ref[...] = run_new
    cnt_ref[...] = run_new.T[0:8, :]

    route = cls * float(2 ** ROUTE_SHIFT) + rank
    route_ref[...] = route.astype(jnp.int32)
    lanes = {LANE_WLO: w_lo, LANE_WHI: w_hi, LANE_ROUTE: route}
    meta_rows = jnp.concatenate([lanes[l] for l in range(len(lanes))]
                                + [jnp.zeros((128 - len(lanes), TM), F32)], axis=0)
    xe_ref[:, D_MODEL:XE_W] = meta_rows.T


def _mix_out(a, w, res, g_moe, w_rt, b_rt, tril):
    n = res.shape[0]
    kdim = w.shape[0]
    row = lambda i: (i, 0)
    fixed = lambda i: (0, 0)
    a_spec = pl.BlockSpec((kdim, TM), lambda i: (0, i))
    w_rt_hi, w_rt_lo = _split_bf16(w_rt)
    return pl.pallas_call(
        _mix_out_kernel,
        grid=(n // TM,),
        in_specs=[a_spec,
                  pl.BlockSpec((kdim, D_MODEL), fixed),
                  pl.BlockSpec((TM, D_MODEL), row),
                  pl.BlockSpec((1, D_MODEL), fixed),
                  pl.BlockSpec((D_MODEL, 128), fixed),
                  pl.BlockSpec((D_MODEL, 128), fixed),
                  pl.BlockSpec((1, 128), fixed),
                  pl.BlockSpec((TM, TM), fixed)],
        out_specs=[pl.BlockSpec((TM, D_MODEL), row),
                   pl.BlockSpec((TM, XE_W), row),
                   pl.BlockSpec((1, TM), lambda i: (0, i)),
                   pl.BlockSpec((8, 128), fixed)],
        out_shape=[jax.ShapeDtypeStruct((n, D_MODEL), F32),
                   jax.ShapeDtypeStruct((n, XE_W), F32),
                   jax.ShapeDtypeStruct((1, n), jnp.int32),
                   jax.ShapeDtypeStruct((8, 128), F32)],
        scratch_shapes=[pltpu.VMEM((128, 128), F32)],
        compiler_params=_cparams("arbitrary"),
        name="mix_out",
    )(a, w, res, g_moe, w_rt_hi, w_rt_lo, b_rt, tril)


def _sorted_row(route_ref, offs_ref, idx):
    r = route_ref[idx]
    return offs_ref[r >> ROUTE_SHIFT] + (r & (2 ** ROUTE_SHIFT - 1))


def _dispatch_kernel(route_ref, offs_ref, cnt_ref, nused_ref, xe_ref, xs_ref, zbuf, sem, zsem):
    i = pl.program_id(0)
    base = i * TM

    @pl.when(i == 0)
    def _():
        zbuf[...] = jnp.zeros_like(zbuf)

        def per_class(c, carry):
            cnt = cnt_ref[c]
            start = offs_ref[c] + cnt
            pad = (-cnt) & (ROW_TILE - 1)
            head = (-cnt) & 7
            blocks = (pad - head) >> 3

            def fill_row(r, inner):
                pltpu.make_async_copy(zbuf.at[pl.ds(0, 1)], xs_ref.at[pl.ds(start + r, 1)], zsem).start()
                return inner

            def fill_block(b, inner):
                row0 = pl.multiple_of(start + head + b * 8, 8)
                pltpu.make_async_copy(zbuf.at[pl.ds(0, 8)], xs_ref.at[pl.ds(row0, 8)], zsem).start()
                return inner

            def drain_row(r, inner):
                pltpu.make_async_copy(zbuf.at[pl.ds(0, 1)], xs_ref.at[pl.ds(0, 1)], zsem).wait()
                return inner

            def drain_block(b, inner):
                pltpu.make_async_copy(zbuf.at[pl.ds(0, 8)], xs_ref.at[pl.ds(0, 8)], zsem).wait()
                return inner

            lax.fori_loop(0, head, fill_row, 0)
            lax.fori_loop(0, blocks, fill_block, 0)
            lax.fori_loop(0, head, drain_row, 0)
            lax.fori_loop(0, blocks, drain_block, 0)
            return carry

        lax.fori_loop(0, MOE_CLASSES, per_class, 0)

        def tail(t, carry):
            row0 = pl.multiple_of(t * ROW_TILE, ROW_TILE)
            cp = pltpu.make_async_copy(zbuf, xs_ref.at[pl.ds(row0, ROW_TILE)], zsem)
            cp.start()
            cp.wait()
            return carry

        lax.fori_loop(nused_ref[0], xs_ref.shape[0] // ROW_TILE, tail, 0)

    def issue(t8, carry):
        t0 = pl.multiple_of(t8 * ISSUE_ROWS, ISSUE_ROWS)
        for r in range(ISSUE_ROWS):
            p = _sorted_row(route_ref, offs_ref, base + t0 + r)
            pltpu.make_async_copy(xe_ref.at[pl.ds(t0 + r, 1)], xs_ref.at[pl.ds(p, 1)], sem).start(priority=r % 2)
        return carry

    lax.fori_loop(0, TM // ISSUE_ROWS, issue, 0)
    pltpu.make_async_copy(xe_ref, xs_ref.at[pl.ds(0, TM)], sem).wait()


def _dispatch(route, offs, cnt, n_used, xe, n_rows):
    n = xe.shape[0]
    grid_spec = pltpu.PrefetchScalarGridSpec(
        num_scalar_prefetch=4,
        grid=(n // TM,),
        in_specs=[pl.BlockSpec((TM, XE_W), lambda i, *_: (i, 0))],
        out_specs=pl.BlockSpec(memory_space=pl.ANY),
        scratch_shapes=[pltpu.VMEM((ROW_TILE, XE_W), F32), pltpu.SemaphoreType.DMA(()),
                        pltpu.SemaphoreType.DMA(())],
    )
    return pl.pallas_call(
        _dispatch_kernel,
        grid_spec=grid_spec,
        out_shape=jax.ShapeDtypeStruct((n_rows, XE_W), F32),
        compiler_params=_cparams("arbitrary"),
        name="moe_dispatch",
    )(route, offs, cnt, n_used, xe)


def _experts_kernel(tlo_ref, thi_ref, nused_ref, xs_ref, wg_lo, wu_lo, wd_lo, wg_hi, wu_hi, wd_hi, y_ref):
    del tlo_ref, thi_ref

    @pl.when(pl.program_id(0) < nused_ref[0])
    def _():
        x = xs_ref[:, 0:D_MODEL].astype(BF16)

        def ffn(wg, wu, wd, w):
            a = jnp.dot(x, wg[0], preferred_element_type=F32)
            u = jnp.dot(x, wu[0], preferred_element_type=F32)
            hid = (a * _sigmoid(a)) * u * w
            return jnp.dot(hid.astype(BF16), wd[0], preferred_element_type=F32)

        w_lo = xs_ref[:, D_MODEL + LANE_WLO:D_MODEL + LANE_WLO + 1]
        w_hi = xs_ref[:, D_MODEL + LANE_WHI:D_MODEL + LANE_WHI + 1]
        y = ffn(wg_lo, wu_lo, wd_lo, w_lo) + ffn(wg_hi, wu_hi, wd_hi, w_hi)
        for j in range(ROW_CHUNKS):
            y_ref[pl.ds(j, ROW_TILE, stride=ROW_CHUNKS), :] = y[:, j * 128:(j + 1) * 128]

    @pl.when(pl.program_id(0) >= nused_ref[0])
    def _():
        y_ref[...] = jnp.zeros_like(y_ref)


def _experts(tile_lo, tile_hi, n_used, xs, w_gate, w_up, w_down, layer):
    n_tiles = xs.shape[0] // ROW_TILE
    rows = lambda i, tlo, thi, nu: (jnp.maximum(jnp.minimum(i, nu[0] - 1), 0), 0)
    lo = lambda i, tlo, thi, nu: (layer, tlo[i], 0, 0)
    hi = lambda i, tlo, thi, nu: (layer, thi[i], 0, 0)
    up_spec = lambda m: pl.BlockSpec((None, 1, D_MODEL, MOE_HIDDEN), m)
    dn_spec = lambda m: pl.BlockSpec((None, 1, MOE_HIDDEN, D_MODEL), m)
    grid_spec = pltpu.PrefetchScalarGridSpec(
        num_scalar_prefetch=3,
        grid=(n_tiles,),
        in_specs=[pl.BlockSpec((ROW_TILE, XE_W), rows),
                  up_spec(lo), up_spec(lo), dn_spec(lo),
                  up_spec(hi), up_spec(hi), dn_spec(hi)],
        out_specs=pl.BlockSpec((ROW_TILE * ROW_CHUNKS, 128), lambda i, tlo, thi, nu: (i, 0)),
    )
    return pl.pallas_call(
        _experts_kernel,
        grid_spec=grid_spec,
        out_shape=jax.ShapeDtypeStruct((xs.shape[0] * ROW_CHUNKS, 128), F32),
        compiler_params=_cparams("arbitrary"),
        name="moe_experts",
    )(tile_lo, tile_hi, n_used, xs, w_gate, w_up, w_down, w_gate, w_up, w_down)


def _combine_kernel(route_ref, offs_ref, h_ref, y_ref, g_ref, out_ref, buf, sem):
    base = pl.program_id(0) * TM

    def issue(t8, carry):
        t0 = pl.multiple_of(t8 * ISSUE_ROWS, ISSUE_ROWS)
        for r in range(ISSUE_ROWS):
            p = pl.multiple_of(_sorted_row(route_ref, offs_ref, base + t0 + r) * ROW_CHUNKS, ROW_CHUNKS)
            dst = pl.multiple_of((t0 + r) * ROW_CHUNKS, ROW_CHUNKS)
            pltpu.make_async_copy(y_ref.at[pl.ds(p, ROW_CHUNKS)], buf.at[pl.ds(dst, ROW_CHUNKS)], sem).start(priority=r % 2)
        return carry

    lax.fori_loop(0, TM // ISSUE_ROWS, issue, 0)
    pltpu.make_async_copy(y_ref.at[pl.ds(0, TM * ROW_CHUNKS)], buf, sem).wait()
    moe = jnp.concatenate([buf[pl.ds(j, TM, stride=ROW_CHUNKS), :] for j in range(ROW_CHUNKS)], axis=1)
    out_ref[...] = _rms(h_ref[...] + moe) * g_ref[...]


def _combine(route, offs, h, y, gain):
    n = h.shape[0]
    grid_spec = pltpu.PrefetchScalarGridSpec(
        num_scalar_prefetch=2,
        grid=(n // TM,),
        in_specs=[pl.BlockSpec((TM, D_MODEL), lambda i, *_: (i, 0)),
                  pl.BlockSpec(memory_space=pl.ANY),
                  pl.BlockSpec((1, D_MODEL), lambda i, *_: (0, 0))],
        out_specs=pl.BlockSpec((TM, D_MODEL), lambda i, *_: (i, 0)),
        scratch_shapes=[pltpu.VMEM((TM * ROW_CHUNKS, 128), F32), pltpu.SemaphoreType.DMA(())],
    )
    return pl.pallas_call(
        _combine_kernel,
        grid_spec=grid_spec,
        out_shape=jax.ShapeDtypeStruct((n, D_MODEL), F32),
        compiler_params=_cparams("arbitrary"),
        name="moe_combine",
    )(route, offs, h, y, gain)


def _moe_layer(a, w_out, res, moe_norm, w_group, b_group, w_router, b_router, w_gate, w_up, w_down, layer,
               tril, final_gain):
    n = res.shape[0]
    unused = 128 - MOE_EXPERTS - MOE_GROUPS
    w_rt = jnp.concatenate([w_router, w_group, jnp.zeros((D_MODEL, unused), F32)], axis=1)
    b_rt = jnp.concatenate([b_router, b_group, jnp.zeros((unused,), F32)]).reshape(1, 128)
    h, xe, route_row, counts = _mix_out(a, w_out, res, moe_norm.reshape(1, D_MODEL), w_rt, b_rt, tril)

    n_tiles = n // ROW_TILE + MOE_CLASSES
    cnt = counts[0].astype(jnp.int32)
    tiles_c = (cnt + ROW_TILE - 1) // ROW_TILE
    tile_end = jnp.cumsum(tiles_c)
    offs = (tile_end - tiles_c) * ROW_TILE
    n_used = tile_end[-1]
    tile_ids = jnp.minimum(jnp.arange(n_tiles, dtype=jnp.int32), n_used - 1)
    tile_cls = jnp.sum((tile_end[None, 0:MOE_CLASSES] <= tile_ids[:, None]).astype(jnp.int32), axis=1)
    tile_cls = jnp.clip(tile_cls, 0, MOE_CLASSES - 1)
    tile_lo = jnp.asarray(_PAIR_LO)[tile_cls]
    tile_hi = jnp.asarray(_PAIR_HI)[tile_cls]
    route = route_row.reshape(n)

    n_used = n_used.reshape(1)
    xs = _dispatch(route, offs, cnt, n_used, xe, n_tiles * ROW_TILE)
    y = _experts(tile_lo, tile_hi, n_used, xs, w_gate, w_up, w_down, layer)
    if final_gain is None:
        return h, y, route, offs
    return _combine(route, offs, h, y, final_gain.reshape(1, D_MODEL))


def _rope_tables(pos, width):
    half = ROPE_DIM // 2
    inv_freq = jnp.power(jnp.float32(ROPE_THETA), -jnp.arange(half, dtype=F32) * (2.0 / ROPE_DIM))
    ang = pos.astype(F32)[:, None] * inv_freq[None, :]
    cos, sin = jnp.cos(ang), jnp.sin(ang)
    t = pos.shape[0]
    rest = NSA_HEAD_DIM - ROPE_DIM
    cos_t = jnp.concatenate([cos, cos, jnp.ones((t, rest), F32)], axis=1)
    sin_a = jnp.concatenate([-sin, jnp.zeros((t, half + rest), F32)], axis=1)
    sin_b = jnp.concatenate([jnp.zeros((t, half), F32), sin, jnp.zeros((t, rest), F32)], axis=1)
    rep = width // NSA_HEAD_DIM
    return jnp.tile(cos_t, (1, rep)), jnp.tile(sin_a, (1, rep)), jnp.tile(sin_b, (1, rep))


def _rope(x, cos_t, sin_a, sin_b):
    half = ROPE_DIM // 2
    parts = []
    for c in range(x.shape[1] // 128):
        xc = x[:, c * 128:(c + 1) * 128]
        parts.append(xc * cos_t + pltpu.roll(xc, 128 - half, axis=1) * sin_a + pltpu.roll(xc, half, axis=1) * sin_b)
    return parts[0] if len(parts) == 1 else jnp.concatenate(parts, axis=1)


def _rope_rows(x, cos_t, sin_a, sin_b):
    half = ROPE_DIM // 2
    reps = x.shape[0] // NSA_HEAD_DIM
    tile = lambda t: jnp.concatenate([t] * reps, axis=0)
    up = jnp.concatenate([x[half:], x[:half]], axis=0)
    down = jnp.concatenate([x[-half:], x[:-half]], axis=0)
    return x * tile(cos_t) + up * tile(sin_a) + down * tile(sin_b)


def _nsa_proj_kernel(route_ref, offs_ref, h_ref, y_ref, gq_ref, gkv_ref, wqt_ref, wkv_ref, wvt_ref,
                     cos_ref, sa_ref, sb_ref, cost_ref, sat_ref, sbt_ref,
                     h2_ref, qt_ref, gatet_ref, kc0_ref, kc1_ref, vc0_ref, vc1_ref, ks_ref, vst_ref, kw_ref, vwt_ref,
                     gbuf, sems):
    i = pl.program_id(0)

    def gather(tile, slot):
        base = tile * TM

        def issue(t8, carry):
            t0 = pl.multiple_of(t8 * ISSUE_ROWS, ISSUE_ROWS)
            for r in range(ISSUE_ROWS):
                p = pl.multiple_of(_sorted_row(route_ref, offs_ref, base + t0 + r) * ROW_CHUNKS, ROW_CHUNKS)
                dst = pl.multiple_of((t0 + r) * ROW_CHUNKS, ROW_CHUNKS)
                pltpu.make_async_copy(y_ref.at[pl.ds(p, ROW_CHUNKS)], gbuf.at[slot, pl.ds(dst, ROW_CHUNKS)],
                                      sems.at[slot]).start(priority=r % 2)
            return carry

        lax.fori_loop(0, TM // ISSUE_ROWS, issue, 0)

    @pl.when(i == 0)
    def _():
        gather(0, 0)

    @pl.when(i + 1 < pl.num_programs(0))
    def _():
        gather(i + 1, (i + 1) % 2)

    slot = i % 2
    pltpu.make_async_copy(y_ref.at[pl.ds(0, TM * ROW_CHUNKS)], gbuf.at[slot], sems.at[slot]).wait()
    moe = jnp.concatenate([gbuf[slot, pl.ds(j, TM, stride=ROW_CHUNKS), :] for j in range(ROW_CHUNKS)], axis=1)
    h2 = h_ref[...] + moe
    h2_ref[...] = h2
    r = _rms(h2)
    hq = (r * gq_ref[...]).astype(BF16)
    hk = (r * gkv_ref[...]).astype(BF16)
    qt = lax.dot_general(wqt_ref[0:1024, :], hq, _NT, preferred_element_type=F32)
    q_scale = NSA_HEAD_DIM ** -0.5 * LOG2_E
    qt_ref[...] = (_rope_rows(qt, cost_ref[...], sat_ref[...], sbt_ref[...]) * q_scale).astype(BF16)
    gatet_ref[...] = _sigmoid(lax.dot_general(wqt_ref[1024:1152, :], hq, _NT, preferred_element_type=F32))
    kc0_ref[...] = jnp.dot(hk, wkv_ref[:, 0:128], preferred_element_type=F32)
    kc1_ref[...] = jnp.dot(hk, wkv_ref[:, 128:256], preferred_element_type=F32)
    vc0_ref[...] = jnp.dot(hk, wkv_ref[:, 256:384], preferred_element_type=F32)
    vc1_ref[...] = jnp.dot(hk, wkv_ref[:, 384:512], preferred_element_type=F32)

    tm = TM
    cos_t, sin_a, sin_b = cos_ref[...], sa_ref[...], sb_ref[...]
    lane = lax.broadcasted_iota(jnp.int32, (tm, NSA_HEAD_DIM), 1)
    pos = (pl.program_id(0) % (SEQ // tm)) * tm + lax.broadcasted_iota(jnp.int32, (tm, NSA_HEAD_DIM), 0)
    blk_onehot = jnp.where(lane == (pos >> 6), 1.0, 0.0).astype(BF16)
    zeros = jnp.zeros((tm, NSA_HEAD_DIM), BF16)

    def store_keys(ref, val, aux):
        for g in range(NSA_GROUPS):
            ref[0, g] = jnp.concatenate([val[:, g * 64:(g + 1) * 64].astype(BF16), aux], axis=1)

    store_keys(ks_ref, _rope(jnp.dot(hk, wkv_ref[:, 512:768], preferred_element_type=F32), cos_t, sin_a, sin_b),
               blk_onehot)
    store_keys(kw_ref, _rope(jnp.dot(hk, wkv_ref[:, 1024:1280], preferred_element_type=F32), cos_t, sin_a, sin_b),
               zeros)

    row = lax.broadcasted_iota(jnp.int32, (V_ROWS - NSA_HEAD_DIM, tm), 0)
    ones_row = jnp.where(row == 0, 1.0, 0.0).astype(BF16)

    def store_values(ref, val_t):
        for g in range(NSA_GROUPS):
            ref[0, g] = jnp.concatenate([val_t[g * 64:(g + 1) * 64, :].astype(BF16), ones_row], axis=0)

    store_values(vst_ref, lax.dot_general(wvt_ref[0:256, :], hk, _NT, preferred_element_type=F32))
    store_values(vwt_ref, lax.dot_general(wvt_ref[256:512, :], hk, _NT, preferred_element_type=F32))


def _nsa_proj(route, offs, h, y, g_q, g_kv, w_qt, w_kv, w_vt, rope_tabs, rope_tabs_t, batch):
    n = h.shape[0]
    nblk = SEQ // TM
    row = lambda i, *_: (i, 0)
    col = lambda i, *_: (0, i)
    fixed = lambda i, *_: (0, 0)
    tab = lambda i, *_: (i % nblk, 0)
    tab_t = lambda i, *_: (0, i % nblk)
    key_spec = pl.BlockSpec((1, NSA_GROUPS, TM, KV_LANES), lambda i, *_: (i // nblk, 0, i % nblk, 0))
    key_shape = jax.ShapeDtypeStruct((batch, NSA_GROUPS, SEQ, KV_LANES), BF16)
    val_spec = pl.BlockSpec((1, NSA_GROUPS, V_ROWS, TM), lambda i, *_: (i // nblk, 0, 0, i % nblk))
    val_shape = jax.ShapeDtypeStruct((batch, NSA_GROUPS, V_ROWS, SEQ), BF16)
    raw_spec = pl.BlockSpec((TM, 128), row)
    raw_shape = jax.ShapeDtypeStruct((n, 128), F32)
    grid_spec = pltpu.PrefetchScalarGridSpec(
        num_scalar_prefetch=2,
        grid=(n // TM,),
        in_specs=[pl.BlockSpec((TM, D_MODEL), row),
                  pl.BlockSpec(memory_space=pl.ANY),
                  pl.BlockSpec((1, D_MODEL), fixed),
                  pl.BlockSpec((1, D_MODEL), fixed),
                  pl.BlockSpec((1152, D_MODEL), fixed),
                  pl.BlockSpec((D_MODEL, 1536), fixed),
                  pl.BlockSpec((512, D_MODEL), fixed),
                  pl.BlockSpec((TM, 128), tab), pl.BlockSpec((TM, 128), tab), pl.BlockSpec((TM, 128), tab),
                  pl.BlockSpec((NSA_HEAD_DIM, TM), tab_t), pl.BlockSpec((NSA_HEAD_DIM, TM), tab_t),
                  pl.BlockSpec((NSA_HEAD_DIM, TM), tab_t)],
        out_specs=[pl.BlockSpec((TM, D_MODEL), row),
                   pl.BlockSpec((1024, TM), col),
                   pl.BlockSpec((128, TM), col),
                   raw_spec, raw_spec, raw_spec, raw_spec,
                   key_spec, val_spec, key_spec, val_spec],
        scratch_shapes=[pltpu.VMEM((2, TM * ROW_CHUNKS, 128), F32), pltpu.SemaphoreType.DMA((2,))],
    )
    return pl.pallas_call(
        _nsa_proj_kernel,
        grid_spec=grid_spec,
        out_shape=[jax.ShapeDtypeStruct((n, D_MODEL), F32),
                   jax.ShapeDtypeStruct((1024, n), BF16),
                   jax.ShapeDtypeStruct((128, n), F32),
                   raw_shape, raw_shape, raw_shape, raw_shape,
                   key_shape, val_shape, key_shape, val_shape],
        compiler_params=_cparams("arbitrary"),
        name="nsa_proj",
    )(route, offs, h, y, g_q, g_kv, w_qt, w_kv, w_vt, *rope_tabs, *rope_tabs_t)


HALF_BLOCKS = SEQ // CMP_STRIDE
CMP_K = CMP_STRIDE * 256


def _compress_kernel(rk0_ref, rk1_ref, rv0_ref, rv1_ref, w1k_ref, w1v_ref, pek_ref, pev_ref, w2k_ref, w2v_ref,
                     cos_ref, sa_ref, sb_ref, kc_ref, vc_ref):
    lane = lax.broadcasted_iota(jnp.int32, (HALF_BLOCKS, 512), 1)
    first_half = (lane & 127) < 64

    def comp(raw_refs, w1_ref, pe_ref, w2_ref):
        x = jnp.concatenate([r[pl.ds(l, HALF_BLOCKS, stride=CMP_STRIDE), :]
                             for l in range(CMP_STRIDE) for r in raw_refs], axis=1).astype(BF16)
        r = jnp.dot(x, w1_ref[...], preferred_element_type=F32)
        rpe = jnp.dot(pe_ref[...].astype(BF16), w1_ref[...], preferred_element_type=F32)
        r = r + jnp.where(first_half, rpe[0:1, :], rpe[1:2, :])
        nxt = pltpu.roll(r, HALF_BLOCKS - 1, axis=0)
        nxt = jnp.concatenate([pltpu.roll(nxt[:, c * 128:(c + 1) * 128], 64, axis=1) for c in range(4)], axis=1)
        pre = r + nxt
        act = pre * _sigmoid(pre)
        return jnp.dot(act.astype(BF16), w2_ref[...], preferred_element_type=F32)

    kc = comp((rk0_ref, rk1_ref), w1k_ref, pek_ref, w2k_ref)
    kc = _rope(kc, cos_ref[...], sa_ref[...], sb_ref[...])
    vc = comp((rv0_ref, rv1_ref), w1v_ref, pev_ref, w2v_ref)
    for g in range(NSA_GROUPS):
        kc_ref[0, g] = kc[:, g * 64:(g + 1) * 64]
        vc_ref[0, g] = vc[:, g * 64:(g + 1) * 64]


def _compress_weights(pe, w1, w2):
    dh = NSA_HEAD_DIM
    w1r = w1.reshape(2, CMP_STRIDE, dh, dh)
    halves = jnp.concatenate([w1r[0], w1r[1]], axis=-1)
    rows = jnp.broadcast_to(halves[:, None], (CMP_STRIDE, NSA_GROUPS, dh, 2 * dh)).reshape(CMP_K, 2 * dh)
    row_group = (jnp.arange(CMP_K) // dh) % NSA_GROUPS
    w1_big = jnp.concatenate([jnp.where((row_group == g)[:, None], rows, 0.0) for g in range(NSA_GROUPS)],
                             axis=1).astype(BF16)
    pe_rows = jnp.broadcast_to(pe.reshape(2, CMP_STRIDE, 1, dh), (2, CMP_STRIDE, NSA_GROUPS, dh)).reshape(2, CMP_K)
    pe_rows = jnp.concatenate([pe_rows, jnp.zeros((6, CMP_K), F32)], axis=0)
    rows2 = jnp.tile(jnp.concatenate([w2, jnp.zeros_like(w2)], axis=0), (NSA_GROUPS, 1))
    row_group2 = jnp.arange(NSA_GROUPS * 2 * dh) // (2 * dh)
    w2_bd = jnp.concatenate([jnp.where((row_group2 == g)[:, None], rows2, 0.0) for g in range(NSA_GROUPS)],
                            axis=1).astype(BF16)
    return w1_big, pe_rows, w2_bd


def _compress(raw_k, raw_v, wk, wv, cmp_tabs, batch):
    fixed = lambda b: (0, 0)
    raw_spec = pl.BlockSpec((SEQ, 128), lambda b: (b, 0))
    out_spec = pl.BlockSpec((1, NSA_GROUPS, N_CMP_PAD, NSA_HEAD_DIM), lambda b: (b, 0, 0, 0))
    out_shape = jax.ShapeDtypeStruct((batch, NSA_GROUPS, N_CMP_PAD, NSA_HEAD_DIM), F32)
    return pl.pallas_call(
        _compress_kernel,
        grid=(batch,),
        in_specs=[raw_spec, raw_spec, raw_spec, raw_spec,
                  pl.BlockSpec((CMP_K, 512), fixed), pl.BlockSpec((CMP_K, 512), fixed),
                  pl.BlockSpec((8, CMP_K), fixed), pl.BlockSpec((8, CMP_K), fixed),
                  pl.BlockSpec((512, 256), fixed), pl.BlockSpec((512, 256), fixed),
                  pl.BlockSpec((N_CMP_PAD, 128), fixed), pl.BlockSpec((N_CMP_PAD, 128), fixed),
                  pl.BlockSpec((N_CMP_PAD, 128), fixed)],
        out_specs=[out_spec, out_spec],
        out_shape=[out_shape, out_shape],
        compiler_params=_cparams("parallel"),
        name="nsa_compress",
    )(*raw_k, *raw_v, wk[0], wv[0], wk[1], wv[1], wk[2], wv[2], *cmp_tabs)


def _nsa_attn_kernel(qt_ref, gatet_ref, kc_ref, vc_ref, ks_ref, vst_ref, kw_ref, vwt_ref, ovl_ref, out_ref,
                     qx_sc, ms_sc, as_sc, aw_sc):
    qb = pl.program_id(2)
    q0 = qb * Q_BLOCK
    Q, HG, DH = Q_BLOCK, NSA_GROUP_SIZE, NSA_HEAD_DIM
    groups = range(ATTN_GROUPS)

    def heads(x):
        return jnp.concatenate([x] * HG, axis=1)

    n_row = lax.broadcasted_iota(jnp.int32, (N_CMP_PAD, Q), 0)
    t_lane = q0 + lax.broadcasted_iota(jnp.int32, (N_CMP_PAD, Q), 1)
    valid_c = heads((n_row * CMP_STRIDE + CMP_BLOCK - 1 <= t_lane) & (n_row < N_CMP_PAD - 1))
    j_row = lax.broadcasted_iota(jnp.int32, (N_SEL, Q), 0)
    cur = (q0 + lax.broadcasted_iota(jnp.int32, (N_SEL, Q), 1)) >> 6
    forced = (j_row == 0) | (j_row == cur) | (j_row == cur - 1)
    o_c = []
    for gi in groups:
        q4 = qt_ref[gi * HG * DH:(gi + 1) * HG * DH, :]
        q_t = jnp.concatenate([q4[h * DH:(h + 1) * DH, :] for h in range(HG)], axis=1)
        s_c = jnp.dot(kc_ref[0, gi].astype(BF16), q_t, preferred_element_type=F32)
        s_c = jnp.where(valid_c, s_c, NEG_INF)
        m_c = jnp.max(s_c, axis=0, keepdims=True)
        e_c = jnp.where(valid_c, jnp.exp2(s_c - m_c), 0.0)
        l_c = jnp.sum(e_c, axis=0, keepdims=True)
        p_c = e_c * jnp.where(l_c > 0.0, 1.0 / l_c, 0.0)
        o_c.append(jnp.dot(vc_ref[0, gi].T.astype(BF16), p_c.astype(BF16), preferred_element_type=F32))

        p_sum = p_c[:, 0:Q] + p_c[:, Q:2 * Q] + p_c[:, 2 * Q:3 * Q] + p_c[:, 3 * Q:4 * Q]
        imp_t = jnp.dot(ovl_ref[...], p_sum, precision=HIGHEST, preferred_element_type=F32)[0:N_SEL]
        imp_t = jnp.where(forced, FORCED_SCORE, imp_t)
        imp_t = jnp.where(j_row > cur, NEG_INF, imp_t)
        cnt = jnp.zeros((N_SEL, Q), F32)
        for i in range(N_SEL):
            ri = imp_t[i:i + 1, :]
            cnt = cnt + jnp.where(ri > imp_t, 1.0, jnp.where((ri == imp_t) & (j_row > i), 1.0, 0.0))
        sel_bias = jnp.where((cnt < SEL_TOPK) & (j_row <= cur), 0.0, NEG_INF).astype(BF16)

        qx_sc[gi] = jnp.concatenate([q_t, heads(sel_bias), jnp.zeros((KV_LANES - DH - N_SEL, HG * Q), BF16)],
                                    axis=0)

    def finish(acc):
        return acc[0:DH] / acc[DEN_ROW:DEN_ROW + 1]

    ms_sc[...] = jnp.full(ms_sc.shape, NEG_INF, F32)
    as_sc[...] = jnp.zeros(as_sc.shape, F32)
    n_kt = (q0 + Q + SLC_TILE - 1) // SLC_TILE

    def slc_tile(kt, bias):
        start = pl.multiple_of(kt * SLC_TILE, SLC_TILE)
        for gi in groups:
            s = jnp.dot(ks_ref[0, gi, pl.ds(start, SLC_TILE), :], qx_sc[gi], preferred_element_type=F32)
            if bias is not None:
                s = s + bias
            m_old = ms_sc[gi, 0:1, :]
            m_new = jnp.maximum(m_old, jnp.max(s, axis=0, keepdims=True))
            p = jnp.exp2(s - m_new).astype(BF16)
            pv = jnp.dot(vst_ref[0, gi, :, pl.ds(start, SLC_TILE)], p, preferred_element_type=F32)
            as_sc[gi] = as_sc[gi] * jnp.exp2(m_old - m_new) + pv
            ms_sc[gi] = jnp.broadcast_to(m_new, ms_sc.shape[1:])

    def slc_full(kt, carry):
        slc_tile(kt, None)
        return carry

    lax.fori_loop(0, n_kt - 1, slc_full, 0)

    w0 = pl.multiple_of(jnp.maximum(q0 - WINDOW, 0), Q)
    s_w = [jnp.dot(kw_ref[0, gi, pl.ds(w0, WIN_KEYS), :], qx_sc[gi], preferred_element_type=F32) for gi in groups]

    c_s = lax.broadcasted_iota(jnp.int32, (SLC_TILE, Q), 0)
    r_s = lax.broadcasted_iota(jnp.int32, (SLC_TILE, Q), 1)
    slc_tile(n_kt - 1, heads(jnp.where((n_kt - 1) * SLC_TILE + c_s <= q0 + r_s, 0.0, NEG_INF)))

    c_w = lax.broadcasted_iota(jnp.int32, (Q, Q), 0)
    r_w = lax.broadcasted_iota(jnp.int32, (Q, Q), 1)
    n_slab = WIN_KEYS // Q
    slab_bias = []
    for j in range(n_slab):
        key = w0 + j * Q + c_w
        tok = q0 + r_w
        slab_bias.append(heads(jnp.where((key <= tok) & (key > tok - WINDOW), 0.0, NEG_INF)))
    for gi in groups:
        slabs = [s_w[gi][j * Q:(j + 1) * Q] + slab_bias[j] for j in range(n_slab)]
        top = slabs[0]
        for sj in slabs[1:]:
            top = jnp.maximum(top, sj)
        m = jnp.max(top, axis=0, keepdims=True)
        p = jnp.concatenate([jnp.exp2(sj - m) for sj in slabs], axis=0).astype(BF16)
        aw_sc[gi] = jnp.dot(vwt_ref[0, gi, :, pl.ds(w0, WIN_KEYS)], p, preferred_element_type=F32)

    for gi in groups:
        g = pl.program_id(1) * ATTN_GROUPS + gi

        def gate_row(branch):
            return jnp.concatenate([gatet_ref[pl.ds((g * HG + h) * 3 + branch, 1), :] for h in range(HG)], axis=1)

        o = gate_row(0) * o_c[gi] + gate_row(1) * finish(as_sc[gi]) + gate_row(2) * finish(aw_sc[gi])
        out_ref[gi * HG * DH:(gi + 1) * HG * DH, :] = jnp.concatenate(
            [o[:, h * Q:(h + 1) * Q] for h in range(HG)], axis=0).astype(BF16)


def _overlap_t():
    n = np.arange(N_CMP_PAD)
    j = np.arange(128)
    cmp_start = n * CMP_STRIDE
    cmp_end = cmp_start + CMP_BLOCK - 1
    sel_start = j * SEL_BLOCK
    ovl = ((cmp_start[None, :] <= sel_start[:, None] + SEL_BLOCK - 1) & (cmp_end[None, :] >= sel_start[:, None])
           & (j[:, None] < N_SEL) & (n[None, :] < N_CMP_PAD - 1))
    return jnp.asarray(ovl.astype(np.float32))


def _nsa_attn(q_t, gates_t, kc, vc, ks, vs_t, kw, vw_t, batch):
    n = q_t.shape[1]
    gs = ATTN_GROUPS
    qcol = lambda b, g, i: (g, b * N_QB + i)
    gcol = lambda b, g, i: (0, b * N_QB + i)
    kv = lambda b, g, i: (b, g, 0, 0)
    rows = gs * NSA_GROUP_SIZE * NSA_HEAD_DIM
    lanes = NSA_GROUP_SIZE * Q_BLOCK
    return pl.pallas_call(
        _nsa_attn_kernel,
        grid=(batch, NSA_GROUPS // gs, N_QB),
        in_specs=[pl.BlockSpec((rows, Q_BLOCK), qcol),
                  pl.BlockSpec((128, Q_BLOCK), gcol),
                  pl.BlockSpec((1, gs, N_CMP_PAD, NSA_HEAD_DIM), kv),
                  pl.BlockSpec((1, gs, N_CMP_PAD, NSA_HEAD_DIM), kv),
                  pl.BlockSpec((1, gs, SEQ, KV_LANES), kv),
                  pl.BlockSpec((1, gs, V_ROWS, SEQ), kv),
                  pl.BlockSpec((1, gs, SEQ, KV_LANES), kv),
                  pl.BlockSpec((1, gs, V_ROWS, SEQ), kv),
                  pl.BlockSpec((128, N_CMP_PAD), lambda b, g, i: (0, 0))],
        out_specs=pl.BlockSpec((rows, Q_BLOCK), qcol),
        out_shape=jax.ShapeDtypeStruct((NSA_HEADS * NSA_HEAD_DIM, n), BF16),
        scratch_shapes=[pltpu.VMEM((gs, KV_LANES, lanes), BF16),
                        pltpu.VMEM((gs, 8, lanes), F32),
                        pltpu.VMEM((gs, V_ROWS, lanes), F32),
                        pltpu.VMEM((gs, V_ROWS, lanes), F32)],
        compiler_params=_cparams("parallel", "parallel", "arbitrary"),
        name="nsa_attn",
    )(q_t, gates_t, kc, vc, ks, vs_t, kw, vw_t, _overlap_t())


def kernel(x, mlstm_norm, mlstm_w_in, mlstm_gate_bias, mlstm_head_norm, mlstm_w_out, kv_norm, kv_w, cmp_pe_k, cmp_w1_k, cmp_w2_k, cmp_pe_v, cmp_w1_v, cmp_w2_v, nsa_norm, nsa_w_q, nsa_w_out, moe_norm, moe_w_group, moe_b_group, moe_w_router, moe_b_router, moe_w_gate, moe_w_up, moe_w_down, final_norm):
    batch, seq, d = x.shape
    assert seq == SEQ and d == D_MODEL
    assert mlstm_norm.shape[0] == 1 and nsa_norm.shape[0] == 1 and moe_norm.shape[0] == 2
    n = batch * seq
    x2d = x.reshape(n, d)
    tril = jnp.tril(jnp.ones((TM, TM), F32)).astype(BF16)

    w_in = mlstm_w_in[0]
    w_gate = jnp.concatenate([w_in[:, 3072:3080], jnp.zeros((d, 120), F32)], axis=1)
    b_gate = jnp.concatenate([mlstm_gate_bias[0], jnp.zeros((120,), F32)]).reshape(1, 128)
    head_norm_cols = jnp.broadcast_to(mlstm_head_norm[0].reshape(d, 1), (d, 128))
    hs_t = _mlstm(x2d, mlstm_norm[0].reshape(1, d), w_in[:, 0:3072].T.astype(BF16), w_in[:, 512:1024].astype(BF16),
                  w_gate, b_gate, head_norm_cols, batch)
    w_gate_all, w_up_all, w_down_all = moe_w_gate.astype(BF16), moe_w_up.astype(BF16), moe_w_down.astype(BF16)
    h_pre, y, route, offs = _moe_layer(hs_t, mlstm_w_out[0].astype(BF16), x2d, moe_norm[0], moe_w_group[0],
                                       moe_b_group[0], moe_w_router[0], moe_b_router[0], w_gate_all, w_up_all,
                                       w_down_all, 0, tril, None)

    w_qt = jnp.concatenate([nsa_w_q[0].T, jnp.zeros((80, d), F32)], axis=0).astype(BF16)
    w_vt = jnp.concatenate([kv_w[:, 768:1024], kv_w[:, 1280:1536]], axis=1).T.astype(BF16)
    seq_tabs = _rope_tables(jnp.arange(SEQ), 128)
    seq_tabs_t = tuple(t[:, 0:NSA_HEAD_DIM].T for t in seq_tabs)
    h, q_t, gates_t, rk0, rk1, rv0, rv1, ks, vs_t, kw, vw_t = _nsa_proj(
        route, offs, h_pre, y, nsa_norm[0].reshape(1, d), kv_norm.reshape(1, d), w_qt, kv_w.astype(BF16), w_vt,
        seq_tabs, seq_tabs_t, batch)
    cmp_pos = jnp.arange(N_CMP_PAD) * CMP_STRIDE + CMP_BLOCK - 1
    kc, vc = _compress((rk0, rk1), (rv0, rv1), _compress_weights(cmp_pe_k, cmp_w1_k, cmp_w2_k),
                       _compress_weights(cmp_pe_v, cmp_w1_v, cmp_w2_v), _rope_tables(cmp_pos, 128), batch)
    att_t = _nsa_attn(q_t, gates_t, kc, vc, ks, vs_t, kw, vw_t, batch)
    out = _moe_layer(att_t, nsa_w_out[0].astype(BF16), h, moe_norm[1], moe_w_group[1], moe_b_group[1],
                     moe_w_router[1], moe_b_router[1], w_gate_all, w_up_all, w_down_all, 1,
                     tril, final_norm)
    return out.reshape(batch, seq, d)
```
